```python
import math
import jax
import jax.numpy as jnp
from jax import lax
import numpy as np

D_MODEL = 1024
BATCH = 4
SEQ = 8192
DEPTH = 2

GRID_W = 64
CTX_LEN = 256

SSM_GROUP_CH = 16
SSM_GROUPS = 16
SSM_WIDTH = SSM_GROUPS * SSM_GROUP_CH
SSM_STATE = 64
DT_MIN = 0.001
DT_MAX = 0.1
FNET_GROUPS = 4
FNET_GROUP_CH = 64
FNET_WIDTH = FNET_GROUPS * FNET_GROUP_CH
NA_HEADS = 8
NA_HEAD_DIM = 64
NA_WIDTH = NA_HEADS * NA_HEAD_DIM
WIN_ROWS = 8
WIN_COLS = 16
ROPE_THETA = 10000.0
N_BRANCH = 3
D_FF = 2816
N_MOD = 9
LN_EPS = 1e-6
DN_ALPHA = (2 * DEPTH) ** 0.25
DN_BETA = (8 * DEPTH) ** -0.25

COL_SSM = 0
COL_K = COL_SSM + SSM_WIDTH
COL_V = COL_K + NA_WIDTH
COL_CTX_END = COL_V + NA_WIDTH
COL_FNET = COL_CTX_END
COL_Q = COL_FNET + FNET_WIDTH
COL_GATE = COL_Q + NA_WIDTH
W_IN = COL_GATE + N_BRANCH * D_MODEL

kernel_name = 'hybrid_s5_fnet_natten_macaron_deepnorm'


def _ln(x):
    xf = x.astype(jnp.float32)
    mu = jnp.mean(xf, axis=-1, keepdims=True)
    var = jnp.mean(jnp.square(xf - mu), axis=-1, keepdims=True)
    return ((xf - mu) * lax.rsqrt(var + LN_EPS)).astype(x.dtype)


def _post_norm(x, res, g, b):
    return _ln(DN_ALPHA * x + res) * g + b


def _modulate(x, shift, scale):
    return _ln(x) * (1.0 + scale) + shift


def _swiglu(u, w_gate, w_up, w_down):
    return (jax.nn.silu(u @ w_gate) * (u @ w_up)) @ w_down


def _heads(t):
    return t.reshape(t.shape[0], t.shape[1], NA_HEADS, NA_HEAD_DIM)


def _cmul(ar, ai, br, bi):
    return ar * br - ai * bi, ar * bi + ai * br


def _scan_combine(e1, e2):
    a1r, a1i, b1r, b1i = e1
    a2r, a2i, b2r, b2i = e2
    ar, ai = _cmul(a2r, a2i, a1r, a1i)
    br, bi = _cmul(a2r, a2i, b1r, b1i)
    return ar, ai, br + b2r, bi + b2i


def _zoh(log_dt, a_re, a_im, b_re, b_im):
    dt = jnp.exp(log_dt.astype(jnp.float32))[:, None]
    a_re = a_re.astype(jnp.float32)
    a_im = a_im.astype(jnp.float32)
    mag = jnp.exp(a_re * dt)
    ab_re = mag * jnp.cos(a_im * dt)
    ab_im = mag * jnp.sin(a_im * dt)
    den = a_re * a_re + a_im * a_im
    nr = ab_re - 1.0
    ni = ab_im
    fr = (nr * a_re + ni * a_im) / den
    fi = (ni * a_re - nr * a_im) / den
    b_re = b_re.astype(jnp.float32)
    b_im = b_im.astype(jnp.float32)
    bb_re = fr[..., None] * b_re - fi[..., None] * b_im
    bb_im = fr[..., None] * b_im + fi[..., None] * b_re
    return ab_re, ab_im, bb_re, bb_im


def _s5_scan(u, disc, h0, reverse):
    ab_re, ab_im, bb_re, bb_im = disc
    uf = u.astype(jnp.float32)
    bu_re = jnp.einsum('blgc,gpc->blgp', uf, bb_re)
    bu_im = jnp.einsum('blgc,gpc->blgp', uf, bb_im)
    if h0 is not None:
        ah_re, ah_im = _cmul(ab_re, ab_im, h0[0], h0[1])
        first = -1 if reverse else 0
        bu_re = bu_re.at[:, first].add(ah_re)
        bu_im = bu_im.at[:, first].add(ah_im)
    a_re = jnp.broadcast_to(ab_re, bu_re.shape)
    a_im = jnp.broadcast_to(ab_im, bu_re.shape)
    _, _, h_re, h_im = lax.associative_scan(_scan_combine, (a_re, a_im, bu_re, bu_im), reverse=reverse, axis=1)
    return h_re, h_im


def _s5_readout(u, h_fwd, h_bwd, c_re, c_im, d, w_glu, b_glu):
    B, L, _ = u.shape
    cr = c_re.astype(jnp.float32)
    ci = c_im.astype(jnp.float32)
    y = (jnp.einsum('blgp,gcp->blgc', h_fwd[0], cr[0]) - jnp.einsum('blgp,gcp->blgc', h_fwd[1], ci[0])
         + jnp.einsum('blgp,gcp->blgc', h_bwd[0], cr[1]) - jnp.einsum('blgp,gcp->blgc', h_bwd[1], ci[1]))
    y = y.reshape(B, L, SSM_WIDTH).astype(u.dtype) + d * u
    y = jax.nn.gelu(y)
    return y * jax.nn.sigmoid(y @ w_glu + b_glu)


def _fourier_mix(f):
    B, L, _ = f.shape
    fg = f.reshape(B, L, FNET_GROUPS, FNET_GROUP_CH).astype(jnp.float32)
    y = jnp.fft.fft2(fg, axes=(1, 3), norm='ortho').real
    return y.reshape(B, L, FNET_WIDTH).astype(f.dtype)


def _axial_rope(x):
    B, L, H, hd = x.shape
    nf = hd // 4
    t = jnp.arange(L, dtype=jnp.int32)
    pos = jnp.stack([t // GRID_W, t % GRID_W], axis=-1).astype(jnp.float32)
    inv_freq = ROPE_THETA ** (-jnp.arange(nf, dtype=jnp.float32) / nf)
    ang = pos[:, :, None] * inv_freq
    cos = jnp.cos(ang)[None, :, None].astype(x.dtype)
    sin = jnp.sin(ang)[None, :, None].astype(x.dtype)
    xr = x.reshape(B, L, H, 2, 2, nf)
    x1 = xr[..., 0, :]
    x2 = xr[..., 1, :]
    out = jnp.stack([x1 * cos - x2 * sin, x2 * cos + x1 * sin], axis=-2)
    return out.reshape(B, L, H, hd)


def _neighbourhood_attention(q, k, v, kc, vc, rpb):
    B, L, H, hd = q.shape
    rows = L // GRID_W
    wr = min(WIN_ROWS, rows)
    wc = WIN_COLS
    scale = hd ** -0.5
    qg = q.reshape(B, rows, GRID_W, H, hd)
    kg = k.reshape(B, rows, GRID_W, H, hd)
    vg = v.reshape(B, rows, GRID_W, H, hd)
    col = jnp.arange(GRID_W, dtype=jnp.int32)
    col_start = jnp.clip(col - wc // 2, 0, GRID_W - wc)
    col_idx = col_start[:, None] + jnp.arange(wc, dtype=jnp.int32)[None, :]
    dc_idx = col_idx - col[:, None] + (WIN_COLS - 1)

    def row_block(r):
        r0 = jnp.clip(r - wr // 2, 0, rows - wr)
        k_rows = lax.dynamic_slice_in_dim(kg, r0, wr, axis=1)
        v_rows = lax.dynamic_slice_in_dim(vg, r0, wr, axis=1)
        k_win = k_rows[:, :, col_idx].transpose(0, 2, 1, 3, 4, 5).reshape(B, GRID_W, wr * wc, H, hd)
        v_win = v_rows[:, :, col_idx].transpose(0, 2, 1, 3, 4, 5).reshape(B, GRID_W, wr * wc, H, hd)
        q_r = lax.dynamic_index_in_dim(qg, r, axis=1, keepdims=False)
        dr_idx = r0 + jnp.arange(wr, dtype=jnp.int32) - r + (WIN_ROWS - 1)
        bias = rpb[:, dr_idx[None, :, None], dc_idx[:, None, :]].reshape(H, GRID_W, wr * wc)
        s_loc = jnp.einsum('bqhd,bqkhd->bhqk', q_r, k_win).astype(jnp.float32) * scale + bias[None].astype(jnp.float32)
        s_ctx = jnp.einsum('bqhd,bkhd->bhqk', q_r, kc).astype(jnp.float32) * scale
        p = jax.nn.softmax(jnp.concatenate([s_loc, s_ctx], axis=-1), axis=-1).astype(v.dtype)
        n_loc = wr * wc
        out = (jnp.einsum('bhqk,bqkhd->bqhd', p[..., :n_loc], v_win)
               + jnp.einsum('bhqk,bkhd->bqhd', p[..., n_loc:], vc))
        return out

    out = lax.map(row_block, jnp.arange(rows, dtype=jnp.int32))
    return out.transpose(1, 0, 2, 3, 4).reshape(B, L, H * hd)


def _context_attention(qc, kc, vc):
    B, Lc, H, hd = qc.shape
    s = jnp.einsum('bqhd,bkhd->bhqk', qc, kc).astype(jnp.float32) * (hd ** -0.5)
    p = jax.nn.softmax(s, axis=-1).astype(vc.dtype)
    return jnp.einsum('bhqk,bkhd->bqhd', p, vc).reshape(B, Lc, H * hd)


def _merge(y_s, y_f, y_n, gates, w_br_ssm, w_br_fnet, w_br_na, w_out):
    B, L, _ = gates.shape
    g = jax.nn.sigmoid(gates).reshape(B, L, N_BRANCH, D_MODEL)
    y = (g[:, :, 0] * (y_s @ w_br_ssm) + g[:, :, 1] * (y_f @ w_br_fnet) + g[:, :, 2] * (y_n @ w_br_na))
    return y @ w_out


def _mixer(u, uc, w_in, log_dt, a_re, a_im, b_re, b_im, c_re, c_im, d, w_glu, b_glu, rpb,
           w_br_ssm, w_br_fnet, w_br_na, w_out, ctx_out):
    B, L, _ = u.shape
    Lc = uc.shape[1]
    z = u @ w_in
    zc = uc @ (w_in if ctx_out else w_in[:, :COL_CTX_END])
    disc_f = _zoh(log_dt[0], a_re[0], a_im[0], b_re[0], b_im[0])
    disc_b = _zoh(log_dt[1], a_re[1], a_im[1], b_re[1], b_im[1])
    us = z[..., COL_SSM:COL_K]
    usc = zc[..., COL_SSM:COL_K]
    usg = us.reshape(B, L, SSM_GROUPS, SSM_GROUP_CH)
    uscg = usc.reshape(B, Lc, SSM_GROUPS, SSM_GROUP_CH)
    hcf = _s5_scan(uscg, disc_f, None, False)
    hcb = _s5_scan(uscg, disc_b, None, True)
    hf = _s5_scan(usg, disc_f, (hcf[0][:, -1], hcf[1][:, -1]), False)
    hb = _s5_scan(usg, disc_b, (hcb[0][:, 0], hcb[1][:, 0]), True)
    y_s = _s5_readout(us, hf, hb, c_re, c_im, d, w_glu, b_glu)
    y_f = _fourier_mix(z[..., COL_FNET:COL_Q])
    q = _axial_rope(_heads(z[..., COL_Q:COL_GATE]))
    k = _axial_rope(_heads(z[..., COL_K:COL_V]))
    v = _heads(z[..., COL_V:COL_CTX_END])
    kc = _heads(zc[..., COL_K:COL_V])
    vc = _heads(zc[..., COL_V:COL_CTX_END])
    y_n = _neighbourhood_attention(q, k, v, kc, vc, rpb)
    y = _merge(y_s, y_f, y_n, z[..., COL_GATE:], w_br_ssm, w_br_fnet, w_br_na, w_out)
    if not ctx_out:
        return y, None
    y_s_c = _s5_readout(usc, hcf, hcb, c_re, c_im, d, w_glu, b_glu)
    y_f_c = _fourier_mix(zc[..., COL_FNET:COL_Q])
    y_n_c = _context_attention(_heads(zc[..., COL_Q:COL_GATE]), kc, vc)
    yc = _merge(y_s_c, y_f_c, y_n_c, zc[..., COL_GATE:], w_br_ssm, w_br_fnet, w_br_na, w_out)
    return y, yc


def setup_inputs(seed: int = 0) -> dict:
    key = jax.random.key(seed)
    ks = jax.random.split(key, 27)
    f32 = jnp.float32

    def nrm(k, shape, scale):
        return jax.random.normal(k, shape, f32) * scale

    G, P, C = SSM_GROUPS, SSM_STATE, SSM_GROUP_CH
    n = jnp.arange(P, dtype=f32)
    return {
        'x': nrm(ks[0], (BATCH, SEQ, D_MODEL), 1.0),
        'c': nrm(ks[1], (BATCH, D_MODEL), 1.0),
        'ctx': nrm(ks[2], (BATCH, CTX_LEN, D_MODEL), 1.0),
        'c_ctx': nrm(ks[3], (D_MODEL,), 1.0),
        'w_mod': nrm(ks[4], (DEPTH, D_MODEL, N_MOD * D_MODEL), D_MODEL ** -0.5),
        'b_mod': nrm(ks[5], (DEPTH, N_MOD * D_MODEL), 0.02),
        'ln_g': 1.0 + nrm(ks[6], (DEPTH, 3, D_MODEL), 0.02),
        'ln_b': nrm(ks[7], (DEPTH, 3, D_MODEL), 0.02),
        'ffn_w_gate': nrm(ks[8], (DEPTH, 2, D_MODEL, D_FF), D_MODEL ** -0.5),
        'ffn_w_up': nrm(ks[9], (DEPTH, 2, D_MODEL, D_FF), D_MODEL ** -0.5),
        'ffn_w_down': nrm(ks[10], (DEPTH, 2, D_FF, D_MODEL), DN_BETA * D_FF ** -0.5),
        'w_in': nrm(ks[11], (DEPTH, D_MODEL, W_IN), D_MODEL ** -0.5),
        'ssm_log_dt': jax.random.uniform(ks[12], (DEPTH, 2, G), f32, math.log(DT_MIN), math.log(DT_MAX)),
        'ssm_a_re': -0.5 + nrm(ks[13], (DEPTH, 2, G, P), 0.01),
        'ssm_a_im': jnp.pi * n + nrm(ks[14], (DEPTH, 2, G, P), 0.01),
        'ssm_b_re': nrm(ks[15], (DEPTH, 2, G, P, C), (2 * C) ** -0.5),
        'ssm_b_im': nrm(ks[16], (DEPTH, 2, G, P, C), (2 * C) ** -0.5),
        'ssm_c_re': nrm(ks[17], (DEPTH, 2, G, C, P), P ** -0.5),
        'ssm_c_im': nrm(ks[18], (DEPTH, 2, G, C, P), P ** -0.5),
        'ssm_d': 1.0 + nrm(ks[19], (DEPTH, SSM_WIDTH), 0.1),
        'ssm_w_glu': nrm(ks[20], (DEPTH, SSM_WIDTH, SSM_WIDTH), SSM_WIDTH ** -0.5),
        'ssm_b_glu': nrm(ks[21], (DEPTH, SSM_WIDTH), 0.02),
        'na_rpb': nrm(ks[22], (DEPTH, NA_HEADS, 2 * WIN_ROWS - 1, 2 * WIN_COLS - 1), 0.02),
        'w_br_ssm': nrm(ks[23], (DEPTH, SSM_WIDTH, D_MODEL), SSM_WIDTH ** -0.5),
        'w_br_fnet': nrm(ks[24], (DEPTH, FNET_WIDTH, D_MODEL), FNET_WIDTH ** -0.5),
        'w_br_na': nrm(ks[25], (DEPTH, NA_WIDTH, D_MODEL), NA_WIDTH ** -0.5),
        'w_out': nrm(ks[26], (DEPTH, D_MODEL, D_MODEL), DN_BETA * D_MODEL ** -0.5),
    }


def reference(x, c, ctx, c_ctx, w_mod, b_mod, ln_g, ln_b, ffn_w_gate, ffn_w_up, ffn_w_down, w_in,
              ssm_log_dt, ssm_a_re, ssm_a_im, ssm_b_re, ssm_b_im, ssm_c_re, ssm_c_im, ssm_d,
              ssm_w_glu, ssm_b_glu, na_rpb, w_br_ssm, w_br_fnet, w_br_na, w_out):
    h, hc = x, ctx
    for l in range(DEPTH):
        last = l == DEPTH - 1
        mod = jax.nn.silu(c) @ w_mod[l] + b_mod[l]
        mod_c = jax.nn.silu(c_ctx) @ w_mod[l] + b_mod[l]
        m = [mod[:, None, i * D_MODEL:(i + 1) * D_MODEL] for i in range(N_MOD)]
        mc = [mod_c[i * D_MODEL:(i + 1) * D_MODEL] for i in range(N_MOD)]
        h = _post_norm(h, 0.5 * m[2] * _swiglu(_modulate(h, m[0], m[1]), ffn_w_gate[l, 0], ffn_w_up[l, 0], ffn_w_down[l, 0]),
                       ln_g[l, 0], ln_b[l, 0])
        hc = _post_norm(hc, 0.5 * mc[2] * _swiglu(_modulate(hc, mc[0], mc[1]), ffn_w_gate[l, 0], ffn_w_up[l, 0], ffn_w_down[l, 0]),
                        ln_g[l, 0], ln_b[l, 0])
        y, yc = _mixer(_modulate(h, m[3], m[4]), _modulate(hc, mc[3], mc[4]), w_in[l],
                       ssm_log_dt[l], ssm_a_re[l], ssm_a_im[l], ssm_b_re[l], ssm_b_im[l], ssm_c_re[l], ssm_c_im[l],
                       ssm_d[l], ssm_w_glu[l], ssm_b_glu[l], na_rpb[l], w_br_ssm[l], w_br_fnet[l], w_br_na[l], w_out[l],
                       not last)
        h = _post_norm(h, m[5] * y, ln_g[l, 1], ln_b[l, 1])
        h = _post_norm(h, 0.5 * m[8] * _swiglu(_modulate(h, m[6], m[7]), ffn_w_gate[l, 1], ffn_w_up[l, 1], ffn_w_down[l, 1]),
                       ln_g[l, 2], ln_b[l, 2])
        if not last:
            hc = _post_norm(hc, mc[5] * yc, ln_g[l, 1], ln_b[l, 1])
            hc = _post_norm(hc, 0.5 * mc[8] * _swiglu(_modulate(hc, mc[6], mc[7]), ffn_w_gate[l, 1], ffn_w_up[l, 1], ffn_w_down[l, 1]),
                            ln_g[l, 2], ln_b[l, 2])
    return h
```

```python
import functools
import math

import jax
import jax.numpy as jnp
import numpy as np
from jax import lax
from jax.experimental import pallas as pl
from jax.experimental.pallas import tpu as pltpu

BF = jnp.bfloat16
F32 = jnp.float32

GRID_W = 64
SSM_GROUPS = 16
SSM_GROUP_CH = 16
SSM_STATE = 64
SSM_WIDTH = SSM_GROUPS * SSM_GROUP_CH
SSM_FLAT = SSM_GROUPS * SSM_STATE
FNET_GROUPS = 4
FNET_GROUP_CH = 64
FNET_WIDTH = FNET_GROUPS * FNET_GROUP_CH
NA_HEADS = 8
NA_HEAD_DIM = 64
NA_WIDTH = NA_HEADS * NA_HEAD_DIM
WIN_ROWS = 8
WIN_COLS = 16
ROPE_THETA = 10000.0
N_MOD = 9
LN_EPS = 1e-6
LOG2E = 1.4426950408889634
NEG_BIG = -1e30

VMEM_LIMIT_BYTES = 56 * 1024 * 1024
ROW_TILE = 512
FFN_CHUNK = 256
SCAN_ROWS = 8
NA_QROWS = 8
NA_KROWS = 16
NA_HQ = 4
FNET_N2 = 128


def _cparams(n_axes):
    return pltpu.CompilerParams(dimension_semantics=("arbitrary",) * n_axes,
                                vmem_limit_bytes=VMEM_LIMIT_BYTES)


def _ln(x):
    mu = jnp.mean(x, axis=-1, keepdims=True)
    xc = x - mu
    var = jnp.mean(xc * xc, axis=-1, keepdims=True)
    return xc * lax.rsqrt(var + LN_EPS)


def _dot(a, b):
    return jnp.dot(a, b, preferred_element_type=F32)


def _dot_t(a, b):
    return lax.dot_general(a, b, (((1,), (1,)), ((), ())), preferred_element_type=F32)


def _mod_kernel(c_ref, w_ref, b_ref, o_ref):
    c = c_ref[...]
    s = (c * jax.nn.sigmoid(c)).astype(BF)
    o_ref[...] = _dot(s, w_ref[...].astype(BF)) + b_ref[...]


def _mod_call(c8, w_mod, b_mod):
    depth, d, nd = w_mod.shape
    nb = nd // d
    return pl.pallas_call(
        _mod_kernel,
        grid=(depth, nb),
        in_specs=[pl.BlockSpec((8, d), lambda l, j: (0, 0)),
                  pl.BlockSpec((None, d, d), lambda l, j: (l, 0, j)),
                  pl.BlockSpec((None, 1, d), lambda l, j: (l, 0, j))],
        out_specs=pl.BlockSpec((None, 8, d), lambda l, j: (l, 0, j)),
        out_shape=jax.ShapeDtypeStruct((depth, 8, nd), F32),
        compiler_params=_cparams(2),
        name="mod",
    )(c8, w_mod, b_mod.reshape(depth, 1, nd))


def _ffn_kernel(x_ref, t_ref, wg_ref, wu_ref, wd_ref, o_ref, *, alpha):
    x = x_ref[...]
    t = t_ref[...]
    u = (_ln(x) * (1.0 + t[1:2]) + t[0:1]).astype(BF)
    acc = jnp.zeros(x.shape, F32)
    for c0 in range(0, wg_ref.shape[1], FFN_CHUNK):
        a = _dot(u, wg_ref[:, c0:c0 + FFN_CHUNK])
        b = _dot(u, wu_ref[:, c0:c0 + FFN_CHUNK])
        act = (a * jax.nn.sigmoid(a) * b).astype(BF)
        acc = acc + _dot(act, wd_ref[c0:c0 + FFN_CHUNK, :])
    v = alpha * x + (0.5 * t[2:3]) * acc
    o_ref[...] = _ln(v) * t[3:4] + t[4:5]


def _ffn_call(x, tab, wg, wu, wd, alpha):
    nb, n, d = x.shape
    dff = wg.shape[1]
    tm = min(ROW_TILE, n)
    const = lambda b, i: (0, 0)
    return pl.pallas_call(
        functools.partial(_ffn_kernel, alpha=alpha),
        grid=(nb, n // tm),
        in_specs=[pl.BlockSpec((None, tm, d), lambda b, i: (b, i, 0)),
                  pl.BlockSpec((None, 8, d), lambda b, i: (b, 0, 0)),
                  pl.BlockSpec((d, dff), const),
                  pl.BlockSpec((d, dff), const),
                  pl.BlockSpec((dff, d), const)],
        out_specs=pl.BlockSpec((None, tm, d), lambda b, i: (b, i, 0)),
        out_shape=jax.ShapeDtypeStruct(x.shape, F32),
        compiler_params=_cparams(2),
        name="ffn",
    )(x, tab, wg, wu, wd)


def _rope(x, cos, sin_signed):
    n = x.shape[-1]
    lane = lax.broadcasted_iota(jnp.int32, x.shape, 1)
    first = (lane % 32) < 16
    partner = jnp.where(first, pltpu.roll(x, n - 16, axis=1), pltpu.roll(x, 16, axis=1))
    return x * cos + partner * sin_signed


def _mixin_kernel(x_ref, t_ref, w_ref, cos_ref, sin_ref,
                  us_ref, k_ref, v_ref, f_ref, q_ref, g_ref, *, q_scale):
    x = x_ref[...]
    t = t_ref[...]
    u = (_ln(x) * (1.0 + t[1:2]) + t[0:1]).astype(BF)
    cos = cos_ref[...]
    sin = sin_ref[...]
    c_k = SSM_WIDTH
    c_v = c_k + NA_WIDTH
    c_f = c_v + NA_WIDTH
    c_q = c_f + FNET_WIDTH
    c_g = c_q + NA_WIDTH
    us_ref[...] = _dot(u, w_ref[:, 0:c_k]).astype(BF)
    k_ref[...] = _rope(_dot(u, w_ref[:, c_k:c_v]), cos, sin).astype(BF)
    v_ref[...] = _dot(u, w_ref[:, c_v:c_f]).astype(BF)
    f_ref[...] = _dot(u, w_ref[:, c_f:c_q]).astype(BF)
    q_ref[...] = (_rope(_dot(u, w_ref[:, c_q:c_g]), cos, sin) * q_scale).astype(BF)
    gw = NA_WIDTH
    for c0 in range(0, g_ref.shape[-1], gw):
        g = _dot(u, w_ref[:, c_g + c0:c_g + c0 + gw])
        g_ref[:, c0:c0 + gw] = jax.nn.sigmoid(g).astype(BF)


def _mixin_call(x, tab, w_in, cos_t, sin_t, q_scale):
    nb, n, d = x.shape
    win = w_in.shape[1]
    ng = win - (SSM_WIDTH + 3 * NA_WIDTH + FNET_WIDTH)
    tm = min(ROW_TILE, n)
    nt_pos = cos_t.shape[0] // tm

    def tok(width):
        return pl.BlockSpec((None, tm, width), lambda b, i: (b, i, 0))

    def shape(width):
        return jax.ShapeDtypeStruct((nb, n, width), BF)

    return pl.pallas_call(
        functools.partial(_mixin_kernel, q_scale=q_scale),
        grid=(nb, n // tm),
        in_specs=[tok(d),
                  pl.BlockSpec((None, 8, d), lambda b, i: (b, 0, 0)),
                  pl.BlockSpec((d, win), lambda b, i: (0, 0)),
                  pl.BlockSpec((tm, NA_WIDTH), lambda b, i: (i % nt_pos, 0)),
                  pl.BlockSpec((tm, NA_WIDTH), lambda b, i: (i % nt_pos, 0))],
        out_specs=[tok(SSM_WIDTH), tok(NA_WIDTH), tok(NA_WIDTH), tok(FNET_WIDTH), tok(NA_WIDTH), tok(ng)],
        out_shape=[shape(SSM_WIDTH), shape(NA_WIDTH), shape(NA_WIDTH), shape(FNET_WIDTH),
                   shape(NA_WIDTH), shape(ng)],
        compiler_params=_cparams(2),
        name="mixin",
    )(x, tab, w_in, cos_t, sin_t)


def _s5_kernel(*refs, reverse, readout, n_chunks):
    if readout:
        (u_ref, h0_ref, bd_ref, cd_ref, sc_ref, yb_ref, d_ref, wglu_ref, bglu_ref,
         y_ref, hfin_ref, hbuf, carry) = refs
    else:
        u_ref, h0_ref, bd_ref, cd_ref, sc_ref, y_ref, hfin_ref, hbuf, carry = refs
    i = pl.program_id(1)
    ns = SSM_FLAT
    t_len = hbuf.shape[0]
    n_tiles = t_len // SCAN_ROWS

    @pl.when(i == 0)
    def _():
        carry[...] = h0_ref[...]

    u = u_ref[...]
    hbuf[...] = _dot(u, bd_ref[...])

    def tile(it, c):
        cr, ci = c
        row = (n_tiles - 1 - it) if reverse else it
        r0 = pl.multiple_of(row * SCAN_ROWS, SCAN_ROWS)
        xr = hbuf[pl.ds(r0, SCAN_ROWS), 0:ns]
        xi = hbuf[pl.ds(r0, SCAN_ROWS), ns:2 * ns]
        for k, s in enumerate((1, 2, 4)):
            ar = sc_ref[k, :, 0:ns]
            ai = sc_ref[k, :, ns:2 * ns]
            shift = (SCAN_ROWS - s) if reverse else s
            sr = pltpu.roll(xr, shift, axis=0)
            si = pltpu.roll(xi, shift, axis=0)
            xr, xi = xr + ar * sr - ai * si, xi + ar * si + ai * sr
        pr = sc_ref[3, :, 0:ns]
        pi = sc_ref[3, :, ns:2 * ns]
        hr = xr + pr * cr - pi * ci
        hi = xi + pr * ci + pi * cr
        hbuf[pl.ds(r0, SCAN_ROWS), 0:ns] = hr
        hbuf[pl.ds(r0, SCAN_ROWS), ns:2 * ns] = hi
        last = 0 if reverse else SCAN_ROWS - 1
        return (jnp.broadcast_to(hr[last:last + 1, :], hr.shape),
                jnp.broadcast_to(hi[last:last + 1, :], hi.shape))

    cr, ci = lax.fori_loop(0, n_tiles, tile, (carry[:, 0:ns], carry[:, ns:2 * ns]))
    carry[:, 0:ns] = cr
    carry[:, ns:2 * ns] = ci
    hfin_ref[...] = carry[...]

    y = _dot(hbuf[...].astype(BF), cd_ref[...])
    if readout:
        y = y + yb_ref[...] + d_ref[...] * u.astype(F32)
        y = jax.nn.gelu(y)
        gate = jax.nn.sigmoid(_dot(y.astype(BF), wglu_ref[...]) + bglu_ref[...])
        y_ref[...] = (y * gate).astype(y_ref.dtype)
    else:
        y_ref[...] = y


def _s5_call(us, h0, bd, cd, sc, t_len, reverse, readout=None):
    b, n, w = us.shape
    nc = n // t_len
    ns2 = 2 * SSM_FLAT
    if reverse:
        cmap = lambda bb, i: (bb, nc - 1 - i, 0)
    else:
        cmap = lambda bb, i: (bb, i, 0)
    const2 = lambda bb, i: (0, 0)
    in_specs = [pl.BlockSpec((None, t_len, w), cmap),
                pl.BlockSpec((None, 8, ns2), lambda bb, i: (bb, 0, 0)),
                pl.BlockSpec((w, ns2), const2),
                pl.BlockSpec((ns2, w), const2),
                pl.BlockSpec((4, 8, ns2), lambda bb, i: (0, 0, 0))]
    args = [us, h0, bd, cd, sc]
    if readout is not None:
        yb, dvec, wglu, bglu = readout
        in_specs += [pl.BlockSpec((None, t_len, w), cmap),
                     pl.BlockSpec((1, w), const2),
                     pl.BlockSpec((w, w), const2),
                     pl.BlockSpec((1, w), const2)]
        args += [yb, dvec, wglu, bglu]
    return pl.pallas_call(
        functools.partial(_s5_kernel, reverse=reverse, readout=readout is not None, n_chunks=nc),
        grid=(b, nc),
        in_specs=in_specs,
        out_specs=[pl.BlockSpec((None, t_len, w), cmap),
                   pl.BlockSpec((None, 8, ns2), lambda bb, i: (bb, 0, 0))],
        out_shape=[jax.ShapeDtypeStruct((b, n, w), BF if readout is not None else F32),
                   jax.ShapeDtypeStruct((b, 8, ns2), F32)],
        scratch_shapes=[pltpu.VMEM((t_len, ns2), F32), pltpu.VMEM((8, ns2), F32)],
        compiler_params=_cparams(2),
        name="s5_fwd" if readout is not None else "s5_bwd",
    )(*args)


def _zoh(log_dt, a_re, a_im, b_re, b_im):
    dt = jnp.exp(log_dt.astype(F32))[:, None]
    mag = jnp.exp(a_re * dt)
    ab_re = mag * jnp.cos(a_im * dt)
    ab_im = mag * jnp.sin(a_im * dt)
    den = a_re * a_re + a_im * a_im
    nr = ab_re - 1.0
    ni = ab_im
    fr = (nr * a_re + ni * a_im) / den
    fi = (ni * a_re - nr * a_im) / den
    bb_re = fr[..., None] * b_re - fi[..., None] * b_im
    bb_im = fr[..., None] * b_im + fi[..., None] * b_re
    return ab_re, ab_im, bb_re, bb_im


def _s5_tables(log_dt, a_re, a_im, b_re, b_im, c_re, c_im, reverse):
    g, p, c = b_re.shape
    ab_re, ab_im, bb_re, bb_im = _zoh(log_dt, a_re, a_im, b_re, b_im)
    eye = jnp.eye(g, dtype=F32)
    bd = jnp.concatenate([jnp.einsum('gpc,gh->gchp', bb_re, eye).reshape(g * c, g * p),
                          jnp.einsum('gpc,gh->gchp', bb_im, eye).reshape(g * c, g * p)], axis=1)
    cd = jnp.concatenate([jnp.einsum('gcp,gh->gphc', c_re, eye).reshape(g * p, g * c),
                          -jnp.einsum('gcp,gh->gphc', c_im, eye).reshape(g * p, g * c)], axis=0)
    ar = ab_re.reshape(-1)
    ai = ab_im.reshape(-1)
    pw = [(ar, ai)]
    for _ in range(SCAN_ROWS - 1):
        qr, qi = pw[-1]
        pw.append((qr * ar - qi * ai, qr * ai + qi * ar))
    rows = np.arange(SCAN_ROWS)
    tabs = []
    for s in (1, 2, 4):
        keep = (rows + s <= SCAN_ROWS - 1) if reverse else (rows >= s)
        keep = jnp.asarray(keep, F32)[:, None]
        tabs.append(jnp.concatenate([keep * pw[s - 1][0][None], keep * pw[s - 1][1][None]], axis=1))
    order = [SCAN_ROWS - 1 - r for r in rows] if reverse else list(rows)
    tabs.append(jnp.concatenate([jnp.stack([pw[k][0] for k in order]),
                                 jnp.stack([pw[k][1] for k in order])], axis=1))
    return bd.astype(BF), cd.astype(BF), jnp.stack(tabs).astype(F32)


def _fnet1_kernel(x_ref, f2_ref, tc_ref, ts_ref, o_ref):
    n2 = x_ref.shape[0]
    y = _dot(f2_ref[...], x_ref[...])
    br = y[0:n2]
    bi = y[n2:2 * n2]
    tc = tc_ref[...]
    ts = ts_ref[...]
    o_ref[0] = (br * tc + bi * ts).astype(BF)
    o_ref[1] = (bi * tc - br * ts).astype(BF)


def _fnet2_kernel(b_ref, f1_ref, cc_ref, sc_ref, o_ref, *, scale):
    kb = b_ref.shape[1]
    n1 = b_ref.shape[2]
    c = b_ref.shape[3]
    zr, zi = [], []
    for j in range(kb):
        rhs = jnp.concatenate([b_ref[0, j], b_ref[1, j]], axis=0)
        z = _dot(f1_ref[...], rhs)
        zr.append(z[0:n1])
        zi.append(z[n1:2 * n1])
    zr = jnp.concatenate(zr, axis=0).astype(BF)
    zi = jnp.concatenate(zi, axis=0).astype(BF)
    y = (_dot(zr, cc_ref[...]) + _dot(zi, sc_ref[...])) * scale
    for j in range(kb):
        o_ref[:, j * c:(j + 1) * c] = y[j * n1:(j + 1) * n1].astype(o_ref.dtype)


def _dft_tables(n):
    idx = np.arange(n)
    ang = 2.0 * np.pi * ((idx[:, None] * idx[None, :]) % n) / n
    return np.cos(ang), np.sin(ang)


def _chan_tables():
    cc, sc = _dft_tables(FNET_GROUP_CH)
    eye = np.eye(FNET_GROUPS)
    return jnp.asarray(np.kron(eye, cc), BF), jnp.asarray(np.kron(eye, sc), BF)


def _fnet_call(f):
    b, n, c = f.shape
    n2 = FNET_N2
    n1 = n // n2
    c2, s2 = _dft_tables(n2)
    f2 = jnp.asarray(np.concatenate([c2, -s2], axis=0), BF)
    k2 = np.arange(n2)[:, None]
    nn1 = np.arange(n1)[None, :]
    ang = 2.0 * np.pi * ((k2 * nn1) % n) / n
    tc = jnp.repeat(jnp.asarray(np.cos(ang), F32), c, axis=1)
    ts = jnp.repeat(jnp.asarray(np.sin(ang), F32), c, axis=1)
    c1, s1 = _dft_tables(n1)
    f1 = jnp.asarray(np.block([[c1, s1], [-s1, c1]]), BF)
    cc, sc = _chan_tables()

    wc = min(n1 * c, 4096)
    x2 = f.reshape(b, n2, n1 * c)
    st1 = pl.pallas_call(
        _fnet1_kernel,
        grid=(b, (n1 * c) // wc),
        in_specs=[pl.BlockSpec((None, n2, wc), lambda bb, j: (bb, 0, j)),
                  pl.BlockSpec((2 * n2, n2), lambda bb, j: (0, 0)),
                  pl.BlockSpec((n2, wc), lambda bb, j: (0, j)),
                  pl.BlockSpec((n2, wc), lambda bb, j: (0, j))],
        out_specs=pl.BlockSpec((None, 2, n2, wc), lambda bb, j: (bb, 0, 0, j)),
        out_shape=jax.ShapeDtypeStruct((b, 2, n2, n1 * c), BF),
        compiler_params=_cparams(2),
        name="fnet1",
    )(x2, f2, tc, ts)
    st1 = st1.reshape(b, 2, n2, n1, c)
    kb = 8
    y = pl.pallas_call(
        functools.partial(_fnet2_kernel, scale=1.0 / math.sqrt(n * FNET_GROUP_CH)),
        grid=(b, n2 // kb),
        in_specs=[pl.BlockSpec((None, 2, kb, n1, c), lambda bb, j: (bb, 0, j, 0, 0)),
                  pl.BlockSpec((2 * n1, 2 * n1), lambda bb, j: (0, 0)),
                  pl.BlockSpec((c, c), lambda bb, j: (0, 0)),
                  pl.BlockSpec((c, c), lambda bb, j: (0, 0))],
        out_specs=pl.BlockSpec((None, n1, kb * c), lambda bb, j: (bb, 0, j)),
        out_shape=jax.ShapeDtypeStruct((b, n1, n2 * c), BF),
        compiler_params=_cparams(2),
        name="fnet2",
    )(st1, f1, cc, sc)
    return y.reshape(b, n, c)


def _fnet_dense_kernel(x_ref, cl_ref, sl_ref, cc_ref, sc_ref, o_ref, *, scale):
    x = x_ref[...]
    xr = _dot(x, cc_ref[...]).astype(BF)
    xi = _dot(x, sc_ref[...]).astype(BF)
    o_ref[...] = ((_dot(cl_ref[...], xr) - _dot(sl_ref[...], xi)) * scale).astype(o_ref.dtype)


def _fnet_dense_call(f):
    b, n, c = f.shape
    cl, sl = _dft_tables(n)
    cc, sc = _chan_tables()
    full = lambda bb: (0, 0)
    return pl.pallas_call(
        functools.partial(_fnet_dense_kernel, scale=1.0 / math.sqrt(n * FNET_GROUP_CH)),
        grid=(b,),
        in_specs=[pl.BlockSpec((None, n, c), lambda bb: (bb, 0, 0)),
                  pl.BlockSpec((n, n), full), pl.BlockSpec((n, n), full),
                  pl.BlockSpec((c, c), full), pl.BlockSpec((c, c), full)],
        out_specs=pl.BlockSpec((None, n, c), lambda bb: (bb, 0, 0)),
        out_shape=jax.ShapeDtypeStruct((b, n, c), BF),
        compiler_params=_cparams(1),
        name="fnet_ctx",
    )(f, jnp.asarray(cl, BF), jnp.asarray(sl, BF), cc, sc)


def _head_masks(width):
    lane = lax.broadcasted_iota(jnp.int32, (1, width), 1)
    return [(lane // NA_HEAD_DIM) == h for h in range(width // NA_HEAD_DIM)]


def _attend(q, k_parts, v_parts, bias_of, o_ref):
    zero = jnp.zeros((), BF)
    out = jnp.zeros(q.shape, F32)
    for h, hm in enumerate(_head_masks(q.shape[-1])):
        s_parts = []
        for j, kp in enumerate(k_parts):
            s = _dot_t(q, jnp.where(hm, kp, zero))
            b = bias_of(h, j)
            s_parts.append(s if b is None else s + b)
        m = s_parts[0].max(axis=-1, keepdims=True)
        for s in s_parts[1:]:
            m = jnp.maximum(m, s.max(axis=-1, keepdims=True))
        den = jnp.zeros_like(m)
        acc = jnp.zeros(q.shape, F32)
        for s, vp in zip(s_parts, v_parts):
            p = jnp.exp2(s - m)
            den = den + p.sum(axis=-1, keepdims=True)
            acc = acc + _dot(p.astype(BF), jnp.where(hm, vp, zero))
        out = out + acc / den
    o_ref[...] = out.astype(o_ref.dtype)


def _natten_kernel(q_ref, k0, k1, k2, k3, v0, v1, v2, v3, kc_ref, vc_ref, bias_ref, o_ref):
    blk = k0.shape[0]
    k_parts = [r[...] for r in (k0, k1, k2, k3)] + [kc_ref[...]]
    v_parts = [r[...] for r in (v0, v1, v2, v3)] + [vc_ref[...]]

    def bias_of(h, j):
        if j >= 4:
            return None
        return bias_ref[h, :, j * blk:(j + 1) * blk].astype(F32)

    _attend(q_ref[...], k_parts, v_parts, bias_of, o_ref)


def _natten_call(q, k, v, kc, vc, bias):
    b, n, w = q.shape
    lc = kc.shape[1]
    hw = NA_HQ * NA_HEAD_DIM
    qt = NA_QROWS * GRID_W
    blk = (NA_KROWS // 4) * GRID_W
    nj = n // qt
    nblk = n // blk
    ratio = qt // blk

    def kv_spec(m):
        def imap(hq, j, bb):
            start = jnp.clip(ratio * j - 1, 0, nblk - 4)
            return (bb, start + m, hq)
        return pl.BlockSpec((None, blk, hw), imap)

    def bias_map(hq, j, bb):
        var = jnp.where(j == 0, 0, jnp.where(j == nj - 1, 2, 1))
        return (var, hq, 0, 0)

    return pl.pallas_call(
        _natten_kernel,
        grid=(w // hw, nj, b),
        in_specs=[pl.BlockSpec((None, qt, hw), lambda hq, j, bb: (bb, j, hq))]
                 + [kv_spec(m) for m in range(4)] + [kv_spec(m) for m in range(4)]
                 + [pl.BlockSpec((None, lc, hw), lambda hq, j, bb: (bb, 0, hq)),
                    pl.BlockSpec((None, lc, hw), lambda hq, j, bb: (bb, 0, hq)),
                    pl.BlockSpec((None, NA_HQ, qt, 4 * blk), bias_map)],
        out_specs=pl.BlockSpec((None, qt, hw), lambda hq, j, bb: (bb, j, hq)),
        out_shape=jax.ShapeDtypeStruct((b, n, w), BF),
        compiler_params=_cparams(3),
        name="natten",
    )(q, k, k, k, k, v, v, v, v, kc, vc, bias)


def _ctx_atten_kernel(q_ref, kc_ref, vc_ref, o_ref):
    _attend(q_ref[...], [kc_ref[...]], [vc_ref[...]], lambda h, j: None, o_ref)


def _ctx_atten_call(qc, kc, vc):
    b, lc, w = qc.shape
    hw = NA_HQ * NA_HEAD_DIM
    spec = pl.BlockSpec((None, lc, hw), lambda bb, hq: (bb, 0, hq))
    return pl.pallas_call(
        _ctx_atten_kernel,
        grid=(b, w // hw),
        in_specs=[spec, spec, spec],
        out_specs=spec,
        out_shape=jax.ShapeDtypeStruct((b, lc, w), BF),
        compiler_params=_cparams(2),
        name="ctx_atten",
    )(qc, kc, vc)


def _na_bias(rpb, rows):
    w = GRID_W
    nj = rows // NA_QROWS
    cq = np.arange(w)
    cs = np.clip(cq - WIN_COLS // 2, 0, w - WIN_COLS)
    ck = np.arange(w)
    valid_c = (ck[None, :] >= cs[:, None]) & (ck[None, :] < cs[:, None] + WIN_COLS)
    dc = np.clip(ck[None, :] - cq[:, None] + (WIN_COLS - 1), 0, 2 * WIN_COLS - 2)
    out = []
    for j in (0, min(1, nj - 1), nj - 1):
        rq = NA_QROWS * j + np.arange(NA_QROWS)
        ws = int(np.clip(NA_QROWS * j - (NA_KROWS - NA_QROWS) // 2, 0, rows - NA_KROWS))
        rk = ws + np.arange(NA_KROWS)
        r0 = np.clip(rq - WIN_ROWS // 2, 0, rows - WIN_ROWS)
        valid_r = (rk[None, :] >= r0[:, None]) & (rk[None, :] < r0[:, None] + WIN_ROWS)
        dr = np.clip(rk[None, :] - rq[:, None] + (WIN_ROWS - 1), 0, 2 * WIN_ROWS - 2)
        bias = rpb[:, dr[:, None, :, None], dc[None, :, None, :]]
        valid = valid_r[:, None, :, None] & valid_c[None, :, None, :]
        bias = jnp.where(valid[None], bias * LOG2E, NEG_BIG)
        out.append(bias.reshape(rpb.shape[0], NA_QROWS * w, NA_KROWS * w))
    return jnp.stack(out).astype(BF)


def _rope_tables(n):
    nf = NA_HEAD_DIM // 4
    t = jnp.arange(n, dtype=jnp.int32)
    pos = jnp.stack([t // GRID_W, t % GRID_W], axis=-1).astype(F32)
    inv_freq = ROPE_THETA ** (-jnp.arange(nf, dtype=F32) / nf)
    ang = pos[:, :, None] * inv_freq
    cos = jnp.cos(ang)
    sin = jnp.sin(ang)
    cos_h = jnp.concatenate([cos, cos], axis=-1).reshape(n, NA_HEAD_DIM)
    sin_h = jnp.concatenate([-sin, sin], axis=-1).reshape(n, NA_HEAD_DIM)
    return jnp.tile(cos_h, (1, NA_HEADS)), jnp.tile(sin_h, (1, NA_HEADS))


def _merge_kernel(ys_ref, yf_ref, yn_ref, g_ref, h_ref, t_ref, wbs_ref, wbf_ref, wbn_ref, wo_ref, o_ref,
                  *, alpha):
    d = h_ref.shape[-1]
    t = t_ref[...]
    y = (g_ref[:, 0:d].astype(F32) * _dot(ys_ref[...], wbs_ref[...])
         + g_ref[:, d:2 * d].astype(F32) * _dot(yf_ref[...], wbf_ref[...])
         + g_ref[:, 2 * d:3 * d].astype(F32) * _dot(yn_ref[...], wbn_ref[...]))
    y = _dot(y.astype(BF), wo_ref[...])
    v = alpha * h_ref[...] + t[2:3] * y
    o_ref[...] = _ln(v) * t[3:4] + t[4:5]


def _merge_call(ys, yf, yn, g, h, tab, wbs, wbf, wbn, wo, alpha):
    nb, n, d = h.shape
    tm = min(ROW_TILE, n)

    def tok(width):
        return pl.BlockSpec((None, tm, width), lambda b, i: (b, i, 0))

    def full(a):
        return pl.BlockSpec(a.shape, lambda b, i: (0, 0))

    return pl.pallas_call(
        functools.partial(_merge_kernel, alpha=alpha),
        grid=(nb, n // tm),
        in_specs=[tok(ys.shape[-1]), tok(yf.shape[-1]), tok(yn.shape[-1]), tok(g.shape[-1]), tok(d),
                  pl.BlockSpec((None, 8, d), lambda b, i: (b, 0, 0)),
                  full(wbs), full(wbf), full(wbn), full(wo)],
        out_specs=tok(d),
        out_shape=jax.ShapeDtypeStruct(h.shape, F32),
        compiler_params=_cparams(2),
        name="merge",
    )(ys, yf, yn, g, h, tab, wbs, wbf, wbn, wo)


def _table(mod, d, chunks, extra):
    rows = [mod[:, i * d:(i + 1) * d] for i in chunks]
    rows += [jnp.broadcast_to(e[None, :], (8, d)) for e in extra]
    rows += [jnp.zeros((8, d), F32)] * (8 - len(rows))
    return jnp.stack(rows, axis=1)


def kernel(x, c, ctx, c_ctx, w_mod, b_mod, ln_g, ln_b, ffn_w_gate, ffn_w_up, ffn_w_down, w_in, ssm_log_dt, ssm_a_re, ssm_a_im, ssm_b_re, ssm_b_im, ssm_c_re, ssm_c_im, ssm_d, ssm_w_glu, ssm_b_glu, na_rpb, w_br_ssm, w_br_fnet, w_br_na, w_out):
    bsz, n, d = x.shape
    lc = ctx.shape[1]
    depth = w_mod.shape[0]
    rows = n // GRID_W
    assert bsz + 1 <= 8 and rows % NA_QROWS == 0 and rows >= NA_KROWS and n % FNET_N2 == 0
    alpha = (2 * depth) ** 0.25
    q_scale = NA_HEAD_DIM ** -0.5 * LOG2E
    s5_chunk = min(ROW_TILE, n)

    c8 = jnp.concatenate([c, c_ctx[None, :], jnp.zeros((8 - bsz - 1, d), F32)], axis=0)
    mod_all = _mod_call(c8, w_mod, b_mod)

    cos_t, sin_t = _rope_tables(n)
    cos_c = jnp.ones((bsz * lc, NA_WIDTH), F32)
    sin_c = jnp.zeros((bsz * lc, NA_WIDTH), F32)
    zero_state = jnp.zeros((bsz, 8, 2 * SSM_FLAT), F32)

    h = x
    hc = ctx.reshape(1, bsz * lc, d)
    for l in range(depth):
        last = l == depth - 1
        mod = mod_all[l]
        wg = ffn_w_gate[l].astype(BF)
        wu = ffn_w_up[l].astype(BF)
        wd = ffn_w_down[l].astype(BF)
        win = w_in[l].astype(BF)

        tab = _table(mod, d, (0, 1, 2), (ln_g[l, 0], ln_b[l, 0]))
        h = _ffn_call(h, tab[:bsz], wg[0], wu[0], wd[0], alpha)
        hc = _ffn_call(hc, tab[bsz:bsz + 1], wg[0], wu[0], wd[0], alpha)

        tab = _table(mod, d, (3, 4, 5), (ln_g[l, 1], ln_b[l, 1]))
        us, k, v, f, q, g = _mixin_call(h, tab[:bsz], win, cos_t, sin_t, q_scale)
        usc, kc, vc, fc, qc, gc = _mixin_call(hc, tab[bsz:bsz + 1], win, cos_c, sin_c, q_scale)
        usc, kc, vc, fc, qc = [a.reshape(bsz, lc, a.shape[-1]) for a in (usc, kc, vc, fc, qc)]

        tabs_f = _s5_tables(ssm_log_dt[l, 0], ssm_a_re[l, 0], ssm_a_im[l, 0], ssm_b_re[l, 0], ssm_b_im[l, 0],
                            ssm_c_re[l, 0], ssm_c_im[l, 0], False)
        tabs_b = _s5_tables(ssm_log_dt[l, 1], ssm_a_re[l, 1], ssm_a_im[l, 1], ssm_b_re[l, 1], ssm_b_im[l, 1],
                            ssm_c_re[l, 1], ssm_c_im[l, 1], True)
        readout_w = (ssm_d[l][None, :], ssm_w_glu[l].astype(BF), ssm_b_glu[l][None, :])
        ybc, hcb = _s5_call(usc, zero_state, *tabs_b, lc, True)
        ysc, hcf = _s5_call(usc, zero_state, *tabs_f, lc, False, (ybc,) + readout_w)
        yb, _ = _s5_call(us, hcb, *tabs_b, s5_chunk, True)
        ys, _ = _s5_call(us, hcf, *tabs_f, s5_chunk, False, (yb,) + readout_w)

        yf = _fnet_call(f)
        yn = _natten_call(q, k, v, kc, vc, _na_bias(na_rpb[l], rows))

        wbs = w_br_ssm[l].astype(BF)
        wbf = w_br_fnet[l].astype(BF)
        wbn = w_br_na[l].astype(BF)
        wo = w_out[l].astype(BF)
        h = _merge_call(ys, yf, yn, g, h, tab[:bsz], wbs, wbf, wbn, wo, alpha)

        tab3 = _table(mod, d, (6, 7, 8), (ln_g[l, 2], ln_b[l, 2]))
        h = _ffn_call(h, tab3[:bsz], wg[1], wu[1], wd[1], alpha)

        if not last:
            yfc = _fnet_dense_call(fc)
            ync = _ctx_atten_call(qc, kc, vc)
            flat = lambda a: a.reshape(1, bsz * lc, a.shape[-1])
            hc = _merge_call(flat(ysc), flat(yfc), flat(ync), gc, hc, tab[bsz:bsz + 1], wbs, wbf, wbn, wo, alpha)
            hc = _ffn_call(hc, tab3[bsz:bsz + 1], wg[1], wu[1], wd[1], alpha)
    return h
```

```python
import functools
import math

import jax
import jax.numpy as jnp
import numpy as np
from jax import lax
from jax.experimental import pallas as pl
from jax.experimental.pallas import tpu as pltpu

BF = jnp.bfloat16
F32 = jnp.float32

GRID_W = 64
SSM_GROUPS = 16
SSM_GROUP_CH = 16
SSM_STATE = 64
SSM_WIDTH = SSM_GROUPS * SSM_GROUP_CH
SSM_FLAT = SSM_GROUPS * SSM_STATE
FNET_GROUPS = 4
FNET_GROUP_CH = 64
FNET_WIDTH = FNET_GROUPS * FNET_GROUP_CH
NA_HEADS = 8
NA_HEAD_DIM = 64
NA_WIDTH = NA_HEADS * NA_HEAD_DIM
WIN_ROWS = 8
WIN_COLS = 16
ROPE_THETA = 10000.0
N_MOD = 9
LN_EPS = 1e-6
LOG2E = 1.4426950408889634
NEG_BIG = -1e30

VMEM_LIMIT_BYTES = 56 * 1024 * 1024
ROW_TILE = 512
FFN_CHUNK = 256
SCAN_ROWS = 8
NA_QROWS = 8
NA_KROWS = 16
NA_HQ = 4
FNET_N2 = 128


def _cparams(n_axes):
    return pltpu.CompilerParams(dimension_semantics=("arbitrary",) * n_axes,
                                vmem_limit_bytes=VMEM_LIMIT_BYTES)


def _ln(x):
    mu = jnp.mean(x, axis=-1, keepdims=True)
    xc = x - mu
    var = jnp.mean(xc * xc, axis=-1, keepdims=True)
    return xc * lax.rsqrt(var + LN_EPS)


def _dot(a, b):
    return jnp.dot(a, b, preferred_element_type=F32)


def _dot_t(a, b):
    return lax.dot_general(a, b, (((1,), (1,)), ((), ())), preferred_element_type=F32)


def _mod_kernel(c_ref, w_ref, b_ref, o_ref):
    c = c_ref[...]
    s = (c * jax.nn.sigmoid(c)).astype(BF)
    o_ref[...] = _dot(s, w_ref[...].astype(BF)) + b_ref[...]


def _mod_call(c8, w_mod, b_mod):
    depth, d, nd = w_mod.shape
    nb = nd // d
    return pl.pallas_call(
        _mod_kernel,
        grid=(depth, nb),
        in_specs=[pl.BlockSpec((8, d), lambda l, j: (0, 0)),
                  pl.BlockSpec((None, d, d), lambda l, j: (l, 0, j)),
                  pl.BlockSpec((None, 1, d), lambda l, j: (l, 0, j))],
        out_specs=pl.BlockSpec((None, 8, d), lambda l, j: (l, 0, j)),
        out_shape=jax.ShapeDtypeStruct((depth, 8, nd), F32),
        compiler_params=_cparams(2),
        name="mod",
    )(c8, w_mod, b_mod.reshape(depth, 1, nd))


def _ffn_kernel(x_ref, t_ref, wg_ref, wu_ref, wd_ref, o_ref, *, alpha):
    x = x_ref[...]
    t = t_ref[...]
    u = (_ln(x) * (1.0 + t[1:2]) + t[0:1]).astype(BF)
    acc = jnp.zeros(x.shape, F32)
    for c0 in range(0, wg_ref.shape[1], FFN_CHUNK):
        a = _dot(u, wg_ref[:, c0:c0 + FFN_CHUNK])
        b = _dot(u, wu_ref[:, c0:c0 + FFN_CHUNK])
        act = (a * jax.nn.sigmoid(a) * b).astype(BF)
        acc = acc + _dot(act, wd_ref[c0:c0 + FFN_CHUNK, :])
    v = alpha * x + (0.5 * t[2:3]) * acc
    o_ref[...] = _ln(v) * t[3:4] + t[4:5]


def _ffn_call(x, tab, wg, wu, wd, alpha):
    nb, n, d = x.shape
    dff = wg.shape[1]
    tm = min(ROW_TILE, n)
    const = lambda b, i: (0, 0)
    return pl.pallas_call(
        functools.partial(_ffn_kernel, alpha=alpha),
        grid=(nb, n // tm),
        in_specs=[pl.BlockSpec((None, tm, d), lambda b, i: (b, i, 0)),
                  pl.BlockSpec((None, 8, d), lambda b, i: (b, 0, 0)),
                  pl.BlockSpec((d, dff), const),
                  pl.BlockSpec((d, dff), const),
                  pl.BlockSpec((dff, d), const)],
        out_specs=pl.BlockSpec((None, tm, d), lambda b, i: (b, i, 0)),
        out_shape=jax.ShapeDtypeStruct(x.shape, F32),
        compiler_params=_cparams(2),
        name="ffn",
    )(x, tab, wg, wu, wd)


def _rope(x, cos, sin_signed):
    n = x.shape[-1]
    lane = lax.broadcasted_iota(jnp.int32, x.shape, 1)
    first = (lane % 32) < 16
    partner = jnp.where(first, pltpu.roll(x, n - 16, axis=1), pltpu.roll(x, 16, axis=1))
    return x * cos + partner * sin_signed


def _mixin_kernel(x_ref, t_ref, w_ref, cos_ref, sin_ref,
                  us_ref, k_ref, v_ref, f_ref, q_ref, g_ref, *, q_scale):
    x = x_ref[...]
    t = t_ref[...]
    u = (_ln(x) * (1.0 + t[1:2]) + t[0:1]).astype(BF)
    cos = cos_ref[...]
    sin = sin_ref[...]
    c_k = SSM_WIDTH
    c_v = c_k + NA_WIDTH
    c_f = c_v + NA_WIDTH
    c_q = c_f + FNET_WIDTH
    c_g = c_q + NA_WIDTH
    us_ref[...] = _dot(u, w_ref[:, 0:c_k]).astype(BF)
    k_ref[...] = _rope(_dot(u, w_ref[:, c_k:c_v]), cos, sin).astype(BF)
    v_ref[...] = _dot(u, w_ref[:, c_v:c_f]).astype(BF)
    f_ref[...] = _dot(u, w_ref[:, c_f:c_q]).astype(BF)
    q_ref[...] = (_rope(_dot(u, w_ref[:, c_q:c_g]), cos, sin) * q_scale).astype(BF)
    gw = NA_WIDTH
    for c0 in range(0, g_ref.shape[-1], gw):
        g = _dot(u, w_ref[:, c_g + c0:c_g + c0 + gw])
        g_ref[:, c0:c0 + gw] = jax.nn.sigmoid(g).astype(BF)


def _mixin_call(x, tab, w_in, cos_t, sin_t, q_scale):
    nb, n, d = x.shape
    win = w_in.shape[1]
    ng = win - (SSM_WIDTH + 3 * NA_WIDTH + FNET_WIDTH)
    tm = min(ROW_TILE, n)
    nt_pos = cos_t.shape[0] // tm

    def tok(width):
        return pl.BlockSpec((None, tm, width), lambda b, i: (b, i, 0))

    def shape(width):
        return jax.ShapeDtypeStruct((nb, n, width), BF)

    return pl.pallas_call(
        functools.partial(_mixin_kernel, q_scale=q_scale),
        grid=(nb, n // tm),
        in_specs=[tok(d),
                  pl.BlockSpec((None, 8, d), lambda b, i: (b, 0, 0)),
                  pl.BlockSpec((d, win), lambda b, i: (0, 0)),
                  pl.BlockSpec((tm, NA_WIDTH), lambda b, i: (i % nt_pos, 0)),
                  pl.BlockSpec((tm, NA_WIDTH), lambda b, i: (i % nt_pos, 0))],
        out_specs=[tok(SSM_WIDTH), tok(NA_WIDTH), tok(NA_WIDTH), tok(FNET_WIDTH), tok(NA_WIDTH), tok(ng)],
        out_shape=[shape(SSM_WIDTH), shape(NA_WIDTH), shape(NA_WIDTH), shape(FNET_WIDTH),
                   shape(NA_WIDTH), shape(ng)],
        compiler_params=_cparams(2),
        name="mixin",
    )(x, tab, w_in, cos_t, sin_t)


def _s5_kernel(*refs, reverse, readout, n_chunks):
    if readout:
        (u_ref, h0_ref, bd_ref, cd_ref, sc_ref, yb_ref, d_ref, wglu_ref, bglu_ref,
         y_ref, hfin_ref, hbuf, carry) = refs
    else:
        u_ref, h0_ref, bd_ref, cd_ref, sc_ref, y_ref, hfin_ref, hbuf, carry = refs
    i = pl.program_id(1)
    ns = SSM_FLAT
    t_len = hbuf.shape[0]
    n_tiles = t_len // SCAN_ROWS

    @pl.when(i == 0)
    def _():
        carry[...] = h0_ref[...]

    u = u_ref[...]
    hbuf[...] = _dot(u, bd_ref[...])

    def tile(it, c):
        cr, ci = c
        row = (n_tiles - 1 - it) if reverse else it
        r0 = pl.multiple_of(row * SCAN_ROWS, SCAN_ROWS)
        xr = hbuf[pl.ds(r0, SCAN_ROWS), 0:ns]
        xi = hbuf[pl.ds(r0, SCAN_ROWS), ns:2 * ns]
        for k, s in enumerate((1, 2, 4)):
            ar = sc_ref[k, :, 0:ns]
            ai = sc_ref[k, :, ns:2 * ns]
            shift = (SCAN_ROWS - s) if reverse else s
            sr = pltpu.roll(xr, shift, axis=0)
            si = pltpu.roll(xi, shift, axis=0)
            xr, xi = xr + ar * sr - ai * si, xi + ar * si + ai * sr
        pr = sc_ref[3, :, 0:ns]
        pi = sc_ref[3, :, ns:2 * ns]
        hr = xr + pr * cr - pi * ci
        hi = xi + pr * ci + pi * cr
        hbuf[pl.ds(r0, SCAN_ROWS), 0:ns] = hr
        hbuf[pl.ds(r0, SCAN_ROWS), ns:2 * ns] = hi
        last = 0 if reverse else SCAN_ROWS - 1
        return (jnp.broadcast_to(hr[last:last + 1, :], hr.shape),
                jnp.broadcast_to(hi[last:last + 1, :], hi.shape))

    cr, ci = lax.fori_loop(0, n_tiles, tile, (carry[:, 0:ns], carry[:, ns:2 * ns]))
    carry[:, 0:ns] = cr
    carry[:, ns:2 * ns] = ci
    hfin_ref[...] = carry[...]

    y = _dot(hbuf[...].astype(BF), cd_ref[...])
    if readout:
        y = y + yb_ref[...] + d_ref[...] * u.astype(F32)
        y = jax.nn.gelu(y)
        gate = jax.nn.sigmoid(_dot(y.astype(BF), wglu_ref[...]) + bglu_ref[...])
        y_ref[...] = (y * gate).astype(y_ref.dtype)
    else:
        y_ref[...] = y


def _s5_call(us, h0, bd, cd, sc, t_len, reverse, readout=None):
    b, n, w = us.shape
    nc = n // t_len
    ns2 = 2 * SSM_FLAT
    if reverse:
        cmap = lambda bb, i: (bb, nc - 1 - i, 0)
    else:
        cmap = lambda bb, i: (bb, i, 0)
    const2 = lambda bb, i: (0, 0)
    in_specs = [pl.BlockSpec((None, t_len, w), cmap),
                pl.BlockSpec((None, 8, ns2), lambda bb, i: (bb, 0, 0)),
                pl.BlockSpec((w, ns2), const2),
                pl.BlockSpec((ns2, w), const2),
                pl.BlockSpec((4, 8, ns2), lambda bb, i: (0, 0, 0))]
    args = [us, h0, bd, cd, sc]
    if readout is not None:
        yb, dvec, wglu, bglu = readout
        in_specs += [pl.BlockSpec((None, t_len, w), cmap),
                     pl.BlockSpec((1, w), const2),
                     pl.BlockSpec((w, w), const2),
                     pl.BlockSpec((1, w), const2)]
        args += [yb, dvec, wglu, bglu]
    return pl.pallas_call(
        functools.partial(_s5_kernel, reverse=reverse, readout=readout is not None, n_chunks=nc),
        grid=(b, nc),
        in_specs=in_specs,
        out_specs=[pl.BlockSpec((None, t_len, w), cmap),
                   pl.BlockSpec((None, 8, ns2), lambda bb, i: (bb, 0, 0))],
        out_shape=[jax.ShapeDtypeStruct((b, n, w), BF if readout is not None else F32),
                   jax.ShapeDtypeStruct((b, 8, ns2), F32)],
        scratch_shapes=[pltpu.VMEM((t_len, ns2), F32), pltpu.VMEM((8, ns2), F32)],
        compiler_params=_cparams(2),
        name="s5_fwd" if readout is not None else "s5_bwd",
    )(*args)


def _zoh(log_dt, a_re, a_im, b_re, b_im):
    dt = jnp.exp(log_dt.astype(F32))[:, None]
    mag = jnp.exp(a_re * dt)
    ab_re = mag * jnp.cos(a_im * dt)
    ab_im = mag * jnp.sin(a_im * dt)
    den = a_re * a_re + a_im * a_im
    nr = ab_re - 1.0
    ni = ab_im
    fr = (nr * a_re + ni * a_im) / den
    fi = (ni * a_re - nr * a_im) / den
    bb_re = fr[..., None] * b_re - fi[..., None] * b_im
    bb_im = fr[..., None] * b_im + fi[..., None] * b_re
    return ab_re, ab_im, bb_re, bb_im


def _s5_tables(log_dt, a_re, a_im, b_re, b_im, c_re, c_im, reverse):
    g, p, c = b_re.shape
    ab_re, ab_im, bb_re, bb_im = _zoh(log_dt, a_re, a_im, b_re, b_im)
    eye = jnp.eye(g, dtype=F32)
    bd = jnp.concatenate([jnp.einsum('gpc,gh->gchp', bb_re, eye).reshape(g * c, g * p),
                          jnp.einsum('gpc,gh->gchp', bb_im, eye).reshape(g * c, g * p)], axis=1)
    cd = jnp.concatenate([jnp.einsum('gcp,gh->gphc', c_re, eye).reshape(g * p, g * c),
                          -jnp.einsum('gcp,gh->gphc', c_im, eye).reshape(g * p, g * c)], axis=0)
    ar = ab_re.reshape(-1)
    ai = ab_im.reshape(-1)
    pw = [(ar, ai)]
    for _ in range(SCAN_ROWS - 1):
        qr, qi = pw[-1]
        pw.append((qr * ar - qi * ai, qr * ai + qi * ar))
    rows = np.arange(SCAN_ROWS)
    tabs = []
    for s in (1, 2, 4):
        keep = (rows + s <= SCAN_ROWS - 1) if reverse else (rows >= s)
        keep = jnp.asarray(keep, F32)[:, None]
        tabs.append(jnp.concatenate([keep * pw[s - 1][0][None], keep * pw[s - 1][1][None]], axis=1))
    order = [SCAN_ROWS - 1 - r for r in rows] if reverse else list(rows)
    tabs.append(jnp.concatenate([jnp.stack([pw[k][0] for k in order]),
                                 jnp.stack([pw[k][1] for k in order])], axis=1))
    return bd.astype(BF), cd.astype(BF), jnp.stack(tabs).astype(F32)


def _fnet1_kernel(x_ref, f2_ref, tc_ref, ts_ref, o_ref):
    n2 = x_ref.shape[0]
    y = _dot(f2_ref[...], x_ref[...])
    br = y[0:n2]
    bi = y[n2:2 * n2]
    tc = tc_ref[...]
    ts = ts_ref[...]
    o_ref[0] = (br * tc + bi * ts).astype(BF)
    o_ref[1] = (bi * tc - br * ts).astype(BF)


def _fnet2_kernel(b_ref, f1_ref, cc_ref, sc_ref, o_ref, *, scale):
    kb = b_ref.shape[1]
    n1 = b_ref.shape[2]
    c = b_ref.shape[3]
    zr, zi = [], []
    for j in range(kb):
        rhs = jnp.concatenate([b_ref[0, j], b_ref[1, j]], axis=0)
        z = _dot(f1_ref[...], rhs)
        zr.append(z[0:n1])
        zi.append(z[n1:2 * n1])
    zr = jnp.concatenate(zr, axis=0).astype(BF)
    zi = jnp.concatenate(zi, axis=0).astype(BF)
    y = (_dot(zr, cc_ref[...]) + _dot(zi, sc_ref[...])) * scale
    for j in range(kb):
        o_ref[:, j * c:(j + 1) * c] = y[j * n1:(j + 1) * n1].astype(o_ref.dtype)


def _dft_tables(n):
    idx = np.arange(n)
    ang = 2.0 * np.pi * ((idx[:, None] * idx[None, :]) % n) / n
    return np.cos(ang), np.sin(ang)


def _chan_tables():
    cc, sc = _dft_tables(FNET_GROUP_CH)
    eye = np.eye(FNET_GROUPS)
    return jnp.asarray(np.kron(eye, cc), BF), jnp.asarray(np.kron(eye, sc), BF)


def _fnet_call(f):
    b, n, c = f.shape
    n2 = FNET_N2
    n1 = n // n2
    c2, s2 = _dft_tables(n2)
    f2 = jnp.asarray(np.concatenate([c2, -s2], axis=0), BF)
    k2 = np.arange(n2)[:, None]
    nn1 = np.arange(n1)[None, :]
    ang = 2.0 * np.pi * ((k2 * nn1) % n) / n
    tc = jnp.repeat(jnp.asarray(np.cos(ang), F32), c, axis=1)
    ts = jnp.repeat(jnp.asarray(np.sin(ang), F32), c, axis=1)
    c1, s1 = _dft_tables(n1)
    f1 = jnp.asarray(np.block([[c1, s1], [-s1, c1]]), BF)
    cc, sc = _chan_tables()

    wc = min(n1 * c, 4096)
    x2 = f.reshape(b, n2, n1 * c)
    st1 = pl.pallas_call(
        _fnet1_kernel,
        grid=(b, (n1 * c) // wc),
        in_specs=[pl.BlockSpec((None, n2, wc), lambda bb, j: (bb, 0, j)),
                  pl.BlockSpec((2 * n2, n2), lambda bb, j: (0, 0)),
                  pl.BlockSpec((n2, wc), lambda bb, j: (0, j)),
                  pl.BlockSpec((n2, wc), lambda bb, j: (0, j))],
        out_specs=pl.BlockSpec((None, 2, n2, wc), lambda bb, j: (bb, 0, 0, j)),
        out_shape=jax.ShapeDtypeStruct((b, 2, n2, n1 * c), BF),
        compiler_params=_cparams(2),
        name="fnet1",
    )(x2, f2, tc, ts)
    st1 = st1.reshape(b, 2, n2, n1, c)
    kb = 8
    y = pl.pallas_call(
        functools.partial(_fnet2_kernel, scale=1.0 / math.sqrt(n * FNET_GROUP_CH)),
        grid=(b, n2 // kb),
        in_specs=[pl.BlockSpec((None, 2, kb, n1, c), lambda bb, j: (bb, 0, j, 0, 0)),
                  pl.BlockSpec((2 * n1, 2 * n1), lambda bb, j: (0, 0)),
                  pl.BlockSpec((c, c), lambda bb, j: (0, 0)),
                  pl.BlockSpec((c, c), lambda bb, j: (0, 0))],
        out_specs=pl.BlockSpec((None, n1, kb * c), lambda bb, j: (bb, 0, j)),
        out_shape=jax.ShapeDtypeStruct((b, n1, n2 * c), BF),
        compiler_params=_cparams(2),
        name="fnet2",
    )(st1, f1, cc, sc)
    return y.reshape(b, n, c)


def _fnet_dense_kernel(x_ref, cl_ref, sl_ref, cc_ref, sc_ref, o_ref, *, scale):
    x = x_ref[...]
    xr = _dot(x, cc_ref[...]).astype(BF)
    xi = _dot(x, sc_ref[...]).astype(BF)
    o_ref[...] = ((_dot(cl_ref[...], xr) - _dot(sl_ref[...], xi)) * scale).astype(o_ref.dtype)


def _fnet_dense_call(f):
    b, n, c = f.shape
    cl, sl = _dft_tables(n)
    cc, sc = _chan_tables()
    full = lambda bb: (0, 0)
    return pl.pallas_call(
        functools.partial(_fnet_dense_kernel, scale=1.0 / math.sqrt(n * FNET_GROUP_CH)),
        grid=(b,),
        in_specs=[pl.BlockSpec((None, n, c), lambda bb: (bb, 0, 0)),
                  pl.BlockSpec((n, n), full), pl.BlockSpec((n, n), full),
                  pl.BlockSpec((c, c), full), pl.BlockSpec((c, c), full)],
        out_specs=pl.BlockSpec((None, n, c), lambda bb: (bb, 0, 0)),
        out_shape=jax.ShapeDtypeStruct((b, n, c), BF),
        compiler_params=_cparams(1),
        name="fnet_ctx",
    )(f, jnp.asarray(cl, BF), jnp.asarray(sl, BF), cc, sc)


def _head_masks(width):
    lane = lax.broadcasted_iota(jnp.int32, (1, width), 1)
    return [(lane // NA_HEAD_DIM) == h for h in range(width // NA_HEAD_DIM)]


def _attend(q, k_parts, v_parts, bias_of, o_ref):
    zero = jnp.zeros((), BF)
    out = jnp.zeros(q.shape, F32)
    for h, hm in enumerate(_head_masks(q.shape[-1])):
        s_parts = []
        for j, kp in enumerate(k_parts):
            s = _dot_t(q, jnp.where(hm, kp, zero))
            b = bias_of(h, j)
            s_parts.append(s if b is None else s + b)
        m = s_parts[0].max(axis=-1, keepdims=True)
        for s in s_parts[1:]:
            m = jnp.maximum(m, s.max(axis=-1, keepdims=True))
        den = jnp.zeros_like(m)
        acc = jnp.zeros(q.shape, F32)
        for s, vp in zip(s_parts, v_parts):
            p = jnp.exp2(s - m)
            den = den + p.sum(axis=-1, keepdims=True)
            acc = acc + _dot(p.astype(BF), jnp.where(hm, vp, zero))
        out = out + acc / den
    o_ref[...] = out.astype(o_ref.dtype)


def _natten_kernel(q_ref, k0, k1, k2, k3, v0, v1, v2, v3, kc_ref, vc_ref, bias_ref, o_ref):
    blk = k0.shape[0]
    k_parts = [r[...] for r in (k0, k1, k2, k3)] + [kc_ref[...]]
    v_parts = [r[...] for r in (v0, v1, v2, v3)] + [vc_ref[...]]

    def bias_of(h, j):
        if j >= 4:
            return None
        return bias_ref[h, :, j * blk:(j + 1) * blk].astype(F32)

    _attend(q_ref[...], k_parts, v_parts, bias_of, o_ref)


def _natten_call(q, k, v, kc, vc, bias):
    b, n, w = q.shape
    lc = kc.shape[1]
    hw = NA_HQ * NA_HEAD_DIM
    qt = NA_QROWS * GRID_W
    blk = (NA_KROWS // 4) * GRID_W
    nj = n // qt
    nblk = n // blk
    ratio = qt // blk

    def kv_spec(m):
        def imap(hq, j, bb):
            start = jnp.clip(ratio * j - 1, 0, nblk - 4)
            return (bb, start + m, hq)
        return pl.BlockSpec((None, blk, hw), imap)

    def bias_map(hq, j, bb):
        var = jnp.where(j == 0, 0, jnp.where(j == nj - 1, 2, 1))
        return (var, hq, 0, 0)

    return pl.pallas_call(
        _natten_kernel,
        grid=(w // hw, nj, b),
        in_specs=[pl.BlockSpec((None, qt, hw), lambda hq, j, bb: (bb, j, hq))]
                 + [kv_spec(m) for m in range(4)] + [kv_spec(m) for m in range(4)]
                 + [pl.BlockSpec((None, lc, hw), lambda hq, j, bb: (bb, 0, hq)),
                    pl.BlockSpec((None, lc, hw), lambda hq, j, bb: (bb, 0, hq)),
                    pl.BlockSpec((None, NA_HQ, qt, 4 * blk), bias_map)],
        out_specs=pl.BlockSpec((None, qt, hw), lambda hq, j, bb: (bb, j, hq)),
        out_shape=jax.ShapeDtypeStruct((b, n, w), BF),
        compiler_params=_cparams(3),
        name="natten",
    )(q, k, k, k, k, v, v, v, v, kc, vc, bias)


def _ctx_atten_kernel(q_ref, kc_ref, vc_ref, o_ref):
    _attend(q_ref[...], [kc_ref[...]], [vc_ref[...]], lambda h, j: None, o_ref)


def _ctx_atten_call(qc, kc, vc):
    b, lc, w = qc.shape
    hw = NA_HQ * NA_HEAD_DIM
    spec = pl.BlockSpec((None, lc, hw), lambda bb, hq: (bb, 0, hq))
    return pl.pallas_call(
        _ctx_atten_kernel,
        grid=(b, w // hw),
        in_specs=[spec, spec, spec],
        out_specs=spec,
        out_shape=jax.ShapeDtypeStruct((b, lc, w), BF),
        compiler_params=_cparams(2),
        name="ctx_atten",
    )(qc, kc, vc)


def _na_bias(rpb, rows):
    w = GRID_W
    nj = rows // NA_QROWS
    n_dr = 2 * WIN_ROWS - 1
    n_dc = 2 * WIN_COLS - 1
    cq = np.arange(w)
    cs = np.clip(cq - WIN_COLS // 2, 0, w - WIN_COLS)
    ck = np.arange(w)
    valid_c = (ck[None, :] >= cs[:, None]) & (ck[None, :] < cs[:, None] + WIN_COLS)
    dc = ck[None, :] - cq[:, None] + (WIN_COLS - 1)
    pick_c = (dc[:, :, None] == np.arange(n_dc)) & valid_c[:, :, None]
    scaled = rpb * LOG2E
    out = []
    for j in (0, min(1, nj - 1), nj - 1):
        rq = NA_QROWS * j + np.arange(NA_QROWS)
        ws = int(np.clip(NA_QROWS * j - (NA_KROWS - NA_QROWS) // 2, 0, rows - NA_KROWS))
        rk = ws + np.arange(NA_KROWS)
        r0 = np.clip(rq - WIN_ROWS // 2, 0, rows - WIN_ROWS)
        valid_r = (rk[None, :] >= r0[:, None]) & (rk[None, :] < r0[:, None] + WIN_ROWS)
        dr = rk[None, :] - rq[:, None] + (WIN_ROWS - 1)
        pick_r = (dr[:, :, None] == np.arange(n_dr)) & valid_r[:, :, None]
        valid = valid_r[:, None, :, None] & valid_c[None, :, None, :]
        mask = jnp.asarray(np.where(valid, 0.0, NEG_BIG), F32)
        bias = jnp.einsum('hab,qka,cdb->hqckd', scaled, jnp.asarray(pick_r, F32), jnp.asarray(pick_c, F32),
                          precision=lax.Precision.HIGHEST)
        out.append((bias + mask[None]).reshape(rpb.shape[0], NA_QROWS * w, NA_KROWS * w))
    return jnp.stack(out).astype(BF)


def _rope_tables(n):
    nf = NA_HEAD_DIM // 4
    t = jnp.arange(n, dtype=jnp.int32)
    pos = jnp.stack([t // GRID_W, t % GRID_W], axis=-1).astype(F32)
    inv_freq = ROPE_THETA ** (-jnp.arange(nf, dtype=F32) / nf)
    ang = pos[:, :, None] * inv_freq
    cos = jnp.cos(ang)
    sin = jnp.sin(ang)
    cos_h = jnp.concatenate([cos, cos], axis=-1).reshape(n, NA_HEAD_DIM)
    sin_h = jnp.concatenate([-sin, sin], axis=-1).reshape(n, NA_HEAD_DIM)
    return jnp.tile(cos_h, (1, NA_HEADS)), jnp.tile(sin_h, (1, NA_HEADS))


def _merge_kernel(ys_ref, yf_ref, yn_ref, g_ref, h_ref, t_ref, wbs_ref, wbf_ref, wbn_ref, wo_ref, o_ref,
                  *, alpha):
    d = h_ref.shape[-1]
    t = t_ref[...]
    y = (g_ref[:, 0:d].astype(F32) * _dot(ys_ref[...], wbs_ref[...])
         + g_ref[:, d:2 * d].astype(F32) * _dot(yf_ref[...], wbf_ref[...])
         + g_ref[:, 2 * d:3 * d].astype(F32) * _dot(yn_ref[...], wbn_ref[...]))
    y = _dot(y.astype(BF), wo_ref[...])
    v = alpha * h_ref[...] + t[2:3] * y
    o_ref[...] = _ln(v) * t[3:4] + t[4:5]


def _merge_call(ys, yf, yn, g, h, tab, wbs, wbf, wbn, wo, alpha):
    nb, n, d = h.shape
    tm = min(ROW_TILE, n)

    def tok(width):
        return pl.BlockSpec((None, tm, width), lambda b, i: (b, i, 0))

    def full(a):
        return pl.BlockSpec(a.shape, lambda b, i: (0, 0))

    return pl.pallas_call(
        functools.partial(_merge_kernel, alpha=alpha),
        grid=(nb, n // tm),
        in_specs=[tok(ys.shape[-1]), tok(yf.shape[-1]), tok(yn.shape[-1]), tok(g.shape[-1]), tok(d),
                  pl.BlockSpec((None, 8, d), lambda b, i: (b, 0, 0)),
                  full(wbs), full(wbf), full(wbn), full(wo)],
        out_specs=tok(d),
        out_shape=jax.ShapeDtypeStruct(h.shape, F32),
        compiler_params=_cparams(2),
        name="merge",
    )(ys, yf, yn, g, h, tab, wbs, wbf, wbn, wo)


def _table(mod, d, chunks, extra):
    rows = [mod[:, i * d:(i + 1) * d] for i in chunks]
    rows += [jnp.broadcast_to(e[None, :], (8, d)) for e in extra]
    rows += [jnp.zeros((8, d), F32)] * (8 - len(rows))
    return jnp.stack(rows, axis=1)


def kernel(x, c, ctx, c_ctx, w_mod, b_mod, ln_g, ln_b, ffn_w_gate, ffn_w_up, ffn_w_down, w_in, ssm_log_dt, ssm_a_re, ssm_a_im, ssm_b_re, ssm_b_im, ssm_c_re, ssm_c_im, ssm_d, ssm_w_glu, ssm_b_glu, na_rpb, w_br_ssm, w_br_fnet, w_br_na, w_out):
    bsz, n, d = x.shape
    lc = ctx.shape[1]
    depth = w_mod.shape[0]
    rows = n // GRID_W
    assert bsz + 1 <= 8 and rows % NA_QROWS == 0 and rows >= NA_KROWS and n % FNET_N2 == 0
    alpha = (2 * depth) ** 0.25
    q_scale = NA_HEAD_DIM ** -0.5 * LOG2E
    s5_chunk = min(ROW_TILE, n)

    c8 = jnp.concatenate([c, c_ctx[None, :], jnp.zeros((8 - bsz - 1, d), F32)], axis=0)
    mod_all = _mod_call(c8, w_mod, b_mod)

    cos_t, sin_t = _rope_tables(n)
    cos_c = jnp.ones((bsz * lc, NA_WIDTH), F32)
    sin_c = jnp.zeros((bsz * lc, NA_WIDTH), F32)
    zero_state = jnp.zeros((bsz, 8, 2 * SSM_FLAT), F32)

    h = x
    hc = ctx.reshape(1, bsz * lc, d)
    for l in range(depth):
        last = l == depth - 1
        mod = mod_all[l]
        wg = ffn_w_gate[l].astype(BF)
        wu = ffn_w_up[l].astype(BF)
        wd = ffn_w_down[l].astype(BF)
        win = w_in[l].astype(BF)

        tab = _table(mod, d, (0, 1, 2), (ln_g[l, 0], ln_b[l, 0]))
        h = _ffn_call(h, tab[:bsz], wg[0], wu[0], wd[0], alpha)
        hc = _ffn_call(hc, tab[bsz:bsz + 1], wg[0], wu[0], wd[0], alpha)

        tab = _table(mod, d, (3, 4, 5), (ln_g[l, 1], ln_b[l, 1]))
        us, k, v, f, q, g = _mixin_call(h, tab[:bsz], win, cos_t, sin_t, q_scale)
        usc, kc, vc, fc, qc, gc = _mixin_call(hc, tab[bsz:bsz + 1], win, cos_c, sin_c, q_scale)
        usc, kc, vc, fc, qc = [a.reshape(bsz, lc, a.shape[-1]) for a in (usc, kc, vc, fc, qc)]

        tabs_f = _s5_tables(ssm_log_dt[l, 0], ssm_a_re[l, 0], ssm_a_im[l, 0], ssm_b_re[l, 0], ssm_b_im[l, 0],
                            ssm_c_re[l, 0], ssm_c_im[l, 0], False)
        tabs_b = _s5_tables(ssm_log_dt[l, 1], ssm_a_re[l, 1], ssm_a_im[l, 1], ssm_b_re[l, 1], ssm_b_im[l, 1],
                            ssm_c_re[l, 1], ssm_c_im[l, 1], True)
        readout_w = (ssm_d[l][None, :], ssm_w_glu[l].astype(BF), ssm_b_glu[l][None, :])
        ybc, hcb = _s5_call(usc, zero_state, *tabs_b, lc, True)
        ysc, hcf = _s5_call(usc, zero_state, *tabs_f, lc, False, (ybc,) + readout_w)
        yb, _ = _s5_call(us, hcb, *tabs_b, s5_chunk, True)
        ys, _ = _s5_call(us, hcf, *tabs_f, s5_chunk, False, (yb,) + readout_w)

        yf = _fnet_call(f)
        yn = _natten_call(q, k, v, kc, vc, _na_bias(na_rpb[l], rows))

        wbs = w_br_ssm[l].astype(BF)
        wbf = w_br_fnet[l].astype(BF)
        wbn = w_br_na[l].astype(BF)
        wo = w_out[l].astype(BF)
        h = _merge_call(ys, yf, yn, g, h, tab[:bsz], wbs, wbf, wbn, wo, alpha)

        tab3 = _table(mod, d, (6, 7, 8), (ln_g[l, 2], ln_b[l, 2]))
        h = _ffn_call(h, tab3[:bsz], wg[1], wu[1], wd[1], alpha)

        if not last:
            yfc = _fnet_dense_call(fc)
            ync = _ctx_atten_call(qc, kc, vc)
            flat = lambda a: a.reshape(1, bsz * lc, a.shape[-1])
            hc = _merge_call(flat(ysc), flat(yfc), flat(ync), gc, hc, tab[bsz:bsz + 1], wbs, wbf, wbn, wo, alpha)
            hc = _ffn_call(hc, tab3[bsz:bsz + 1], wg[1], wu[1], wd[1], alpha)
    return h
```

```python
import functools
import math

import jax
import jax.numpy as jnp
import numpy as np
from jax import lax
from jax.experimental import pallas as pl
from jax.experimental.pallas import tpu as pltpu

BF = jnp.bfloat16
F32 = jnp.float32

GRID_W = 64
SSM_GROUPS = 16
SSM_GROUP_CH = 16
SSM_STATE = 64
SSM_WIDTH = SSM_GROUPS * SSM_GROUP_CH
SSM_FLAT = SSM_GROUPS * SSM_STATE
FNET_GROUPS = 4
FNET_GROUP_CH = 64
FNET_WIDTH = FNET_GROUPS * FNET_GROUP_CH
NA_HEADS = 8
NA_HEAD_DIM = 64
NA_WIDTH = NA_HEADS * NA_HEAD_DIM
WIN_ROWS = 8
WIN_COLS = 16
ROPE_THETA = 10000.0
N_MOD = 9
LN_EPS = 1e-6
LOG2E = 1.4426950408889634
NEG_BIG = -1e30

VMEM_LIMIT_BYTES = 56 * 1024 * 1024
ROW_TILE = 512
FFN_CHUNK = 256
SCAN_ROWS = 8
S5_BLOCK = 4
S5_ROWS = 256
NA_QROWS = 8
NA_KROWS = 16
NA_HQ = 4
FNET_N2 = 128


def _cparams(n_axes):
    return pltpu.CompilerParams(dimension_semantics=("arbitrary",) * n_axes,
                                vmem_limit_bytes=VMEM_LIMIT_BYTES)


def _ln(x):
    mu = jnp.mean(x, axis=-1, keepdims=True)
    xc = x - mu
    var = jnp.mean(xc * xc, axis=-1, keepdims=True)
    return xc * lax.rsqrt(var + LN_EPS)


def _dot(a, b):
    return jnp.dot(a, b, preferred_element_type=F32)


def _dot_t(a, b):
    return lax.dot_general(a, b, (((1,), (1,)), ((), ())), preferred_element_type=F32)


def _mod_kernel(c_ref, w_ref, b_ref, o_ref):
    c = c_ref[...]
    s = (c * jax.nn.sigmoid(c)).astype(BF)
    o_ref[...] = _dot(s, w_ref[...].astype(BF)) + b_ref[...]


def _mod_call(c8, w_mod, b_mod):
    depth, d, nd = w_mod.shape
    nb = nd // d
    return pl.pallas_call(
        _mod_kernel,
        grid=(depth, nb),
        in_specs=[pl.BlockSpec((8, d), lambda l, j: (0, 0)),
                  pl.BlockSpec((None, d, d), lambda l, j: (l, 0, j)),
                  pl.BlockSpec((None, 1, d), lambda l, j: (l, 0, j))],
        out_specs=pl.BlockSpec((None, 8, d), lambda l, j: (l, 0, j)),
        out_shape=jax.ShapeDtypeStruct((depth, 8, nd), F32),
        compiler_params=_cparams(2),
        name="mod",
    )(c8, w_mod, b_mod.reshape(depth, 1, nd))


def _ffn_kernel(x_ref, t_ref, wg_ref, wu_ref, wd_ref, o_ref, *, alpha):
    x = x_ref[...]
    t = t_ref[...]
    u = (_ln(x) * (1.0 + t[1:2]) + t[0:1]).astype(BF)
    acc = jnp.zeros(x.shape, F32)
    for c0 in range(0, wg_ref.shape[1], FFN_CHUNK):
        a = _dot(u, wg_ref[:, c0:c0 + FFN_CHUNK])
        b = _dot(u, wu_ref[:, c0:c0 + FFN_CHUNK])
        act = (a * jax.nn.sigmoid(a) * b).astype(BF)
        acc = acc + _dot(act, wd_ref[c0:c0 + FFN_CHUNK, :])
    v = alpha * x + (0.5 * t[2:3]) * acc
    o_ref[...] = _ln(v) * t[3:4] + t[4:5]


def _ffn_call(x, tab, wg, wu, wd, alpha):
    nb, n, d = x.shape
    dff = wg.shape[1]
    tm = min(ROW_TILE, n)
    const = lambda b, i: (0, 0)
    return pl.pallas_call(
        functools.partial(_ffn_kernel, alpha=alpha),
        grid=(nb, n // tm),
        in_specs=[pl.BlockSpec((None, tm, d), lambda b, i: (b, i, 0)),
                  pl.BlockSpec((None, 8, d), lambda b, i: (b, 0, 0)),
                  pl.BlockSpec((d, dff), const),
                  pl.BlockSpec((d, dff), const),
                  pl.BlockSpec((dff, d), const)],
        out_specs=pl.BlockSpec((None, tm, d), lambda b, i: (b, i, 0)),
        out_shape=jax.ShapeDtypeStruct(x.shape, F32),
        compiler_params=_cparams(2),
        name="ffn",
    )(x, tab, wg, wu, wd)


def _rope(x, cos, sin_signed):
    n = x.shape[-1]
    lane = lax.broadcasted_iota(jnp.int32, x.shape, 1)
    first = (lane % 32) < 16
    partner = jnp.where(first, pltpu.roll(x, n - 16, axis=1), pltpu.roll(x, 16, axis=1))
    return x * cos + partner * sin_signed


def _mixin_kernel(x_ref, t_ref, w_ref, cos_ref, sin_ref,
                  us_ref, k_ref, v_ref, f_ref, q_ref, g_ref, *, q_scale):
    x = x_ref[...]
    t = t_ref[...]
    u = (_ln(x) * (1.0 + t[1:2]) + t[0:1]).astype(BF)
    cos = cos_ref[...]
    sin = sin_ref[...]
    c_k = SSM_WIDTH
    c_v = c_k + NA_WIDTH
    c_f = c_v + NA_WIDTH
    c_q = c_f + FNET_WIDTH
    c_g = c_q + NA_WIDTH
    us_ref[...] = _dot(u, w_ref[:, 0:c_k]).astype(BF)
    k_ref[...] = _rope(_dot(u, w_ref[:, c_k:c_v]), cos, sin).astype(BF)
    v_ref[...] = _dot(u, w_ref[:, c_v:c_f]).astype(BF)
    f_ref[...] = _dot(u, w_ref[:, c_f:c_q]).astype(BF)
    q_ref[...] = (_rope(_dot(u, w_ref[:, c_q:c_g]), cos, sin) * q_scale).astype(BF)
    gw = NA_WIDTH
    for c0 in range(0, g_ref.shape[-1], gw):
        g = _dot(u, w_ref[:, c_g + c0:c_g + c0 + gw])
        g_ref[:, c0:c0 + gw] = jax.nn.sigmoid(g).astype(BF)


def _mixin_call(x, tab, w_in, cos_t, sin_t, q_scale):
    nb, n, d = x.shape
    win = w_in.shape[1]
    ng = win - (SSM_WIDTH + 3 * NA_WIDTH + FNET_WIDTH)
    tm = min(ROW_TILE, n)
    nt_pos = cos_t.shape[0] // tm

    def tok(width):
        return pl.BlockSpec((None, tm, width), lambda b, i: (b, i, 0))

    def shape(width):
        return jax.ShapeDtypeStruct((nb, n, width), BF)

    return pl.pallas_call(
        functools.partial(_mixin_kernel, q_scale=q_scale),
        grid=(nb, n // tm),
        in_specs=[tok(d),
                  pl.BlockSpec((None, 8, d), lambda b, i: (b, 0, 0)),
                  pl.BlockSpec((d, win), lambda b, i: (0, 0)),
                  pl.BlockSpec((tm, NA_WIDTH), lambda b, i: (i % nt_pos, 0)),
                  pl.BlockSpec((tm, NA_WIDTH), lambda b, i: (i % nt_pos, 0))],
        out_specs=[tok(SSM_WIDTH), tok(NA_WIDTH), tok(NA_WIDTH), tok(FNET_WIDTH), tok(NA_WIDTH), tok(ng)],
        out_shape=[shape(SSM_WIDTH), shape(NA_WIDTH), shape(NA_WIDTH), shape(FNET_WIDTH),
                   shape(NA_WIDTH), shape(ng)],
        compiler_params=_cparams(2),
        name="mixin",
    )(x, tab, w_in, cos_t, sin_t)


def _s5_kernel(u_ref, h0_ref, wst_ref, wconv_ref, wout_ref, sc_ref, y_ref, hfin_ref, hbuf, hprev, carry,
               *, reverse):
    i = pl.program_id(1)
    ns = SSM_FLAT
    n_tiles = hbuf.shape[0] // SCAN_ROWS

    @pl.when(i == 0)
    def _():
        carry[...] = h0_ref[...]

    u = u_ref[...]
    hbuf[...] = _dot(u, wst_ref[...])
    row_id = lax.broadcasted_iota(jnp.int32, (SCAN_ROWS, ns), 0)
    edge = row_id == ((SCAN_ROWS - 1) if reverse else 0)

    def tile(it, c):
        cr, ci = c
        row = (n_tiles - 1 - it) if reverse else it
        r0 = pl.multiple_of(row * SCAN_ROWS, SCAN_ROWS)
        xr = hbuf[pl.ds(r0, SCAN_ROWS), 0:ns]
        xi = hbuf[pl.ds(r0, SCAN_ROWS), ns:2 * ns]
        for k, s in enumerate((1, 2, 4)):
            ar = sc_ref[k, :, 0:ns]
            ai = sc_ref[k, :, ns:2 * ns]
            shift = (SCAN_ROWS - s) if reverse else s
            sr = pltpu.roll(xr, shift, axis=0)
            si = pltpu.roll(xi, shift, axis=0)
            xr, xi = xr + ar * sr - ai * si, xi + ar * si + ai * sr
        pr = sc_ref[3, :, 0:ns]
        pi = sc_ref[3, :, ns:2 * ns]
        hr = xr + pr * cr - pi * ci
        hi = xi + pr * ci + pi * cr
        one = (SCAN_ROWS - 1) if reverse else 1
        hprev[pl.ds(r0, SCAN_ROWS), 0:ns] = jnp.where(edge, cr, pltpu.roll(hr, one, axis=0))
        hprev[pl.ds(r0, SCAN_ROWS), ns:2 * ns] = jnp.where(edge, ci, pltpu.roll(hi, one, axis=0))
        last = 0 if reverse else SCAN_ROWS - 1
        return (jnp.broadcast_to(hr[last:last + 1, :], hr.shape),
                jnp.broadcast_to(hi[last:last + 1, :], hi.shape))

    cr, ci = lax.fori_loop(0, n_tiles, tile, (carry[:, 0:ns], carry[:, ns:2 * ns]))
    carry[:, 0:ns] = cr
    carry[:, ns:2 * ns] = ci
    hfin_ref[...] = carry[...]

    y = _dot(u, wconv_ref[...]) + _dot(hprev[...].astype(BF), wout_ref[...])
    y_ref[...] = y.astype(y_ref.dtype)


def _s5_call(u4, h0, wst, wconv, wout, sc, rows, reverse):
    b, n, w = u4.shape
    nc = n // rows
    ns2 = 2 * SSM_FLAT
    if reverse:
        cmap = lambda bb, i: (bb, nc - 1 - i, 0)
    else:
        cmap = lambda bb, i: (bb, i, 0)
    const2 = lambda bb, i: (0, 0)
    return pl.pallas_call(
        functools.partial(_s5_kernel, reverse=reverse),
        grid=(b, nc),
        in_specs=[pl.BlockSpec((None, rows, w), cmap),
                  pl.BlockSpec((None, 8, ns2), lambda bb, i: (bb, 0, 0)),
                  pl.BlockSpec((w, ns2), const2),
                  pl.BlockSpec((w, w), const2),
                  pl.BlockSpec((ns2, w), const2),
                  pl.BlockSpec((4, 8, ns2), lambda bb, i: (0, 0, 0))],
        out_specs=[pl.BlockSpec((None, rows, w), cmap),
                   pl.BlockSpec((None, 8, ns2), lambda bb, i: (bb, 0, 0))],
        out_shape=[jax.ShapeDtypeStruct((b, n, w), BF),
                   jax.ShapeDtypeStruct((b, 8, ns2), F32)],
        scratch_shapes=[pltpu.VMEM((rows, ns2), F32), pltpu.VMEM((rows, ns2), F32), pltpu.VMEM((8, ns2), F32)],
        compiler_params=_cparams(2),
        name="s5_bwd" if reverse else "s5_fwd",
    )(u4, h0, wst, wconv, wout, sc)


def _zoh(log_dt, a_re, a_im, b_re, b_im):
    dt = jnp.exp(log_dt.astype(F32))[:, None]
    mag = jnp.exp(a_re * dt)
    ab_re = mag * jnp.cos(a_im * dt)
    ab_im = mag * jnp.sin(a_im * dt)
    den = a_re * a_re + a_im * a_im
    nr = ab_re - 1.0
    ni = ab_im
    fr = (nr * a_re + ni * a_im) / den
    fi = (ni * a_re - nr * a_im) / den
    bb_re = fr[..., None] * b_re - fi[..., None] * b_im
    bb_im = fr[..., None] * b_im + fi[..., None] * b_re
    return ab_re, ab_im, bb_re, bb_im


def _s5_tables(log_dt, a_re, a_im, b_re, b_im, c_re, c_im, reverse):
    g, p, c = b_re.shape
    tb = S5_BLOCK
    hp = lax.Precision.HIGHEST
    _, _, bb_re, bb_im = _zoh(log_dt, a_re, a_im, b_re, b_im)
    dt = jnp.exp(log_dt.astype(F32))[:, None]
    eye = jnp.eye(g, dtype=F32)

    def apow(k):
        mag = jnp.exp(k * a_re * dt)
        return mag * jnp.cos(k * a_im * dt), mag * jnp.sin(k * a_im * dt)

    def chan_to_state(m):
        return (jnp.transpose(m, (0, 2, 1))[:, :, None, :] * eye[:, None, :, None]).reshape(g * c, g * p)

    def state_to_chan(m):
        return (jnp.transpose(m, (0, 2, 1))[:, :, None, :] * eye[:, None, :, None]).reshape(g * p, g * c)

    def chan_to_chan(m):
        return (m[:, :, None, :] * eye[:, None, :, None]).reshape(g * c, g * c)

    wst = []
    for i in range(tb):
        pr, pi = apow(i if reverse else tb - 1 - i)
        wr = pr[..., None] * bb_re - pi[..., None] * bb_im
        wi = pr[..., None] * bb_im + pi[..., None] * bb_re
        wst.append(jnp.concatenate([chan_to_state(wr), chan_to_state(wi)], axis=1))
    wst = jnp.concatenate(wst, axis=0)

    wout = []
    for j in range(tb):
        pr, pi = apow(tb - j if reverse else j + 1)
        cm_re = c_re * pr[:, None, :] - c_im * pi[:, None, :]
        cm_im = c_re * pi[:, None, :] + c_im * pr[:, None, :]
        wout.append(jnp.concatenate([state_to_chan(cm_re), state_to_chan(-cm_im)], axis=0))
    wout = jnp.concatenate(wout, axis=1)

    taps = []
    for tau in range(tb):
        pr, pi = apow(tau)
        abr = pr[..., None] * bb_re - pi[..., None] * bb_im
        abi = pr[..., None] * bb_im + pi[..., None] * bb_re
        kt = (jnp.einsum('gcp,gpd->gdc', c_re, abr, precision=hp)
              - jnp.einsum('gcp,gpd->gdc', c_im, abi, precision=hp))
        taps.append(chan_to_chan(kt))
    zero = jnp.zeros_like(taps[0])
    wconv = jnp.concatenate(
        [jnp.concatenate([taps[abs(j - i)] if ((i >= j) if reverse else (j >= i)) else zero
                          for j in range(tb)], axis=1) for i in range(tb)], axis=0)

    pw = [tuple(t.reshape(-1) for t in apow(tb * (k + 1))) for k in range(SCAN_ROWS)]
    rows = np.arange(SCAN_ROWS)
    tabs = []
    for s in (1, 2, 4):
        keep = (rows + s <= SCAN_ROWS - 1) if reverse else (rows >= s)
        keep = jnp.asarray(keep, F32)[:, None]
        tabs.append(jnp.concatenate([keep * pw[s - 1][0][None], keep * pw[s - 1][1][None]], axis=1))
    order = [SCAN_ROWS - 1 - r for r in rows] if reverse else list(rows)
    tabs.append(jnp.concatenate([jnp.stack([pw[k][0] for k in order]),
                                 jnp.stack([pw[k][1] for k in order])], axis=1))
    return wst.astype(BF), wconv.astype(BF), wout.astype(BF), jnp.stack(tabs).astype(F32)


def _fnet1_kernel(x_ref, f2_ref, tc_ref, ts_ref, o_ref):
    n2 = x_ref.shape[0]
    y = _dot(f2_ref[...], x_ref[...])
    br = y[0:n2]
    bi = y[n2:2 * n2]
    tc = tc_ref[...]
    ts = ts_ref[...]
    o_ref[0] = (br * tc + bi * ts).astype(BF)
    o_ref[1] = (bi * tc - br * ts).astype(BF)


def _fnet2_kernel(b_ref, f1_ref, cc_ref, sc_ref, o_ref, *, scale):
    kb = b_ref.shape[1]
    n1 = b_ref.shape[2]
    c = b_ref.shape[3]
    zr, zi = [], []
    for j in range(kb):
        rhs = jnp.concatenate([b_ref[0, j], b_ref[1, j]], axis=0)
        z = _dot(f1_ref[...], rhs)
        zr.append(z[0:n1])
        zi.append(z[n1:2 * n1])
    zr = jnp.concatenate(zr, axis=0).astype(BF)
    zi = jnp.concatenate(zi, axis=0).astype(BF)
    y = (_dot(zr, cc_ref[...]) + _dot(zi, sc_ref[...])) * scale
    for j in range(kb):
        o_ref[:, j * c:(j + 1) * c] = y[j * n1:(j + 1) * n1].astype(o_ref.dtype)


def _dft_tables(n):
    idx = np.arange(n)
    ang = 2.0 * np.pi * ((idx[:, None] * idx[None, :]) % n) / n
    return np.cos(ang), np.sin(ang)


def _chan_tables():
    cc, sc = _dft_tables(FNET_GROUP_CH)
    eye = np.eye(FNET_GROUPS)
    return jnp.asarray(np.kron(eye, cc), BF), jnp.asarray(np.kron(eye, sc), BF)


def _fnet_call(f):
    b, n, c = f.shape
    n2 = FNET_N2
    n1 = n // n2
    c2, s2 = _dft_tables(n2)
    f2 = jnp.asarray(np.concatenate([c2, -s2], axis=0), BF)
    k2 = np.arange(n2)[:, None]
    nn1 = np.arange(n1)[None, :]
    ang = 2.0 * np.pi * ((k2 * nn1) % n) / n
    tc = jnp.repeat(jnp.asarray(np.cos(ang), F32), c, axis=1)
    ts = jnp.repeat(jnp.asarray(np.sin(ang), F32), c, axis=1)
    c1, s1 = _dft_tables(n1)
    f1 = jnp.asarray(np.block([[c1, s1], [-s1, c1]]), BF)
    cc, sc = _chan_tables()

    wc = min(n1 * c, 4096)
    x2 = f.reshape(b, n2, n1 * c)
    st1 = pl.pallas_call(
        _fnet1_kernel,
        grid=(b, (n1 * c) // wc),
        in_specs=[pl.BlockSpec((None, n2, wc), lambda bb, j: (bb, 0, j)),
                  pl.BlockSpec((2 * n2, n2), lambda bb, j: (0, 0)),
                  pl.BlockSpec((n2, wc), lambda bb, j: (0, j)),
                  pl.BlockSpec((n2, wc), lambda bb, j: (0, j))],
        out_specs=pl.BlockSpec((None, 2, n2, wc), lambda bb, j: (bb, 0, 0, j)),
        out_shape=jax.ShapeDtypeStruct((b, 2, n2, n1 * c), BF),
        compiler_params=_cparams(2),
        name="fnet1",
    )(x2, f2, tc, ts)
    st1 = st1.reshape(b, 2, n2, n1, c)
    kb = 8
    y = pl.pallas_call(
        functools.partial(_fnet2_kernel, scale=1.0 / math.sqrt(n * FNET_GROUP_CH)),
        grid=(b, n2 // kb),
        in_specs=[pl.BlockSpec((None, 2, kb, n1, c), lambda bb, j: (bb, 0, j, 0, 0)),
                  pl.BlockSpec((2 * n1, 2 * n1), lambda bb, j: (0, 0)),
                  pl.BlockSpec((c, c), lambda bb, j: (0, 0)),
                  pl.BlockSpec((c, c), lambda bb, j: (0, 0))],
        out_specs=pl.BlockSpec((None, n1, kb * c), lambda bb, j: (bb, 0, j)),
        out_shape=jax.ShapeDtypeStruct((b, n1, n2 * c), BF),
        compiler_params=_cparams(2),
        name="fnet2",
    )(st1, f1, cc, sc)
    return y.reshape(b, n, c)


def _fnet_dense_kernel(x_ref, cl_ref, sl_ref, cc_ref, sc_ref, o_ref, *, scale):
    x = x_ref[...]
    xr = _dot(x, cc_ref[...]).astype(BF)
    xi = _dot(x, sc_ref[...]).astype(BF)
    o_ref[...] = ((_dot(cl_ref[...], xr) - _dot(sl_ref[...], xi)) * scale).astype(o_ref.dtype)


def _fnet_dense_call(f):
    b, n, c = f.shape
    cl, sl = _dft_tables(n)
    cc, sc = _chan_tables()
    full = lambda bb: (0, 0)
    return pl.pallas_call(
        functools.partial(_fnet_dense_kernel, scale=1.0 / math.sqrt(n * FNET_GROUP_CH)),
        grid=(b,),
        in_specs=[pl.BlockSpec((None, n, c), lambda bb: (bb, 0, 0)),
                  pl.BlockSpec((n, n), full), pl.BlockSpec((n, n), full),
                  pl.BlockSpec((c, c), full), pl.BlockSpec((c, c), full)],
        out_specs=pl.BlockSpec((None, n, c), lambda bb: (bb, 0, 0)),
        out_shape=jax.ShapeDtypeStruct((b, n, c), BF),
        compiler_params=_cparams(1),
        name="fnet_ctx",
    )(f, jnp.asarray(cl, BF), jnp.asarray(sl, BF), cc, sc)


def _head_masks(width):
    lane = lax.broadcasted_iota(jnp.int32, (1, width), 1)
    return [(lane // NA_HEAD_DIM) == h for h in range(width // NA_HEAD_DIM)]


def _attend(q, k_parts, v_parts, bias_of, o_ref):
    zero = jnp.zeros((), BF)
    out = jnp.zeros(q.shape, F32)
    for h, hm in enumerate(_head_masks(q.shape[-1])):
        s_parts = []
        for j, kp in enumerate(k_parts):
            s = _dot_t(q, jnp.where(hm, kp, zero))
            b = bias_of(h, j)
            s_parts.append(s if b is None else s + b)
        m = s_parts[0].max(axis=-1, keepdims=True)
        for s in s_parts[1:]:
            m = jnp.maximum(m, s.max(axis=-1, keepdims=True))
        den = jnp.zeros_like(m)
        acc = jnp.zeros(q.shape, F32)
        for s, vp in zip(s_parts, v_parts):
            p = jnp.exp2(s - m)
            den = den + p.sum(axis=-1, keepdims=True)
            acc = acc + _dot(p.astype(BF), jnp.where(hm, vp, zero))
        out = out + acc / den
    o_ref[...] = out.astype(o_ref.dtype)


def _natten_kernel(q_ref, k0, k1, k2, k3, v0, v1, v2, v3, kc_ref, vc_ref, bias_ref, o_ref):
    blk = k0.shape[0]
    k_parts = [r[...] for r in (k0, k1, k2, k3)] + [kc_ref[...]]
    v_parts = [r[...] for r in (v0, v1, v2, v3)] + [vc_ref[...]]

    def bias_of(h, j):
        if j >= 4:
            return None
        return bias_ref[h, :, j * blk:(j + 1) * blk]

    _attend(q_ref[...], k_parts, v_parts, bias_of, o_ref)


def _natten_call(q, k, v, kc, vc, bias):
    b, n, w = q.shape
    lc = kc.shape[1]
    hw = NA_HQ * NA_HEAD_DIM
    qt = NA_QROWS * GRID_W
    blk = (NA_KROWS // 4) * GRID_W
    nj = n // qt
    nblk = n // blk
    ratio = qt // blk

    def kv_spec(m):
        def imap(hq, j, bb):
            start = jnp.clip(ratio * j - 1, 0, nblk - 4)
            return (bb, start + m, hq)
        return pl.BlockSpec((None, blk, hw), imap)

    def bias_map(hq, j, bb):
        var = jnp.where(j == 0, 0, jnp.where(j == nj - 1, 2, 1))
        return (var, hq, 0, 0)

    return pl.pallas_call(
        _natten_kernel,
        grid=(w // hw, nj, b),
        in_specs=[pl.BlockSpec((None, qt, hw), lambda hq, j, bb: (bb, j, hq))]
                 + [kv_spec(m) for m in range(4)] + [kv_spec(m) for m in range(4)]
                 + [pl.BlockSpec((None, lc, hw), lambda hq, j, bb: (bb, 0, hq)),
                    pl.BlockSpec((None, lc, hw), lambda hq, j, bb: (bb, 0, hq)),
                    pl.BlockSpec((None, NA_HQ, qt, 4 * blk), bias_map)],
        out_specs=pl.BlockSpec((None, qt, hw), lambda hq, j, bb: (bb, j, hq)),
        out_shape=jax.ShapeDtypeStruct((b, n, w), BF),
        compiler_params=_cparams(3),
        name="natten",
    )(q, k, k, k, k, v, v, v, v, kc, vc, bias)


def _ctx_atten_kernel(q_ref, kc_ref, vc_ref, o_ref):
    _attend(q_ref[...], [kc_ref[...]], [vc_ref[...]], lambda h, j: None, o_ref)


def _ctx_atten_call(qc, kc, vc):
    b, lc, w = qc.shape
    hw = NA_HQ * NA_HEAD_DIM
    spec = pl.BlockSpec((None, lc, hw), lambda bb, hq: (bb, 0, hq))
    return pl.pallas_call(
        _ctx_atten_kernel,
        grid=(b, w // hw),
        in_specs=[spec, spec, spec],
        out_specs=spec,
        out_shape=jax.ShapeDtypeStruct((b, lc, w), BF),
        compiler_params=_cparams(2),
        name="ctx_atten",
    )(qc, kc, vc)


def _na_bias(rpb, rows):
    w = GRID_W
    nj = rows // NA_QROWS
    n_dr = 2 * WIN_ROWS - 1
    n_dc = 2 * WIN_COLS - 1
    cq = np.arange(w)
    cs = np.clip(cq - WIN_COLS // 2, 0, w - WIN_COLS)
    ck = np.arange(w)
    valid_c = (ck[None, :] >= cs[:, None]) & (ck[None, :] < cs[:, None] + WIN_COLS)
    dc = ck[None, :] - cq[:, None] + (WIN_COLS - 1)
    pick_c = (dc[:, :, None] == np.arange(n_dc)) & valid_c[:, :, None]
    scaled = rpb * LOG2E
    out = []
    for j in (0, min(1, nj - 1), nj - 1):
        rq = NA_QROWS * j + np.arange(NA_QROWS)
        ws = int(np.clip(NA_QROWS * j - (NA_KROWS - NA_QROWS) // 2, 0, rows - NA_KROWS))
        rk = ws + np.arange(NA_KROWS)
        r0 = np.clip(rq - WIN_ROWS // 2, 0, rows - WIN_ROWS)
        valid_r = (rk[None, :] >= r0[:, None]) & (rk[None, :] < r0[:, None] + WIN_ROWS)
        dr = rk[None, :] - rq[:, None] + (WIN_ROWS - 1)
        pick_r = (dr[:, :, None] == np.arange(n_dr)) & valid_r[:, :, None]
        valid = valid_r[:, None, :, None] & valid_c[None, :, None, :]
        mask = jnp.asarray(np.where(valid, 0.0, NEG_BIG), F32)
        bias = jnp.einsum('hab,qka,cdb->hqckd', scaled, jnp.asarray(pick_r, F32), jnp.asarray(pick_c, F32),
                          precision=lax.Precision.HIGHEST)
        out.append((bias + mask[None]).reshape(rpb.shape[0], NA_QROWS * w, NA_KROWS * w))
    return jnp.stack(out)


def _rope_tables(n):
    nf = NA_HEAD_DIM // 4
    t = jnp.arange(n, dtype=jnp.int32)
    pos = jnp.stack([t // GRID_W, t % GRID_W], axis=-1).astype(F32)
    inv_freq = ROPE_THETA ** (-jnp.arange(nf, dtype=F32) / nf)
    ang = pos[:, :, None] * inv_freq
    cos = jnp.cos(ang)
    sin = jnp.sin(ang)
    cos_h = jnp.concatenate([cos, cos], axis=-1).reshape(n, NA_HEAD_DIM)
    sin_h = jnp.concatenate([-sin, sin], axis=-1).reshape(n, NA_HEAD_DIM)
    return jnp.tile(cos_h, (1, NA_HEADS)), jnp.tile(sin_h, (1, NA_HEADS))


def _merge_kernel(ysf_ref, ysb_ref, us_ref, yf_ref, yn_ref, g_ref, h_ref, t_ref,
                  sd_ref, wglu_ref, bglu_ref, wbs_ref, wbf_ref, wbn_ref, wo_ref, o_ref, *, alpha):
    d = h_ref.shape[-1]
    t = t_ref[...]
    ys = ysf_ref[...].astype(F32) + ysb_ref[...].astype(F32) + sd_ref[...] * us_ref[...].astype(F32)
    ys = jax.nn.gelu(ys)
    ys = (ys * jax.nn.sigmoid(_dot(ys.astype(BF), wglu_ref[...]) + bglu_ref[...])).astype(BF)
    y = (g_ref[:, 0:d].astype(F32) * _dot(ys, wbs_ref[...])
         + g_ref[:, d:2 * d].astype(F32) * _dot(yf_ref[...], wbf_ref[...])
         + g_ref[:, 2 * d:3 * d].astype(F32) * _dot(yn_ref[...], wbn_ref[...]))
    y = _dot(y.astype(BF), wo_ref[...])
    v = alpha * h_ref[...] + t[2:3] * y
    o_ref[...] = _ln(v) * t[3:4] + t[4:5]


def _merge_call(ysf, ysb, us, yf, yn, g, h, tab, s5w, wbs, wbf, wbn, wo, alpha):
    nb, n, d = h.shape
    tm = min(ROW_TILE, n)

    def tok(a):
        return pl.BlockSpec((None, tm, a.shape[-1]), lambda b, i: (b, i, 0))

    def full(a):
        return pl.BlockSpec(a.shape, lambda b, i: (0, 0))

    toks = (ysf, ysb, us, yf, yn, g, h)
    consts = tuple(s5w) + (wbs, wbf, wbn, wo)
    return pl.pallas_call(
        functools.partial(_merge_kernel, alpha=alpha),
        grid=(nb, n // tm),
        in_specs=[tok(a) for a in toks] + [pl.BlockSpec((None, 8, d), lambda b, i: (b, 0, 0))]
                 + [full(a) for a in consts],
        out_specs=tok(h),
        out_shape=jax.ShapeDtypeStruct(h.shape, F32),
        compiler_params=_cparams(2),
        name="merge",
    )(*toks, tab, *consts)


def _table(mod, d, chunks, extra):
    rows = [mod[:, i * d:(i + 1) * d] for i in chunks]
    rows += [jnp.broadcast_to(e[None, :], (8, d)) for e in extra]
    rows += [jnp.zeros((8, d), F32)] * (8 - len(rows))
    return jnp.stack(rows, axis=1)


def kernel(x, c, ctx, c_ctx, w_mod, b_mod, ln_g, ln_b, ffn_w_gate, ffn_w_up, ffn_w_down, w_in, ssm_log_dt, ssm_a_re, ssm_a_im, ssm_b_re, ssm_b_im, ssm_c_re, ssm_c_im, ssm_d, ssm_w_glu, ssm_b_glu, na_rpb, w_br_ssm, w_br_fnet, w_br_na, w_out):
    bsz, n, d = x.shape
    lc = ctx.shape[1]
    depth = w_mod.shape[0]
    rows = n // GRID_W
    assert bsz + 1 <= 8 and rows % NA_QROWS == 0 and rows >= NA_KROWS and n % FNET_N2 == 0
    alpha = (2 * depth) ** 0.25
    q_scale = NA_HEAD_DIM ** -0.5 * LOG2E

    c8 = jnp.concatenate([c, c_ctx[None, :], jnp.zeros((8 - bsz - 1, d), F32)], axis=0)
    mod_all = _mod_call(c8, w_mod, b_mod)

    cos_t, sin_t = _rope_tables(n)
    cos_c = jnp.ones((bsz * lc, NA_WIDTH), F32)
    sin_c = jnp.zeros((bsz * lc, NA_WIDTH), F32)
    zero_state = jnp.zeros((bsz, 8, 2 * SSM_FLAT), F32)

    h = x
    hc = ctx.reshape(1, bsz * lc, d)
    for l in range(depth):
        last = l == depth - 1
        mod = mod_all[l]
        wg = ffn_w_gate[l].astype(BF)
        wu = ffn_w_up[l].astype(BF)
        wd = ffn_w_down[l].astype(BF)
        win = w_in[l].astype(BF)

        tab = _table(mod, d, (0, 1, 2), (ln_g[l, 0], ln_b[l, 0]))
        h = _ffn_call(h, tab[:bsz], wg[0], wu[0], wd[0], alpha)
        hc = _ffn_call(hc, tab[bsz:bsz + 1], wg[0], wu[0], wd[0], alpha)

        tab = _table(mod, d, (3, 4, 5), (ln_g[l, 1], ln_b[l, 1]))
        us, k, v, f, q, g = _mixin_call(h, tab[:bsz], win, cos_t, sin_t, q_scale)
        usc, kc, vc, fc, qc, gc = _mixin_call(hc, tab[bsz:bsz + 1], win, cos_c, sin_c, q_scale)
        usc, kc, vc, fc, qc = [a.reshape(bsz, lc, a.shape[-1]) for a in (usc, kc, vc, fc, qc)]

        tabs_f = _s5_tables(ssm_log_dt[l, 0], ssm_a_re[l, 0], ssm_a_im[l, 0], ssm_b_re[l, 0], ssm_b_im[l, 0],
                            ssm_c_re[l, 0], ssm_c_im[l, 0], False)
        tabs_b = _s5_tables(ssm_log_dt[l, 1], ssm_a_re[l, 1], ssm_a_im[l, 1], ssm_b_re[l, 1], ssm_b_im[l, 1],
                            ssm_c_re[l, 1], ssm_c_im[l, 1], True)
        s5w = (ssm_d[l][None, :], ssm_w_glu[l].astype(BF), ssm_b_glu[l][None, :])
        blk = lambda a: a.reshape(a.shape[0], a.shape[1] // S5_BLOCK, S5_BLOCK * SSM_WIDTH)
        unblk = lambda a: a.reshape(a.shape[0], a.shape[1] * S5_BLOCK, SSM_WIDTH)
        u4, u4c = blk(us), blk(usc)
        ysbc, hcb = _s5_call(u4c, zero_state, *tabs_b, u4c.shape[1], True)
        ysfc, hcf = _s5_call(u4c, zero_state, *tabs_f, u4c.shape[1], False)
        ysb, _ = _s5_call(u4, hcb, *tabs_b, min(S5_ROWS, u4.shape[1]), True)
        ysf, _ = _s5_call(u4, hcf, *tabs_f, min(S5_ROWS, u4.shape[1]), False)
        ysf, ysb = unblk(ysf), unblk(ysb)

        yf = _fnet_call(f)
        yn = _natten_call(q, k, v, kc, vc, _na_bias(na_rpb[l], rows))

        wbs = w_br_ssm[l].astype(BF)
        wbf = w_br_fnet[l].astype(BF)
        wbn = w_br_na[l].astype(BF)
        wo = w_out[l].astype(BF)
        h = _merge_call(ysf, ysb, us, yf, yn, g, h, tab[:bsz], s5w, wbs, wbf, wbn, wo, alpha)

        tab3 = _table(mod, d, (6, 7, 8), (ln_g[l, 2], ln_b[l, 2]))
        h = _ffn_call(h, tab3[:bsz], wg[1], wu[1], wd[1], alpha)

        if not last:
            yfc = _fnet_dense_call(fc)
            ync = _ctx_atten_call(qc, kc, vc)
            flat = lambda a: a.reshape(1, bsz * lc, a.shape[-1])
            hc = _merge_call(flat(unblk(ysfc)), flat(unblk(ysbc)), flat(usc), flat(yfc), flat(ync), gc, hc,
                             tab[bsz:bsz + 1], s5w, wbs, wbf, wbn, wo, alpha)
            hc = _ffn_call(hc, tab3[bsz:bsz + 1], wg[1], wu[1], wd[1], alpha)
    return h
```

```python
import functools
import math

import jax
import jax.numpy as jnp
import numpy as np
from jax import lax
from jax.experimental import pallas as pl
from jax.experimental.pallas import tpu as pltpu

BF = jnp.bfloat16
F32 = jnp.float32

GRID_W = 64
SSM_GROUPS = 16
SSM_GROUP_CH = 16
SSM_STATE = 64
SSM_WIDTH = SSM_GROUPS * SSM_GROUP_CH
SSM_FLAT = SSM_GROUPS * SSM_STATE
FNET_GROUPS = 4
FNET_GROUP_CH = 64
FNET_WIDTH = FNET_GROUPS * FNET_GROUP_CH
NA_HEADS = 8
NA_HEAD_DIM = 64
NA_WIDTH = NA_HEADS * NA_HEAD_DIM
WIN_ROWS = 8
WIN_COLS = 16
ROPE_THETA = 10000.0
N_MOD = 9
LN_EPS = 1e-6
LOG2E = 1.4426950408889634
NEG_BIG = -1e30

VMEM_LIMIT_BYTES = 56 * 1024 * 1024
ROW_TILE = 512
FFN_CHUNK = 256
LANES = 128
SCAN_ROWS = 8
S5_BLOCK = 4
S5_ROWS = 256
NA_QROWS = 8
NA_KROWS = 16
NA_HQ = 4
FNET_N2 = 128


def _cparams(n_axes):
    return pltpu.CompilerParams(dimension_semantics=("arbitrary",) * n_axes,
                                vmem_limit_bytes=VMEM_LIMIT_BYTES)


def _ln(x):
    mu = jnp.mean(x, axis=-1, keepdims=True)
    xc = x - mu
    var = jnp.mean(xc * xc, axis=-1, keepdims=True)
    return xc * lax.rsqrt(var + LN_EPS)


def _dot(a, b):
    return jnp.dot(a, b, preferred_element_type=F32)


def _dot_t(a, b):
    return lax.dot_general(a, b, (((1,), (1,)), ((), ())), preferred_element_type=F32)


def _layer_spec(a, *lead):
    nl = len(lead)
    zeros = (0,) * (a.ndim - nl)
    return pl.BlockSpec((None,) * nl + tuple(a.shape[nl:]), lambda *_: tuple(lead) + zeros)


def _mod_kernel(c_ref, w_ref, b_ref, o_ref):
    c = c_ref[...]
    s = (c * jax.nn.sigmoid(c)).astype(BF)
    o_ref[...] = _dot(s, w_ref[...].astype(BF)) + b_ref[...]


def _mod_call(c8, w_mod, b_mod):
    depth, d, nd = w_mod.shape
    nb = nd // d
    return pl.pallas_call(
        _mod_kernel,
        grid=(depth, nb),
        in_specs=[pl.BlockSpec((8, d), lambda l, j: (0, 0)),
                  pl.BlockSpec((None, d, d), lambda l, j: (l, 0, j)),
                  pl.BlockSpec((None, 1, d), lambda l, j: (l, 0, j))],
        out_specs=pl.BlockSpec((None, 8, d), lambda l, j: (l, 0, j)),
        out_shape=jax.ShapeDtypeStruct((depth, 8, nd), F32),
        compiler_params=_cparams(2),
        name="mod",
    )(c8, w_mod, b_mod.reshape(depth, 1, nd))


def _ffn_kernel(x_ref, t_ref, wg_ref, wu_ref, wd_ref, o_ref, *, alpha):
    x = x_ref[...]
    t = t_ref[...]
    u = (_ln(x) * (1.0 + t[1:2]) + t[0:1]).astype(BF)
    acc = jnp.zeros(x.shape, F32)
    for c0 in range(0, wg_ref.shape[1], FFN_CHUNK):
        a = _dot(u, wg_ref[:, c0:c0 + FFN_CHUNK])
        b = _dot(u, wu_ref[:, c0:c0 + FFN_CHUNK])
        act = (a * jax.nn.sigmoid(a) * b).astype(BF)
        acc = acc + _dot(act, wd_ref[c0:c0 + FFN_CHUNK, :])
    v = alpha * x + (0.5 * t[2:3]) * acc
    o_ref[...] = _ln(v) * t[3:4] + t[4:5]


def _ffn_call(x, tab, wg, wu, wd, l, k, alpha):
    nb, n, d = x.shape
    tm = min(ROW_TILE, n)
    return pl.pallas_call(
        functools.partial(_ffn_kernel, alpha=alpha),
        grid=(nb, n // tm),
        in_specs=[pl.BlockSpec((None, tm, d), lambda b, i: (b, i, 0)),
                  pl.BlockSpec((None, 8, d), lambda b, i: (b, 0, 0)),
                  _layer_spec(wg, l, k), _layer_spec(wu, l, k), _layer_spec(wd, l, k)],
        out_specs=pl.BlockSpec((None, tm, d), lambda b, i: (b, i, 0)),
        out_shape=jax.ShapeDtypeStruct(x.shape, F32),
        compiler_params=_cparams(2),
        name="ffn",
    )(x, tab, wg, wu, wd)


def _rope(x, cos, sin_signed):
    n = x.shape[-1]
    lane = lax.broadcasted_iota(jnp.int32, x.shape, 1)
    first = (lane % 32) < 16
    partner = jnp.where(first, pltpu.roll(x, n - 16, axis=1), pltpu.roll(x, 16, axis=1))
    return x * cos + partner * sin_signed


def _mixin_kernel(x_ref, t_ref, w_ref, cos_ref, sin_ref,
                  us_ref, u4_ref, k_ref, v_ref, f_ref, q_ref, g_ref, us_f32, *, q_scale):
    x = x_ref[...]
    t = t_ref[...]
    u = (_ln(x) * (1.0 + t[1:2]) + t[0:1]).astype(BF)
    cos = cos_ref[...]
    sin = sin_ref[...]
    c_k = SSM_WIDTH
    c_v = c_k + NA_WIDTH
    c_f = c_v + NA_WIDTH
    c_q = c_f + FNET_WIDTH
    c_g = c_q + NA_WIDTH
    zs = _dot(u, w_ref[:, 0:c_k])
    us_ref[...] = zs.astype(BF)
    rows4 = us_f32.shape[1] // S5_BLOCK
    for hf in range(c_k // LANES):
        us_f32[hf] = zs[:, hf * LANES:(hf + 1) * LANES]
    for j in range(S5_BLOCK):
        for hf in range(c_k // LANES):
            c0 = j * c_k + hf * LANES
            u4_ref[:, c0:c0 + LANES] = us_f32[hf, pl.ds(j, rows4, stride=S5_BLOCK), :].astype(BF)
    k_ref[...] = _rope(_dot(u, w_ref[:, c_k:c_v]), cos, sin).astype(BF)
    v_ref[...] = _dot(u, w_ref[:, c_v:c_f]).astype(BF)
    f_ref[...] = _dot(u, w_ref[:, c_f:c_q]).astype(BF)
    q_ref[...] = (_rope(_dot(u, w_ref[:, c_q:c_g]), cos, sin) * q_scale).astype(BF)
    gw = NA_WIDTH
    for c0 in range(0, g_ref.shape[-1], gw):
        g = _dot(u, w_ref[:, c_g + c0:c_g + c0 + gw])
        g_ref[:, c0:c0 + gw] = jax.nn.sigmoid(g).astype(BF)


def _mixin_call(x, tab, w_in, l, cos_t, sin_t, q_scale):
    nb, n, d = x.shape
    win = w_in.shape[-1]
    ng = win - (SSM_WIDTH + 3 * NA_WIDTH + FNET_WIDTH)
    tm = min(ROW_TILE, n)
    nt_pos = cos_t.shape[0] // tm

    def tok(width):
        return pl.BlockSpec((None, tm, width), lambda b, i: (b, i, 0))

    def shape(width):
        return jax.ShapeDtypeStruct((nb, n, width), BF)

    u4_spec = pl.BlockSpec((None, tm // S5_BLOCK, S5_BLOCK * SSM_WIDTH), lambda b, i: (b, i, 0))
    u4_shape = jax.ShapeDtypeStruct((nb, n // S5_BLOCK, S5_BLOCK * SSM_WIDTH), BF)
    return pl.pallas_call(
        functools.partial(_mixin_kernel, q_scale=q_scale),
        grid=(nb, n // tm),
        in_specs=[tok(d),
                  pl.BlockSpec((None, 8, d), lambda b, i: (b, 0, 0)),
                  _layer_spec(w_in, l),
                  pl.BlockSpec((tm, NA_WIDTH), lambda b, i: (i % nt_pos, 0)),
                  pl.BlockSpec((tm, NA_WIDTH), lambda b, i: (i % nt_pos, 0))],
        out_specs=[tok(SSM_WIDTH), u4_spec, tok(NA_WIDTH), tok(NA_WIDTH), tok(FNET_WIDTH), tok(NA_WIDTH),
                   tok(ng)],
        out_shape=[shape(SSM_WIDTH), u4_shape, shape(NA_WIDTH), shape(NA_WIDTH), shape(FNET_WIDTH),
                   shape(NA_WIDTH), shape(ng)],
        scratch_shapes=[pltpu.VMEM((SSM_WIDTH // LANES, tm, LANES), F32)],
        compiler_params=_cparams(2),
        name="mixin",
    )(x, tab, w_in, cos_t, sin_t)


def _s5_kernel(u_ref, h0_ref, wst_ref, wconv_ref, wout_ref, sc_ref, y_ref, hfin_ref, hbuf, hprev, carry,
               *, reverse):
    i = pl.program_id(1)
    ns = SSM_FLAT
    n_tiles = hbuf.shape[0] // SCAN_ROWS

    @pl.when(i == 0)
    def _():
        carry[...] = h0_ref[...]

    u = u_ref[...]
    hbuf[...] = _dot(u, wst_ref[...])
    row_id = lax.broadcasted_iota(jnp.int32, (SCAN_ROWS, ns), 0)
    edge = row_id == ((SCAN_ROWS - 1) if reverse else 0)

    def tile(it, c):
        cr, ci = c
        row = (n_tiles - 1 - it) if reverse else it
        r0 = pl.multiple_of(row * SCAN_ROWS, SCAN_ROWS)
        xr = hbuf[pl.ds(r0, SCAN_ROWS), 0:ns]
        xi = hbuf[pl.ds(r0, SCAN_ROWS), ns:2 * ns]
        for k, s in enumerate((1, 2, 4)):
            ar = sc_ref[k, :, 0:ns]
            ai = sc_ref[k, :, ns:2 * ns]
            shift = (SCAN_ROWS - s) if reverse else s
            sr = pltpu.roll(xr, shift, axis=0)
            si = pltpu.roll(xi, shift, axis=0)
            xr, xi = xr + ar * sr - ai * si, xi + ar * si + ai * sr
        pr = sc_ref[3, :, 0:ns]
        pi = sc_ref[3, :, ns:2 * ns]
        hr = xr + pr * cr - pi * ci
        hi = xi + pr * ci + pi * cr
        one = (SCAN_ROWS - 1) if reverse else 1
        hprev[pl.ds(r0, SCAN_ROWS), 0:ns] = jnp.where(edge, cr, pltpu.roll(hr, one, axis=0))
        hprev[pl.ds(r0, SCAN_ROWS), ns:2 * ns] = jnp.where(edge, ci, pltpu.roll(hi, one, axis=0))
        last = 0 if reverse else SCAN_ROWS - 1
        return (jnp.broadcast_to(hr[last:last + 1, :], hr.shape),
                jnp.broadcast_to(hi[last:last + 1, :], hi.shape))

    cr, ci = lax.fori_loop(0, n_tiles, tile, (carry[:, 0:ns], carry[:, ns:2 * ns]))
    carry[:, 0:ns] = cr
    carry[:, ns:2 * ns] = ci
    hfin_ref[...] = carry[...]

    y = _dot(u, wconv_ref[...]) + _dot(hprev[...].astype(BF), wout_ref[...])
    y_ref[...] = y.astype(y_ref.dtype)


def _s5_call(u4, h0, tables, l, z, rows):
    wst, wconv, wout, sc = tables
    b, n, w = u4.shape
    nc = n // rows
    ns2 = 2 * SSM_FLAT
    reverse = z == 1
    if reverse:
        cmap = lambda bb, i: (bb, nc - 1 - i, 0)
    else:
        cmap = lambda bb, i: (bb, i, 0)
    return pl.pallas_call(
        functools.partial(_s5_kernel, reverse=reverse),
        grid=(b, nc),
        in_specs=[pl.BlockSpec((None, rows, w), cmap),
                  pl.BlockSpec((None, 8, ns2), lambda bb, i: (bb, 0, 0)),
                  _layer_spec(wst, l, z), _layer_spec(wconv, l, z), _layer_spec(wout, l, z),
                  _layer_spec(sc, l, z)],
        out_specs=[pl.BlockSpec((None, rows, w), cmap),
                   pl.BlockSpec((None, 8, ns2), lambda bb, i: (bb, 0, 0))],
        out_shape=[jax.ShapeDtypeStruct((b, n, w), BF),
                   jax.ShapeDtypeStruct((b, 8, ns2), F32)],
        scratch_shapes=[pltpu.VMEM((rows, ns2), F32), pltpu.VMEM((rows, ns2), F32), pltpu.VMEM((8, ns2), F32)],
        compiler_params=_cparams(2),
        name="s5_bwd" if reverse else "s5_fwd",
    )(u4, h0, wst, wconv, wout, sc)


def _s5_tables(log_dt, a_re, a_im, b_re, b_im, c_re, c_im):
    depth = log_dt.shape[0]
    g, p, c = b_re.shape[-3:]
    tb = S5_BLOCK
    hp = lax.Precision.HIGHEST
    f32 = lambda a: a.astype(F32)
    a_re, a_im, b_re, b_im, c_re, c_im = map(f32, (a_re, a_im, b_re, b_im, c_re, c_im))
    dt = jnp.exp(f32(log_dt))[..., None]
    adt_re = a_re * dt
    adt_im = a_im * dt

    def apow(k):
        kk = jnp.asarray(k, F32)[None, :, :, None, None]
        mag = jnp.exp(kk * adt_re[:, :, None])
        ang = kk * adt_im[:, :, None]
        return mag * jnp.cos(ang), mag * jnp.sin(ang)

    mag = jnp.exp(adt_re)
    ab_re = mag * jnp.cos(adt_im)
    ab_im = mag * jnp.sin(adt_im)
    den = a_re * a_re + a_im * a_im
    nr = ab_re - 1.0
    fr = (nr * a_re + ab_im * a_im) / den
    fi = (ab_im * a_re - nr * a_im) / den
    bb_re = fr[..., None] * b_re - fi[..., None] * b_im
    bb_im = fr[..., None] * b_im + fi[..., None] * b_re

    step = np.arange(tb)
    k_state = np.stack([tb - 1 - step, step])
    k_out = np.stack([step + 1, tb - step])
    k_tap = np.stack([step, step])
    k_scan = np.stack([tb * (np.arange(SCAN_ROWS) + 1)] * 2)
    eye = jnp.eye(g, dtype=F32)[None, None, None, :, None, None, :, None]

    def cmul_b(pr, pi):
        return (pr[..., None] * bb_re[:, :, None] - pi[..., None] * bb_im[:, :, None],
                pr[..., None] * bb_im[:, :, None] + pi[..., None] * bb_re[:, :, None])

    wr, wi = cmul_b(*apow(k_state))
    w2 = jnp.stack([wr, wi], axis=3)
    w2 = jnp.transpose(w2, (0, 1, 2, 4, 6, 3, 5))
    wst = (w2[:, :, :, :, :, :, None, :] * eye).reshape(depth, 2, tb * g * c, 2 * g * p)

    pr, pi = apow(k_out)
    cm_re = c_re[:, :, None] * pr[:, :, :, :, None, :] - c_im[:, :, None] * pi[:, :, :, :, None, :]
    cm_im = c_re[:, :, None] * pi[:, :, :, :, None, :] + c_im[:, :, None] * pr[:, :, :, :, None, :]
    cm = jnp.stack([cm_re, -cm_im], axis=3)
    cm = jnp.transpose(cm, (0, 1, 3, 4, 6, 2, 5))
    wout = (cm[:, :, :, :, :, :, None, :] * eye).reshape(depth, 2, 2 * g * p, tb * g * c)

    abr, abi = cmul_b(*apow(k_tap))
    taps = (jnp.einsum('dzgcp,dztgpe->dztgec', c_re, abr, precision=hp)
            - jnp.einsum('dzgcp,dztgpe->dztgec', c_im, abi, precision=hp))
    sel = np.zeros((2, tb, tb, tb), np.float32)
    for i in range(tb):
        for j in range(tb):
            if j >= i:
                sel[0, i, j, j - i] = 1.0
            if i >= j:
                sel[1, i, j, i - j] = 1.0
    kij = jnp.einsum('dztgec,zijt->dzigejc', taps, jnp.asarray(sel), precision=hp)
    wconv = (kij[:, :, :, :, :, :, None, :] * eye).reshape(depth, 2, tb * g * c, tb * g * c)

    qr, qi = apow(k_scan)
    qr = qr.reshape(depth, 2, SCAN_ROWS, g * p)
    qi = qi.reshape(depth, 2, SCAN_ROWS, g * p)
    rows = np.arange(SCAN_ROWS)
    tabs = []
    for s in (1, 2, 4):
        keep = jnp.asarray(np.stack([rows >= s, rows + s <= SCAN_ROWS - 1]), F32)[None, :, :, None]
        tabs.append(jnp.concatenate([keep * qr[:, :, s - 1:s], keep * qi[:, :, s - 1:s]], axis=-1))
    carry_pow = lambda t: jnp.stack([t[:, 0], t[:, 1, ::-1]], axis=1)
    tabs.append(jnp.concatenate([carry_pow(qr), carry_pow(qi)], axis=-1))
    sc = jnp.stack(tabs, axis=2)
    return wst.astype(BF), wconv.astype(BF), wout.astype(BF), sc


def _fnet1_kernel(x_ref, f2_ref, tc_ref, ts_ref, o_ref):
    n2 = x_ref.shape[0]
    y = _dot(f2_ref[...], x_ref[...])
    br = y[0:n2]
    bi = y[n2:2 * n2]
    tc = tc_ref[...]
    ts = ts_ref[...]
    o_ref[0] = (br * tc + bi * ts).astype(BF)
    o_ref[1] = (bi * tc - br * ts).astype(BF)


def _fnet2_kernel(b_ref, f1_ref, cc_ref, sc_ref, o_ref, *, scale):
    kb = b_ref.shape[1]
    n1 = b_ref.shape[2]
    c = b_ref.shape[3]
    zr, zi = [], []
    for j in range(kb):
        rhs = jnp.concatenate([b_ref[0, j], b_ref[1, j]], axis=0)
        z = _dot(f1_ref[...], rhs)
        zr.append(z[0:n1])
        zi.append(z[n1:2 * n1])
    zr = jnp.concatenate(zr, axis=0).astype(BF)
    zi = jnp.concatenate(zi, axis=0).astype(BF)
    y = (_dot(zr, cc_ref[...]) + _dot(zi, sc_ref[...])) * scale
    for j in range(kb):
        o_ref[:, j * c:(j + 1) * c] = y[j * n1:(j + 1) * n1].astype(o_ref.dtype)


def _dft_tables(n):
    idx = np.arange(n)
    ang = 2.0 * np.pi * ((idx[:, None] * idx[None, :]) % n) / n
    return np.cos(ang), np.sin(ang)


def _chan_tables():
    cc, sc = _dft_tables(FNET_GROUP_CH)
    eye = np.eye(FNET_GROUPS)
    return jnp.asarray(np.kron(eye, cc), BF), jnp.asarray(np.kron(eye, sc), BF)


def _fnet_call(f):
    b, n, c = f.shape
    n2 = FNET_N2
    n1 = n // n2
    c2, s2 = _dft_tables(n2)
    f2 = jnp.asarray(np.concatenate([c2, -s2], axis=0), BF)
    k2 = np.arange(n2)[:, None]
    nn1 = np.arange(n1)[None, :]
    ang = 2.0 * np.pi * ((k2 * nn1) % n) / n
    tc = jnp.asarray(np.repeat(np.cos(ang), c, axis=1), F32)
    ts = jnp.asarray(np.repeat(np.sin(ang), c, axis=1), F32)
    c1, s1 = _dft_tables(n1)
    f1 = jnp.asarray(np.block([[c1, s1], [-s1, c1]]), BF)
    cc, sc = _chan_tables()

    wc = min(n1 * c, 4096)
    x2 = f.reshape(b, n2, n1 * c)
    st1 = pl.pallas_call(
        _fnet1_kernel,
        grid=(b, (n1 * c) // wc),
        in_specs=[pl.BlockSpec((None, n2, wc), lambda bb, j: (bb, 0, j)),
                  pl.BlockSpec((2 * n2, n2), lambda bb, j: (0, 0)),
                  pl.BlockSpec((n2, wc), lambda bb, j: (0, j)),
                  pl.BlockSpec((n2, wc), lambda bb, j: (0, j))],
        out_specs=pl.BlockSpec((None, 2, n2, wc), lambda bb, j: (bb, 0, 0, j)),
        out_shape=jax.ShapeDtypeStruct((b, 2, n2, n1 * c), BF),
        compiler_params=_cparams(2),
        name="fnet1",
    )(x2, f2, tc, ts)
    st1 = st1.reshape(b, 2, n2, n1, c)
    kb = 8
    y = pl.pallas_call(
        functools.partial(_fnet2_kernel, scale=1.0 / math.sqrt(n * FNET_GROUP_CH)),
        grid=(b, n2 // kb),
        in_specs=[pl.BlockSpec((None, 2, kb, n1, c), lambda bb, j: (bb, 0, j, 0, 0)),
                  pl.BlockSpec((2 * n1, 2 * n1), lambda bb, j: (0, 0)),
                  pl.BlockSpec((c, c), lambda bb, j: (0, 0)),
                  pl.BlockSpec((c, c), lambda bb, j: (0, 0))],
        out_specs=pl.BlockSpec((None, n1, kb * c), lambda bb, j: (bb, 0, j)),
        out_shape=jax.ShapeDtypeStruct((b, n1, n2 * c), BF),
        compiler_params=_cparams(2),
        name="fnet2",
    )(st1, f1, cc, sc)
    return y.reshape(b, n, c)


def _fnet_dense_kernel(x_ref, cl_ref, sl_ref, cc_ref, sc_ref, o_ref, *, scale):
    x = x_ref[...]
    xr = _dot(x, cc_ref[...]).astype(BF)
    xi = _dot(x, sc_ref[...]).astype(BF)
    o_ref[...] = ((_dot(cl_ref[...], xr) - _dot(sl_ref[...], xi)) * scale).astype(o_ref.dtype)


def _fnet_dense_call(f):
    b, n, c = f.shape
    cl, sl = _dft_tables(n)
    cc, sc = _chan_tables()
    full = lambda bb: (0, 0)
    return pl.pallas_call(
        functools.partial(_fnet_dense_kernel, scale=1.0 / math.sqrt(n * FNET_GROUP_CH)),
        grid=(b,),
        in_specs=[pl.BlockSpec((None, n, c), lambda bb: (bb, 0, 0)),
                  pl.BlockSpec((n, n), full), pl.BlockSpec((n, n), full),
                  pl.BlockSpec((c, c), full), pl.BlockSpec((c, c), full)],
        out_specs=pl.BlockSpec((None, n, c), lambda bb: (bb, 0, 0)),
        out_shape=jax.ShapeDtypeStruct((b, n, c), BF),
        compiler_params=_cparams(1),
        name="fnet_ctx",
    )(f, jnp.asarray(cl, BF), jnp.asarray(sl, BF), cc, sc)


def _head_masks(width):
    lane = lax.broadcasted_iota(jnp.int32, (1, width), 1)
    return [(lane // NA_HEAD_DIM) == h for h in range(width // NA_HEAD_DIM)]


def _attend(q, k_parts, v_parts, bias_of, o_ref):
    zero = jnp.zeros((), BF)
    out = jnp.zeros(q.shape, F32)
    for h, hm in enumerate(_head_masks(q.shape[-1])):
        s_parts = []
        for j, kp in enumerate(k_parts):
            s = _dot_t(q, jnp.where(hm, kp, zero))
            b = bias_of(h, j)
            s_parts.append(s if b is None else s + b)
        m = s_parts[0].max(axis=-1, keepdims=True)
        for s in s_parts[1:]:
            m = jnp.maximum(m, s.max(axis=-1, keepdims=True))
        den = jnp.zeros_like(m)
        acc = jnp.zeros(q.shape, F32)
        for s, vp in zip(s_parts, v_parts):
            p = jnp.exp2(s - m)
            den = den + p.sum(axis=-1, keepdims=True)
            acc = acc + _dot(p.astype(BF), jnp.where(hm, vp, zero))
        out = out + acc / den
    o_ref[...] = out.astype(o_ref.dtype)


def _natten_kernel(q_ref, k0, k1, k2, k3, v0, v1, v2, v3, kc_ref, vc_ref, bias_ref, o_ref):
    blk = k0.shape[0]
    k_parts = [r[...] for r in (k0, k1, k2, k3)] + [kc_ref[...]]
    v_parts = [r[...] for r in (v0, v1, v2, v3)] + [vc_ref[...]]

    def bias_of(h, j):
        if j >= 4:
            return None
        return bias_ref[h, :, j * blk:(j + 1) * blk].astype(F32)

    _attend(q_ref[...], k_parts, v_parts, bias_of, o_ref)


def _natten_call(q, k, v, kc, vc, bias, l):
    b, n, w = q.shape
    lc = kc.shape[1]
    hw = NA_HQ * NA_HEAD_DIM
    qt = NA_QROWS * GRID_W
    blk = (NA_KROWS // 4) * GRID_W
    nj = n // qt
    nblk = n // blk
    ratio = qt // blk

    def kv_spec(m):
        def imap(hq, j, bb):
            start = jnp.clip(ratio * j - 1, 0, nblk - 4)
            return (bb, start + m, hq)
        return pl.BlockSpec((None, blk, hw), imap)

    def bias_map(hq, j, bb):
        var = jnp.where(j == 0, 0, jnp.where(j == nj - 1, 2, 1))
        return (l, var, hq, 0, 0)

    return pl.pallas_call(
        _natten_kernel,
        grid=(w // hw, nj, b),
        in_specs=[pl.BlockSpec((None, qt, hw), lambda hq, j, bb: (bb, j, hq))]
                 + [kv_spec(m) for m in range(4)] + [kv_spec(m) for m in range(4)]
                 + [pl.BlockSpec((None, lc, hw), lambda hq, j, bb: (bb, 0, hq)),
                    pl.BlockSpec((None, lc, hw), lambda hq, j, bb: (bb, 0, hq)),
                    pl.BlockSpec((None, None, NA_HQ, qt, 4 * blk), bias_map)],
        out_specs=pl.BlockSpec((None, qt, hw), lambda hq, j, bb: (bb, j, hq)),
        out_shape=jax.ShapeDtypeStruct((b, n, w), BF),
        compiler_params=_cparams(3),
        name="natten",
    )(q, k, k, k, k, v, v, v, v, kc, vc, bias)


def _ctx_atten_kernel(q_ref, kc_ref, vc_ref, o_ref):
    _attend(q_ref[...], [kc_ref[...]], [vc_ref[...]], lambda h, j: None, o_ref)


def _ctx_atten_call(qc, kc, vc):
    b, lc, w = qc.shape
    hw = NA_HQ * NA_HEAD_DIM
    spec = pl.BlockSpec((None, lc, hw), lambda bb, hq: (bb, 0, hq))
    return pl.pallas_call(
        _ctx_atten_kernel,
        grid=(b, w // hw),
        in_specs=[spec, spec, spec],
        out_specs=spec,
        out_shape=jax.ShapeDtypeStruct((b, lc, w), BF),
        compiler_params=_cparams(2),
        name="ctx_atten",
    )(qc, kc, vc)


def _na_bias(rpb, rows):
    w = GRID_W
    nj = rows // NA_QROWS
    n_dr = 2 * WIN_ROWS - 1
    n_dc = 2 * WIN_COLS - 1
    cq = np.arange(w)
    cs = np.clip(cq - WIN_COLS // 2, 0, w - WIN_COLS)
    ck = np.arange(w)
    valid_c = (ck[None, :] >= cs[:, None]) & (ck[None, :] < cs[:, None] + WIN_COLS)
    dc = ck[None, :] - cq[:, None] + (WIN_COLS - 1)
    pick_c = (dc[:, :, None] == np.arange(n_dc)) & valid_c[:, :, None]
    pick_r, mask = [], []
    for j in (0, min(1, nj - 1), nj - 1):
        rq = NA_QROWS * j + np.arange(NA_QROWS)
        ws = int(np.clip(NA_QROWS * j - (NA_KROWS - NA_QROWS) // 2, 0, rows - NA_KROWS))
        rk = ws + np.arange(NA_KROWS)
        r0 = np.clip(rq - WIN_ROWS // 2, 0, rows - WIN_ROWS)
        valid_r = (rk[None, :] >= r0[:, None]) & (rk[None, :] < r0[:, None] + WIN_ROWS)
        dr = rk[None, :] - rq[:, None] + (WIN_ROWS - 1)
        pick_r.append((dr[:, :, None] == np.arange(n_dr)) & valid_r[:, :, None])
        valid = valid_r[:, None, :, None] & valid_c[None, :, None, :]
        mask.append(np.where(valid, 0.0, NEG_BIG))
    pick_r = jnp.asarray(np.stack(pick_r), F32)
    mask = jnp.asarray(np.stack(mask), F32)
    bias = jnp.einsum('lhab,vqka,cdb->lvhqckd', rpb.astype(F32) * LOG2E, pick_r, jnp.asarray(pick_c, F32),
                      precision=lax.Precision.HIGHEST)
    bias = (bias + mask[None, :, None]).astype(BF)
    return bias.reshape(rpb.shape[0], 3, rpb.shape[1], NA_QROWS * w, NA_KROWS * w)


def _rope_tables(n):
    nf = NA_HEAD_DIM // 4
    t = jnp.arange(n, dtype=jnp.int32)
    pos = jnp.stack([t // GRID_W, t % GRID_W], axis=-1).astype(F32)
    inv_freq = ROPE_THETA ** (-jnp.arange(nf, dtype=F32) / nf)
    ang = pos[:, :, None] * inv_freq
    cos = jnp.cos(ang)
    sin = jnp.sin(ang)
    cos_h = jnp.concatenate([cos, cos], axis=-1).reshape(n, NA_HEAD_DIM)
    sin_h = jnp.concatenate([-sin, sin], axis=-1).reshape(n, NA_HEAD_DIM)
    return jnp.tile(cos_h, (1, NA_HEADS)), jnp.tile(sin_h, (1, NA_HEADS))


def _merge_kernel(ysf_ref, ysb_ref, us_ref, yf_ref, yn_ref, g_ref, h_ref, t_ref,
                  sd_ref, wglu_ref, bglu_ref, wbs_ref, wbf_ref, wbn_ref, wo_ref, o_ref, ys_f32, *, alpha):
    d = h_ref.shape[-1]
    t = t_ref[...]
    y4 = ysf_ref[...].astype(F32) + ysb_ref[...].astype(F32)
    rows4 = y4.shape[0]
    halves = SSM_WIDTH // LANES
    for j in range(S5_BLOCK):
        for hf in range(halves):
            c0 = j * SSM_WIDTH + hf * LANES
            ys_f32[hf, pl.ds(j, rows4, stride=S5_BLOCK), :] = y4[:, c0:c0 + LANES]
    ys = jnp.concatenate([ys_f32[hf] for hf in range(halves)], axis=-1)
    ys = ys + sd_ref[...] * us_ref[...].astype(F32)
    ys = jax.nn.gelu(ys)
    ys = (ys * jax.nn.sigmoid(_dot(ys.astype(BF), wglu_ref[...]) + bglu_ref[...])).astype(BF)
    y = (g_ref[:, 0:d].astype(F32) * _dot(ys, wbs_ref[...])
         + g_ref[:, d:2 * d].astype(F32) * _dot(yf_ref[...], wbf_ref[...])
         + g_ref[:, 2 * d:3 * d].astype(F32) * _dot(yn_ref[...], wbn_ref[...]))
    y = _dot(y.astype(BF), wo_ref[...])
    v = alpha * h_ref[...] + t[2:3] * y
    o_ref[...] = _ln(v) * t[3:4] + t[4:5]


def _merge_call(ysf4, ysb4, us, yf, yn, g, h, tab, s5w, mats, l, alpha):
    nb, n, d = h.shape
    tm = min(ROW_TILE, n)

    def tok(a):
        return pl.BlockSpec((None, tm, a.shape[-1]), lambda b, i: (b, i, 0))

    def tok4(a):
        return pl.BlockSpec((None, tm // S5_BLOCK, a.shape[-1]), lambda b, i: (b, i, 0))

    toks = (us, yf, yn, g, h)
    return pl.pallas_call(
        functools.partial(_merge_kernel, alpha=alpha),
        grid=(nb, n // tm),
        in_specs=[tok4(ysf4), tok4(ysb4)] + [tok(a) for a in toks]
                 + [pl.BlockSpec((None, 8, d), lambda b, i: (b, 0, 0))]
                 + [_layer_spec(a, l) for a in tuple(s5w) + tuple(mats)],
        out_specs=tok(h),
        out_shape=jax.ShapeDtypeStruct(h.shape, F32),
        scratch_shapes=[pltpu.VMEM((SSM_WIDTH // LANES, tm, LANES), F32)],
        compiler_params=_cparams(2),
        name="merge",
    )(ysf4, ysb4, *toks, tab, *s5w, *mats)


def _table(mod, d, chunks, extra):
    rows = [mod[:, i * d:(i + 1) * d] for i in chunks]
    rows += [jnp.broadcast_to(e[None, :], (8, d)) for e in extra]
    rows += [jnp.zeros((8, d), F32)] * (8 - len(rows))
    return jnp.stack(rows, axis=1)


def kernel(x, c, ctx, c_ctx, w_mod, b_mod, ln_g, ln_b, ffn_w_gate, ffn_w_up, ffn_w_down, w_in, ssm_log_dt, ssm_a_re, ssm_a_im, ssm_b_re, ssm_b_im, ssm_c_re, ssm_c_im, ssm_d, ssm_w_glu, ssm_b_glu, na_rpb, w_br_ssm, w_br_fnet, w_br_na, w_out):
    bsz, n, d = x.shape
    lc = ctx.shape[1]
    depth = w_mod.shape[0]
    rows = n // GRID_W
    assert bsz + 1 <= 8 and rows % NA_QROWS == 0 and rows >= NA_KROWS and n % FNET_N2 == 0
    alpha = (2 * depth) ** 0.25
    q_scale = NA_HEAD_DIM ** -0.5 * LOG2E

    c8 = jnp.concatenate([c, c_ctx[None, :], jnp.zeros((8 - bsz - 1, d), F32)], axis=0)
    mod_all = _mod_call(c8, w_mod, b_mod)

    wg = ffn_w_gate.astype(BF)
    wu = ffn_w_up.astype(BF)
    wd = ffn_w_down.astype(BF)
    win = w_in.astype(BF)
    mats = (w_br_ssm.astype(BF), w_br_fnet.astype(BF), w_br_na.astype(BF), w_out.astype(BF))
    s5w = (ssm_d[:, None, :], ssm_w_glu.astype(BF), ssm_b_glu[:, None, :])
    s5_tabs = _s5_tables(ssm_log_dt, ssm_a_re, ssm_a_im, ssm_b_re, ssm_b_im, ssm_c_re, ssm_c_im)
    na_bias = _na_bias(na_rpb, rows)
    cos_t, sin_t = _rope_tables(n)
    cos_c = jnp.ones((bsz * lc, NA_WIDTH), F32)
    sin_c = jnp.zeros((bsz * lc, NA_WIDTH), F32)
    zero_state = jnp.zeros((bsz, 8, 2 * SSM_FLAT), F32)

    h = x
    hc = ctx.reshape(1, bsz * lc, d)
    for l in range(depth):
        last = l == depth - 1
        mod = mod_all[l]

        tab = _table(mod, d, (0, 1, 2), (ln_g[l, 0], ln_b[l, 0]))
        h = _ffn_call(h, tab[:bsz], wg, wu, wd, l, 0, alpha)
        hc = _ffn_call(hc, tab[bsz:bsz + 1], wg, wu, wd, l, 0, alpha)

        tab = _table(mod, d, (3, 4, 5), (ln_g[l, 1], ln_b[l, 1]))
        us, u4, k, v, f, q, g = _mixin_call(h, tab[:bsz], win, l, cos_t, sin_t, q_scale)
        usc, u4c, kc, vc, fc, qc, gc = _mixin_call(hc, tab[bsz:bsz + 1], win, l, cos_c, sin_c, q_scale)
        u4c, kc, vc, fc, qc = [a.reshape(bsz, a.shape[1] // bsz, a.shape[-1]) for a in (u4c, kc, vc, fc, qc)]

        ysbc, hcb = _s5_call(u4c, zero_state, s5_tabs, l, 1, u4c.shape[1])
        ysfc, hcf = _s5_call(u4c, zero_state, s5_tabs, l, 0, u4c.shape[1])
        ysb, _ = _s5_call(u4, hcb, s5_tabs, l, 1, min(S5_ROWS, u4.shape[1]))
        ysf, _ = _s5_call(u4, hcf, s5_tabs, l, 0, min(S5_ROWS, u4.shape[1]))

        yf = _fnet_call(f)
        yn = _natten_call(q, k, v, kc, vc, na_bias, l)

        h = _merge_call(ysf, ysb, us, yf, yn, g, h, tab[:bsz], s5w, mats, l, alpha)

        tab3 = _table(mod, d, (6, 7, 8), (ln_g[l, 2], ln_b[l, 2]))
        h = _ffn_call(h, tab3[:bsz], wg, wu, wd, l, 1, alpha)

        if not last:
            yfc = _fnet_dense_call(fc)
            ync = _ctx_atten_call(qc, kc, vc)
            flat = lambda a: a.reshape(1, a.shape[0] * a.shape[1], a.shape[-1])
            hc = _merge_call(flat(ysfc), flat(ysbc), usc, flat(yfc), flat(ync), gc, hc, tab[bsz:bsz + 1],
                             s5w, mats, l, alpha)
            hc = _ffn_call(hc, tab3[bsz:bsz + 1], wg, wu, wd, l, 1, alpha)
    return h
```

```python
import functools
import math

import jax
import jax.numpy as jnp
import numpy as np
from jax import lax
from jax.experimental import pallas as pl
from jax.experimental.pallas import tpu as pltpu

BF = jnp.bfloat16
F32 = jnp.float32

GRID_W = 64
SSM_GROUPS = 16
SSM_GROUP_CH = 16
SSM_STATE = 64
SSM_WIDTH = SSM_GROUPS * SSM_GROUP_CH
SSM_FLAT = SSM_GROUPS * SSM_STATE
FNET_GROUPS = 4
FNET_GROUP_CH = 64
FNET_WIDTH = FNET_GROUPS * FNET_GROUP_CH
NA_HEADS = 8
NA_HEAD_DIM = 64
NA_WIDTH = NA_HEADS * NA_HEAD_DIM
WIN_ROWS = 8
WIN_COLS = 16
ROPE_THETA = 10000.0
N_MOD = 9
LN_EPS = 1e-6
LOG2E = 1.4426950408889634
NEG_BIG = -1e30

VMEM_LIMIT_BYTES = 56 * 1024 * 1024
ROW_TILE = 512
FFN_CHUNK = 256
LANES = 128
SCAN_ROWS = 8
S5_BLOCK = 4
S5_ROWS = 256
NA_QROWS = 8
NA_KROWS = 16
NA_HQ = 4
FNET_N2 = 128


def _cparams(n_axes):
    return pltpu.CompilerParams(dimension_semantics=("arbitrary",) * n_axes,
                                vmem_limit_bytes=VMEM_LIMIT_BYTES)


def _ln(x):
    mu = jnp.mean(x, axis=-1, keepdims=True)
    xc = x - mu
    var = jnp.mean(xc * xc, axis=-1, keepdims=True)
    return xc * lax.rsqrt(var + LN_EPS)


def _dot(a, b):
    return jnp.dot(a, b, preferred_element_type=F32)


def _dot_t(a, b):
    return lax.dot_general(a, b, (((1,), (1,)), ((), ())), preferred_element_type=F32)


def _layer_spec(a, *lead):
    nl = len(lead)
    zeros = (0,) * (a.ndim - nl)
    return pl.BlockSpec((None,) * nl + tuple(a.shape[nl:]), lambda *_: tuple(lead) + zeros)


def _mod_kernel(c_ref, w_ref, b_ref, o_ref):
    c = c_ref[...]
    s = (c * jax.nn.sigmoid(c)).astype(BF)
    o_ref[...] = _dot(s, w_ref[...].astype(BF)) + b_ref[...]


def _mod_call(c8, w_mod, b_mod):
    depth, d, nd = w_mod.shape
    nb = nd // d
    return pl.pallas_call(
        _mod_kernel,
        grid=(depth, nb),
        in_specs=[pl.BlockSpec((8, d), lambda l, j: (0, 0)),
                  pl.BlockSpec((None, d, d), lambda l, j: (l, 0, j)),
                  pl.BlockSpec((None, 1, d), lambda l, j: (l, 0, j))],
        out_specs=pl.BlockSpec((None, 8, d), lambda l, j: (l, 0, j)),
        out_shape=jax.ShapeDtypeStruct((depth, 8, nd), F32),
        compiler_params=_cparams(2),
        name="mod",
    )(c8, w_mod, b_mod.reshape(depth, 1, nd))


def _ffn_kernel(x_ref, t_ref, wg_ref, wu_ref, wd_ref, o_ref, *, alpha):
    x = x_ref[...]
    t = t_ref[...]
    u = (_ln(x) * (1.0 + t[1:2]) + t[0:1]).astype(BF)
    acc = jnp.zeros(x.shape, F32)
    for c0 in range(0, wg_ref.shape[1], FFN_CHUNK):
        a = _dot(u, wg_ref[:, c0:c0 + FFN_CHUNK])
        b = _dot(u, wu_ref[:, c0:c0 + FFN_CHUNK])
        act = (a * jax.nn.sigmoid(a) * b).astype(BF)
        acc = acc + _dot(act, wd_ref[c0:c0 + FFN_CHUNK, :])
    v = alpha * x + (0.5 * t[2:3]) * acc
    o_ref[...] = _ln(v) * t[3:4] + t[4:5]


def _ffn_call(x, tab, wg, wu, wd, l, k, alpha):
    nb, n, d = x.shape
    tm = min(ROW_TILE, n)
    return pl.pallas_call(
        functools.partial(_ffn_kernel, alpha=alpha),
        grid=(nb, n // tm),
        in_specs=[pl.BlockSpec((None, tm, d), lambda b, i: (b, i, 0)),
                  pl.BlockSpec((None, 8, d), lambda b, i: (b, 0, 0)),
                  _layer_spec(wg, l, k), _layer_spec(wu, l, k), _layer_spec(wd, l, k)],
        out_specs=pl.BlockSpec((None, tm, d), lambda b, i: (b, i, 0)),
        out_shape=jax.ShapeDtypeStruct(x.shape, F32),
        compiler_params=_cparams(2),
        name="ffn",
    )(x, tab, wg, wu, wd)


def _rope(x, cos, sin_signed):
    n = x.shape[-1]
    lane = lax.broadcasted_iota(jnp.int32, x.shape, 1)
    first = (lane % 32) < 16
    partner = jnp.where(first, pltpu.roll(x, n - 16, axis=1), pltpu.roll(x, 16, axis=1))
    return x * cos + partner * sin_signed


def _mixin_kernel(x_ref, t_ref, w_ref, cos_ref, sin_ref,
                  us_ref, u4_ref, k_ref, v_ref, f_ref, q_ref, g_ref, us_f32, *, q_scale):
    x = x_ref[...]
    t = t_ref[...]
    u = (_ln(x) * (1.0 + t[1:2]) + t[0:1]).astype(BF)
    cos = cos_ref[...]
    sin = sin_ref[...]
    c_k = SSM_WIDTH
    c_v = c_k + NA_WIDTH
    c_f = c_v + NA_WIDTH
    c_q = c_f + FNET_WIDTH
    c_g = c_q + NA_WIDTH
    zs = _dot(u, w_ref[:, 0:c_k])
    us_ref[...] = zs.astype(BF)
    rows4 = us_f32.shape[1] // S5_BLOCK
    for hf in range(c_k // LANES):
        us_f32[hf] = zs[:, hf * LANES:(hf + 1) * LANES]
    for j in range(S5_BLOCK):
        for hf in range(c_k // LANES):
            c0 = j * c_k + hf * LANES
            u4_ref[:, c0:c0 + LANES] = us_f32[hf, pl.ds(j, rows4, stride=S5_BLOCK), :].astype(BF)
    k_ref[...] = _rope(_dot(u, w_ref[:, c_k:c_v]), cos, sin).astype(BF)
    v_ref[...] = _dot(u, w_ref[:, c_v:c_f]).astype(BF)
    f_ref[...] = _dot(u, w_ref[:, c_f:c_q]).astype(BF)
    q_ref[...] = (_rope(_dot(u, w_ref[:, c_q:c_g]), cos, sin) * q_scale).astype(BF)
    gw = NA_WIDTH
    for c0 in range(0, g_ref.shape[-1], gw):
        g = _dot(u, w_ref[:, c_g + c0:c_g + c0 + gw])
        g_ref[:, c0:c0 + gw] = jax.nn.sigmoid(g).astype(BF)


def _mixin_call(x, tab, w_in, l, cos_t, sin_t, q_scale):
    nb, n, d = x.shape
    win = w_in.shape[-1]
    ng = win - (SSM_WIDTH + 3 * NA_WIDTH + FNET_WIDTH)
    tm = min(ROW_TILE, n)
    nt_pos = cos_t.shape[0] // tm

    def tok(width):
        return pl.BlockSpec((None, tm, width), lambda b, i: (b, i, 0))

    def shape(width):
        return jax.ShapeDtypeStruct((nb, n, width), BF)

    u4_spec = pl.BlockSpec((None, tm // S5_BLOCK, S5_BLOCK * SSM_WIDTH), lambda b, i: (b, i, 0))
    u4_shape = jax.ShapeDtypeStruct((nb, n // S5_BLOCK, S5_BLOCK * SSM_WIDTH), BF)
    return pl.pallas_call(
        functools.partial(_mixin_kernel, q_scale=q_scale),
        grid=(nb, n // tm),
        in_specs=[tok(d),
                  pl.BlockSpec((None, 8, d), lambda b, i: (b, 0, 0)),
                  _layer_spec(w_in, l),
                  pl.BlockSpec((tm, NA_WIDTH), lambda b, i: (i % nt_pos, 0)),
                  pl.BlockSpec((tm, NA_WIDTH), lambda b, i: (i % nt_pos, 0))],
        out_specs=[tok(SSM_WIDTH), u4_spec, tok(NA_WIDTH), tok(NA_WIDTH), tok(FNET_WIDTH), tok(NA_WIDTH),
                   tok(ng)],
        out_shape=[shape(SSM_WIDTH), u4_shape, shape(NA_WIDTH), shape(NA_WIDTH), shape(FNET_WIDTH),
                   shape(NA_WIDTH), shape(ng)],
        scratch_shapes=[pltpu.VMEM((SSM_WIDTH // LANES, tm, LANES), F32)],
        compiler_params=_cparams(2),
        name="mixin",
    )(x, tab, w_in, cos_t, sin_t)


def _s5_kernel(u_ref, h0_ref, wst_ref, wconv_ref, wout_ref, sc_ref, y_ref, hfin_ref, hbuf, hprev, carry,
               *, reverse):
    i = pl.program_id(1)
    ns = SSM_FLAT
    n_tiles = hbuf.shape[0] // SCAN_ROWS

    @pl.when(i == 0)
    def _():
        carry[...] = h0_ref[...]

    u = u_ref[...]
    hbuf[...] = _dot(u, wst_ref[...])
    row_id = lax.broadcasted_iota(jnp.int32, (SCAN_ROWS, ns), 0)
    edge = row_id == ((SCAN_ROWS - 1) if reverse else 0)

    def tile(it, c):
        cr, ci = c
        row = (n_tiles - 1 - it) if reverse else it
        r0 = pl.multiple_of(row * SCAN_ROWS, SCAN_ROWS)
        xr = hbuf[pl.ds(r0, SCAN_ROWS), 0:ns]
        xi = hbuf[pl.ds(r0, SCAN_ROWS), ns:2 * ns]
        for k, s in enumerate((1, 2, 4)):
            ar = sc_ref[k, :, 0:ns]
            ai = sc_ref[k, :, ns:2 * ns]
            shift = (SCAN_ROWS - s) if reverse else s
            sr = pltpu.roll(xr, shift, axis=0)
            si = pltpu.roll(xi, shift, axis=0)
            xr, xi = xr + ar * sr - ai * si, xi + ar * si + ai * sr
        pr = sc_ref[3, :, 0:ns]
        pi = sc_ref[3, :, ns:2 * ns]
        hr = xr + pr * cr - pi * ci
        hi = xi + pr * ci + pi * cr
        one = (SCAN_ROWS - 1) if reverse else 1
        hprev[pl.ds(r0, SCAN_ROWS), 0:ns] = jnp.where(edge, cr, pltpu.roll(hr, one, axis=0))
        hprev[pl.ds(r0, SCAN_ROWS), ns:2 * ns] = jnp.where(edge, ci, pltpu.roll(hi, one, axis=0))
        last = 0 if reverse else SCAN_ROWS - 1
        return (jnp.broadcast_to(hr[last:last + 1, :], hr.shape),
                jnp.broadcast_to(hi[last:last + 1, :], hi.shape))

    cr, ci = lax.fori_loop(0, n_tiles, tile, (carry[:, 0:ns], carry[:, ns:2 * ns]))
    carry[:, 0:ns] = cr
    carry[:, ns:2 * ns] = ci
    hfin_ref[...] = carry[...]

    y = _dot(u, wconv_ref[...]) + _dot(hprev[...].astype(BF), wout_ref[...])
    y_ref[...] = y.astype(y_ref.dtype)


def _s5_call(u4, h0, tables, l, z, rows):
    wst, wconv, wout, sc = tables
    b, n, w = u4.shape
    nc = n // rows
    ns2 = 2 * SSM_FLAT
    reverse = z == 1
    if reverse:
        cmap = lambda bb, i: (bb, nc - 1 - i, 0)
    else:
        cmap = lambda bb, i: (bb, i, 0)
    return pl.pallas_call(
        functools.partial(_s5_kernel, reverse=reverse),
        grid=(b, nc),
        in_specs=[pl.BlockSpec((None, rows, w), cmap),
                  pl.BlockSpec((None, 8, ns2), lambda bb, i: (bb, 0, 0)),
                  _layer_spec(wst, l, z), _layer_spec(wconv, l, z), _layer_spec(wout, l, z),
                  _layer_spec(sc, l, z)],
        out_specs=[pl.BlockSpec((None, rows, w), cmap),
                   pl.BlockSpec((None, 8, ns2), lambda bb, i: (bb, 0, 0))],
        out_shape=[jax.ShapeDtypeStruct((b, n, w), BF),
                   jax.ShapeDtypeStruct((b, 8, ns2), F32)],
        scratch_shapes=[pltpu.VMEM((rows, ns2), F32), pltpu.VMEM((rows, ns2), F32), pltpu.VMEM((8, ns2), F32)],
        compiler_params=_cparams(2),
        name="s5_bwd" if reverse else "s5_fwd",
    )(u4, h0, wst, wconv, wout, sc)


def _s5_tables(log_dt, a_re, a_im, b_re, b_im, c_re, c_im):
    depth = log_dt.shape[0]
    g, p, c = b_re.shape[-3:]
    tb = S5_BLOCK
    hp = lax.Precision.HIGHEST
    f32 = lambda a: a.astype(F32)
    a_re, a_im, b_re, b_im, c_re, c_im = map(f32, (a_re, a_im, b_re, b_im, c_re, c_im))
    dt = jnp.exp(f32(log_dt))[..., None]
    adt_re = a_re * dt
    adt_im = a_im * dt

    def apow(k):
        kk = jnp.asarray(k, F32)[None, :, :, None, None]
        mag = jnp.exp(kk * adt_re[:, :, None])
        ang = kk * adt_im[:, :, None]
        return mag * jnp.cos(ang), mag * jnp.sin(ang)

    mag = jnp.exp(adt_re)
    ab_re = mag * jnp.cos(adt_im)
    ab_im = mag * jnp.sin(adt_im)
    den = a_re * a_re + a_im * a_im
    nr = ab_re - 1.0
    fr = (nr * a_re + ab_im * a_im) / den
    fi = (ab_im * a_re - nr * a_im) / den
    bb_re = fr[..., None] * b_re - fi[..., None] * b_im
    bb_im = fr[..., None] * b_im + fi[..., None] * b_re

    step = np.arange(tb)
    k_state = np.stack([tb - 1 - step, step])
    k_out = np.stack([step + 1, tb - step])
    k_tap = np.stack([step, step])
    k_scan = np.stack([tb * (np.arange(SCAN_ROWS) + 1)] * 2)

    def block_diag(compact, row_group, col_width):
        n = compact.shape[3]
        cols = jnp.concatenate([compact[:, :, :, k] for k in range(n) for _ in range(g)], axis=-1)
        col_group = (np.arange(n * g * col_width) // col_width) % g
        keep = jnp.asarray(row_group[:, None] == col_group[None, :])
        return jnp.where(keep, cols, 0.0).astype(BF)

    chan_rows = (np.arange(tb * g * c) // c) % g
    state_rows = (np.arange(2 * g * p) // p) % g

    def cmul_b(pr, pi):
        return (pr[..., None] * bb_re[:, :, None] - pi[..., None] * bb_im[:, :, None],
                pr[..., None] * bb_im[:, :, None] + pi[..., None] * bb_re[:, :, None])

    wr, wi = cmul_b(*apow(k_state))
    w2 = jnp.stack([wr, wi], axis=3)
    w2 = jnp.transpose(w2, (0, 1, 2, 4, 6, 3, 5))
    wst = block_diag(w2.reshape(depth, 2, tb * g * c, 2, p), chan_rows, p)

    pr, pi = apow(k_out)
    cm_re = c_re[:, :, None] * pr[:, :, :, :, None, :] - c_im[:, :, None] * pi[:, :, :, :, None, :]
    cm_im = c_re[:, :, None] * pi[:, :, :, :, None, :] + c_im[:, :, None] * pr[:, :, :, :, None, :]
    cm = jnp.stack([cm_re, -cm_im], axis=3)
    cm = jnp.transpose(cm, (0, 1, 3, 4, 6, 2, 5))
    wout = block_diag(cm.reshape(depth, 2, 2 * g * p, tb, c), state_rows, c)

    abr, abi = cmul_b(*apow(k_tap))
    taps = (jnp.einsum('dzgcp,dztgpe->dztgec', c_re, abr, precision=hp)
            - jnp.einsum('dzgcp,dztgpe->dztgec', c_im, abi, precision=hp))
    sel = np.zeros((2, tb, tb, tb), np.float32)
    for i in range(tb):
        for j in range(tb):
            if j >= i:
                sel[0, i, j, j - i] = 1.0
            if i >= j:
                sel[1, i, j, i - j] = 1.0
    kij = jnp.einsum('dztgec,zijt->dzigejc', taps, jnp.asarray(sel), precision=hp)
    wconv = block_diag(kij.reshape(depth, 2, tb * g * c, tb, c), chan_rows, c)

    qr, qi = apow(k_scan)
    qr = qr.reshape(depth, 2, SCAN_ROWS, g * p)
    qi = qi.reshape(depth, 2, SCAN_ROWS, g * p)
    rows = np.arange(SCAN_ROWS)
    tabs = []
    for s in (1, 2, 4):
        keep = jnp.asarray(np.stack([rows >= s, rows + s <= SCAN_ROWS - 1]), F32)[None, :, :, None]
        tabs.append(jnp.concatenate([keep * qr[:, :, s - 1:s], keep * qi[:, :, s - 1:s]], axis=-1))
    carry_pow = lambda t: jnp.stack([t[:, 0], t[:, 1, ::-1]], axis=1)
    tabs.append(jnp.concatenate([carry_pow(qr), carry_pow(qi)], axis=-1))
    sc = jnp.stack(tabs, axis=2)
    return wst, wconv, wout, sc


def _fnet1_kernel(x_ref, f2_ref, tc_ref, ts_ref, o_ref):
    n2 = x_ref.shape[0]
    y = _dot(f2_ref[...], x_ref[...])
    br = y[0:n2]
    bi = y[n2:2 * n2]
    tc = tc_ref[...]
    ts = ts_ref[...]
    o_ref[0] = (br * tc + bi * ts).astype(BF)
    o_ref[1] = (bi * tc - br * ts).astype(BF)


def _fnet2_kernel(b_ref, f1_ref, cc_ref, sc_ref, o_ref, *, scale):
    kb = b_ref.shape[1]
    n1 = b_ref.shape[2]
    c = b_ref.shape[3]
    zr, zi = [], []
    for j in range(kb):
        rhs = jnp.concatenate([b_ref[0, j], b_ref[1, j]], axis=0)
        z = _dot(f1_ref[...], rhs)
        zr.append(z[0:n1])
        zi.append(z[n1:2 * n1])
    zr = jnp.concatenate(zr, axis=0).astype(BF)
    zi = jnp.concatenate(zi, axis=0).astype(BF)
    y = (_dot(zr, cc_ref[...]) + _dot(zi, sc_ref[...])) * scale
    for j in range(kb):
        o_ref[:, j * c:(j + 1) * c] = y[j * n1:(j + 1) * n1].astype(o_ref.dtype)


def _dft_tables(n):
    idx = np.arange(n)
    ang = 2.0 * np.pi * ((idx[:, None] * idx[None, :]) % n) / n
    return np.cos(ang), np.sin(ang)


def _chan_tables():
    cc, sc = _dft_tables(FNET_GROUP_CH)
    eye = np.eye(FNET_GROUPS)
    return jnp.asarray(np.kron(eye, cc), BF), jnp.asarray(np.kron(eye, sc), BF)


def _fnet_call(f):
    b, n, c = f.shape
    n2 = FNET_N2
    n1 = n // n2
    c2, s2 = _dft_tables(n2)
    f2 = jnp.asarray(np.concatenate([c2, -s2], axis=0), BF)
    k2 = np.arange(n2)[:, None]
    nn1 = np.arange(n1)[None, :]
    ang = 2.0 * np.pi * ((k2 * nn1) % n) / n
    tc = jnp.asarray(np.repeat(np.cos(ang), c, axis=1), F32)
    ts = jnp.asarray(np.repeat(np.sin(ang), c, axis=1), F32)
    c1, s1 = _dft_tables(n1)
    f1 = jnp.asarray(np.block([[c1, s1], [-s1, c1]]), BF)
    cc, sc = _chan_tables()

    wc = min(n1 * c, 4096)
    x2 = f.reshape(b, n2, n1 * c)
    st1 = pl.pallas_call(
        _fnet1_kernel,
        grid=(b, (n1 * c) // wc),
        in_specs=[pl.BlockSpec((None, n2, wc), lambda bb, j: (bb, 0, j)),
                  pl.BlockSpec((2 * n2, n2), lambda bb, j: (0, 0)),
                  pl.BlockSpec((n2, wc), lambda bb, j: (0, j)),
                  pl.BlockSpec((n2, wc), lambda bb, j: (0, j))],
        out_specs=pl.BlockSpec((None, 2, n2, wc), lambda bb, j: (bb, 0, 0, j)),
        out_shape=jax.ShapeDtypeStruct((b, 2, n2, n1 * c), BF),
        compiler_params=_cparams(2),
        name="fnet1",
    )(x2, f2, tc, ts)
    st1 = st1.reshape(b, 2, n2, n1, c)
    kb = 8
    y = pl.pallas_call(
        functools.partial(_fnet2_kernel, scale=1.0 / math.sqrt(n * FNET_GROUP_CH)),
        grid=(b, n2 // kb),
        in_specs=[pl.BlockSpec((None, 2, kb, n1, c), lambda bb, j: (bb, 0, j, 0, 0)),
                  pl.BlockSpec((2 * n1, 2 * n1), lambda bb, j: (0, 0)),
                  pl.BlockSpec((c, c), lambda bb, j: (0, 0)),
                  pl.BlockSpec((c, c), lambda bb, j: (0, 0))],
        out_specs=pl.BlockSpec((None, n1, kb * c), lambda bb, j: (bb, 0, j)),
        out_shape=jax.ShapeDtypeStruct((b, n1, n2 * c), BF),
        compiler_params=_cparams(2),
        name="fnet2",
    )(st1, f1, cc, sc)
    return y.reshape(b, n, c)


def _fnet_dense_kernel(x_ref, cl_ref, sl_ref, cc_ref, sc_ref, o_ref, *, scale):
    x = x_ref[...]
    xr = _dot(x, cc_ref[...]).astype(BF)
    xi = _dot(x, sc_ref[...]).astype(BF)
    o_ref[...] = ((_dot(cl_ref[...], xr) - _dot(sl_ref[...], xi)) * scale).astype(o_ref.dtype)


def _fnet_dense_call(f):
    b, n, c = f.shape
    cl, sl = _dft_tables(n)
    cc, sc = _chan_tables()
    full = lambda bb: (0, 0)
    return pl.pallas_call(
        functools.partial(_fnet_dense_kernel, scale=1.0 / math.sqrt(n * FNET_GROUP_CH)),
        grid=(b,),
        in_specs=[pl.BlockSpec((None, n, c), lambda bb: (bb, 0, 0)),
                  pl.BlockSpec((n, n), full), pl.BlockSpec((n, n), full),
                  pl.BlockSpec((c, c), full), pl.BlockSpec((c, c), full)],
        out_specs=pl.BlockSpec((None, n, c), lambda bb: (bb, 0, 0)),
        out_shape=jax.ShapeDtypeStruct((b, n, c), BF),
        compiler_params=_cparams(1),
        name="fnet_ctx",
    )(f, jnp.asarray(cl, BF), jnp.asarray(sl, BF), cc, sc)


def _head_masks(width):
    lane = lax.broadcasted_iota(jnp.int32, (1, width), 1)
    return [(lane // NA_HEAD_DIM) == h for h in range(width // NA_HEAD_DIM)]


def _attend(q, k_parts, v_parts, bias_of, o_ref):
    zero = jnp.zeros((), BF)
    out = jnp.zeros(q.shape, F32)
    for h, hm in enumerate(_head_masks(q.shape[-1])):
        s_parts = []
        for j, kp in enumerate(k_parts):
            s = _dot_t(q, jnp.where(hm, kp, zero))
            b = bias_of(h, j)
            s_parts.append(s if b is None else s + b)
        m = s_parts[0].max(axis=-1, keepdims=True)
        for s in s_parts[1:]:
            m = jnp.maximum(m, s.max(axis=-1, keepdims=True))
        den = jnp.zeros_like(m)
        acc = jnp.zeros(q.shape, F32)
        for s, vp in zip(s_parts, v_parts):
            p = jnp.exp2(s - m)
            den = den + p.sum(axis=-1, keepdims=True)
            acc = acc + _dot(p.astype(BF), jnp.where(hm, vp, zero))
        out = out + acc / den
    o_ref[...] = out.astype(o_ref.dtype)


def _natten_kernel(q_ref, k0, k1, k2, k3, v0, v1, v2, v3, kc_ref, vc_ref, bias_ref, o_ref):
    blk = k0.shape[0]
    k_parts = [r[...] for r in (k0, k1, k2, k3)] + [kc_ref[...]]
    v_parts = [r[...] for r in (v0, v1, v2, v3)] + [vc_ref[...]]

    def bias_of(h, j):
        if j >= 4:
            return None
        return bias_ref[h, :, j * blk:(j + 1) * blk].astype(F32)

    _attend(q_ref[...], k_parts, v_parts, bias_of, o_ref)


def _natten_call(q, k, v, kc, vc, bias, l):
    b, n, w = q.shape
    lc = kc.shape[1]
    hw = NA_HQ * NA_HEAD_DIM
    qt = NA_QROWS * GRID_W
    blk = (NA_KROWS // 4) * GRID_W
    nj = n // qt
    nblk = n // blk
    ratio = qt // blk

    def kv_spec(m):
        def imap(hq, j, bb):
            start = jnp.clip(ratio * j - 1, 0, nblk - 4)
            return (bb, start + m, hq)
        return pl.BlockSpec((None, blk, hw), imap)

    def bias_map(hq, j, bb):
        var = jnp.where(j == 0, 0, jnp.where(j == nj - 1, 2, 1))
        return (l, var, hq, 0, 0)

    return pl.pallas_call(
        _natten_kernel,
        grid=(w // hw, nj, b),
        in_specs=[pl.BlockSpec((None, qt, hw), lambda hq, j, bb: (bb, j, hq))]
                 + [kv_spec(m) for m in range(4)] + [kv_spec(m) for m in range(4)]
                 + [pl.BlockSpec((None, lc, hw), lambda hq, j, bb: (bb, 0, hq)),
                    pl.BlockSpec((None, lc, hw), lambda hq, j, bb: (bb, 0, hq)),
                    pl.BlockSpec((None, None, NA_HQ, qt, 4 * blk), bias_map)],
        out_specs=pl.BlockSpec((None, qt, hw), lambda hq, j, bb: (bb, j, hq)),
        out_shape=jax.ShapeDtypeStruct((b, n, w), BF),
        compiler_params=_cparams(3),
        name="natten",
    )(q, k, k, k, k, v, v, v, v, kc, vc, bias)


def _ctx_atten_kernel(q_ref, kc_ref, vc_ref, o_ref):
    _attend(q_ref[...], [kc_ref[...]], [vc_ref[...]], lambda h, j: None, o_ref)


def _ctx_atten_call(qc, kc, vc):
    b, lc, w = qc.shape
    hw = NA_HQ * NA_HEAD_DIM
    spec = pl.BlockSpec((None, lc, hw), lambda bb, hq: (bb, 0, hq))
    return pl.pallas_call(
        _ctx_atten_kernel,
        grid=(b, w // hw),
        in_specs=[spec, spec, spec],
        out_specs=spec,
        out_shape=jax.ShapeDtypeStruct((b, lc, w), BF),
        compiler_params=_cparams(2),
        name="ctx_atten",
    )(qc, kc, vc)


def _na_bias(rpb, rows):
    w = GRID_W
    nj = rows // NA_QROWS
    n_dr = 2 * WIN_ROWS - 1
    n_dc = 2 * WIN_COLS - 1
    cq = np.arange(w)
    cs = np.clip(cq - WIN_COLS // 2, 0, w - WIN_COLS)
    ck = np.arange(w)
    valid_c = (ck[None, :] >= cs[:, None]) & (ck[None, :] < cs[:, None] + WIN_COLS)
    dc = ck[None, :] - cq[:, None] + (WIN_COLS - 1)
    pick_c = (dc[:, :, None] == np.arange(n_dc)) & valid_c[:, :, None]
    tiles = jnp.einsum('lhab,cdb->lhacd', rpb.astype(F32) * LOG2E, jnp.asarray(pick_c, F32),
                       precision=lax.Precision.HIGHEST)
    tiles = tiles + jnp.asarray(np.where(valid_c, 0.0, NEG_BIG), F32)
    tiles = jnp.concatenate([tiles, jnp.full_like(tiles[:, :, :1], NEG_BIG)], axis=2).astype(BF)
    out = []
    for j in (0, min(1, nj - 1), nj - 1):
        rq = NA_QROWS * j + np.arange(NA_QROWS)
        ws = int(np.clip(NA_QROWS * j - (NA_KROWS - NA_QROWS) // 2, 0, rows - NA_KROWS))
        rk = ws + np.arange(NA_KROWS)
        r0 = np.clip(rq - WIN_ROWS // 2, 0, rows - WIN_ROWS)
        valid_r = (rk[None, :] >= r0[:, None]) & (rk[None, :] < r0[:, None] + WIN_ROWS)
        dr = np.where(valid_r, rk[None, :] - rq[:, None] + (WIN_ROWS - 1), n_dr)
        out.append(jnp.concatenate(
            [jnp.concatenate([tiles[:, :, int(dr[a, b])] for b in range(NA_KROWS)], axis=-1)
             for a in range(NA_QROWS)], axis=-2))
    return jnp.stack(out, axis=1)


def _rope_tables(n):
    nf = NA_HEAD_DIM // 4
    t = jnp.arange(n, dtype=jnp.int32)
    pos = jnp.stack([t // GRID_W, t % GRID_W], axis=-1).astype(F32)
    inv_freq = ROPE_THETA ** (-jnp.arange(nf, dtype=F32) / nf)
    ang = pos[:, :, None] * inv_freq
    cos = jnp.cos(ang)
    sin = jnp.sin(ang)
    cos_h = jnp.concatenate([cos, cos], axis=-1).reshape(n, NA_HEAD_DIM)
    sin_h = jnp.concatenate([-sin, sin], axis=-1).reshape(n, NA_HEAD_DIM)
    return jnp.tile(cos_h, (1, NA_HEADS)), jnp.tile(sin_h, (1, NA_HEADS))


def _merge_kernel(ysf_ref, ysb_ref, us_ref, yf_ref, yn_ref, g_ref, h_ref, t_ref,
                  sd_ref, wglu_ref, bglu_ref, wbs_ref, wbf_ref, wbn_ref, wo_ref, o_ref, ys_f32, *, alpha):
    d = h_ref.shape[-1]
    t = t_ref[...]
    y4 = ysf_ref[...].astype(F32) + ysb_ref[...].astype(F32)
    rows4 = y4.shape[0]
    halves = SSM_WIDTH // LANES
    for j in range(S5_BLOCK):
        for hf in range(halves):
            c0 = j * SSM_WIDTH + hf * LANES
            ys_f32[hf, pl.ds(j, rows4, stride=S5_BLOCK), :] = y4[:, c0:c0 + LANES]
    ys = jnp.concatenate([ys_f32[hf] for hf in range(halves)], axis=-1)
    ys = ys + sd_ref[...] * us_ref[...].astype(F32)
    ys = jax.nn.gelu(ys)
    ys = (ys * jax.nn.sigmoid(_dot(ys.astype(BF), wglu_ref[...]) + bglu_ref[...])).astype(BF)
    y = (g_ref[:, 0:d].astype(F32) * _dot(ys, wbs_ref[...])
         + g_ref[:, d:2 * d].astype(F32) * _dot(yf_ref[...], wbf_ref[...])
         + g_ref[:, 2 * d:3 * d].astype(F32) * _dot(yn_ref[...], wbn_ref[...]))
    y = _dot(y.astype(BF), wo_ref[...])
    v = alpha * h_ref[...] + t[2:3] * y
    o_ref[...] = _ln(v) * t[3:4] + t[4:5]


def _merge_call(ysf4, ysb4, us, yf, yn, g, h, tab, s5w, mats, l, alpha):
    nb, n, d = h.shape
    tm = min(ROW_TILE, n)

    def tok(a):
        return pl.BlockSpec((None, tm, a.shape[-1]), lambda b, i: (b, i, 0))

    def tok4(a):
        return pl.BlockSpec((None, tm // S5_BLOCK, a.shape[-1]), lambda b, i: (b, i, 0))

    toks = (us, yf, yn, g, h)
    return pl.pallas_call(
        functools.partial(_merge_kernel, alpha=alpha),
        grid=(nb, n // tm),
        in_specs=[tok4(ysf4), tok4(ysb4)] + [tok(a) for a in toks]
                 + [pl.BlockSpec((None, 8, d), lambda b, i: (b, 0, 0))]
                 + [_layer_spec(a, l) for a in tuple(s5w) + tuple(mats)],
        out_specs=tok(h),
        out_shape=jax.ShapeDtypeStruct(h.shape, F32),
        scratch_shapes=[pltpu.VMEM((SSM_WIDTH // LANES, tm, LANES), F32)],
        compiler_params=_cparams(2),
        name="merge",
    )(ysf4, ysb4, *toks, tab, *s5w, *mats)


def _table(mod, d, chunks, extra):
    rows = [mod[:, i * d:(i + 1) * d] for i in chunks]
    rows += [jnp.broadcast_to(e[None, :], (8, d)) for e in extra]
    rows += [jnp.zeros((8, d), F32)] * (8 - len(rows))
    return jnp.stack(rows, axis=1)


def kernel(x, c, ctx, c_ctx, w_mod, b_mod, ln_g, ln_b, ffn_w_gate, ffn_w_up, ffn_w_down, w_in, ssm_log_dt, ssm_a_re, ssm_a_im, ssm_b_re, ssm_b_im, ssm_c_re, ssm_c_im, ssm_d, ssm_w_glu, ssm_b_glu, na_rpb, w_br_ssm, w_br_fnet, w_br_na, w_out):
    bsz, n, d = x.shape
    lc = ctx.shape[1]
    depth = w_mod.shape[0]
    rows = n // GRID_W
    assert bsz + 1 <= 8 and rows % NA_QROWS == 0 and rows >= NA_KROWS and n % FNET_N2 == 0
    alpha = (2 * depth) ** 0.25
    q_scale = NA_HEAD_DIM ** -0.5 * LOG2E

    c8 = jnp.concatenate([c, c_ctx[None, :], jnp.zeros((8 - bsz - 1, d), F32)], axis=0)
    mod_all = _mod_call(c8, w_mod, b_mod)

    wg = ffn_w_gate.astype(BF)
    wu = ffn_w_up.astype(BF)
    wd = ffn_w_down.astype(BF)
    win = w_in.astype(BF)
    mats = (w_br_ssm.astype(BF), w_br_fnet.astype(BF), w_br_na.astype(BF), w_out.astype(BF))
    s5w = (ssm_d[:, None, :], ssm_w_glu.astype(BF), ssm_b_glu[:, None, :])
    s5_tabs = _s5_tables(ssm_log_dt, ssm_a_re, ssm_a_im, ssm_b_re, ssm_b_im, ssm_c_re, ssm_c_im)
    na_bias = _na_bias(na_rpb, rows)
    cos_t, sin_t = _rope_tables(n)
    cos_c = jnp.ones((bsz * lc, NA_WIDTH), F32)
    sin_c = jnp.zeros((bsz * lc, NA_WIDTH), F32)
    zero_state = jnp.zeros((bsz, 8, 2 * SSM_FLAT), F32)

    h = x
    hc = ctx.reshape(1, bsz * lc, d)
    for l in range(depth):
        last = l == depth - 1
        mod = mod_all[l]

        tab = _table(mod, d, (0, 1, 2), (ln_g[l, 0], ln_b[l, 0]))
        h = _ffn_call(h, tab[:bsz], wg, wu, wd, l, 0, alpha)
        hc = _ffn_call(hc, tab[bsz:bsz + 1], wg, wu, wd, l, 0, alpha)

        tab = _table(mod, d, (3, 4, 5), (ln_g[l, 1], ln_b[l, 1]))
        us, u4, k, v, f, q, g = _mixin_call(h, tab[:bsz], win, l, cos_t, sin_t, q_scale)
        usc, u4c, kc, vc, fc, qc, gc = _mixin_call(hc, tab[bsz:bsz + 1], win, l, cos_c, sin_c, q_scale)
        u4c, kc, vc, fc, qc = [a.reshape(bsz, a.shape[1] // bsz, a.shape[-1]) for a in (u4c, kc, vc, fc, qc)]

        ysbc, hcb = _s5_call(u4c, zero_state, s5_tabs, l, 1, u4c.shape[1])
        ysfc, hcf = _s5_call(u4c, zero_state, s5_tabs, l, 0, u4c.shape[1])
        ysb, _ = _s5_call(u4, hcb, s5_tabs, l, 1, min(S5_ROWS, u4.shape[1]))
        ysf, _ = _s5_call(u4, hcf, s5_tabs, l, 0, min(S5_ROWS, u4.shape[1]))

        yf = _fnet_call(f)
        yn = _natten_call(q, k, v, kc, vc, na_bias, l)

        h = _merge_call(ysf, ysb, us, yf, yn, g, h, tab[:bsz], s5w, mats, l, alpha)

        tab3 = _table(mod, d, (6, 7, 8), (ln_g[l, 2], ln_b[l, 2]))
        h = _ffn_call(h, tab3[:bsz], wg, wu, wd, l, 1, alpha)

        if not last:
            yfc = _fnet_dense_call(fc)
            ync = _ctx_atten_call(qc, kc, vc)
            flat = lambda a: a.reshape(1, a.shape[0] * a.shape[1], a.shape[-1])
            hc = _merge_call(flat(ysfc), flat(ysbc), usc, flat(yfc), flat(ync), gc, hc, tab[bsz:bsz + 1],
                             s5w, mats, l, alpha)
            hc = _ffn_call(hc, tab3[bsz:bsz + 1], wg, wu, wd, l, 1, alpha)
    return h
```

```python
import functools
import math

import jax
import jax.numpy as jnp
import numpy as np
from jax import lax
from jax.experimental import pallas as pl
from jax.experimental.pallas import tpu as pltpu

BF = jnp.bfloat16
F32 = jnp.float32

GRID_W = 64
SSM_GROUPS = 16
SSM_GROUP_CH = 16
SSM_STATE = 64
SSM_WIDTH = SSM_GROUPS * SSM_GROUP_CH
SSM_FLAT = SSM_GROUPS * SSM_STATE
FNET_GROUPS = 4
FNET_GROUP_CH = 64
FNET_WIDTH = FNET_GROUPS * FNET_GROUP_CH
NA_HEADS = 8
NA_HEAD_DIM = 64
NA_WIDTH = NA_HEADS * NA_HEAD_DIM
WIN_ROWS = 8
WIN_COLS = 16
ROPE_THETA = 10000.0
N_MOD = 9
LN_EPS = 1e-6
LOG2E = 1.4426950408889634
NEG_BIG = -1e30

VMEM_LIMIT_BYTES = 56 * 1024 * 1024
ROW_TILE = 512
FFN_ROWS = 1024
FFN_CHUNK = 256
LANES = 128
SCAN_ROWS = 8
S5_BLOCK = 4
S5_ROWS = 256
NA_QROWS = 8
NA_KROWS = 16
NA_HQ = 4
FNET_N2 = 128


def _cparams(n_axes):
    return pltpu.CompilerParams(dimension_semantics=("arbitrary",) * n_axes,
                                vmem_limit_bytes=VMEM_LIMIT_BYTES)


def _ln(x):
    mu = jnp.mean(x, axis=-1, keepdims=True)
    xc = x - mu
    var = jnp.mean(xc * xc, axis=-1, keepdims=True)
    return xc * lax.rsqrt(var + LN_EPS)


def _dot(a, b):
    return jnp.dot(a, b, preferred_element_type=F32)


def _dot_t(a, b):
    return lax.dot_general(a, b, (((1,), (1,)), ((), ())), preferred_element_type=F32)


def _layer_spec(a, *lead, mode=None):
    nl = len(lead)
    zeros = (0,) * (a.ndim - nl)
    return pl.BlockSpec((None,) * nl + tuple(a.shape[nl:]), lambda *_: tuple(lead) + zeros, pipeline_mode=mode)


def _mod_kernel(c_ref, w_ref, b_ref, o_ref):
    c = c_ref[...]
    s = (c * jax.nn.sigmoid(c)).astype(BF)
    o_ref[...] = _dot(s, w_ref[...].astype(BF)) + b_ref[...]


def _mod_call(c8, w_mod, b_mod):
    depth, d, nd = w_mod.shape
    nb = nd // d
    return pl.pallas_call(
        _mod_kernel,
        grid=(depth, nb),
        in_specs=[pl.BlockSpec((8, d), lambda l, j: (0, 0)),
                  pl.BlockSpec((None, d, d), lambda l, j: (l, 0, j)),
                  pl.BlockSpec((None, 1, d), lambda l, j: (l, 0, j))],
        out_specs=pl.BlockSpec((None, 8, d), lambda l, j: (l, 0, j)),
        out_shape=jax.ShapeDtypeStruct((depth, 8, nd), F32),
        compiler_params=_cparams(2),
        name="mod",
    )(c8, w_mod, b_mod.reshape(depth, 1, nd))


def _ffn_kernel(x_ref, t_ref, wg_ref, wu_ref, wd_ref, o_ref, *, alpha):
    t = t_ref[...]
    for r0 in range(0, x_ref.shape[0], ROW_TILE):
        x = x_ref[r0:r0 + ROW_TILE, :]
        u = (_ln(x) * (1.0 + t[1:2]) + t[0:1]).astype(BF)
        acc = jnp.zeros(x.shape, F32)
        for c0 in range(0, wg_ref.shape[1], FFN_CHUNK):
            a = _dot(u, wg_ref[:, c0:c0 + FFN_CHUNK])
            b = _dot(u, wu_ref[:, c0:c0 + FFN_CHUNK])
            act = (a * jax.nn.sigmoid(a) * b).astype(BF)
            acc = acc + _dot(act, wd_ref[c0:c0 + FFN_CHUNK, :])
        v = alpha * x + (0.5 * t[2:3]) * acc
        o_ref[r0:r0 + ROW_TILE, :] = _ln(v) * t[3:4] + t[4:5]


def _ffn_call(x, tab, wg, wu, wd, l, k, alpha):
    nb, n, d = x.shape
    tm = min(FFN_ROWS, n)
    once = pl.Buffered(1)
    return pl.pallas_call(
        functools.partial(_ffn_kernel, alpha=alpha),
        grid=(nb, n // tm),
        in_specs=[pl.BlockSpec((None, tm, d), lambda b, i: (b, i, 0)),
                  pl.BlockSpec((None, 8, d), lambda b, i: (b, 0, 0)),
                  _layer_spec(wg, l, k, mode=once), _layer_spec(wu, l, k, mode=once),
                  _layer_spec(wd, l, k, mode=once)],
        out_specs=pl.BlockSpec((None, tm, d), lambda b, i: (b, i, 0)),
        out_shape=jax.ShapeDtypeStruct(x.shape, F32),
        compiler_params=_cparams(2),
        name="ffn",
    )(x, tab, wg, wu, wd)


def _rope(x, cos, sin_signed):
    n = x.shape[-1]
    lane = lax.broadcasted_iota(jnp.int32, x.shape, 1)
    first = (lane % 32) < 16
    partner = jnp.where(first, pltpu.roll(x, n - 16, axis=1), pltpu.roll(x, 16, axis=1))
    return x * cos + partner * sin_signed


def _mixin_kernel(x_ref, t_ref, w_ref, cos_ref, sin_ref,
                  us_ref, u4_ref, k_ref, v_ref, f_ref, q_ref, g_ref, us_f32, *, q_scale):
    x = x_ref[...]
    t = t_ref[...]
    u = (_ln(x) * (1.0 + t[1:2]) + t[0:1]).astype(BF)
    cos = cos_ref[...]
    sin = sin_ref[...]
    c_k = SSM_WIDTH
    c_v = c_k + NA_WIDTH
    c_f = c_v + NA_WIDTH
    c_q = c_f + FNET_WIDTH
    c_g = c_q + NA_WIDTH
    zs = _dot(u, w_ref[:, 0:c_k])
    us_ref[...] = zs.astype(BF)
    rows4 = us_f32.shape[1] // S5_BLOCK
    for hf in range(c_k // LANES):
        us_f32[hf] = zs[:, hf * LANES:(hf + 1) * LANES]
    for j in range(S5_BLOCK):
        for hf in range(c_k // LANES):
            c0 = j * c_k + hf * LANES
            u4_ref[:, c0:c0 + LANES] = us_f32[hf, pl.ds(j, rows4, stride=S5_BLOCK), :].astype(BF)
    k_ref[...] = _rope(_dot(u, w_ref[:, c_k:c_v]), cos, sin).astype(BF)
    v_ref[...] = _dot(u, w_ref[:, c_v:c_f]).astype(BF)
    f_ref[...] = _dot(u, w_ref[:, c_f:c_q]).astype(BF)
    q_ref[...] = (_rope(_dot(u, w_ref[:, c_q:c_g]), cos, sin) * q_scale).astype(BF)
    gw = NA_WIDTH
    for c0 in range(0, g_ref.shape[-1], gw):
        g = _dot(u, w_ref[:, c_g + c0:c_g + c0 + gw])
        g_ref[:, c0:c0 + gw] = jax.nn.sigmoid(g).astype(BF)


def _mixin_call(x, tab, w_in, l, cos_t, sin_t, q_scale):
    nb, n, d = x.shape
    win = w_in.shape[-1]
    ng = win - (SSM_WIDTH + 3 * NA_WIDTH + FNET_WIDTH)
    tm = min(ROW_TILE, n)
    nt_pos = cos_t.shape[0] // tm

    def tok(width):
        return pl.BlockSpec((None, tm, width), lambda b, i: (b, i, 0))

    def shape(width):
        return jax.ShapeDtypeStruct((nb, n, width), BF)

    u4_spec = pl.BlockSpec((None, tm // S5_BLOCK, S5_BLOCK * SSM_WIDTH), lambda b, i: (b, i, 0))
    u4_shape = jax.ShapeDtypeStruct((nb, n // S5_BLOCK, S5_BLOCK * SSM_WIDTH), BF)
    return pl.pallas_call(
        functools.partial(_mixin_kernel, q_scale=q_scale),
        grid=(nb, n // tm),
        in_specs=[tok(d),
                  pl.BlockSpec((None, 8, d), lambda b, i: (b, 0, 0)),
                  _layer_spec(w_in, l),
                  pl.BlockSpec((tm, NA_WIDTH), lambda b, i: (i % nt_pos, 0)),
                  pl.BlockSpec((tm, NA_WIDTH), lambda b, i: (i % nt_pos, 0))],
        out_specs=[tok(SSM_WIDTH), u4_spec, tok(NA_WIDTH), tok(NA_WIDTH), tok(FNET_WIDTH), tok(NA_WIDTH),
                   tok(ng)],
        out_shape=[shape(SSM_WIDTH), u4_shape, shape(NA_WIDTH), shape(NA_WIDTH), shape(FNET_WIDTH),
                   shape(NA_WIDTH), shape(ng)],
        scratch_shapes=[pltpu.VMEM((SSM_WIDTH // LANES, tm, LANES), F32)],
        compiler_params=_cparams(2),
        name="mixin",
    )(x, tab, w_in, cos_t, sin_t)


def _s5_kernel(u_ref, h0_ref, wst_ref, wconv_ref, wout_ref, sc_ref, y_ref, hfin_ref, hbuf, hprev, carry,
               *, reverse):
    i = pl.program_id(1)
    ns = SSM_FLAT
    n_tiles = hbuf.shape[0] // SCAN_ROWS

    @pl.when(i == 0)
    def _():
        carry[...] = h0_ref[...]

    u = u_ref[...]
    hbuf[...] = _dot(u, wst_ref[...])
    row_id = lax.broadcasted_iota(jnp.int32, (SCAN_ROWS, ns), 0)
    edge = row_id == ((SCAN_ROWS - 1) if reverse else 0)

    def tile(it, c):
        cr, ci = c
        row = (n_tiles - 1 - it) if reverse else it
        r0 = pl.multiple_of(row * SCAN_ROWS, SCAN_ROWS)
        xr = hbuf[pl.ds(r0, SCAN_ROWS), 0:ns]
        xi = hbuf[pl.ds(r0, SCAN_ROWS), ns:2 * ns]
        for k, s in enumerate((1, 2, 4)):
            ar = sc_ref[k, :, 0:ns]
            ai = sc_ref[k, :, ns:2 * ns]
            shift = (SCAN_ROWS - s) if reverse else s
            sr = pltpu.roll(xr, shift, axis=0)
            si = pltpu.roll(xi, shift, axis=0)
            xr, xi = xr + ar * sr - ai * si, xi + ar * si + ai * sr
        pr = sc_ref[3, :, 0:ns]
        pi = sc_ref[3, :, ns:2 * ns]
        hr = xr + pr * cr - pi * ci
        hi = xi + pr * ci + pi * cr
        one = (SCAN_ROWS - 1) if reverse else 1
        hprev[pl.ds(r0, SCAN_ROWS), 0:ns] = jnp.where(edge, cr, pltpu.roll(hr, one, axis=0))
        hprev[pl.ds(r0, SCAN_ROWS), ns:2 * ns] = jnp.where(edge, ci, pltpu.roll(hi, one, axis=0))
        last = 0 if reverse else SCAN_ROWS - 1
        return (jnp.broadcast_to(hr[last:last + 1, :], hr.shape),
                jnp.broadcast_to(hi[last:last + 1, :], hi.shape))

    cr, ci = lax.fori_loop(0, n_tiles, tile, (carry[:, 0:ns], carry[:, ns:2 * ns]))
    carry[:, 0:ns] = cr
    carry[:, ns:2 * ns] = ci
    hfin_ref[...] = carry[...]

    y = _dot(u, wconv_ref[...]) + _dot(hprev[...].astype(BF), wout_ref[...])
    y_ref[...] = y.astype(y_ref.dtype)


def _s5_call(u4, h0, tables, l, z, rows):
    wst, wconv, wout, sc = tables
    b, n, w = u4.shape
    nc = n // rows
    ns2 = 2 * SSM_FLAT
    reverse = z == 1
    if reverse:
        cmap = lambda bb, i: (bb, nc - 1 - i, 0)
    else:
        cmap = lambda bb, i: (bb, i, 0)
    return pl.pallas_call(
        functools.partial(_s5_kernel, reverse=reverse),
        grid=(b, nc),
        in_specs=[pl.BlockSpec((None, rows, w), cmap),
                  pl.BlockSpec((None, 8, ns2), lambda bb, i: (bb, 0, 0)),
                  _layer_spec(wst, l, z), _layer_spec(wconv, l, z), _layer_spec(wout, l, z),
                  _layer_spec(sc, l, z)],
        out_specs=[pl.BlockSpec((None, rows, w), cmap),
                   pl.BlockSpec((None, 8, ns2), lambda bb, i: (bb, 0, 0))],
        out_shape=[jax.ShapeDtypeStruct((b, n, w), BF),
                   jax.ShapeDtypeStruct((b, 8, ns2), F32)],
        scratch_shapes=[pltpu.VMEM((rows, ns2), F32), pltpu.VMEM((rows, ns2), F32), pltpu.VMEM((8, ns2), F32)],
        compiler_params=_cparams(2),
        name="s5_bwd" if reverse else "s5_fwd",
    )(u4, h0, wst, wconv, wout, sc)


def _s5_tables(log_dt, a_re, a_im, b_re, b_im, c_re, c_im):
    depth = log_dt.shape[0]
    g, p, c = b_re.shape[-3:]
    tb = S5_BLOCK
    hp = lax.Precision.HIGHEST
    f32 = lambda a: a.astype(F32)
    a_re, a_im, b_re, b_im, c_re, c_im = map(f32, (a_re, a_im, b_re, b_im, c_re, c_im))
    dt = jnp.exp(f32(log_dt))[..., None]
    adt_re = a_re * dt
    adt_im = a_im * dt

    def apow(k):
        kk = jnp.asarray(k, F32)[None, :, :, None, None]
        mag = jnp.exp(kk * adt_re[:, :, None])
        ang = kk * adt_im[:, :, None]
        return mag * jnp.cos(ang), mag * jnp.sin(ang)

    mag = jnp.exp(adt_re)
    ab_re = mag * jnp.cos(adt_im)
    ab_im = mag * jnp.sin(adt_im)
    den = a_re * a_re + a_im * a_im
    nr = ab_re - 1.0
    fr = (nr * a_re + ab_im * a_im) / den
    fi = (ab_im * a_re - nr * a_im) / den
    bb_re = fr[..., None] * b_re - fi[..., None] * b_im
    bb_im = fr[..., None] * b_im + fi[..., None] * b_re

    step = np.arange(tb)
    k_state = np.stack([tb - 1 - step, step])
    k_out = np.stack([step + 1, tb - step])
    k_tap = np.stack([step, step])
    k_scan = np.stack([tb * (np.arange(SCAN_ROWS) + 1)] * 2)

    def block_diag(compact, row_group, col_width):
        n = compact.shape[3]
        dst = np.arange(n * g * col_width)
        src = (dst // (g * col_width)) * col_width + dst % col_width
        copy = np.zeros((n * col_width, n * g * col_width), np.float32)
        copy[src, dst] = 1.0
        cols = jnp.einsum('dzrk,kn->dzrn', compact.reshape(compact.shape[:3] + (n * col_width,)),
                          jnp.asarray(copy), precision=hp)
        keep = jnp.asarray(row_group[:, None] == ((dst // col_width) % g)[None, :])
        return jnp.where(keep, cols, 0.0).astype(BF)

    chan_rows = (np.arange(tb * g * c) // c) % g
    state_rows = (np.arange(2 * g * p) // p) % g

    def cmul_b(pr, pi):
        return (pr[..., None] * bb_re[:, :, None] - pi[..., None] * bb_im[:, :, None],
                pr[..., None] * bb_im[:, :, None] + pi[..., None] * bb_re[:, :, None])

    wr, wi = cmul_b(*apow(k_state))
    w2 = jnp.stack([wr, wi], axis=3)
    w2 = jnp.transpose(w2, (0, 1, 2, 4, 6, 3, 5))
    wst = block_diag(w2.reshape(depth, 2, tb * g * c, 2, p), chan_rows, p)

    pr, pi = apow(k_out)
    cm_re = c_re[:, :, None] * pr[:, :, :, :, None, :] - c_im[:, :, None] * pi[:, :, :, :, None, :]
    cm_im = c_re[:, :, None] * pi[:, :, :, :, None, :] + c_im[:, :, None] * pr[:, :, :, :, None, :]
    cm = jnp.stack([cm_re, -cm_im], axis=3)
    cm = jnp.transpose(cm, (0, 1, 3, 4, 6, 2, 5))
    wout = block_diag(cm.reshape(depth, 2, 2 * g * p, tb, c), state_rows, c)

    abr, abi = cmul_b(*apow(k_tap))
    taps = (jnp.einsum('dzgcp,dztgpe->dztgec', c_re, abr, precision=hp)
            - jnp.einsum('dzgcp,dztgpe->dztgec', c_im, abi, precision=hp))
    sel = np.zeros((2, tb, tb, tb), np.float32)
    for i in range(tb):
        for j in range(tb):
            if j >= i:
                sel[0, i, j, j - i] = 1.0
            if i >= j:
                sel[1, i, j, i - j] = 1.0
    kij = jnp.einsum('dztgec,zijt->dzigejc', taps, jnp.asarray(sel), precision=hp)
    wconv = block_diag(kij.reshape(depth, 2, tb * g * c, tb, c), chan_rows, c)

    qr, qi = apow(k_scan)
    qr = qr.reshape(depth, 2, SCAN_ROWS, g * p)
    qi = qi.reshape(depth, 2, SCAN_ROWS, g * p)
    rows = np.arange(SCAN_ROWS)
    tabs = []
    for s in (1, 2, 4):
        keep = jnp.asarray(np.stack([rows >= s, rows + s <= SCAN_ROWS - 1]), F32)[None, :, :, None]
        tabs.append(jnp.concatenate([keep * qr[:, :, s - 1:s], keep * qi[:, :, s - 1:s]], axis=-1))
    carry_pow = lambda t: jnp.stack([t[:, 0], t[:, 1, ::-1]], axis=1)
    tabs.append(jnp.concatenate([carry_pow(qr), carry_pow(qi)], axis=-1))
    sc = jnp.stack(tabs, axis=2)
    return wst, wconv, wout, sc


def _fnet1_kernel(x_ref, f2_ref, tc_ref, ts_ref, o_ref):
    n2 = x_ref.shape[0]
    y = _dot(f2_ref[...], x_ref[...])
    br = y[0:n2]
    bi = y[n2:2 * n2]
    tc = tc_ref[...]
    ts = ts_ref[...]
    o_ref[0] = (br * tc + bi * ts).astype(BF)
    o_ref[1] = (bi * tc - br * ts).astype(BF)


def _fnet2_kernel(b_ref, f1_ref, cc_ref, sc_ref, o_ref, *, scale):
    kb = b_ref.shape[1]
    n1 = b_ref.shape[2]
    c = b_ref.shape[3]
    zr, zi = [], []
    for j in range(kb):
        rhs = jnp.concatenate([b_ref[0, j], b_ref[1, j]], axis=0)
        z = _dot(f1_ref[...], rhs)
        zr.append(z[0:n1])
        zi.append(z[n1:2 * n1])
    zr = jnp.concatenate(zr, axis=0).astype(BF)
    zi = jnp.concatenate(zi, axis=0).astype(BF)
    y = (_dot(zr, cc_ref[...]) + _dot(zi, sc_ref[...])) * scale
    for j in range(kb):
        o_ref[:, j * c:(j + 1) * c] = y[j * n1:(j + 1) * n1].astype(o_ref.dtype)


def _dft_tables(n):
    idx = np.arange(n)
    ang = 2.0 * np.pi * ((idx[:, None] * idx[None, :]) % n) / n
    return np.cos(ang), np.sin(ang)


def _chan_tables():
    cc, sc = _dft_tables(FNET_GROUP_CH)
    eye = np.eye(FNET_GROUPS)
    return jnp.asarray(np.kron(eye, cc), BF), jnp.asarray(np.kron(eye, sc), BF)


def _fnet_call(f):
    b, n, c = f.shape
    n2 = FNET_N2
    n1 = n // n2
    c2, s2 = _dft_tables(n2)
    f2 = jnp.asarray(np.concatenate([c2, -s2], axis=0), BF)
    k2 = np.arange(n2)[:, None]
    nn1 = np.arange(n1)[None, :]
    ang = 2.0 * np.pi * ((k2 * nn1) % n) / n
    tc = jnp.asarray(np.repeat(np.cos(ang), c, axis=1), F32)
    ts = jnp.asarray(np.repeat(np.sin(ang), c, axis=1), F32)
    c1, s1 = _dft_tables(n1)
    f1 = jnp.asarray(np.block([[c1, s1], [-s1, c1]]), BF)
    cc, sc = _chan_tables()

    wc = min(n1 * c, 4096)
    x2 = f.reshape(b, n2, n1 * c)
    st1 = pl.pallas_call(
        _fnet1_kernel,
        grid=(b, (n1 * c) // wc),
        in_specs=[pl.BlockSpec((None, n2, wc), lambda bb, j: (bb, 0, j)),
                  pl.BlockSpec((2 * n2, n2), lambda bb, j: (0, 0)),
                  pl.BlockSpec((n2, wc), lambda bb, j: (0, j)),
                  pl.BlockSpec((n2, wc), lambda bb, j: (0, j))],
        out_specs=pl.BlockSpec((None, 2, n2, wc), lambda bb, j: (bb, 0, 0, j)),
        out_shape=jax.ShapeDtypeStruct((b, 2, n2, n1 * c), BF),
        compiler_params=_cparams(2),
        name="fnet1",
    )(x2, f2, tc, ts)
    st1 = st1.reshape(b, 2, n2, n1, c)
    kb = 8
    y = pl.pallas_call(
        functools.partial(_fnet2_kernel, scale=1.0 / math.sqrt(n * FNET_GROUP_CH)),
        grid=(b, n2 // kb),
        in_specs=[pl.BlockSpec((None, 2, kb, n1, c), lambda bb, j: (bb, 0, j, 0, 0)),
                  pl.BlockSpec((2 * n1, 2 * n1), lambda bb, j: (0, 0)),
                  pl.BlockSpec((c, c), lambda bb, j: (0, 0)),
                  pl.BlockSpec((c, c), lambda bb, j: (0, 0))],
        out_specs=pl.BlockSpec((None, n1, kb * c), lambda bb, j: (bb, 0, j)),
        out_shape=jax.ShapeDtypeStruct((b, n1, n2 * c), BF),
        compiler_params=_cparams(2),
        name="fnet2",
    )(st1, f1, cc, sc)
    return y.reshape(b, n, c)


def _fnet_dense_kernel(x_ref, cl_ref, sl_ref, cc_ref, sc_ref, o_ref, *, scale):
    x = x_ref[...]
    xr = _dot(x, cc_ref[...]).astype(BF)
    xi = _dot(x, sc_ref[...]).astype(BF)
    o_ref[...] = ((_dot(cl_ref[...], xr) - _dot(sl_ref[...], xi)) * scale).astype(o_ref.dtype)


def _fnet_dense_call(f):
    b, n, c = f.shape
    cl, sl = _dft_tables(n)
    cc, sc = _chan_tables()
    full = lambda bb: (0, 0)
    return pl.pallas_call(
        functools.partial(_fnet_dense_kernel, scale=1.0 / math.sqrt(n * FNET_GROUP_CH)),
        grid=(b,),
        in_specs=[pl.BlockSpec((None, n, c), lambda bb: (bb, 0, 0)),
                  pl.BlockSpec((n, n), full), pl.BlockSpec((n, n), full),
                  pl.BlockSpec((c, c), full), pl.BlockSpec((c, c), full)],
        out_specs=pl.BlockSpec((None, n, c), lambda bb: (bb, 0, 0)),
        out_shape=jax.ShapeDtypeStruct((b, n, c), BF),
        compiler_params=_cparams(1),
        name="fnet_ctx",
    )(f, jnp.asarray(cl, BF), jnp.asarray(sl, BF), cc, sc)


def _head_masks(width):
    lane = lax.broadcasted_iota(jnp.int32, (1, width), 1)
    return [(lane // NA_HEAD_DIM) == h for h in range(width // NA_HEAD_DIM)]


def _attend(q, k_parts, v_parts, bias_of, o_ref):
    zero = jnp.zeros((), BF)
    out = jnp.zeros(q.shape, F32)
    for h, hm in enumerate(_head_masks(q.shape[-1])):
        s_parts = []
        for j, kp in enumerate(k_parts):
            s = _dot_t(q, jnp.where(hm, kp, zero))
            b = bias_of(h, j)
            s_parts.append(s if b is None else s + b)
        m = s_parts[0].max(axis=-1, keepdims=True)
        for s in s_parts[1:]:
            m = jnp.maximum(m, s.max(axis=-1, keepdims=True))
        den = jnp.zeros_like(m)
        acc = jnp.zeros(q.shape, F32)
        for s, vp in zip(s_parts, v_parts):
            p = jnp.exp2(s - m)
            den = den + p.sum(axis=-1, keepdims=True)
            acc = acc + _dot(p.astype(BF), jnp.where(hm, vp, zero))
        out = out + acc / den
    o_ref[...] = out.astype(o_ref.dtype)


def _natten_kernel(q_ref, k0, k1, k2, k3, v0, v1, v2, v3, kc_ref, vc_ref, bias_ref, o_ref):
    blk = k0.shape[0]
    k_parts = [r[...] for r in (k0, k1, k2, k3)] + [kc_ref[...]]
    v_parts = [r[...] for r in (v0, v1, v2, v3)] + [vc_ref[...]]

    def bias_of(h, j):
        if j >= 4:
            return None
        return bias_ref[h, :, j * blk:(j + 1) * blk].astype(F32)

    _attend(q_ref[...], k_parts, v_parts, bias_of, o_ref)


def _natten_call(q, k, v, kc, vc, bias, l):
    b, n, w = q.shape
    lc = kc.shape[1]
    hw = NA_HQ * NA_HEAD_DIM
    qt = NA_QROWS * GRID_W
    blk = (NA_KROWS // 4) * GRID_W
    nj = n // qt
    nblk = n // blk
    ratio = qt // blk

    def kv_spec(m):
        def imap(hq, j, bb):
            start = jnp.clip(ratio * j - 1, 0, nblk - 4)
            return (bb, start + m, hq)
        return pl.BlockSpec((None, blk, hw), imap)

    def bias_map(hq, j, bb):
        var = jnp.where(j == 0, 0, jnp.where(j == nj - 1, 2, 1))
        return (l, var, hq, 0, 0)

    return pl.pallas_call(
        _natten_kernel,
        grid=(w // hw, nj, b),
        in_specs=[pl.BlockSpec((None, qt, hw), lambda hq, j, bb: (bb, j, hq))]
                 + [kv_spec(m) for m in range(4)] + [kv_spec(m) for m in range(4)]
                 + [pl.BlockSpec((None, lc, hw), lambda hq, j, bb: (bb, 0, hq)),
                    pl.BlockSpec((None, lc, hw), lambda hq, j, bb: (bb, 0, hq)),
                    pl.BlockSpec((None, None, NA_HQ, qt, 4 * blk), bias_map)],
        out_specs=pl.BlockSpec((None, qt, hw), lambda hq, j, bb: (bb, j, hq)),
        out_shape=jax.ShapeDtypeStruct((b, n, w), BF),
        compiler_params=_cparams(3),
        name="natten",
    )(q, k, k, k, k, v, v, v, v, kc, vc, bias)


def _ctx_atten_kernel(q_ref, kc_ref, vc_ref, o_ref):
    _attend(q_ref[...], [kc_ref[...]], [vc_ref[...]], lambda h, j: None, o_ref)


def _ctx_atten_call(qc, kc, vc):
    b, lc, w = qc.shape
    hw = NA_HQ * NA_HEAD_DIM
    spec = pl.BlockSpec((None, lc, hw), lambda bb, hq: (bb, 0, hq))
    return pl.pallas_call(
        _ctx_atten_kernel,
        grid=(b, w // hw),
        in_specs=[spec, spec, spec],
        out_specs=spec,
        out_shape=jax.ShapeDtypeStruct((b, lc, w), BF),
        compiler_params=_cparams(2),
        name="ctx_atten",
    )(qc, kc, vc)


def _na_bias(rpb, rows):
    w = GRID_W
    nj = rows // NA_QROWS
    n_dr = 2 * WIN_ROWS - 1
    n_dc = 2 * WIN_COLS - 1
    cq = np.arange(w)
    cs = np.clip(cq - WIN_COLS // 2, 0, w - WIN_COLS)
    ck = np.arange(w)
    valid_c = (ck[None, :] >= cs[:, None]) & (ck[None, :] < cs[:, None] + WIN_COLS)
    dc = ck[None, :] - cq[:, None] + (WIN_COLS - 1)
    pick_c = (dc[:, :, None] == np.arange(n_dc)) & valid_c[:, :, None]
    tiles = jnp.einsum('lhab,cdb->lhacd', rpb.astype(F32) * LOG2E, jnp.asarray(pick_c, F32),
                       precision=lax.Precision.HIGHEST)
    tiles = tiles + jnp.asarray(np.where(valid_c, 0.0, NEG_BIG), F32)
    tiles = jnp.concatenate([tiles, jnp.full_like(tiles[:, :, :1], NEG_BIG)], axis=2).astype(BF)
    out = []
    for j in (0, min(1, nj - 1), nj - 1):
        rq = NA_QROWS * j + np.arange(NA_QROWS)
        ws = int(np.clip(NA_QROWS * j - (NA_KROWS - NA_QROWS) // 2, 0, rows - NA_KROWS))
        rk = ws + np.arange(NA_KROWS)
        r0 = np.clip(rq - WIN_ROWS // 2, 0, rows - WIN_ROWS)
        valid_r = (rk[None, :] >= r0[:, None]) & (rk[None, :] < r0[:, None] + WIN_ROWS)
        dr = np.where(valid_r, rk[None, :] - rq[:, None] + (WIN_ROWS - 1), n_dr)
        out.append(jnp.concatenate(
            [jnp.concatenate([tiles[:, :, int(dr[a, b])] for b in range(NA_KROWS)], axis=-1)
             for a in range(NA_QROWS)], axis=-2))
    return jnp.stack(out, axis=1)


def _rope_tables(n):
    nf = NA_HEAD_DIM // 4
    t = jnp.arange(n, dtype=jnp.int32)
    pos = jnp.stack([t // GRID_W, t % GRID_W], axis=-1).astype(F32)
    inv_freq = ROPE_THETA ** (-jnp.arange(nf, dtype=F32) / nf)
    ang = pos[:, :, None] * inv_freq
    cos = jnp.cos(ang)
    sin = jnp.sin(ang)
    cos_h = jnp.concatenate([cos, cos], axis=-1).reshape(n, NA_HEAD_DIM)
    sin_h = jnp.concatenate([-sin, sin], axis=-1).reshape(n, NA_HEAD_DIM)
    return jnp.tile(cos_h, (1, NA_HEADS)), jnp.tile(sin_h, (1, NA_HEADS))


def _merge_kernel(ysf_ref, ysb_ref, us_ref, yf_ref, yn_ref, g_ref, h_ref, t_ref,
                  sd_ref, wglu_ref, bglu_ref, wbs_ref, wbf_ref, wbn_ref, wo_ref, o_ref, ys_f32, *, alpha):
    d = h_ref.shape[-1]
    t = t_ref[...]
    y4 = ysf_ref[...].astype(F32) + ysb_ref[...].astype(F32)
    rows4 = y4.shape[0]
    halves = SSM_WIDTH // LANES
    for j in range(S5_BLOCK):
        for hf in range(halves):
            c0 = j * SSM_WIDTH + hf * LANES
            ys_f32[hf, pl.ds(j, rows4, stride=S5_BLOCK), :] = y4[:, c0:c0 + LANES]
    ys = jnp.concatenate([ys_f32[hf] for hf in range(halves)], axis=-1)
    ys = ys + sd_ref[...] * us_ref[...].astype(F32)
    ys = jax.nn.gelu(ys)
    ys = (ys * jax.nn.sigmoid(_dot(ys.astype(BF), wglu_ref[...]) + bglu_ref[...])).astype(BF)
    y = (g_ref[:, 0:d].astype(F32) * _dot(ys, wbs_ref[...])
         + g_ref[:, d:2 * d].astype(F32) * _dot(yf_ref[...], wbf_ref[...])
         + g_ref[:, 2 * d:3 * d].astype(F32) * _dot(yn_ref[...], wbn_ref[...]))
    y = _dot(y.astype(BF), wo_ref[...])
    v = alpha * h_ref[...] + t[2:3] * y
    o_ref[...] = _ln(v) * t[3:4] + t[4:5]


def _merge_call(ysf4, ysb4, us, yf, yn, g, h, tab, s5w, mats, l, alpha):
    nb, n, d = h.shape
    tm = min(ROW_TILE, n)

    def tok(a):
        return pl.BlockSpec((None, tm, a.shape[-1]), lambda b, i: (b, i, 0))

    def tok4(a):
        return pl.BlockSpec((None, tm // S5_BLOCK, a.shape[-1]), lambda b, i: (b, i, 0))

    toks = (us, yf, yn, g, h)
    return pl.pallas_call(
        functools.partial(_merge_kernel, alpha=alpha),
        grid=(nb, n // tm),
        in_specs=[tok4(ysf4), tok4(ysb4)] + [tok(a) for a in toks]
                 + [pl.BlockSpec((None, 8, d), lambda b, i: (b, 0, 0))]
                 + [_layer_spec(a, l) for a in tuple(s5w) + tuple(mats)],
        out_specs=tok(h),
        out_shape=jax.ShapeDtypeStruct(h.shape, F32),
        scratch_shapes=[pltpu.VMEM((SSM_WIDTH // LANES, tm, LANES), F32)],
        compiler_params=_cparams(2),
        name="merge",
    )(ysf4, ysb4, *toks, tab, *s5w, *mats)


def _table(mod, d, chunks, extra):
    rows = [mod[:, i * d:(i + 1) * d] for i in chunks]
    rows += [jnp.broadcast_to(e[None, :], (8, d)) for e in extra]
    rows += [jnp.zeros((8, d), F32)] * (8 - len(rows))
    return jnp.stack(rows, axis=1)


def kernel(x, c, ctx, c_ctx, w_mod, b_mod, ln_g, ln_b, ffn_w_gate, ffn_w_up, ffn_w_down, w_in, ssm_log_dt, ssm_a_re, ssm_a_im, ssm_b_re, ssm_b_im, ssm_c_re, ssm_c_im, ssm_d, ssm_w_glu, ssm_b_glu, na_rpb, w_br_ssm, w_br_fnet, w_br_na, w_out):
    bsz, n, d = x.shape
    lc = ctx.shape[1]
    depth = w_mod.shape[0]
    rows = n // GRID_W
    assert bsz + 1 <= 8 and rows % NA_QROWS == 0 and rows >= NA_KROWS and n % FNET_N2 == 0
    alpha = (2 * depth) ** 0.25
    q_scale = NA_HEAD_DIM ** -0.5 * LOG2E

    c8 = jnp.concatenate([c, c_ctx[None, :], jnp.zeros((8 - bsz - 1, d), F32)], axis=0)
    mod_all = _mod_call(c8, w_mod, b_mod)

    wg = ffn_w_gate.astype(BF)
    wu = ffn_w_up.astype(BF)
    wd = ffn_w_down.astype(BF)
    win = w_in.astype(BF)
    mats = (w_br_ssm.astype(BF), w_br_fnet.astype(BF), w_br_na.astype(BF), w_out.astype(BF))
    s5w = (ssm_d[:, None, :], ssm_w_glu.astype(BF), ssm_b_glu[:, None, :])
    s5_tabs = _s5_tables(ssm_log_dt, ssm_a_re, ssm_a_im, ssm_b_re, ssm_b_im, ssm_c_re, ssm_c_im)
    na_bias = _na_bias(na_rpb, rows)
    cos_t, sin_t = _rope_tables(n)
    cos_c = jnp.ones((bsz * lc, NA_WIDTH), F32)
    sin_c = jnp.zeros((bsz * lc, NA_WIDTH), F32)
    zero_state = jnp.zeros((bsz, 8, 2 * SSM_FLAT), F32)

    h = x
    hc = ctx.reshape(1, bsz * lc, d)
    for l in range(depth):
        last = l == depth - 1
        mod = mod_all[l]

        tab = _table(mod, d, (0, 1, 2), (ln_g[l, 0], ln_b[l, 0]))
        h = _ffn_call(h, tab[:bsz], wg, wu, wd, l, 0, alpha)
        hc = _ffn_call(hc, tab[bsz:bsz + 1], wg, wu, wd, l, 0, alpha)

        tab = _table(mod, d, (3, 4, 5), (ln_g[l, 1], ln_b[l, 1]))
        us, u4, k, v, f, q, g = _mixin_call(h, tab[:bsz], win, l, cos_t, sin_t, q_scale)
        usc, u4c, kc, vc, fc, qc, gc = _mixin_call(hc, tab[bsz:bsz + 1], win, l, cos_c, sin_c, q_scale)
        u4c, kc, vc, fc, qc = [a.reshape(bsz, a.shape[1] // bsz, a.shape[-1]) for a in (u4c, kc, vc, fc, qc)]

        ysbc, hcb = _s5_call(u4c, zero_state, s5_tabs, l, 1, u4c.shape[1])
        ysfc, hcf = _s5_call(u4c, zero_state, s5_tabs, l, 0, u4c.shape[1])
        ysb, _ = _s5_call(u4, hcb, s5_tabs, l, 1, min(S5_ROWS, u4.shape[1]))
        ysf, _ = _s5_call(u4, hcf, s5_tabs, l, 0, min(S5_ROWS, u4.shape[1]))

        yf = _fnet_call(f)
        yn = _natten_call(q, k, v, kc, vc, na_bias, l)

        h = _merge_call(ysf, ysb, us, yf, yn, g, h, tab[:bsz], s5w, mats, l, alpha)

        tab3 = _table(mod, d, (6, 7, 8), (ln_g[l, 2], ln_b[l, 2]))
        h = _ffn_call(h, tab3[:bsz], wg, wu, wd, l, 1, alpha)

        if not last:
            yfc = _fnet_dense_call(fc)
            ync = _ctx_atten_call(qc, kc, vc)
            flat = lambda a: a.reshape(1, a.shape[0] * a.shape[1], a.shape[-1])
            hc = _merge_call(flat(ysfc), flat(ysbc), usc, flat(yfc), flat(ync), gc, hc, tab[bsz:bsz + 1],
                             s5w, mats, l, alpha)
            hc = _ffn_call(hc, tab3[bsz:bsz + 1], wg, wu, wd, l, 1, alpha)
    return h
```

```python
import functools
import math

import jax
import jax.numpy as jnp
import numpy as np
from jax import lax
from jax.experimental import pallas as pl
from jax.experimental.pallas import tpu as pltpu

BF = jnp.bfloat16
F32 = jnp.float32

GRID_W = 64
SSM_GROUPS = 16
SSM_GROUP_CH = 16
SSM_STATE = 64
SSM_WIDTH = SSM_GROUPS * SSM_GROUP_CH
SSM_FLAT = SSM_GROUPS * SSM_STATE
FNET_GROUPS = 4
FNET_GROUP_CH = 64
FNET_WIDTH = FNET_GROUPS * FNET_GROUP_CH
NA_HEADS = 8
NA_HEAD_DIM = 64
NA_WIDTH = NA_HEADS * NA_HEAD_DIM
WIN_ROWS = 8
WIN_COLS = 16
ROPE_THETA = 10000.0
N_MOD = 9
LN_EPS = 1e-6
LOG2E = 1.4426950408889634
NEG_BIG = -1e30

VMEM_LIMIT_BYTES = 56 * 1024 * 1024
ROW_TILE = 512
FFN_ROWS = 1024
FFN_CHUNK = 256
LANES = 128
SCAN_ROWS = 8
S5_BLOCK = 4
S5_ROWS = 512
S5_SUB = 256
NA_QROWS = 8
NA_KROWS = 16
NA_HQ = 4
FNET_N2 = 128


def _cparams(n_axes, flags=None):
    return pltpu.CompilerParams(dimension_semantics=("arbitrary",) * n_axes,
                                vmem_limit_bytes=VMEM_LIMIT_BYTES, flags=flags)


def _ln(x):
    mu = jnp.mean(x, axis=-1, keepdims=True)
    xc = x - mu
    var = jnp.mean(xc * xc, axis=-1, keepdims=True)
    return xc * lax.rsqrt(var + LN_EPS)


def _dot(a, b):
    return jnp.dot(a, b, preferred_element_type=F32)


def _dot_t(a, b):
    return lax.dot_general(a, b, (((1,), (1,)), ((), ())), preferred_element_type=F32)


def _layer_spec(a, *lead, mode=None):
    nl = len(lead)
    zeros = (0,) * (a.ndim - nl)
    return pl.BlockSpec((None,) * nl + tuple(a.shape[nl:]), lambda *_: tuple(lead) + zeros, pipeline_mode=mode)


def _mod_kernel(c_ref, w_ref, b_ref, o_ref):
    c = c_ref[...]
    s = (c * jax.nn.sigmoid(c)).astype(BF)
    o_ref[...] = _dot(s, w_ref[...].astype(BF)) + b_ref[...]


def _mod_call(c8, w_mod, b_mod):
    depth, d, nd = w_mod.shape
    nb = nd // d
    return pl.pallas_call(
        _mod_kernel,
        grid=(depth, nb),
        in_specs=[pl.BlockSpec((8, d), lambda l, j: (0, 0)),
                  pl.BlockSpec((None, d, d), lambda l, j: (l, 0, j)),
                  pl.BlockSpec((None, 1, d), lambda l, j: (l, 0, j))],
        out_specs=pl.BlockSpec((None, 8, d), lambda l, j: (l, 0, j)),
        out_shape=jax.ShapeDtypeStruct((depth, 8, nd), F32),
        compiler_params=_cparams(2),
        name="mod",
    )(c8, w_mod, b_mod.reshape(depth, 1, nd))


def _ffn_kernel(x_ref, t_ref, wg_ref, wu_ref, wd_ref, o_ref, act_scr, *, alpha):
    t = t_ref[...]
    for k, r0 in enumerate(range(0, x_ref.shape[0], ROW_TILE)):
        x = x_ref[r0:r0 + ROW_TILE, :]
        u = (_ln(x) * (1.0 + t[1:2]) + t[0:1]).astype(BF)
        for c0 in range(0, wg_ref.shape[1], FFN_CHUNK):
            a = _dot(u, wg_ref[:, c0:c0 + FFN_CHUNK])
            b = _dot(u, wu_ref[:, c0:c0 + FFN_CHUNK])
            act_scr[k, :, c0:c0 + FFN_CHUNK] = (a * jax.nn.sigmoid(a) * b).astype(BF)
        y = _dot(act_scr[k], wd_ref[...])
        v = alpha * x + (0.5 * t[2:3]) * y
        o_ref[r0:r0 + ROW_TILE, :] = _ln(v) * t[3:4] + t[4:5]


def _ffn_call(x, tab, wg, wu, wd, l, k, alpha):
    nb, n, d = x.shape
    tm = min(FFN_ROWS, n)
    once = pl.Buffered(1)
    return pl.pallas_call(
        functools.partial(_ffn_kernel, alpha=alpha),
        grid=(nb, n // tm),
        in_specs=[pl.BlockSpec((None, tm, d), lambda b, i: (b, i, 0)),
                  pl.BlockSpec((None, 8, d), lambda b, i: (b, 0, 0)),
                  _layer_spec(wg, l, k, mode=once), _layer_spec(wu, l, k, mode=once),
                  _layer_spec(wd, l, k, mode=once)],
        out_specs=pl.BlockSpec((None, tm, d), lambda b, i: (b, i, 0)),
        out_shape=jax.ShapeDtypeStruct(x.shape, F32),
        scratch_shapes=[pltpu.VMEM((pl.cdiv(tm, ROW_TILE), min(ROW_TILE, tm), wg.shape[-1]), BF)],
        compiler_params=_cparams(2),
        name="ffn",
    )(x, tab, wg, wu, wd)


def _rope(x, cos, sin_signed):
    n = x.shape[-1]
    lane = lax.broadcasted_iota(jnp.int32, x.shape, 1)
    first = (lane % 32) < 16
    partner = jnp.where(first, pltpu.roll(x, n - 16, axis=1), pltpu.roll(x, 16, axis=1))
    return x * cos + partner * sin_signed


def _mixin_kernel(x_ref, t_ref, w_ref, cos_ref, sin_ref,
                  us_ref, u4_ref, k_ref, v_ref, f_ref, q_ref, g_ref, us_f32, *, q_scale):
    x = x_ref[...]
    t = t_ref[...]
    u = (_ln(x) * (1.0 + t[1:2]) + t[0:1]).astype(BF)
    cos = cos_ref[...]
    sin = sin_ref[...]
    c_k = SSM_WIDTH
    c_v = c_k + NA_WIDTH
    c_f = c_v + NA_WIDTH
    c_q = c_f + FNET_WIDTH
    c_g = c_q + NA_WIDTH
    zs = _dot(u, w_ref[:, 0:c_k])
    us_ref[...] = zs.astype(BF)
    rows4 = us_f32.shape[1] // S5_BLOCK
    for hf in range(c_k // LANES):
        us_f32[hf] = zs[:, hf * LANES:(hf + 1) * LANES]
    for j in range(S5_BLOCK):
        for hf in range(c_k // LANES):
            c0 = j * c_k + hf * LANES
            u4_ref[:, c0:c0 + LANES] = us_f32[hf, pl.ds(j, rows4, stride=S5_BLOCK), :].astype(BF)
    k_ref[...] = _rope(_dot(u, w_ref[:, c_k:c_v]), cos, sin).astype(BF)
    v_ref[...] = _dot(u, w_ref[:, c_v:c_f]).astype(BF)
    f_ref[...] = _dot(u, w_ref[:, c_f:c_q]).astype(BF)
    q_ref[...] = (_rope(_dot(u, w_ref[:, c_q:c_g]), cos, sin) * q_scale).astype(BF)
    gw = NA_WIDTH
    for c0 in range(0, g_ref.shape[-1], gw):
        g = _dot(u, w_ref[:, c_g + c0:c_g + c0 + gw])
        g_ref[:, c0:c0 + gw] = jax.nn.sigmoid(g).astype(BF)


def _mixin_call(x, tab, w_in, l, cos_t, sin_t, q_scale):
    nb, n, d = x.shape
    win = w_in.shape[-1]
    ng = win - (SSM_WIDTH + 3 * NA_WIDTH + FNET_WIDTH)
    tm = min(ROW_TILE, n)
    nt_pos = cos_t.shape[0] // tm

    def tok(width):
        return pl.BlockSpec((None, tm, width), lambda b, i: (b, i, 0))

    def shape(width):
        return jax.ShapeDtypeStruct((nb, n, width), BF)

    u4_spec = pl.BlockSpec((None, tm // S5_BLOCK, S5_BLOCK * SSM_WIDTH), lambda b, i: (b, i, 0))
    u4_shape = jax.ShapeDtypeStruct((nb, n // S5_BLOCK, S5_BLOCK * SSM_WIDTH), BF)
    return pl.pallas_call(
        functools.partial(_mixin_kernel, q_scale=q_scale),
        grid=(nb, n // tm),
        in_specs=[tok(d),
                  pl.BlockSpec((None, 8, d), lambda b, i: (b, 0, 0)),
                  _layer_spec(w_in, l),
                  pl.BlockSpec((tm, NA_WIDTH), lambda b, i: (i % nt_pos, 0)),
                  pl.BlockSpec((tm, NA_WIDTH), lambda b, i: (i % nt_pos, 0))],
        out_specs=[tok(SSM_WIDTH), u4_spec, tok(NA_WIDTH), tok(NA_WIDTH), tok(FNET_WIDTH), tok(NA_WIDTH),
                   tok(ng)],
        out_shape=[shape(SSM_WIDTH), u4_shape, shape(NA_WIDTH), shape(NA_WIDTH), shape(FNET_WIDTH),
                   shape(NA_WIDTH), shape(ng)],
        scratch_shapes=[pltpu.VMEM((SSM_WIDTH // LANES, tm, LANES), F32)],
        compiler_params=_cparams(2),
        name="mixin",
    )(x, tab, w_in, cos_t, sin_t)


def _s5_kernel(u_ref, h0_ref, wst_ref, wconv_ref, wout_ref, sc_ref, y_ref, hfin_ref, hbuf, hprev, carry,
               *, reverse):
    i = pl.program_id(1)
    ns = SSM_FLAT
    rows = hbuf.shape[0]
    sub = min(S5_SUB, rows)
    order = (lambda r: list(r)[::-1]) if reverse else list
    subs = order(range(0, rows, sub))

    @pl.when(i == 0)
    def _():
        carry[...] = h0_ref[...]

    for s0 in subs:
        hbuf[s0:s0 + sub, :] = _dot(u_ref[s0:s0 + sub, :], wst_ref[...])
    row_id = lax.broadcasted_iota(jnp.int32, (SCAN_ROWS, ns), 0)
    edge = row_id == ((SCAN_ROWS - 1) if reverse else 0)

    cr = carry[:, 0:ns]
    ci = carry[:, ns:2 * ns]
    for s0 in subs:
        for r0 in order(range(s0, s0 + sub, SCAN_ROWS)):
            xr = hbuf[r0:r0 + SCAN_ROWS, 0:ns]
            xi = hbuf[r0:r0 + SCAN_ROWS, ns:2 * ns]
            for k, s in enumerate((1, 2, 4)):
                ar = sc_ref[k, :, 0:ns]
                ai = sc_ref[k, :, ns:2 * ns]
                shift = (SCAN_ROWS - s) if reverse else s
                sr = pltpu.roll(xr, shift, axis=0)
                si = pltpu.roll(xi, shift, axis=0)
                xr, xi = xr + ar * sr - ai * si, xi + ar * si + ai * sr
            pr = sc_ref[3, :, 0:ns]
            pi = sc_ref[3, :, ns:2 * ns]
            hr = xr + pr * cr - pi * ci
            hi = xi + pr * ci + pi * cr
            one = (SCAN_ROWS - 1) if reverse else 1
            hprev[r0:r0 + SCAN_ROWS, 0:ns] = jnp.where(edge, cr, pltpu.roll(hr, one, axis=0))
            hprev[r0:r0 + SCAN_ROWS, ns:2 * ns] = jnp.where(edge, ci, pltpu.roll(hi, one, axis=0))
            last = 0 if reverse else SCAN_ROWS - 1
            cr = jnp.broadcast_to(hr[last:last + 1, :], hr.shape)
            ci = jnp.broadcast_to(hi[last:last + 1, :], hi.shape)
    carry[:, 0:ns] = cr
    carry[:, ns:2 * ns] = ci
    hfin_ref[...] = carry[...]

    for s0 in subs:
        y = (_dot(u_ref[s0:s0 + sub, :], wconv_ref[...])
             + _dot(hprev[s0:s0 + sub, :].astype(BF), wout_ref[...]))
        y_ref[s0:s0 + sub, :] = y.astype(y_ref.dtype)


def _s5_call(u4, h0, tables, l, z, rows):
    wst, wconv, wout, sc = tables
    b, n, w = u4.shape
    nc = n // rows
    ns2 = 2 * SSM_FLAT
    reverse = z == 1
    if reverse:
        cmap = lambda bb, i: (bb, nc - 1 - i, 0)
    else:
        cmap = lambda bb, i: (bb, i, 0)
    return pl.pallas_call(
        functools.partial(_s5_kernel, reverse=reverse),
        grid=(b, nc),
        in_specs=[pl.BlockSpec((None, rows, w), cmap),
                  pl.BlockSpec((None, 8, ns2), lambda bb, i: (bb, 0, 0)),
                  _layer_spec(wst, l, z), _layer_spec(wconv, l, z), _layer_spec(wout, l, z),
                  _layer_spec(sc, l, z)],
        out_specs=[pl.BlockSpec((None, rows, w), cmap),
                   pl.BlockSpec((None, 8, ns2), lambda bb, i: (bb, 0, 0))],
        out_shape=[jax.ShapeDtypeStruct((b, n, w), BF),
                   jax.ShapeDtypeStruct((b, 8, ns2), F32)],
        scratch_shapes=[pltpu.VMEM((rows, ns2), F32), pltpu.VMEM((rows, ns2), F32), pltpu.VMEM((8, ns2), F32)],
        compiler_params=_cparams(2),
        name="s5_bwd" if reverse else "s5_fwd",
    )(u4, h0, wst, wconv, wout, sc)


def _s5_tables(log_dt, a_re, a_im, b_re, b_im, c_re, c_im):
    depth = log_dt.shape[0]
    g, p, c = b_re.shape[-3:]
    tb = S5_BLOCK
    hp = lax.Precision.HIGHEST
    f32 = lambda a: a.astype(F32)
    a_re, a_im, b_re, b_im, c_re, c_im = map(f32, (a_re, a_im, b_re, b_im, c_re, c_im))
    dt = jnp.exp(f32(log_dt))[..., None]
    adt_re = a_re * dt
    adt_im = a_im * dt

    def apow(k):
        kk = jnp.asarray(k, F32)[None, :, :, None, None]
        mag = jnp.exp(kk * adt_re[:, :, None])
        ang = kk * adt_im[:, :, None]
        return mag * jnp.cos(ang), mag * jnp.sin(ang)

    mag = jnp.exp(adt_re)
    ab_re = mag * jnp.cos(adt_im)
    ab_im = mag * jnp.sin(adt_im)
    den = a_re * a_re + a_im * a_im
    nr = ab_re - 1.0
    fr = (nr * a_re + ab_im * a_im) / den
    fi = (ab_im * a_re - nr * a_im) / den
    bb_re = fr[..., None] * b_re - fi[..., None] * b_im
    bb_im = fr[..., None] * b_im + fi[..., None] * b_re

    step = np.arange(tb)
    k_state = np.stack([tb - 1 - step, step])
    k_out = np.stack([step + 1, tb - step])
    k_tap = np.stack([step, step])
    k_scan = np.stack([tb * (np.arange(SCAN_ROWS) + 1)] * 2)

    def block_diag(compact, row_group, col_width):
        n = compact.shape[3]
        dst = np.arange(n * g * col_width)
        src = (dst // (g * col_width)) * col_width + dst % col_width
        copy = np.zeros((n * col_width, n * g * col_width), np.float32)
        copy[src, dst] = 1.0
        cols = jnp.einsum('dzrk,kn->dzrn', compact.reshape(compact.shape[:3] + (n * col_width,)),
                          jnp.asarray(copy), precision=hp)
        keep = jnp.asarray(row_group[:, None] == ((dst // col_width) % g)[None, :])
        return jnp.where(keep, cols, 0.0).astype(BF)

    chan_rows = (np.arange(tb * g * c) // c) % g
    state_rows = (np.arange(2 * g * p) // p) % g

    def cmul_b(pr, pi):
        return (pr[..., None] * bb_re[:, :, None] - pi[..., None] * bb_im[:, :, None],
                pr[..., None] * bb_im[:, :, None] + pi[..., None] * bb_re[:, :, None])

    wr, wi = cmul_b(*apow(k_state))
    w2 = jnp.stack([wr, wi], axis=3)
    w2 = jnp.transpose(w2, (0, 1, 2, 4, 6, 3, 5))
    wst = block_diag(w2.reshape(depth, 2, tb * g * c, 2, p), chan_rows, p)

    pr, pi = apow(k_out)
    cm_re = c_re[:, :, None] * pr[:, :, :, :, None, :] - c_im[:, :, None] * pi[:, :, :, :, None, :]
    cm_im = c_re[:, :, None] * pi[:, :, :, :, None, :] + c_im[:, :, None] * pr[:, :, :, :, None, :]
    cm = jnp.stack([cm_re, -cm_im], axis=3)
    cm = jnp.transpose(cm, (0, 1, 3, 4, 6, 2, 5))
    wout = block_diag(cm.reshape(depth, 2, 2 * g * p, tb, c), state_rows, c)

    abr, abi = cmul_b(*apow(k_tap))
    taps = (jnp.einsum('dzgcp,dztgpe->dztgec', c_re, abr, precision=hp)
            - jnp.einsum('dzgcp,dztgpe->dztgec', c_im, abi, precision=hp))
    sel = np.zeros((2, tb, tb, tb), np.float32)
    for i in range(tb):
        for j in range(tb):
            if j >= i:
                sel[0, i, j, j - i] = 1.0
            if i >= j:
                sel[1, i, j, i - j] = 1.0
    kij = jnp.einsum('dztgec,zijt->dzigejc', taps, jnp.asarray(sel), precision=hp)
    wconv = block_diag(kij.reshape(depth, 2, tb * g * c, tb, c), chan_rows, c)

    qr, qi = apow(k_scan)
    qr = qr.reshape(depth, 2, SCAN_ROWS, g * p)
    qi = qi.reshape(depth, 2, SCAN_ROWS, g * p)
    rows = np.arange(SCAN_ROWS)
    tabs = []
    for s in (1, 2, 4):
        keep = jnp.asarray(np.stack([rows >= s, rows + s <= SCAN_ROWS - 1]), F32)[None, :, :, None]
        tabs.append(jnp.concatenate([keep * qr[:, :, s - 1:s], keep * qi[:, :, s - 1:s]], axis=-1))
    carry_pow = lambda t: jnp.stack([t[:, 0], t[:, 1, ::-1]], axis=1)
    tabs.append(jnp.concatenate([carry_pow(qr), carry_pow(qi)], axis=-1))
    sc = jnp.stack(tabs, axis=2)
    return wst, wconv, wout, sc


def _fnet1_kernel(x_ref, f2_ref, tc_ref, ts_ref, o_ref):
    n2 = x_ref.shape[0]
    y = _dot(f2_ref[...], x_ref[...])
    br = y[0:n2]
    bi = y[n2:2 * n2]
    tc = tc_ref[...]
    ts = ts_ref[...]
    o_ref[0] = (br * tc + bi * ts).astype(BF)
    o_ref[1] = (bi * tc - br * ts).astype(BF)


def _fnet2_kernel(b_ref, f1_ref, cc_ref, sc_ref, o_ref, *, scale):
    kb = b_ref.shape[1]
    n1 = b_ref.shape[2]
    c = b_ref.shape[3]
    zr, zi = [], []
    for j in range(kb):
        rhs = jnp.concatenate([b_ref[0, j], b_ref[1, j]], axis=0)
        z = _dot(f1_ref[...], rhs)
        zr.append(z[0:n1])
        zi.append(z[n1:2 * n1])
    zr = jnp.concatenate(zr, axis=0).astype(BF)
    zi = jnp.concatenate(zi, axis=0).astype(BF)
    y = (_dot(zr, cc_ref[...]) + _dot(zi, sc_ref[...])) * scale
    for j in range(kb):
        o_ref[:, j * c:(j + 1) * c] = y[j * n1:(j + 1) * n1].astype(o_ref.dtype)


def _dft_tables(n):
    idx = np.arange(n)
    ang = 2.0 * np.pi * ((idx[:, None] * idx[None, :]) % n) / n
    return np.cos(ang), np.sin(ang)


def _chan_tables():
    cc, sc = _dft_tables(FNET_GROUP_CH)
    eye = np.eye(FNET_GROUPS)
    return jnp.asarray(np.kron(eye, cc), BF), jnp.asarray(np.kron(eye, sc), BF)


def _fnet_call(f):
    b, n, c = f.shape
    n2 = FNET_N2
    n1 = n // n2
    c2, s2 = _dft_tables(n2)
    f2 = jnp.asarray(np.concatenate([c2, -s2], axis=0), BF)
    k2 = np.arange(n2)[:, None]
    nn1 = np.arange(n1)[None, :]
    ang = 2.0 * np.pi * ((k2 * nn1) % n) / n
    tc = jnp.asarray(np.repeat(np.cos(ang), c, axis=1), F32)
    ts = jnp.asarray(np.repeat(np.sin(ang), c, axis=1), F32)
    c1, s1 = _dft_tables(n1)
    f1 = jnp.asarray(np.block([[c1, s1], [-s1, c1]]), BF)
    cc, sc = _chan_tables()

    wc = min(n1 * c, 4096)
    x2 = f.reshape(b, n2, n1 * c)
    st1 = pl.pallas_call(
        _fnet1_kernel,
        grid=(b, (n1 * c) // wc),
        in_specs=[pl.BlockSpec((None, n2, wc), lambda bb, j: (bb, 0, j)),
                  pl.BlockSpec((2 * n2, n2), lambda bb, j: (0, 0)),
                  pl.BlockSpec((n2, wc), lambda bb, j: (0, j)),
                  pl.BlockSpec((n2, wc), lambda bb, j: (0, j))],
        out_specs=pl.BlockSpec((None, 2, n2, wc), lambda bb, j: (bb, 0, 0, j)),
        out_shape=jax.ShapeDtypeStruct((b, 2, n2, n1 * c), BF),
        compiler_params=_cparams(2),
        name="fnet1",
    )(x2, f2, tc, ts)
    st1 = st1.reshape(b, 2, n2, n1, c)
    kb = 8
    y = pl.pallas_call(
        functools.partial(_fnet2_kernel, scale=1.0 / math.sqrt(n * FNET_GROUP_CH)),
        grid=(b, n2 // kb),
        in_specs=[pl.BlockSpec((None, 2, kb, n1, c), lambda bb, j: (bb, 0, j, 0, 0)),
                  pl.BlockSpec((2 * n1, 2 * n1), lambda bb, j: (0, 0)),
                  pl.BlockSpec((c, c), lambda bb, j: (0, 0)),
                  pl.BlockSpec((c, c), lambda bb, j: (0, 0))],
        out_specs=pl.BlockSpec((None, n1, kb * c), lambda bb, j: (bb, 0, j)),
        out_shape=jax.ShapeDtypeStruct((b, n1, n2 * c), BF),
        compiler_params=_cparams(2),
        name="fnet2",
    )(st1, f1, cc, sc)
    return y.reshape(b, n, c)


def _fnet_dense_kernel(x_ref, cl_ref, sl_ref, cc_ref, sc_ref, o_ref, *, scale):
    x = x_ref[...]
    xr = _dot(x, cc_ref[...]).astype(BF)
    xi = _dot(x, sc_ref[...]).astype(BF)
    o_ref[...] = ((_dot(cl_ref[...], xr) - _dot(sl_ref[...], xi)) * scale).astype(o_ref.dtype)


def _fnet_dense_call(f):
    b, n, c = f.shape
    cl, sl = _dft_tables(n)
    cc, sc = _chan_tables()
    full = lambda bb: (0, 0)
    return pl.pallas_call(
        functools.partial(_fnet_dense_kernel, scale=1.0 / math.sqrt(n * FNET_GROUP_CH)),
        grid=(b,),
        in_specs=[pl.BlockSpec((None, n, c), lambda bb: (bb, 0, 0)),
                  pl.BlockSpec((n, n), full), pl.BlockSpec((n, n), full),
                  pl.BlockSpec((c, c), full), pl.BlockSpec((c, c), full)],
        out_specs=pl.BlockSpec((None, n, c), lambda bb: (bb, 0, 0)),
        out_shape=jax.ShapeDtypeStruct((b, n, c), BF),
        compiler_params=_cparams(1),
        name="fnet_ctx",
    )(f, jnp.asarray(cl, BF), jnp.asarray(sl, BF), cc, sc)


def _head_masks(width):
    lane = lax.broadcasted_iota(jnp.int32, (1, width), 1)
    return [(lane // NA_HEAD_DIM) == h for h in range(width // NA_HEAD_DIM)]


def _attend(q, k_parts, v_parts, bias_of, o_ref):
    zero = jnp.zeros((), BF)
    out = jnp.zeros(q.shape, F32)
    for h, hm in enumerate(_head_masks(q.shape[-1])):
        s_parts = []
        for j, kp in enumerate(k_parts):
            s = _dot_t(q, jnp.where(hm, kp, zero))
            b = bias_of(h, j)
            s_parts.append(s if b is None else s + b)
        m = s_parts[0].max(axis=-1, keepdims=True)
        for s in s_parts[1:]:
            m = jnp.maximum(m, s.max(axis=-1, keepdims=True))
        den = jnp.zeros_like(m)
        p_parts = []
        for s in s_parts:
            p = jnp.exp2(s - m)
            den = den + p.sum(axis=-1, keepdims=True)
            p_parts.append(p.astype(BF))
        vcat = jnp.concatenate([jnp.where(hm, vp, zero) for vp in v_parts], axis=0)
        out = out + _dot(jnp.concatenate(p_parts, axis=1), vcat) / den
    o_ref[...] = out.astype(o_ref.dtype)


def _natten_kernel(q_ref, k0, k1, k2, k3, v0, v1, v2, v3, kc_ref, vc_ref, bias_ref, o_ref):
    blk = k0.shape[0]
    k_parts = [r[...] for r in (k0, k1, k2, k3)] + [kc_ref[...]]
    v_parts = [r[...] for r in (v0, v1, v2, v3)] + [vc_ref[...]]

    def bias_of(h, j):
        if j >= 4:
            return None
        return bias_ref[h, :, j * blk:(j + 1) * blk].astype(F32)

    _attend(q_ref[...], k_parts, v_parts, bias_of, o_ref)


def _natten_call(q, k, v, kc, vc, bias, l):
    b, n, w = q.shape
    lc = kc.shape[1]
    hw = NA_HQ * NA_HEAD_DIM
    qt = NA_QROWS * GRID_W
    blk = (NA_KROWS // 4) * GRID_W
    nj = n // qt
    nblk = n // blk
    ratio = qt // blk

    def kv_spec(m):
        def imap(hq, j, bb):
            start = jnp.clip(ratio * j - 1, 0, nblk - 4)
            return (bb, start + m, hq)
        return pl.BlockSpec((None, blk, hw), imap)

    def bias_map(hq, j, bb):
        var = jnp.where(j == 0, 0, jnp.where(j == nj - 1, 2, 1))
        return (l, var, hq, 0, 0)

    return pl.pallas_call(
        _natten_kernel,
        grid=(w // hw, nj, b),
        in_specs=[pl.BlockSpec((None, qt, hw), lambda hq, j, bb: (bb, j, hq))]
                 + [kv_spec(m) for m in range(4)] + [kv_spec(m) for m in range(4)]
                 + [pl.BlockSpec((None, lc, hw), lambda hq, j, bb: (bb, 0, hq)),
                    pl.BlockSpec((None, lc, hw), lambda hq, j, bb: (bb, 0, hq)),
                    pl.BlockSpec((None, None, NA_HQ, qt, 4 * blk), bias_map)],
        out_specs=pl.BlockSpec((None, qt, hw), lambda hq, j, bb: (bb, j, hq)),
        out_shape=jax.ShapeDtypeStruct((b, n, w), BF),
        compiler_params=_cparams(3),
        name="natten",
    )(q, k, k, k, k, v, v, v, v, kc, vc, bias)


def _ctx_atten_kernel(q_ref, kc_ref, vc_ref, o_ref):
    _attend(q_ref[...], [kc_ref[...]], [vc_ref[...]], lambda h, j: None, o_ref)


def _ctx_atten_call(qc, kc, vc):
    b, lc, w = qc.shape
    hw = NA_HQ * NA_HEAD_DIM
    spec = pl.BlockSpec((None, lc, hw), lambda bb, hq: (bb, 0, hq))
    return pl.pallas_call(
        _ctx_atten_kernel,
        grid=(b, w // hw),
        in_specs=[spec, spec, spec],
        out_specs=spec,
        out_shape=jax.ShapeDtypeStruct((b, lc, w), BF),
        compiler_params=_cparams(2),
        name="ctx_atten",
    )(qc, kc, vc)


def _na_bias(rpb, rows):
    w = GRID_W
    nj = rows // NA_QROWS
    n_dr = 2 * WIN_ROWS - 1
    n_dc = 2 * WIN_COLS - 1
    cq = np.arange(w)
    cs = np.clip(cq - WIN_COLS // 2, 0, w - WIN_COLS)
    ck = np.arange(w)
    valid_c = (ck[None, :] >= cs[:, None]) & (ck[None, :] < cs[:, None] + WIN_COLS)
    dc = ck[None, :] - cq[:, None] + (WIN_COLS - 1)
    pick_c = (dc[:, :, None] == np.arange(n_dc)) & valid_c[:, :, None]
    tiles = jnp.einsum('lhab,cdb->lhacd', rpb.astype(F32) * LOG2E, jnp.asarray(pick_c, F32),
                       precision=lax.Precision.HIGHEST)
    tiles = tiles + jnp.asarray(np.where(valid_c, 0.0, NEG_BIG), F32)
    tiles = jnp.concatenate([tiles, jnp.full_like(tiles[:, :, :1], NEG_BIG)], axis=2).astype(BF)
    out = []
    for j in (0, min(1, nj - 1), nj - 1):
        rq = NA_QROWS * j + np.arange(NA_QROWS)
        ws = int(np.clip(NA_QROWS * j - (NA_KROWS - NA_QROWS) // 2, 0, rows - NA_KROWS))
        rk = ws + np.arange(NA_KROWS)
        r0 = np.clip(rq - WIN_ROWS // 2, 0, rows - WIN_ROWS)
        valid_r = (rk[None, :] >= r0[:, None]) & (rk[None, :] < r0[:, None] + WIN_ROWS)
        dr = np.where(valid_r, rk[None, :] - rq[:, None] + (WIN_ROWS - 1), n_dr)
        out.append(jnp.concatenate(
            [jnp.concatenate([tiles[:, :, int(dr[a, b])] for b in range(NA_KROWS)], axis=-1)
             for a in range(NA_QROWS)], axis=-2))
    return jnp.stack(out, axis=1)


def _rope_tables(n):
    nf = NA_HEAD_DIM // 4
    t = jnp.arange(n, dtype=jnp.int32)
    pos = jnp.stack([t // GRID_W, t % GRID_W], axis=-1).astype(F32)
    inv_freq = ROPE_THETA ** (-jnp.arange(nf, dtype=F32) / nf)
    ang = pos[:, :, None] * inv_freq
    cos = jnp.cos(ang)
    sin = jnp.sin(ang)
    cos_h = jnp.concatenate([cos, cos], axis=-1).reshape(n, NA_HEAD_DIM)
    sin_h = jnp.concatenate([-sin, sin], axis=-1).reshape(n, NA_HEAD_DIM)
    return jnp.tile(cos_h, (1, NA_HEADS)), jnp.tile(sin_h, (1, NA_HEADS))


def _merge_kernel(ysf_ref, ysb_ref, us_ref, yf_ref, yn_ref, g_ref, h_ref, t_ref,
                  sd_ref, wglu_ref, bglu_ref, wbs_ref, wbf_ref, wbn_ref, wo_ref, o_ref, ys_f32, *, alpha):
    d = h_ref.shape[-1]
    t = t_ref[...]
    y4 = ysf_ref[...].astype(F32) + ysb_ref[...].astype(F32)
    rows4 = y4.shape[0]
    halves = SSM_WIDTH // LANES
    for j in range(S5_BLOCK):
        for hf in range(halves):
            c0 = j * SSM_WIDTH + hf * LANES
            ys_f32[hf, pl.ds(j, rows4, stride=S5_BLOCK), :] = y4[:, c0:c0 + LANES]
    ys = jnp.concatenate([ys_f32[hf] for hf in range(halves)], axis=-1)
    ys = ys + sd_ref[...] * us_ref[...].astype(F32)
    ys = jax.nn.gelu(ys)
    ys = (ys * jax.nn.sigmoid(_dot(ys.astype(BF), wglu_ref[...]) + bglu_ref[...])).astype(BF)
    y = (g_ref[:, 0:d].astype(F32) * _dot(ys, wbs_ref[...])
         + g_ref[:, d:2 * d].astype(F32) * _dot(yf_ref[...], wbf_ref[...])
         + g_ref[:, 2 * d:3 * d].astype(F32) * _dot(yn_ref[...], wbn_ref[...]))
    y = _dot(y.astype(BF), wo_ref[...])
    v = alpha * h_ref[...] + t[2:3] * y
    o_ref[...] = _ln(v) * t[3:4] + t[4:5]


def _merge_call(ysf4, ysb4, us, yf, yn, g, h, tab, s5w, mats, l, alpha):
    nb, n, d = h.shape
    tm = min(ROW_TILE, n)

    def tok(a):
        return pl.BlockSpec((None, tm, a.shape[-1]), lambda b, i: (b, i, 0))

    def tok4(a):
        return pl.BlockSpec((None, tm // S5_BLOCK, a.shape[-1]), lambda b, i: (b, i, 0))

    toks = (us, yf, yn, g, h)
    return pl.pallas_call(
        functools.partial(_merge_kernel, alpha=alpha),
        grid=(nb, n // tm),
        in_specs=[tok4(ysf4), tok4(ysb4)] + [tok(a) for a in toks]
                 + [pl.BlockSpec((None, 8, d), lambda b, i: (b, 0, 0))]
                 + [_layer_spec(a, l) for a in tuple(s5w) + tuple(mats)],
        out_specs=tok(h),
        out_shape=jax.ShapeDtypeStruct(h.shape, F32),
        scratch_shapes=[pltpu.VMEM((SSM_WIDTH // LANES, tm, LANES), F32)],
        compiler_params=_cparams(2),
        name="merge",
    )(ysf4, ysb4, *toks, tab, *s5w, *mats)


def _table(mod, d, chunks, extra):
    rows = [mod[:, i * d:(i + 1) * d] for i in chunks]
    rows += [jnp.broadcast_to(e[None, :], (8, d)) for e in extra]
    rows += [jnp.zeros((8, d), F32)] * (8 - len(rows))
    return jnp.stack(rows, axis=1)


def kernel(x, c, ctx, c_ctx, w_mod, b_mod, ln_g, ln_b, ffn_w_gate, ffn_w_up, ffn_w_down, w_in, ssm_log_dt, ssm_a_re, ssm_a_im, ssm_b_re, ssm_b_im, ssm_c_re, ssm_c_im, ssm_d, ssm_w_glu, ssm_b_glu, na_rpb, w_br_ssm, w_br_fnet, w_br_na, w_out):
    bsz, n, d = x.shape
    lc = ctx.shape[1]
    depth = w_mod.shape[0]
    rows = n // GRID_W
    assert bsz + 1 <= 8 and rows % NA_QROWS == 0 and rows >= NA_KROWS and n % FNET_N2 == 0
    alpha = (2 * depth) ** 0.25
    q_scale = NA_HEAD_DIM ** -0.5 * LOG2E

    c8 = jnp.concatenate([c, c_ctx[None, :], jnp.zeros((8 - bsz - 1, d), F32)], axis=0)
    mod_all = _mod_call(c8, w_mod, b_mod)

    wg = ffn_w_gate.astype(BF)
    wu = ffn_w_up.astype(BF)
    wd = ffn_w_down.astype(BF)
    win = w_in.astype(BF)
    mats = (w_br_ssm.astype(BF), w_br_fnet.astype(BF), w_br_na.astype(BF), w_out.astype(BF))
    s5w = (ssm_d[:, None, :], ssm_w_glu.astype(BF), ssm_b_glu[:, None, :])
    s5_tabs = _s5_tables(ssm_log_dt, ssm_a_re, ssm_a_im, ssm_b_re, ssm_b_im, ssm_c_re, ssm_c_im)
    na_bias = _na_bias(na_rpb, rows)
    cos_t, sin_t = _rope_tables(n)
    cos_c = jnp.ones((bsz * lc, NA_WIDTH), F32)
    sin_c = jnp.zeros((bsz * lc, NA_WIDTH), F32)
    zero_state = jnp.zeros((bsz, 8, 2 * SSM_FLAT), F32)

    h = x
    hc = ctx.reshape(1, bsz * lc, d)
    for l in range(depth):
        last = l == depth - 1
        mod = mod_all[l]

        tab = _table(mod, d, (0, 1, 2), (ln_g[l, 0], ln_b[l, 0]))
        h = _ffn_call(h, tab[:bsz], wg, wu, wd, l, 0, alpha)
        hc = _ffn_call(hc, tab[bsz:bsz + 1], wg, wu, wd, l, 0, alpha)

        tab = _table(mod, d, (3, 4, 5), (ln_g[l, 1], ln_b[l, 1]))
        us, u4, k, v, f, q, g = _mixin_call(h, tab[:bsz], win, l, cos_t, sin_t, q_scale)
        usc, u4c, kc, vc, fc, qc, gc = _mixin_call(hc, tab[bsz:bsz + 1], win, l, cos_c, sin_c, q_scale)
        u4c, kc, vc, fc, qc = [a.reshape(bsz, a.shape[1] // bsz, a.shape[-1]) for a in (u4c, kc, vc, fc, qc)]

        ysbc, hcb = _s5_call(u4c, zero_state, s5_tabs, l, 1, u4c.shape[1])
        ysfc, hcf = _s5_call(u4c, zero_state, s5_tabs, l, 0, u4c.shape[1])
        ysb, _ = _s5_call(u4, hcb, s5_tabs, l, 1, min(S5_ROWS, u4.shape[1]))
        ysf, _ = _s5_call(u4, hcf, s5_tabs, l, 0, min(S5_ROWS, u4.shape[1]))

        yf = _fnet_call(f)
        yn = _natten_call(q, k, v, kc, vc, na_bias, l)

        h = _merge_call(ysf, ysb, us, yf, yn, g, h, tab[:bsz], s5w, mats, l, alpha)

        tab3 = _table(mod, d, (6, 7, 8), (ln_g[l, 2], ln_b[l, 2]))
        h = _ffn_call(h, tab3[:bsz], wg, wu, wd, l, 1, alpha)

        if not last:
            yfc = _fnet_dense_call(fc)
            ync = _ctx_atten_call(qc, kc, vc)
            flat = lambda a: a.reshape(1, a.shape[0] * a.shape[1], a.shape[-1])
            hc = _merge_call(flat(ysfc), flat(ysbc), usc, flat(yfc), flat(ync), gc, hc, tab[bsz:bsz + 1],
                             s5w, mats, l, alpha)
            hc = _ffn_call(hc, tab3[bsz:bsz + 1], wg, wu, wd, l, 1, alpha)
    return h
```

```python
import functools
import math

import jax
import jax.numpy as jnp
import numpy as np
from jax import lax
from jax.experimental import pallas as pl
from jax.experimental.pallas import tpu as pltpu

BF = jnp.bfloat16
F32 = jnp.float32

GRID_W = 64
SSM_GROUPS = 16
SSM_GROUP_CH = 16
SSM_STATE = 64
SSM_WIDTH = SSM_GROUPS * SSM_GROUP_CH
SSM_FLAT = SSM_GROUPS * SSM_STATE
FNET_GROUPS = 4
FNET_GROUP_CH = 64
FNET_WIDTH = FNET_GROUPS * FNET_GROUP_CH
NA_HEADS = 8
NA_HEAD_DIM = 64
NA_WIDTH = NA_HEADS * NA_HEAD_DIM
WIN_ROWS = 8
WIN_COLS = 16
ROPE_THETA = 10000.0
N_MOD = 9
LN_EPS = 1e-6
LOG2E = 1.4426950408889634
NEG_BIG = -1e30

VMEM_LIMIT_BYTES = 56 * 1024 * 1024
ROW_TILE = 512
FFN_ROWS = 1024
MERGE_ROWS = 1024
FFN_CHUNK = 256
LANES = 128
SCAN_ROWS = 8
S5_BLOCK = 4
S5_ROWS = 512
S5_SUB = 256
NA_QROWS = 8
NA_KROWS = 16
NA_HQ = 4
FNET_N2 = 128


def _cparams(n_axes, flags=None):
    return pltpu.CompilerParams(dimension_semantics=("arbitrary",) * n_axes,
                                vmem_limit_bytes=VMEM_LIMIT_BYTES, flags=flags)


def _ln(x):
    mu = jnp.mean(x, axis=-1, keepdims=True)
    xc = x - mu
    var = jnp.mean(xc * xc, axis=-1, keepdims=True)
    return xc * lax.rsqrt(var + LN_EPS)


def _dot(a, b):
    return jnp.dot(a, b, preferred_element_type=F32)


def _dot_t(a, b):
    return lax.dot_general(a, b, (((1,), (1,)), ((), ())), preferred_element_type=F32)


def _layer_spec(a, *lead, mode=None):
    nl = len(lead)
    zeros = (0,) * (a.ndim - nl)
    return pl.BlockSpec((None,) * nl + tuple(a.shape[nl:]), lambda *_: tuple(lead) + zeros, pipeline_mode=mode)


def _mod_kernel(c_ref, w_ref, b_ref, o_ref):
    c = c_ref[...]
    s = (c * jax.nn.sigmoid(c)).astype(BF)
    o_ref[...] = _dot(s, w_ref[...].astype(BF)) + b_ref[...]


def _mod_call(c8, w_mod, b_mod):
    depth, d, nd = w_mod.shape
    nb = nd // d
    return pl.pallas_call(
        _mod_kernel,
        grid=(depth, nb),
        in_specs=[pl.BlockSpec((8, d), lambda l, j: (0, 0)),
                  pl.BlockSpec((None, d, d), lambda l, j: (l, 0, j)),
                  pl.BlockSpec((None, 1, d), lambda l, j: (l, 0, j))],
        out_specs=pl.BlockSpec((None, 8, d), lambda l, j: (l, 0, j)),
        out_shape=jax.ShapeDtypeStruct((depth, 8, nd), F32),
        compiler_params=_cparams(2),
        name="mod",
    )(c8, w_mod, b_mod.reshape(depth, 1, nd))


def _ffn_kernel(x_ref, t_ref, wg_ref, wu_ref, wd_ref, o_ref, act_scr, *, alpha):
    t = t_ref[...]
    tiles = [slice(r0, r0 + ROW_TILE) for r0 in range(0, x_ref.shape[0], ROW_TILE)]
    us = [(_ln(x_ref[rows, :]) * (1.0 + t[1:2]) + t[0:1]).astype(BF) for rows in tiles]
    for c0 in range(0, wg_ref.shape[1], FFN_CHUNK):
        for k, u in enumerate(us):
            a = _dot(u, wg_ref[:, c0:c0 + FFN_CHUNK])
            b = _dot(u, wu_ref[:, c0:c0 + FFN_CHUNK])
            act_scr[k, :, c0:c0 + FFN_CHUNK] = (a * jax.nn.sigmoid(a) * b).astype(BF)
    ys = [_dot(act_scr[k], wd_ref[...]) for k in range(len(tiles))]
    for y, rows in zip(ys, tiles):
        v = alpha * x_ref[rows, :] + (0.5 * t[2:3]) * y
        o_ref[rows, :] = _ln(v) * t[3:4] + t[4:5]


def _ffn_call(x, tab, wg, wu, wd, l, k, alpha):
    nb, n, d = x.shape
    tm = min(FFN_ROWS, n)
    once = pl.Buffered(1)
    return pl.pallas_call(
        functools.partial(_ffn_kernel, alpha=alpha),
        grid=(nb, n // tm),
        in_specs=[pl.BlockSpec((None, tm, d), lambda b, i: (b, i, 0)),
                  pl.BlockSpec((None, 8, d), lambda b, i: (b, 0, 0)),
                  _layer_spec(wg, l, k, mode=once), _layer_spec(wu, l, k, mode=once),
                  _layer_spec(wd, l, k, mode=once)],
        out_specs=pl.BlockSpec((None, tm, d), lambda b, i: (b, i, 0)),
        out_shape=jax.ShapeDtypeStruct(x.shape, F32),
        scratch_shapes=[pltpu.VMEM((pl.cdiv(tm, ROW_TILE), min(ROW_TILE, tm), wg.shape[-1]), BF)],
        compiler_params=_cparams(2),
        name="ffn",
    )(x, tab, wg, wu, wd)


def _rope(x, cos, sin_signed):
    n = x.shape[-1]
    lane = lax.broadcasted_iota(jnp.int32, x.shape, 1)
    first = (lane % 32) < 16
    partner = jnp.where(first, pltpu.roll(x, n - 16, axis=1), pltpu.roll(x, 16, axis=1))
    return x * cos + partner * sin_signed


def _mixin_kernel(x_ref, t_ref, w_ref, cos_ref, sin_ref,
                  us_ref, u4_ref, k_ref, v_ref, f_ref, q_ref, g_ref, us_f32, *, q_scale):
    x = x_ref[...]
    t = t_ref[...]
    u = (_ln(x) * (1.0 + t[1:2]) + t[0:1]).astype(BF)
    cos = cos_ref[...]
    sin = sin_ref[...]
    c_k = SSM_WIDTH
    c_v = c_k + NA_WIDTH
    c_f = c_v + NA_WIDTH
    c_q = c_f + FNET_WIDTH
    c_g = c_q + NA_WIDTH
    zs = _dot(u, w_ref[:, 0:c_k])
    us_ref[...] = zs.astype(BF)
    rows4 = us_f32.shape[1] // S5_BLOCK
    for hf in range(c_k // LANES):
        us_f32[hf] = zs[:, hf * LANES:(hf + 1) * LANES]
    for j in range(S5_BLOCK):
        for hf in range(c_k // LANES):
            c0 = j * c_k + hf * LANES
            u4_ref[:, c0:c0 + LANES] = us_f32[hf, pl.ds(j, rows4, stride=S5_BLOCK), :].astype(BF)
    k_ref[...] = _rope(_dot(u, w_ref[:, c_k:c_v]), cos, sin).astype(BF)
    v_ref[...] = _dot(u, w_ref[:, c_v:c_f]).astype(BF)
    f_ref[...] = _dot(u, w_ref[:, c_f:c_q]).astype(BF)
    q_ref[...] = (_rope(_dot(u, w_ref[:, c_q:c_g]), cos, sin) * q_scale).astype(BF)
    gw = NA_WIDTH
    for c0 in range(0, g_ref.shape[-1], gw):
        g = _dot(u, w_ref[:, c_g + c0:c_g + c0 + gw])
        g_ref[:, c0:c0 + gw] = jax.nn.sigmoid(g).astype(BF)


def _mixin_call(x, tab, w_in, l, cos_t, sin_t, q_scale):
    nb, n, d = x.shape
    win = w_in.shape[-1]
    ng = win - (SSM_WIDTH + 3 * NA_WIDTH + FNET_WIDTH)
    tm = min(ROW_TILE, n)
    nt_pos = cos_t.shape[0] // tm

    def tok(width):
        return pl.BlockSpec((None, tm, width), lambda b, i: (b, i, 0))

    def shape(width):
        return jax.ShapeDtypeStruct((nb, n, width), BF)

    u4_spec = pl.BlockSpec((None, tm // S5_BLOCK, S5_BLOCK * SSM_WIDTH), lambda b, i: (b, i, 0))
    u4_shape = jax.ShapeDtypeStruct((nb, n // S5_BLOCK, S5_BLOCK * SSM_WIDTH), BF)
    return pl.pallas_call(
        functools.partial(_mixin_kernel, q_scale=q_scale),
        grid=(nb, n // tm),
        in_specs=[tok(d),
                  pl.BlockSpec((None, 8, d), lambda b, i: (b, 0, 0)),
                  _layer_spec(w_in, l),
                  pl.BlockSpec((tm, NA_WIDTH), lambda b, i: (i % nt_pos, 0)),
                  pl.BlockSpec((tm, NA_WIDTH), lambda b, i: (i % nt_pos, 0))],
        out_specs=[tok(SSM_WIDTH), u4_spec, tok(NA_WIDTH), tok(NA_WIDTH), tok(FNET_WIDTH), tok(NA_WIDTH),
                   tok(ng)],
        out_shape=[shape(SSM_WIDTH), u4_shape, shape(NA_WIDTH), shape(NA_WIDTH), shape(FNET_WIDTH),
                   shape(NA_WIDTH), shape(ng)],
        scratch_shapes=[pltpu.VMEM((SSM_WIDTH // LANES, tm, LANES), F32)],
        compiler_params=_cparams(2),
        name="mixin",
    )(x, tab, w_in, cos_t, sin_t)


def _s5_kernel(u_ref, h0_ref, wst_ref, wconv_ref, wout_ref, sc_ref, y_ref, hfin_ref, hbuf, hprev, carry,
               *, reverse):
    i = pl.program_id(1)
    ns = SSM_FLAT
    rows = hbuf.shape[0]
    sub = min(S5_SUB, rows)
    order = (lambda r: list(r)[::-1]) if reverse else list
    subs = order(range(0, rows, sub))

    @pl.when(i == 0)
    def _():
        carry[...] = h0_ref[...]

    for s0 in subs:
        hbuf[s0:s0 + sub, :] = _dot(u_ref[s0:s0 + sub, :], wst_ref[...])
    row_id = lax.broadcasted_iota(jnp.int32, (SCAN_ROWS, ns), 0)
    edge = row_id == ((SCAN_ROWS - 1) if reverse else 0)

    cr = carry[:, 0:ns]
    ci = carry[:, ns:2 * ns]
    for s0 in subs:
        for r0 in order(range(s0, s0 + sub, SCAN_ROWS)):
            xr = hbuf[r0:r0 + SCAN_ROWS, 0:ns]
            xi = hbuf[r0:r0 + SCAN_ROWS, ns:2 * ns]
            for k, s in enumerate((1, 2, 4)):
                ar = sc_ref[k, :, 0:ns]
                ai = sc_ref[k, :, ns:2 * ns]
                shift = (SCAN_ROWS - s) if reverse else s
                sr = pltpu.roll(xr, shift, axis=0)
                si = pltpu.roll(xi, shift, axis=0)
                xr, xi = xr + ar * sr - ai * si, xi + ar * si + ai * sr
            pr = sc_ref[3, :, 0:ns]
            pi = sc_ref[3, :, ns:2 * ns]
            hr = xr + pr * cr - pi * ci
            hi = xi + pr * ci + pi * cr
            one = (SCAN_ROWS - 1) if reverse else 1
            hprev[r0:r0 + SCAN_ROWS, 0:ns] = jnp.where(edge, cr, pltpu.roll(hr, one, axis=0))
            hprev[r0:r0 + SCAN_ROWS, ns:2 * ns] = jnp.where(edge, ci, pltpu.roll(hi, one, axis=0))
            last = 0 if reverse else SCAN_ROWS - 1
            cr = jnp.broadcast_to(hr[last:last + 1, :], hr.shape)
            ci = jnp.broadcast_to(hi[last:last + 1, :], hi.shape)
    carry[:, 0:ns] = cr
    carry[:, ns:2 * ns] = ci
    hfin_ref[...] = carry[...]

    for s0 in subs:
        y = (_dot(u_ref[s0:s0 + sub, :], wconv_ref[...])
             + _dot(hprev[s0:s0 + sub, :].astype(BF), wout_ref[...]))
        y_ref[s0:s0 + sub, :] = y.astype(y_ref.dtype)


def _s5_call(u4, h0, tables, l, z, rows):
    wst, wconv, wout, sc = tables
    b, n, w = u4.shape
    nc = n // rows
    ns2 = 2 * SSM_FLAT
    reverse = z == 1
    if reverse:
        cmap = lambda bb, i: (bb, nc - 1 - i, 0)
    else:
        cmap = lambda bb, i: (bb, i, 0)
    return pl.pallas_call(
        functools.partial(_s5_kernel, reverse=reverse),
        grid=(b, nc),
        in_specs=[pl.BlockSpec((None, rows, w), cmap),
                  pl.BlockSpec((None, 8, ns2), lambda bb, i: (bb, 0, 0)),
                  _layer_spec(wst, l, z), _layer_spec(wconv, l, z), _layer_spec(wout, l, z),
                  _layer_spec(sc, l, z)],
        out_specs=[pl.BlockSpec((None, rows, w), cmap),
                   pl.BlockSpec((None, 8, ns2), lambda bb, i: (bb, 0, 0))],
        out_shape=[jax.ShapeDtypeStruct((b, n, w), BF),
                   jax.ShapeDtypeStruct((b, 8, ns2), F32)],
        scratch_shapes=[pltpu.VMEM((rows, ns2), F32), pltpu.VMEM((rows, ns2), F32), pltpu.VMEM((8, ns2), F32)],
        compiler_params=_cparams(2),
        name="s5_bwd" if reverse else "s5_fwd",
    )(u4, h0, wst, wconv, wout, sc)


def _s5_tables(log_dt, a_re, a_im, b_re, b_im, c_re, c_im):
    depth = log_dt.shape[0]
    g, p, c = b_re.shape[-3:]
    tb = S5_BLOCK
    hp = lax.Precision.HIGHEST
    f32 = lambda a: a.astype(F32)
    a_re, a_im, b_re, b_im, c_re, c_im = map(f32, (a_re, a_im, b_re, b_im, c_re, c_im))
    dt = jnp.exp(f32(log_dt))[..., None]
    adt_re = a_re * dt
    adt_im = a_im * dt

    def apow(k):
        kk = jnp.asarray(k, F32)[None, :, :, None, None]
        mag = jnp.exp(kk * adt_re[:, :, None])
        ang = kk * adt_im[:, :, None]
        return mag * jnp.cos(ang), mag * jnp.sin(ang)

    mag = jnp.exp(adt_re)
    ab_re = mag * jnp.cos(adt_im)
    ab_im = mag * jnp.sin(adt_im)
    den = a_re * a_re + a_im * a_im
    nr = ab_re - 1.0
    fr = (nr * a_re + ab_im * a_im) / den
    fi = (ab_im * a_re - nr * a_im) / den
    bb_re = fr[..., None] * b_re - fi[..., None] * b_im
    bb_im = fr[..., None] * b_im + fi[..., None] * b_re

    step = np.arange(tb)
    k_state = np.stack([tb - 1 - step, step])
    k_out = np.stack([step + 1, tb - step])
    k_tap = np.stack([step, step])
    k_scan = np.stack([tb * (np.arange(SCAN_ROWS) + 1)] * 2)

    def block_diag(compact, row_group, col_width):
        n = compact.shape[3]
        dst = np.arange(n * g * col_width)
        src = (dst // (g * col_width)) * col_width + dst % col_width
        copy = np.zeros((n * col_width, n * g * col_width), np.float32)
        copy[src, dst] = 1.0
        cols = jnp.einsum('dzrk,kn->dzrn', compact.reshape(compact.shape[:3] + (n * col_width,)),
                          jnp.asarray(copy), precision=hp)
        keep = jnp.asarray(row_group[:, None] == ((dst // col_width) % g)[None, :])
        return jnp.where(keep, cols, 0.0).astype(BF)

    chan_rows = (np.arange(tb * g * c) // c) % g
    state_rows = (np.arange(2 * g * p) // p) % g

    def cmul_b(pr, pi):
        return (pr[..., None] * bb_re[:, :, None] - pi[..., None] * bb_im[:, :, None],
                pr[..., None] * bb_im[:, :, None] + pi[..., None] * bb_re[:, :, None])

    wr, wi = cmul_b(*apow(k_state))
    w2 = jnp.stack([wr, wi], axis=3)
    w2 = jnp.transpose(w2, (0, 1, 2, 4, 6, 3, 5))
    wst = block_diag(w2.reshape(depth, 2, tb * g * c, 2, p), chan_rows, p)

    pr, pi = apow(k_out)
    cm_re = c_re[:, :, None] * pr[:, :, :, :, None, :] - c_im[:, :, None] * pi[:, :, :, :, None, :]
    cm_im = c_re[:, :, None] * pi[:, :, :, :, None, :] + c_im[:, :, None] * pr[:, :, :, :, None, :]
    cm = jnp.stack([cm_re, -cm_im], axis=3)
    cm = jnp.transpose(cm, (0, 1, 3, 4, 6, 2, 5))
    wout = block_diag(cm.reshape(depth, 2, 2 * g * p, tb, c), state_rows, c)

    abr, abi = cmul_b(*apow(k_tap))
    taps = (jnp.einsum('dzgcp,dztgpe->dztgec', c_re, abr, precision=hp)
            - jnp.einsum('dzgcp,dztgpe->dztgec', c_im, abi, precision=hp))
    sel = np.zeros((2, tb, tb, tb), np.float32)
    for i in range(tb):
        for j in range(tb):
            if j >= i:
                sel[0, i, j, j - i] = 1.0
            if i >= j:
                sel[1, i, j, i - j] = 1.0
    kij = jnp.einsum('dztgec,zijt->dzigejc', taps, jnp.asarray(sel), precision=hp)
    wconv = block_diag(kij.reshape(depth, 2, tb * g * c, tb, c), chan_rows, c)

    qr, qi = apow(k_scan)
    qr = qr.reshape(depth, 2, SCAN_ROWS, g * p)
    qi = qi.reshape(depth, 2, SCAN_ROWS, g * p)
    rows = np.arange(SCAN_ROWS)
    tabs = []
    for s in (1, 2, 4):
        keep = jnp.asarray(np.stack([rows >= s, rows + s <= SCAN_ROWS - 1]), F32)[None, :, :, None]
        tabs.append(jnp.concatenate([keep * qr[:, :, s - 1:s], keep * qi[:, :, s - 1:s]], axis=-1))
    carry_pow = lambda t: jnp.stack([t[:, 0], t[:, 1, ::-1]], axis=1)
    tabs.append(jnp.concatenate([carry_pow(qr), carry_pow(qi)], axis=-1))
    sc = jnp.stack(tabs, axis=2)
    return wst, wconv, wout, sc


def _fnet1_kernel(x_ref, f2_ref, tc_ref, ts_ref, o_ref):
    n2 = x_ref.shape[0]
    y = _dot(f2_ref[...], x_ref[...])
    br = y[0:n2]
    bi = y[n2:2 * n2]
    tc = tc_ref[...]
    ts = ts_ref[...]
    o_ref[0] = (br * tc + bi * ts).astype(BF)
    o_ref[1] = (bi * tc - br * ts).astype(BF)


def _fnet2_kernel(b_ref, f1_ref, cc_ref, sc_ref, o_ref, *, scale):
    kb = b_ref.shape[1]
    n1 = b_ref.shape[2]
    c = b_ref.shape[3]
    zr, zi = [], []
    for j in range(kb):
        rhs = jnp.concatenate([b_ref[0, j], b_ref[1, j]], axis=0)
        z = _dot(f1_ref[...], rhs)
        zr.append(z[0:n1])
        zi.append(z[n1:2 * n1])
    zr = jnp.concatenate(zr, axis=0).astype(BF)
    zi = jnp.concatenate(zi, axis=0).astype(BF)
    y = (_dot(zr, cc_ref[...]) + _dot(zi, sc_ref[...])) * scale
    for j in range(kb):
        o_ref[:, j * c:(j + 1) * c] = y[j * n1:(j + 1) * n1].astype(o_ref.dtype)


def _dft_tables(n):
    idx = np.arange(n)
    ang = 2.0 * np.pi * ((idx[:, None] * idx[None, :]) % n) / n
    return np.cos(ang), np.sin(ang)


def _chan_tables():
    cc, sc = _dft_tables(FNET_GROUP_CH)
    eye = np.eye(FNET_GROUPS)
    return jnp.asarray(np.kron(eye, cc), BF), jnp.asarray(np.kron(eye, sc), BF)


def _fnet_call(f):
    b, n, c = f.shape
    n2 = FNET_N2
    n1 = n // n2
    c2, s2 = _dft_tables(n2)
    f2 = jnp.asarray(np.concatenate([c2, -s2], axis=0), BF)
    k2 = np.arange(n2)[:, None]
    nn1 = np.arange(n1)[None, :]
    ang = 2.0 * np.pi * ((k2 * nn1) % n) / n
    tc = jnp.asarray(np.repeat(np.cos(ang), c, axis=1), F32)
    ts = jnp.asarray(np.repeat(np.sin(ang), c, axis=1), F32)
    c1, s1 = _dft_tables(n1)
    f1 = jnp.asarray(np.block([[c1, s1], [-s1, c1]]), BF)
    cc, sc = _chan_tables()

    wc = min(n1 * c, 4096)
    x2 = f.reshape(b, n2, n1 * c)
    st1 = pl.pallas_call(
        _fnet1_kernel,
        grid=(b, (n1 * c) // wc),
        in_specs=[pl.BlockSpec((None, n2, wc), lambda bb, j: (bb, 0, j)),
                  pl.BlockSpec((2 * n2, n2), lambda bb, j: (0, 0)),
                  pl.BlockSpec((n2, wc), lambda bb, j: (0, j)),
                  pl.BlockSpec((n2, wc), lambda bb, j: (0, j))],
        out_specs=pl.BlockSpec((None, 2, n2, wc), lambda bb, j: (bb, 0, 0, j)),
        out_shape=jax.ShapeDtypeStruct((b, 2, n2, n1 * c), BF),
        compiler_params=_cparams(2),
        name="fnet1",
    )(x2, f2, tc, ts)
    st1 = st1.reshape(b, 2, n2, n1, c)
    kb = 8
    y = pl.pallas_call(
        functools.partial(_fnet2_kernel, scale=1.0 / math.sqrt(n * FNET_GROUP_CH)),
        grid=(b, n2 // kb),
        in_specs=[pl.BlockSpec((None, 2, kb, n1, c), lambda bb, j: (bb, 0, j, 0, 0)),
                  pl.BlockSpec((2 * n1, 2 * n1), lambda bb, j: (0, 0)),
                  pl.BlockSpec((c, c), lambda bb, j: (0, 0)),
                  pl.BlockSpec((c, c), lambda bb, j: (0, 0))],
        out_specs=pl.BlockSpec((None, n1, kb * c), lambda bb, j: (bb, 0, j)),
        out_shape=jax.ShapeDtypeStruct((b, n1, n2 * c), BF),
        compiler_params=_cparams(2),
        name="fnet2",
    )(st1, f1, cc, sc)
    return y.reshape(b, n, c)


def _fnet_dense_kernel(x_ref, cl_ref, sl_ref, cc_ref, sc_ref, o_ref, *, scale):
    x = x_ref[...]
    xr = _dot(x, cc_ref[...]).astype(BF)
    xi = _dot(x, sc_ref[...]).astype(BF)
    o_ref[...] = ((_dot(cl_ref[...], xr) - _dot(sl_ref[...], xi)) * scale).astype(o_ref.dtype)


def _fnet_dense_call(f):
    b, n, c = f.shape
    cl, sl = _dft_tables(n)
    cc, sc = _chan_tables()
    full = lambda bb: (0, 0)
    return pl.pallas_call(
        functools.partial(_fnet_dense_kernel, scale=1.0 / math.sqrt(n * FNET_GROUP_CH)),
        grid=(b,),
        in_specs=[pl.BlockSpec((None, n, c), lambda bb: (bb, 0, 0)),
                  pl.BlockSpec((n, n), full), pl.BlockSpec((n, n), full),
                  pl.BlockSpec((c, c), full), pl.BlockSpec((c, c), full)],
        out_specs=pl.BlockSpec((None, n, c), lambda bb: (bb, 0, 0)),
        out_shape=jax.ShapeDtypeStruct((b, n, c), BF),
        compiler_params=_cparams(1),
        name="fnet_ctx",
    )(f, jnp.asarray(cl, BF), jnp.asarray(sl, BF), cc, sc)


def _head_masks(width):
    lane = lax.broadcasted_iota(jnp.int32, (1, width), 1)
    return [(lane // NA_HEAD_DIM) == h for h in range(width // NA_HEAD_DIM)]


def _attend(q, k_parts, v_parts, bias_of, o_ref):
    zero = jnp.zeros((), BF)
    out = jnp.zeros(q.shape, F32)
    for h, hm in enumerate(_head_masks(q.shape[-1])):
        s_parts = []
        for j, kp in enumerate(k_parts):
            s = _dot_t(q, jnp.where(hm, kp, zero))
            b = bias_of(h, j)
            s_parts.append(s if b is None else s + b)
        m = s_parts[0].max(axis=-1, keepdims=True)
        for s in s_parts[1:]:
            m = jnp.maximum(m, s.max(axis=-1, keepdims=True))
        den = jnp.zeros_like(m)
        p_parts = []
        for s in s_parts:
            p = jnp.exp2(s - m)
            den = den + p.sum(axis=-1, keepdims=True)
            p_parts.append(p.astype(BF))
        vcat = jnp.concatenate([jnp.where(hm, vp, zero) for vp in v_parts], axis=0)
        out = out + _dot(jnp.concatenate(p_parts, axis=1), vcat) / den
    o_ref[...] = out.astype(o_ref.dtype)


def _natten_kernel(q_ref, k0, k1, k2, k3, v0, v1, v2, v3, kc_ref, vc_ref, bias_ref, o_ref):
    blk = k0.shape[0]
    k_parts = [r[...] for r in (k0, k1, k2, k3)] + [kc_ref[...]]
    v_parts = [r[...] for r in (v0, v1, v2, v3)] + [vc_ref[...]]

    def bias_of(h, j):
        if j >= 4:
            return None
        return bias_ref[h, :, j * blk:(j + 1) * blk].astype(F32)

    _attend(q_ref[...], k_parts, v_parts, bias_of, o_ref)


def _natten_call(q, k, v, kc, vc, bias, l):
    b, n, w = q.shape
    lc = kc.shape[1]
    hw = NA_HQ * NA_HEAD_DIM
    qt = NA_QROWS * GRID_W
    blk = (NA_KROWS // 4) * GRID_W
    nj = n // qt
    nblk = n // blk
    ratio = qt // blk

    def kv_spec(m):
        def imap(hq, j, bb):
            start = jnp.clip(ratio * j - 1, 0, nblk - 4)
            return (bb, start + m, hq)
        return pl.BlockSpec((None, blk, hw), imap)

    def bias_map(hq, j, bb):
        var = jnp.where(j == 0, 0, jnp.where(j == nj - 1, 2, 1))
        return (l, var, hq, 0, 0)

    return pl.pallas_call(
        _natten_kernel,
        grid=(w // hw, nj, b),
        in_specs=[pl.BlockSpec((None, qt, hw), lambda hq, j, bb: (bb, j, hq))]
                 + [kv_spec(m) for m in range(4)] + [kv_spec(m) for m in range(4)]
                 + [pl.BlockSpec((None, lc, hw), lambda hq, j, bb: (bb, 0, hq)),
                    pl.BlockSpec((None, lc, hw), lambda hq, j, bb: (bb, 0, hq)),
                    pl.BlockSpec((None, None, NA_HQ, qt, 4 * blk), bias_map)],
        out_specs=pl.BlockSpec((None, qt, hw), lambda hq, j, bb: (bb, j, hq)),
        out_shape=jax.ShapeDtypeStruct((b, n, w), BF),
        compiler_params=_cparams(3),
        name="natten",
    )(q, k, k, k, k, v, v, v, v, kc, vc, bias)


def _ctx_atten_kernel(q_ref, kc_ref, vc_ref, o_ref):
    _attend(q_ref[...], [kc_ref[...]], [vc_ref[...]], lambda h, j: None, o_ref)


def _ctx_atten_call(qc, kc, vc):
    b, lc, w = qc.shape
    hw = NA_HQ * NA_HEAD_DIM
    spec = pl.BlockSpec((None, lc, hw), lambda bb, hq: (bb, 0, hq))
    return pl.pallas_call(
        _ctx_atten_kernel,
        grid=(b, w // hw),
        in_specs=[spec, spec, spec],
        out_specs=spec,
        out_shape=jax.ShapeDtypeStruct((b, lc, w), BF),
        compiler_params=_cparams(2),
        name="ctx_atten",
    )(qc, kc, vc)


def _na_bias(rpb, rows):
    w = GRID_W
    nj = rows // NA_QROWS
    n_dr = 2 * WIN_ROWS - 1
    n_dc = 2 * WIN_COLS - 1
    cq = np.arange(w)
    cs = np.clip(cq - WIN_COLS // 2, 0, w - WIN_COLS)
    ck = np.arange(w)
    valid_c = (ck[None, :] >= cs[:, None]) & (ck[None, :] < cs[:, None] + WIN_COLS)
    dc = ck[None, :] - cq[:, None] + (WIN_COLS - 1)
    pick_c = (dc[:, :, None] == np.arange(n_dc)) & valid_c[:, :, None]
    tiles = jnp.einsum('lhab,cdb->lhacd', rpb.astype(F32) * LOG2E, jnp.asarray(pick_c, F32),
                       precision=lax.Precision.HIGHEST)
    tiles = tiles + jnp.asarray(np.where(valid_c, 0.0, NEG_BIG), F32)
    tiles = jnp.concatenate([tiles, jnp.full_like(tiles[:, :, :1], NEG_BIG)], axis=2).astype(BF)
    out = []
    for j in (0, min(1, nj - 1), nj - 1):
        rq = NA_QROWS * j + np.arange(NA_QROWS)
        ws = int(np.clip(NA_QROWS * j - (NA_KROWS - NA_QROWS) // 2, 0, rows - NA_KROWS))
        rk = ws + np.arange(NA_KROWS)
        r0 = np.clip(rq - WIN_ROWS // 2, 0, rows - WIN_ROWS)
        valid_r = (rk[None, :] >= r0[:, None]) & (rk[None, :] < r0[:, None] + WIN_ROWS)
        dr = np.where(valid_r, rk[None, :] - rq[:, None] + (WIN_ROWS - 1), n_dr)
        out.append(jnp.concatenate(
            [jnp.concatenate([tiles[:, :, int(dr[a, b])] for b in range(NA_KROWS)], axis=-1)
             for a in range(NA_QROWS)], axis=-2))
    return jnp.stack(out, axis=1)


def _rope_tables(n):
    nf = NA_HEAD_DIM // 4
    t = jnp.arange(n, dtype=jnp.int32)
    pos = jnp.stack([t // GRID_W, t % GRID_W], axis=-1).astype(F32)
    inv_freq = ROPE_THETA ** (-jnp.arange(nf, dtype=F32) / nf)
    ang = pos[:, :, None] * inv_freq
    cos = jnp.cos(ang)
    sin = jnp.sin(ang)
    cos_h = jnp.concatenate([cos, cos], axis=-1).reshape(n, NA_HEAD_DIM)
    sin_h = jnp.concatenate([-sin, sin], axis=-1).reshape(n, NA_HEAD_DIM)
    return jnp.tile(cos_h, (1, NA_HEADS)), jnp.tile(sin_h, (1, NA_HEADS))


def _merge_kernel(ysf_ref, ysb_ref, us_ref, yf_ref, yn_ref, g_ref, h_ref, t_ref,
                  sd_ref, wglu_ref, bglu_ref, wbs_ref, wbf_ref, wbn_ref, wo_ref, o_ref, ys_f32, *, alpha):
    d = h_ref.shape[-1]
    t = t_ref[...]
    halves = SSM_WIDTH // LANES
    tiles = [slice(r0, r0 + ROW_TILE) for r0 in range(0, h_ref.shape[0], ROW_TILE)]
    rows4 = ROW_TILE // S5_BLOCK
    ys = []
    for rows in tiles:
        q0 = rows.start // S5_BLOCK
        y4 = ysf_ref[q0:q0 + rows4, :].astype(F32) + ysb_ref[q0:q0 + rows4, :].astype(F32)
        for j in range(S5_BLOCK):
            for hf in range(halves):
                c0 = j * SSM_WIDTH + hf * LANES
                ys_f32[hf, pl.ds(rows.start + j, rows4, stride=S5_BLOCK), :] = y4[:, c0:c0 + LANES]
        y = jnp.concatenate([ys_f32[hf, rows, :] for hf in range(halves)], axis=-1)
        ys.append(jax.nn.gelu(y + sd_ref[...] * us_ref[rows, :].astype(F32)))
    glu = [_dot(y.astype(BF), wglu_ref[...]) for y in ys]
    ys = [(y * jax.nn.sigmoid(z + bglu_ref[...])).astype(BF) for y, z in zip(ys, glu)]
    br = [(_dot(y, wbs_ref[...]), _dot(yf_ref[rows, :], wbf_ref[...]), _dot(yn_ref[rows, :], wbn_ref[...]))
          for y, rows in zip(ys, tiles)]
    mix = [(g_ref[rows, 0:d].astype(F32) * bs + g_ref[rows, d:2 * d].astype(F32) * bf
            + g_ref[rows, 2 * d:3 * d].astype(F32) * bn).astype(BF) for (bs, bf, bn), rows in zip(br, tiles)]
    proj = [_dot(m, wo_ref[...]) for m in mix]
    for y, rows in zip(proj, tiles):
        v = alpha * h_ref[rows, :] + t[2:3] * y
        o_ref[rows, :] = _ln(v) * t[3:4] + t[4:5]


def _merge_call(ysf4, ysb4, us, yf, yn, g, h, tab, s5w, mats, l, alpha):
    nb, n, d = h.shape
    tm = min(MERGE_ROWS, n)

    def tok(a):
        return pl.BlockSpec((None, tm, a.shape[-1]), lambda b, i: (b, i, 0))

    def tok4(a):
        return pl.BlockSpec((None, tm // S5_BLOCK, a.shape[-1]), lambda b, i: (b, i, 0))

    toks = (us, yf, yn, g, h)
    return pl.pallas_call(
        functools.partial(_merge_kernel, alpha=alpha),
        grid=(nb, n // tm),
        in_specs=[tok4(ysf4), tok4(ysb4)] + [tok(a) for a in toks]
                 + [pl.BlockSpec((None, 8, d), lambda b, i: (b, 0, 0))]
                 + [_layer_spec(a, l) for a in tuple(s5w) + tuple(mats)],
        out_specs=tok(h),
        out_shape=jax.ShapeDtypeStruct(h.shape, F32),
        scratch_shapes=[pltpu.VMEM((SSM_WIDTH // LANES, tm, LANES), F32)],
        compiler_params=_cparams(2),
        name="merge",
    )(ysf4, ysb4, *toks, tab, *s5w, *mats)


def _table(mod, d, chunks, extra):
    rows = [mod[:, i * d:(i + 1) * d] for i in chunks]
    rows += [jnp.broadcast_to(e[None, :], (8, d)) for e in extra]
    rows += [jnp.zeros((8, d), F32)] * (8 - len(rows))
    return jnp.stack(rows, axis=1)


def kernel(x, c, ctx, c_ctx, w_mod, b_mod, ln_g, ln_b, ffn_w_gate, ffn_w_up, ffn_w_down, w_in, ssm_log_dt, ssm_a_re, ssm_a_im, ssm_b_re, ssm_b_im, ssm_c_re, ssm_c_im, ssm_d, ssm_w_glu, ssm_b_glu, na_rpb, w_br_ssm, w_br_fnet, w_br_na, w_out):
    bsz, n, d = x.shape
    lc = ctx.shape[1]
    depth = w_mod.shape[0]
    rows = n // GRID_W
    assert bsz + 1 <= 8 and rows % NA_QROWS == 0 and rows >= NA_KROWS and n % FNET_N2 == 0
    alpha = (2 * depth) ** 0.25
    q_scale = NA_HEAD_DIM ** -0.5 * LOG2E

    c8 = jnp.concatenate([c, c_ctx[None, :], jnp.zeros((8 - bsz - 1, d), F32)], axis=0)
    mod_all = _mod_call(c8, w_mod, b_mod)

    wg = ffn_w_gate.astype(BF)
    wu = ffn_w_up.astype(BF)
    wd = ffn_w_down.astype(BF)
    win = w_in.astype(BF)
    mats = (w_br_ssm.astype(BF), w_br_fnet.astype(BF), w_br_na.astype(BF), w_out.astype(BF))
    s5w = (ssm_d[:, None, :], ssm_w_glu.astype(BF), ssm_b_glu[:, None, :])
    s5_tabs = _s5_tables(ssm_log_dt, ssm_a_re, ssm_a_im, ssm_b_re, ssm_b_im, ssm_c_re, ssm_c_im)
    na_bias = _na_bias(na_rpb, rows)
    cos_t, sin_t = _rope_tables(n)
    cos_c = jnp.ones((bsz * lc, NA_WIDTH), F32)
    sin_c = jnp.zeros((bsz * lc, NA_WIDTH), F32)
    zero_state = jnp.zeros((bsz, 8, 2 * SSM_FLAT), F32)

    h = x
    hc = ctx.reshape(1, bsz * lc, d)
    for l in range(depth):
        last = l == depth - 1
        mod = mod_all[l]

        tab = _table(mod, d, (0, 1, 2), (ln_g[l, 0], ln_b[l, 0]))
        h = _ffn_call(h, tab[:bsz], wg, wu, wd, l, 0, alpha)
        hc = _ffn_call(hc, tab[bsz:bsz + 1], wg, wu, wd, l, 0, alpha)

        tab = _table(mod, d, (3, 4, 5), (ln_g[l, 1], ln_b[l, 1]))
        us, u4, k, v, f, q, g = _mixin_call(h, tab[:bsz], win, l, cos_t, sin_t, q_scale)
        usc, u4c, kc, vc, fc, qc, gc = _mixin_call(hc, tab[bsz:bsz + 1], win, l, cos_c, sin_c, q_scale)
        u4c, kc, vc, fc, qc = [a.reshape(bsz, a.shape[1] // bsz, a.shape[-1]) for a in (u4c, kc, vc, fc, qc)]

        ysbc, hcb = _s5_call(u4c, zero_state, s5_tabs, l, 1, u4c.shape[1])
        ysfc, hcf = _s5_call(u4c, zero_state, s5_tabs, l, 0, u4c.shape[1])
        ysb, _ = _s5_call(u4, hcb, s5_tabs, l, 1, min(S5_ROWS, u4.shape[1]))
        ysf, _ = _s5_call(u4, hcf, s5_tabs, l, 0, min(S5_ROWS, u4.shape[1]))

        yf = _fnet_call(f)
        yn = _natten_call(q, k, v, kc, vc, na_bias, l)

        h = _merge_call(ysf, ysb, us, yf, yn, g, h, tab[:bsz], s5w, mats, l, alpha)

        tab3 = _table(mod, d, (6, 7, 8), (ln_g[l, 2], ln_b[l, 2]))
        h = _ffn_call(h, tab3[:bsz], wg, wu, wd, l, 1, alpha)

        if not last:
            yfc = _fnet_dense_call(fc)
            ync = _ctx_atten_call(qc, kc, vc)
            flat = lambda a: a.reshape(1, a.shape[0] * a.shape[1], a.shape[-1])
            hc = _merge_call(flat(ysfc), flat(ysbc), usc, flat(yfc), flat(ync), gc, hc, tab[bsz:bsz + 1],
                             s5w, mats, l, alpha)
            hc = _ffn_call(hc, tab3[bsz:bsz + 1], wg, wu, wd, l, 1, alpha)
    return h
```

```python
import functools
import math

import jax
import jax.numpy as jnp
import numpy as np
from jax import lax
from jax.experimental import pallas as pl
from jax.experimental.pallas import tpu as pltpu

BF = jnp.bfloat16
F32 = jnp.float32

GRID_W = 64
SSM_GROUPS = 16
SSM_GROUP_CH = 16
SSM_STATE = 64
SSM_WIDTH = SSM_GROUPS * SSM_GROUP_CH
SSM_FLAT = SSM_GROUPS * SSM_STATE
FNET_GROUPS = 4
FNET_GROUP_CH = 64
FNET_WIDTH = FNET_GROUPS * FNET_GROUP_CH
NA_HEADS = 8
NA_HEAD_DIM = 64
NA_WIDTH = NA_HEADS * NA_HEAD_DIM
WIN_ROWS = 8
WIN_COLS = 16
ROPE_THETA = 10000.0
N_MOD = 9
LN_EPS = 1e-6
LOG2E = 1.4426950408889634
NEG_BIG = -1e30

VMEM_LIMIT_BYTES = 56 * 1024 * 1024
ROW_TILE = 512
FFN_ROWS = 1024
MERGE_ROWS = 1024
MIXIN_ROWS = 1024
FFN_CHUNK = 256
LANES = 128
SCAN_ROWS = 8
S5_BLOCK = 4
S5_ROWS = 512
S5_SUB = 256
NA_QROWS = 8
NA_KROWS = 16
NA_HQ = 4
FNET_N2 = 128


def _cparams(n_axes, flags=None):
    return pltpu.CompilerParams(dimension_semantics=("arbitrary",) * n_axes,
                                vmem_limit_bytes=VMEM_LIMIT_BYTES, flags=flags)


def _ln(x):
    mu = jnp.mean(x, axis=-1, keepdims=True)
    xc = x - mu
    var = jnp.mean(xc * xc, axis=-1, keepdims=True)
    return xc * lax.rsqrt(var + LN_EPS)


def _dot(a, b):
    return jnp.dot(a, b, preferred_element_type=F32)


def _dot_t(a, b):
    return lax.dot_general(a, b, (((1,), (1,)), ((), ())), preferred_element_type=F32)


def _layer_spec(a, *lead, mode=None):
    nl = len(lead)
    zeros = (0,) * (a.ndim - nl)
    return pl.BlockSpec((None,) * nl + tuple(a.shape[nl:]), lambda *_: tuple(lead) + zeros, pipeline_mode=mode)


def _mod_kernel(c_ref, w_ref, b_ref, o_ref):
    c = c_ref[...]
    s = (c * jax.nn.sigmoid(c)).astype(BF)
    o_ref[...] = _dot(s, w_ref[...].astype(BF)) + b_ref[...]


def _mod_call(c8, w_mod, b_mod):
    depth, d, nd = w_mod.shape
    nb = nd // d
    return pl.pallas_call(
        _mod_kernel,
        grid=(depth, nb),
        in_specs=[pl.BlockSpec((8, d), lambda l, j: (0, 0)),
                  pl.BlockSpec((None, d, d), lambda l, j: (l, 0, j)),
                  pl.BlockSpec((None, 1, d), lambda l, j: (l, 0, j))],
        out_specs=pl.BlockSpec((None, 8, d), lambda l, j: (l, 0, j)),
        out_shape=jax.ShapeDtypeStruct((depth, 8, nd), F32),
        compiler_params=_cparams(2),
        name="mod",
    )(c8, w_mod, b_mod.reshape(depth, 1, nd))


def _ffn_kernel(x_ref, t_ref, wg_ref, wu_ref, wd_ref, o_ref, act_scr, *, alpha):
    t = t_ref[...]
    tiles = [slice(r0, r0 + ROW_TILE) for r0 in range(0, x_ref.shape[0], ROW_TILE)]
    us = [(_ln(x_ref[rows, :]) * (1.0 + t[1:2]) + t[0:1]).astype(BF) for rows in tiles]
    for c0 in range(0, wg_ref.shape[1], FFN_CHUNK):
        for k, u in enumerate(us):
            a = _dot(u, wg_ref[:, c0:c0 + FFN_CHUNK])
            b = _dot(u, wu_ref[:, c0:c0 + FFN_CHUNK])
            act_scr[k, :, c0:c0 + FFN_CHUNK] = (a * jax.nn.sigmoid(a) * b).astype(BF)
    ys = [_dot(act_scr[k], wd_ref[...]) for k in range(len(tiles))]
    for y, rows in zip(ys, tiles):
        v = alpha * x_ref[rows, :] + (0.5 * t[2:3]) * y
        o_ref[rows, :] = _ln(v) * t[3:4] + t[4:5]


def _ffn_call(x, tab, wg, wu, wd, l, k, alpha):
    nb, n, d = x.shape
    tm = min(FFN_ROWS, n)
    once = pl.Buffered(1)
    return pl.pallas_call(
        functools.partial(_ffn_kernel, alpha=alpha),
        grid=(nb, n // tm),
        in_specs=[pl.BlockSpec((None, tm, d), lambda b, i: (b, i, 0)),
                  pl.BlockSpec((None, 8, d), lambda b, i: (b, 0, 0)),
                  _layer_spec(wg, l, k, mode=once), _layer_spec(wu, l, k, mode=once),
                  _layer_spec(wd, l, k, mode=once)],
        out_specs=pl.BlockSpec((None, tm, d), lambda b, i: (b, i, 0)),
        out_shape=jax.ShapeDtypeStruct(x.shape, F32),
        scratch_shapes=[pltpu.VMEM((pl.cdiv(tm, ROW_TILE), min(ROW_TILE, tm), wg.shape[-1]), BF)],
        compiler_params=_cparams(2),
        name="ffn",
    )(x, tab, wg, wu, wd)


def _rope(x, cos, sin_signed):
    n = x.shape[-1]
    lane = lax.broadcasted_iota(jnp.int32, x.shape, 1)
    first = (lane % 32) < 16
    partner = jnp.where(first, pltpu.roll(x, n - 16, axis=1), pltpu.roll(x, 16, axis=1))
    return x * cos + partner * sin_signed


def _mixin_kernel(x_ref, t_ref, w_ref, cos_ref, sin_ref,
                  us_ref, u4_ref, k_ref, v_ref, f_ref, q_ref, g_ref, us_f32, *, q_scale):
    t = t_ref[...]
    c_k = SSM_WIDTH
    c_v = c_k + NA_WIDTH
    c_f = c_v + NA_WIDTH
    c_q = c_f + FNET_WIDTH
    c_g = c_q + NA_WIDTH
    tiles = [slice(r0, r0 + ROW_TILE) for r0 in range(0, x_ref.shape[0], ROW_TILE)]
    us = [(_ln(x_ref[rows, :]) * (1.0 + t[1:2]) + t[0:1]).astype(BF) for rows in tiles]
    rows4 = ROW_TILE // S5_BLOCK
    for u, rows in zip(us, tiles):
        zs = _dot(u, w_ref[:, 0:c_k])
        us_ref[rows, :] = zs.astype(BF)
        for hf in range(c_k // LANES):
            us_f32[hf, rows, :] = zs[:, hf * LANES:(hf + 1) * LANES]
        q0 = rows.start // S5_BLOCK
        for j in range(S5_BLOCK):
            for hf in range(c_k // LANES):
                c0 = j * c_k + hf * LANES
                u4_ref[q0:q0 + rows4, c0:c0 + LANES] = (
                    us_f32[hf, pl.ds(rows.start + j, rows4, stride=S5_BLOCK), :].astype(BF))
    for u, rows in zip(us, tiles):
        k_ref[rows, :] = _rope(_dot(u, w_ref[:, c_k:c_v]), cos_ref[rows, :], sin_ref[rows, :]).astype(BF)
    for u, rows in zip(us, tiles):
        v_ref[rows, :] = _dot(u, w_ref[:, c_v:c_f]).astype(BF)
    for u, rows in zip(us, tiles):
        f_ref[rows, :] = _dot(u, w_ref[:, c_f:c_q]).astype(BF)
    for u, rows in zip(us, tiles):
        q = _rope(_dot(u, w_ref[:, c_q:c_g]), cos_ref[rows, :], sin_ref[rows, :])
        q_ref[rows, :] = (q * q_scale).astype(BF)
    gw = NA_WIDTH
    for c0 in range(0, g_ref.shape[-1], gw):
        for u, rows in zip(us, tiles):
            g = _dot(u, w_ref[:, c_g + c0:c_g + c0 + gw])
            g_ref[rows, c0:c0 + gw] = jax.nn.sigmoid(g).astype(BF)


def _mixin_call(x, tab, w_in, l, cos_t, sin_t, q_scale):
    nb, n, d = x.shape
    win = w_in.shape[-1]
    ng = win - (SSM_WIDTH + 3 * NA_WIDTH + FNET_WIDTH)
    tm = min(MIXIN_ROWS, n)
    nt_pos = cos_t.shape[0] // tm

    def tok(width):
        return pl.BlockSpec((None, tm, width), lambda b, i: (b, i, 0))

    def shape(width):
        return jax.ShapeDtypeStruct((nb, n, width), BF)

    u4_spec = pl.BlockSpec((None, tm // S5_BLOCK, S5_BLOCK * SSM_WIDTH), lambda b, i: (b, i, 0))
    u4_shape = jax.ShapeDtypeStruct((nb, n // S5_BLOCK, S5_BLOCK * SSM_WIDTH), BF)
    return pl.pallas_call(
        functools.partial(_mixin_kernel, q_scale=q_scale),
        grid=(nb, n // tm),
        in_specs=[tok(d),
                  pl.BlockSpec((None, 8, d), lambda b, i: (b, 0, 0)),
                  _layer_spec(w_in, l, mode=pl.Buffered(1)),
                  pl.BlockSpec((tm, NA_WIDTH), lambda b, i: (i % nt_pos, 0)),
                  pl.BlockSpec((tm, NA_WIDTH), lambda b, i: (i % nt_pos, 0))],
        out_specs=[tok(SSM_WIDTH), u4_spec, tok(NA_WIDTH), tok(NA_WIDTH), tok(FNET_WIDTH), tok(NA_WIDTH),
                   tok(ng)],
        out_shape=[shape(SSM_WIDTH), u4_shape, shape(NA_WIDTH), shape(NA_WIDTH), shape(FNET_WIDTH),
                   shape(NA_WIDTH), shape(ng)],
        scratch_shapes=[pltpu.VMEM((SSM_WIDTH // LANES, tm, LANES), F32)],
        compiler_params=_cparams(2),
        name="mixin",
    )(x, tab, w_in, cos_t, sin_t)


def _s5_kernel(u_ref, h0_ref, wst_ref, wconv_ref, wout_ref, sc_ref, y_ref, hfin_ref, hbuf, hprev, carry,
               *, reverse):
    i = pl.program_id(1)
    ns = SSM_FLAT
    rows = hbuf.shape[0]
    sub = min(S5_SUB, rows)
    order = (lambda r: list(r)[::-1]) if reverse else list
    subs = order(range(0, rows, sub))

    @pl.when(i == 0)
    def _():
        carry[...] = h0_ref[...]

    for s0 in subs:
        hbuf[s0:s0 + sub, :] = _dot(u_ref[s0:s0 + sub, :], wst_ref[...])
    row_id = lax.broadcasted_iota(jnp.int32, (SCAN_ROWS, ns), 0)
    edge = row_id == ((SCAN_ROWS - 1) if reverse else 0)

    cr = carry[:, 0:ns]
    ci = carry[:, ns:2 * ns]
    for s0 in subs:
        for r0 in order(range(s0, s0 + sub, SCAN_ROWS)):
            xr = hbuf[r0:r0 + SCAN_ROWS, 0:ns]
            xi = hbuf[r0:r0 + SCAN_ROWS, ns:2 * ns]
            for k, s in enumerate((1, 2, 4)):
                ar = sc_ref[k, :, 0:ns]
                ai = sc_ref[k, :, ns:2 * ns]
                shift = (SCAN_ROWS - s) if reverse else s
                sr = pltpu.roll(xr, shift, axis=0)
                si = pltpu.roll(xi, shift, axis=0)
                xr, xi = xr + ar * sr - ai * si, xi + ar * si + ai * sr
            pr = sc_ref[3, :, 0:ns]
            pi = sc_ref[3, :, ns:2 * ns]
            hr = xr + pr * cr - pi * ci
            hi = xi + pr * ci + pi * cr
            one = (SCAN_ROWS - 1) if reverse else 1
            hprev[r0:r0 + SCAN_ROWS, 0:ns] = jnp.where(edge, cr, pltpu.roll(hr, one, axis=0))
            hprev[r0:r0 + SCAN_ROWS, ns:2 * ns] = jnp.where(edge, ci, pltpu.roll(hi, one, axis=0))
            last = 0 if reverse else SCAN_ROWS - 1
            cr = jnp.broadcast_to(hr[last:last + 1, :], hr.shape)
            ci = jnp.broadcast_to(hi[last:last + 1, :], hi.shape)
    carry[:, 0:ns] = cr
    carry[:, ns:2 * ns] = ci
    hfin_ref[...] = carry[...]

    for s0 in subs:
        y = (_dot(u_ref[s0:s0 + sub, :], wconv_ref[...])
             + _dot(hprev[s0:s0 + sub, :].astype(BF), wout_ref[...]))
        y_ref[s0:s0 + sub, :] = y.astype(y_ref.dtype)


def _s5_call(u4, h0, tables, l, z, rows):
    wst, wconv, wout, sc = tables
    b, n, w = u4.shape
    nc = n // rows
    ns2 = 2 * SSM_FLAT
    reverse = z == 1
    if reverse:
        cmap = lambda bb, i: (bb, nc - 1 - i, 0)
    else:
        cmap = lambda bb, i: (bb, i, 0)
    return pl.pallas_call(
        functools.partial(_s5_kernel, reverse=reverse),
        grid=(b, nc),
        in_specs=[pl.BlockSpec((None, rows, w), cmap),
                  pl.BlockSpec((None, 8, ns2), lambda bb, i: (bb, 0, 0)),
                  _layer_spec(wst, l, z), _layer_spec(wconv, l, z), _layer_spec(wout, l, z),
                  _layer_spec(sc, l, z)],
        out_specs=[pl.BlockSpec((None, rows, w), cmap),
                   pl.BlockSpec((None, 8, ns2), lambda bb, i: (bb, 0, 0))],
        out_shape=[jax.ShapeDtypeStruct((b, n, w), BF),
                   jax.ShapeDtypeStruct((b, 8, ns2), F32)],
        scratch_shapes=[pltpu.VMEM((rows, ns2), F32), pltpu.VMEM((rows, ns2), F32), pltpu.VMEM((8, ns2), F32)],
        compiler_params=_cparams(2),
        name="s5_bwd" if reverse else "s5_fwd",
    )(u4, h0, wst, wconv, wout, sc)


def _s5_tables(log_dt, a_re, a_im, b_re, b_im, c_re, c_im):
    depth = log_dt.shape[0]
    g, p, c = b_re.shape[-3:]
    tb = S5_BLOCK
    hp = lax.Precision.HIGHEST
    f32 = lambda a: a.astype(F32)
    a_re, a_im, b_re, b_im, c_re, c_im = map(f32, (a_re, a_im, b_re, b_im, c_re, c_im))
    dt = jnp.exp(f32(log_dt))[..., None]
    adt_re = a_re * dt
    adt_im = a_im * dt

    def apow(k):
        kk = jnp.asarray(k, F32)[None, :, :, None, None]
        mag = jnp.exp(kk * adt_re[:, :, None])
        ang = kk * adt_im[:, :, None]
        return mag * jnp.cos(ang), mag * jnp.sin(ang)

    mag = jnp.exp(adt_re)
    ab_re = mag * jnp.cos(adt_im)
    ab_im = mag * jnp.sin(adt_im)
    den = a_re * a_re + a_im * a_im
    nr = ab_re - 1.0
    fr = (nr * a_re + ab_im * a_im) / den
    fi = (ab_im * a_re - nr * a_im) / den
    bb_re = fr[..., None] * b_re - fi[..., None] * b_im
    bb_im = fr[..., None] * b_im + fi[..., None] * b_re

    step = np.arange(tb)
    k_state = np.stack([tb - 1 - step, step])
    k_out = np.stack([step + 1, tb - step])
    k_tap = np.stack([step, step])
    k_scan = np.stack([tb * (np.arange(SCAN_ROWS) + 1)] * 2)

    def block_diag(compact, row_group, col_width):
        n = compact.shape[3]
        dst = np.arange(n * g * col_width)
        src = (dst // (g * col_width)) * col_width + dst % col_width
        copy = np.zeros((n * col_width, n * g * col_width), np.float32)
        copy[src, dst] = 1.0
        cols = jnp.einsum('dzrk,kn->dzrn', compact.reshape(compact.shape[:3] + (n * col_width,)),
                          jnp.asarray(copy), precision=hp)
        keep = jnp.asarray(row_group[:, None] == ((dst // col_width) % g)[None, :])
        return jnp.where(keep, cols, 0.0).astype(BF)

    chan_rows = (np.arange(tb * g * c) // c) % g
    state_rows = (np.arange(2 * g * p) // p) % g

    def cmul_b(pr, pi):
        return (pr[..., None] * bb_re[:, :, None] - pi[..., None] * bb_im[:, :, None],
                pr[..., None] * bb_im[:, :, None] + pi[..., None] * bb_re[:, :, None])

    wr, wi = cmul_b(*apow(k_state))
    w2 = jnp.stack([wr, wi], axis=3)
    w2 = jnp.transpose(w2, (0, 1, 2, 4, 6, 3, 5))
    wst = block_diag(w2.reshape(depth, 2, tb * g * c, 2, p), chan_rows, p)

    pr, pi = apow(k_out)
    cm_re = c_re[:, :, None] * pr[:, :, :, :, None, :] - c_im[:, :, None] * pi[:, :, :, :, None, :]
    cm_im = c_re[:, :, None] * pi[:, :, :, :, None, :] + c_im[:, :, None] * pr[:, :, :, :, None, :]
    cm = jnp.stack([cm_re, -cm_im], axis=3)
    cm = jnp.transpose(cm, (0, 1, 3, 4, 6, 2, 5))
    wout = block_diag(cm.reshape(depth, 2, 2 * g * p, tb, c), state_rows, c)

    abr, abi = cmul_b(*apow(k_tap))
    taps = (jnp.einsum('dzgcp,dztgpe->dztgec', c_re, abr, precision=hp)
            - jnp.einsum('dzgcp,dztgpe->dztgec', c_im, abi, precision=hp))
    sel = np.zeros((2, tb, tb, tb), np.float32)
    for i in range(tb):
        for j in range(tb):
            if j >= i:
                sel[0, i, j, j - i] = 1.0
            if i >= j:
                sel[1, i, j, i - j] = 1.0
    kij = jnp.einsum('dztgec,zijt->dzigejc', taps, jnp.asarray(sel), precision=hp)
    wconv = block_diag(kij.reshape(depth, 2, tb * g * c, tb, c), chan_rows, c)

    qr, qi = apow(k_scan)
    qr = qr.reshape(depth, 2, SCAN_ROWS, g * p)
    qi = qi.reshape(depth, 2, SCAN_ROWS, g * p)
    rows = np.arange(SCAN_ROWS)
    tabs = []
    for s in (1, 2, 4):
        keep = jnp.asarray(np.stack([rows >= s, rows + s <= SCAN_ROWS - 1]), F32)[None, :, :, None]
        tabs.append(jnp.concatenate([keep * qr[:, :, s - 1:s], keep * qi[:, :, s - 1:s]], axis=-1))
    carry_pow = lambda t: jnp.stack([t[:, 0], t[:, 1, ::-1]], axis=1)
    tabs.append(jnp.concatenate([carry_pow(qr), carry_pow(qi)], axis=-1))
    sc = jnp.stack(tabs, axis=2)
    return wst, wconv, wout, sc


def _fnet1_kernel(x_ref, f2_ref, tc_ref, ts_ref, o_ref):
    n2 = x_ref.shape[0]
    y = _dot(f2_ref[...], x_ref[...])
    br = y[0:n2]
    bi = y[n2:2 * n2]
    tc = tc_ref[...]
    ts = ts_ref[...]
    o_ref[0] = (br * tc + bi * ts).astype(BF)
    o_ref[1] = (bi * tc - br * ts).astype(BF)


def _fnet2_kernel(b_ref, f1_ref, cc_ref, sc_ref, o_ref, *, scale):
    kb = b_ref.shape[1]
    n1 = b_ref.shape[2]
    c = b_ref.shape[3]
    zr, zi = [], []
    for j in range(kb):
        rhs = jnp.concatenate([b_ref[0, j], b_ref[1, j]], axis=0)
        z = _dot(f1_ref[...], rhs)
        zr.append(z[0:n1])
        zi.append(z[n1:2 * n1])
    zr = jnp.concatenate(zr, axis=0).astype(BF)
    zi = jnp.concatenate(zi, axis=0).astype(BF)
    y = (_dot(zr, cc_ref[...]) + _dot(zi, sc_ref[...])) * scale
    for j in range(kb):
        o_ref[:, j * c:(j + 1) * c] = y[j * n1:(j + 1) * n1].astype(o_ref.dtype)


def _dft_tables(n):
    idx = np.arange(n)
    ang = 2.0 * np.pi * ((idx[:, None] * idx[None, :]) % n) / n
    return np.cos(ang), np.sin(ang)


def _chan_tables():
    cc, sc = _dft_tables(FNET_GROUP_CH)
    eye = np.eye(FNET_GROUPS)
    return jnp.asarray(np.kron(eye, cc), BF), jnp.asarray(np.kron(eye, sc), BF)


def _fnet_call(f):
    b, n, c = f.shape
    n2 = FNET_N2
    n1 = n // n2
    c2, s2 = _dft_tables(n2)
    f2 = jnp.asarray(np.concatenate([c2, -s2], axis=0), BF)
    k2 = np.arange(n2)[:, None]
    nn1 = np.arange(n1)[None, :]
    ang = 2.0 * np.pi * ((k2 * nn1) % n) / n
    tc = jnp.asarray(np.repeat(np.cos(ang), c, axis=1), F32)
    ts = jnp.asarray(np.repeat(np.sin(ang), c, axis=1), F32)
    c1, s1 = _dft_tables(n1)
    f1 = jnp.asarray(np.block([[c1, s1], [-s1, c1]]), BF)
    cc, sc = _chan_tables()

    wc = min(n1 * c, 4096)
    x2 = f.reshape(b, n2, n1 * c)
    st1 = pl.pallas_call(
        _fnet1_kernel,
        grid=(b, (n1 * c) // wc),
        in_specs=[pl.BlockSpec((None, n2, wc), lambda bb, j: (bb, 0, j)),
                  pl.BlockSpec((2 * n2, n2), lambda bb, j: (0, 0)),
                  pl.BlockSpec((n2, wc), lambda bb, j: (0, j)),
                  pl.BlockSpec((n2, wc), lambda bb, j: (0, j))],
        out_specs=pl.BlockSpec((None, 2, n2, wc), lambda bb, j: (bb, 0, 0, j)),
        out_shape=jax.ShapeDtypeStruct((b, 2, n2, n1 * c), BF),
        compiler_params=_cparams(2),
        name="fnet1",
    )(x2, f2, tc, ts)
    st1 = st1.reshape(b, 2, n2, n1, c)
    kb = 8
    y = pl.pallas_call(
        functools.partial(_fnet2_kernel, scale=1.0 / math.sqrt(n * FNET_GROUP_CH)),
        grid=(b, n2 // kb),
        in_specs=[pl.BlockSpec((None, 2, kb, n1, c), lambda bb, j: (bb, 0, j, 0, 0)),
                  pl.BlockSpec((2 * n1, 2 * n1), lambda bb, j: (0, 0)),
                  pl.BlockSpec((c, c), lambda bb, j: (0, 0)),
                  pl.BlockSpec((c, c), lambda bb, j: (0, 0))],
        out_specs=pl.BlockSpec((None, n1, kb * c), lambda bb, j: (bb, 0, j)),
        out_shape=jax.ShapeDtypeStruct((b, n1, n2 * c), BF),
        compiler_params=_cparams(2),
        name="fnet2",
    )(st1, f1, cc, sc)
    return y.reshape(b, n, c)


def _fnet_dense_kernel(x_ref, cl_ref, sl_ref, cc_ref, sc_ref, o_ref, *, scale):
    x = x_ref[...]
    xr = _dot(x, cc_ref[...]).astype(BF)
    xi = _dot(x, sc_ref[...]).astype(BF)
    o_ref[...] = ((_dot(cl_ref[...], xr) - _dot(sl_ref[...], xi)) * scale).astype(o_ref.dtype)


def _fnet_dense_call(f):
    b, n, c = f.shape
    cl, sl = _dft_tables(n)
    cc, sc = _chan_tables()
    full = lambda bb: (0, 0)
    return pl.pallas_call(
        functools.partial(_fnet_dense_kernel, scale=1.0 / math.sqrt(n * FNET_GROUP_CH)),
        grid=(b,),
        in_specs=[pl.BlockSpec((None, n, c), lambda bb: (bb, 0, 0)),
                  pl.BlockSpec((n, n), full), pl.BlockSpec((n, n), full),
                  pl.BlockSpec((c, c), full), pl.BlockSpec((c, c), full)],
        out_specs=pl.BlockSpec((None, n, c), lambda bb: (bb, 0, 0)),
        out_shape=jax.ShapeDtypeStruct((b, n, c), BF),
        compiler_params=_cparams(1),
        name="fnet_ctx",
    )(f, jnp.asarray(cl, BF), jnp.asarray(sl, BF), cc, sc)


def _head_masks(width):
    lane = lax.broadcasted_iota(jnp.int32, (1, width), 1)
    return [(lane // NA_HEAD_DIM) == h for h in range(width // NA_HEAD_DIM)]


def _attend(q, k_parts, v_parts, bias_of, o_ref):
    zero = jnp.zeros((), BF)
    masks = _head_masks(q.shape[-1])
    vcat = jnp.concatenate(v_parts, axis=0)

    def scores(h):
        qh = jnp.where(masks[h], q, zero)
        s_parts = []
        for j, kp in enumerate(k_parts):
            s = _dot_t(qh, kp)
            b = bias_of(h, j)
            s_parts.append(s if b is None else s + b)
        return s_parts

    def softmax(s_parts):
        m = s_parts[0].max(axis=-1, keepdims=True)
        for s in s_parts[1:]:
            m = jnp.maximum(m, s.max(axis=-1, keepdims=True))
        den = jnp.zeros_like(m)
        p_parts = []
        for s in s_parts:
            p = jnp.exp2(s - m)
            den = den + p.sum(axis=-1, keepdims=True)
            p_parts.append(p.astype(BF))
        return jnp.concatenate(p_parts, axis=1), den

    def values(p, den):
        return _dot(p, vcat) / den

    n = len(masks)
    s_of, p_of, out = {}, {}, None
    for step in range(n + 2):
        if step < n:
            s_of[step] = scores(step)
        if 0 <= step - 1 < n:
            p_of[step - 1] = softmax(s_of.pop(step - 1))
        if 0 <= step - 2 < n:
            o = values(*p_of.pop(step - 2))
            out = o if out is None else jnp.where(masks[step - 2], o, out)
    o_ref[...] = out.astype(o_ref.dtype)


def _natten_kernel(q_ref, k0, k1, k2, k3, v0, v1, v2, v3, kc_ref, vc_ref, bias_ref, o_ref):
    blk = k0.shape[0]
    k_parts = [r[...] for r in (k0, k1, k2, k3)] + [kc_ref[...]]
    v_parts = [r[...] for r in (v0, v1, v2, v3)] + [vc_ref[...]]

    def bias_of(h, j):
        if j >= 4:
            return None
        return bias_ref[h, :, j * blk:(j + 1) * blk].astype(F32)

    _attend(q_ref[...], k_parts, v_parts, bias_of, o_ref)


def _natten_call(q, k, v, kc, vc, bias, l):
    b, n, w = q.shape
    lc = kc.shape[1]
    hw = NA_HQ * NA_HEAD_DIM
    qt = NA_QROWS * GRID_W
    blk = (NA_KROWS // 4) * GRID_W
    nj = n // qt
    nblk = n // blk
    ratio = qt // blk

    def kv_spec(m):
        def imap(hq, j, bb):
            start = jnp.clip(ratio * j - 1, 0, nblk - 4)
            return (bb, start + m, hq)
        return pl.BlockSpec((None, blk, hw), imap)

    def bias_map(hq, j, bb):
        var = jnp.where(j == 0, 0, jnp.where(j == nj - 1, 2, 1))
        return (l, var, hq, 0, 0)

    return pl.pallas_call(
        _natten_kernel,
        grid=(w // hw, nj, b),
        in_specs=[pl.BlockSpec((None, qt, hw), lambda hq, j, bb: (bb, j, hq))]
                 + [kv_spec(m) for m in range(4)] + [kv_spec(m) for m in range(4)]
                 + [pl.BlockSpec((None, lc, hw), lambda hq, j, bb: (bb, 0, hq)),
                    pl.BlockSpec((None, lc, hw), lambda hq, j, bb: (bb, 0, hq)),
                    pl.BlockSpec((None, None, NA_HQ, qt, 4 * blk), bias_map)],
        out_specs=pl.BlockSpec((None, qt, hw), lambda hq, j, bb: (bb, j, hq)),
        out_shape=jax.ShapeDtypeStruct((b, n, w), BF),
        compiler_params=_cparams(3),
        name="natten",
    )(q, k, k, k, k, v, v, v, v, kc, vc, bias)


def _ctx_atten_kernel(q_ref, kc_ref, vc_ref, o_ref):
    _attend(q_ref[...], [kc_ref[...]], [vc_ref[...]], lambda h, j: None, o_ref)


def _ctx_atten_call(qc, kc, vc):
    b, lc, w = qc.shape
    hw = NA_HQ * NA_HEAD_DIM
    spec = pl.BlockSpec((None, lc, hw), lambda bb, hq: (bb, 0, hq))
    return pl.pallas_call(
        _ctx_atten_kernel,
        grid=(b, w // hw),
        in_specs=[spec, spec, spec],
        out_specs=spec,
        out_shape=jax.ShapeDtypeStruct((b, lc, w), BF),
        compiler_params=_cparams(2),
        name="ctx_atten",
    )(qc, kc, vc)


def _na_bias(rpb, rows):
    w = GRID_W
    nj = rows // NA_QROWS
    n_dr = 2 * WIN_ROWS - 1
    n_dc = 2 * WIN_COLS - 1
    cq = np.arange(w)
    cs = np.clip(cq - WIN_COLS // 2, 0, w - WIN_COLS)
    ck = np.arange(w)
    valid_c = (ck[None, :] >= cs[:, None]) & (ck[None, :] < cs[:, None] + WIN_COLS)
    dc = ck[None, :] - cq[:, None] + (WIN_COLS - 1)
    pick_c = (dc[:, :, None] == np.arange(n_dc)) & valid_c[:, :, None]
    tiles = jnp.einsum('lhab,cdb->lhacd', rpb.astype(F32) * LOG2E, jnp.asarray(pick_c, F32),
                       precision=lax.Precision.HIGHEST)
    tiles = tiles + jnp.asarray(np.where(valid_c, 0.0, NEG_BIG), F32)
    tiles = jnp.concatenate([tiles, jnp.full_like(tiles[:, :, :1], NEG_BIG)], axis=2).astype(BF)
    out = []
    for j in (0, min(1, nj - 1), nj - 1):
        rq = NA_QROWS * j + np.arange(NA_QROWS)
        ws = int(np.clip(NA_QROWS * j - (NA_KROWS - NA_QROWS) // 2, 0, rows - NA_KROWS))
        rk = ws + np.arange(NA_KROWS)
        r0 = np.clip(rq - WIN_ROWS // 2, 0, rows - WIN_ROWS)
        valid_r = (rk[None, :] >= r0[:, None]) & (rk[None, :] < r0[:, None] + WIN_ROWS)
        dr = np.where(valid_r, rk[None, :] - rq[:, None] + (WIN_ROWS - 1), n_dr)
        out.append(jnp.concatenate(
            [jnp.concatenate([tiles[:, :, int(dr[a, b])] for b in range(NA_KROWS)], axis=-1)
             for a in range(NA_QROWS)], axis=-2))
    return jnp.stack(out, axis=1)


def _rope_tables(n):
    nf = NA_HEAD_DIM // 4
    t = jnp.arange(n, dtype=jnp.int32)
    pos = jnp.stack([t // GRID_W, t % GRID_W], axis=-1).astype(F32)
    inv_freq = ROPE_THETA ** (-jnp.arange(nf, dtype=F32) / nf)
    ang = pos[:, :, None] * inv_freq
    cos = jnp.cos(ang)
    sin = jnp.sin(ang)
    cos_h = jnp.concatenate([cos, cos], axis=-1).reshape(n, NA_HEAD_DIM)
    sin_h = jnp.concatenate([-sin, sin], axis=-1).reshape(n, NA_HEAD_DIM)
    return jnp.tile(cos_h, (1, NA_HEADS)), jnp.tile(sin_h, (1, NA_HEADS))


def _merge_kernel(ysf_ref, ysb_ref, us_ref, yf_ref, yn_ref, g_ref, h_ref, t_ref,
                  sd_ref, wglu_ref, bglu_ref, wbs_ref, wbf_ref, wbn_ref, wo_ref, o_ref, ys_f32, *, alpha):
    d = h_ref.shape[-1]
    t = t_ref[...]
    halves = SSM_WIDTH // LANES
    tiles = [slice(r0, r0 + ROW_TILE) for r0 in range(0, h_ref.shape[0], ROW_TILE)]
    rows4 = ROW_TILE // S5_BLOCK
    ys = []
    for rows in tiles:
        q0 = rows.start // S5_BLOCK
        y4 = ysf_ref[q0:q0 + rows4, :].astype(F32) + ysb_ref[q0:q0 + rows4, :].astype(F32)
        for j in range(S5_BLOCK):
            for hf in range(halves):
                c0 = j * SSM_WIDTH + hf * LANES
                ys_f32[hf, pl.ds(rows.start + j, rows4, stride=S5_BLOCK), :] = y4[:, c0:c0 + LANES]
        y = jnp.concatenate([ys_f32[hf, rows, :] for hf in range(halves)], axis=-1)
        ys.append(jax.nn.gelu(y + sd_ref[...] * us_ref[rows, :].astype(F32)))
    glu = [_dot(y.astype(BF), wglu_ref[...]) for y in ys]
    ys = [(y * jax.nn.sigmoid(z + bglu_ref[...])).astype(BF) for y, z in zip(ys, glu)]
    br = [(_dot(y, wbs_ref[...]), _dot(yf_ref[rows, :], wbf_ref[...]), _dot(yn_ref[rows, :], wbn_ref[...]))
          for y, rows in zip(ys, tiles)]
    mix = [(g_ref[rows, 0:d].astype(F32) * bs + g_ref[rows, d:2 * d].astype(F32) * bf
            + g_ref[rows, 2 * d:3 * d].astype(F32) * bn).astype(BF) for (bs, bf, bn), rows in zip(br, tiles)]
    proj = [_dot(m, wo_ref[...]) for m in mix]
    for y, rows in zip(proj, tiles):
        v = alpha * h_ref[rows, :] + t[2:3] * y
        o_ref[rows, :] = _ln(v) * t[3:4] + t[4:5]


def _merge_call(ysf4, ysb4, us, yf, yn, g, h, tab, s5w, mats, l, alpha):
    nb, n, d = h.shape
    tm = min(MERGE_ROWS, n)

    def tok(a):
        return pl.BlockSpec((None, tm, a.shape[-1]), lambda b, i: (b, i, 0))

    def tok4(a):
        return pl.BlockSpec((None, tm // S5_BLOCK, a.shape[-1]), lambda b, i: (b, i, 0))

    toks = (us, yf, yn, g, h)
    return pl.pallas_call(
        functools.partial(_merge_kernel, alpha=alpha),
        grid=(nb, n // tm),
        in_specs=[tok4(ysf4), tok4(ysb4)] + [tok(a) for a in toks]
                 + [pl.BlockSpec((None, 8, d), lambda b, i: (b, 0, 0))]
                 + [_layer_spec(a, l) for a in tuple(s5w) + tuple(mats)],
        out_specs=tok(h),
        out_shape=jax.ShapeDtypeStruct(h.shape, F32),
        scratch_shapes=[pltpu.VMEM((SSM_WIDTH // LANES, tm, LANES), F32)],
        compiler_params=_cparams(2),
        name="merge",
    )(ysf4, ysb4, *toks, tab, *s5w, *mats)


def _table(mod, d, chunks, extra):
    rows = [mod[:, i * d:(i + 1) * d] for i in chunks]
    rows += [jnp.broadcast_to(e[None, :], (8, d)) for e in extra]
    rows += [jnp.zeros((8, d), F32)] * (8 - len(rows))
    return jnp.stack(rows, axis=1)


def kernel(x, c, ctx, c_ctx, w_mod, b_mod, ln_g, ln_b, ffn_w_gate, ffn_w_up, ffn_w_down, w_in, ssm_log_dt, ssm_a_re, ssm_a_im, ssm_b_re, ssm_b_im, ssm_c_re, ssm_c_im, ssm_d, ssm_w_glu, ssm_b_glu, na_rpb, w_br_ssm, w_br_fnet, w_br_na, w_out):
    bsz, n, d = x.shape
    lc = ctx.shape[1]
    depth = w_mod.shape[0]
    rows = n // GRID_W
    assert bsz + 1 <= 8 and rows % NA_QROWS == 0 and rows >= NA_KROWS and n % FNET_N2 == 0
    alpha = (2 * depth) ** 0.25
    q_scale = NA_HEAD_DIM ** -0.5 * LOG2E

    c8 = jnp.concatenate([c, c_ctx[None, :], jnp.zeros((8 - bsz - 1, d), F32)], axis=0)
    mod_all = _mod_call(c8, w_mod, b_mod)

    wg = ffn_w_gate.astype(BF)
    wu = ffn_w_up.astype(BF)
    wd = ffn_w_down.astype(BF)
    win = w_in.astype(BF)
    mats = (w_br_ssm.astype(BF), w_br_fnet.astype(BF), w_br_na.astype(BF), w_out.astype(BF))
    s5w = (ssm_d[:, None, :], ssm_w_glu.astype(BF), ssm_b_glu[:, None, :])
    s5_tabs = _s5_tables(ssm_log_dt, ssm_a_re, ssm_a_im, ssm_b_re, ssm_b_im, ssm_c_re, ssm_c_im)
    na_bias = _na_bias(na_rpb, rows)
    cos_t, sin_t = _rope_tables(n)
    cos_c = jnp.ones((bsz * lc, NA_WIDTH), F32)
    sin_c = jnp.zeros((bsz * lc, NA_WIDTH), F32)
    zero_state = jnp.zeros((bsz, 8, 2 * SSM_FLAT), F32)

    h = x
    hc = ctx.reshape(1, bsz * lc, d)
    for l in range(depth):
        last = l == depth - 1
        mod = mod_all[l]

        tab = _table(mod, d, (0, 1, 2), (ln_g[l, 0], ln_b[l, 0]))
        h = _ffn_call(h, tab[:bsz], wg, wu, wd, l, 0, alpha)
        hc = _ffn_call(hc, tab[bsz:bsz + 1], wg, wu, wd, l, 0, alpha)

        tab = _table(mod, d, (3, 4, 5), (ln_g[l, 1], ln_b[l, 1]))
        us, u4, k, v, f, q, g = _mixin_call(h, tab[:bsz], win, l, cos_t, sin_t, q_scale)
        usc, u4c, kc, vc, fc, qc, gc = _mixin_call(hc, tab[bsz:bsz + 1], win, l, cos_c, sin_c, q_scale)
        u4c, kc, vc, fc, qc = [a.reshape(bsz, a.shape[1] // bsz, a.shape[-1]) for a in (u4c, kc, vc, fc, qc)]

        ysbc, hcb = _s5_call(u4c, zero_state, s5_tabs, l, 1, u4c.shape[1])
        ysfc, hcf = _s5_call(u4c, zero_state, s5_tabs, l, 0, u4c.shape[1])
        ysb, _ = _s5_call(u4, hcb, s5_tabs, l, 1, min(S5_ROWS, u4.shape[1]))
        ysf, _ = _s5_call(u4, hcf, s5_tabs, l, 0, min(S5_ROWS, u4.shape[1]))

        yf = _fnet_call(f)
        yn = _natten_call(q, k, v, kc, vc, na_bias, l)

        h = _merge_call(ysf, ysb, us, yf, yn, g, h, tab[:bsz], s5w, mats, l, alpha)

        tab3 = _table(mod, d, (6, 7, 8), (ln_g[l, 2], ln_b[l, 2]))
        h = _ffn_call(h, tab3[:bsz], wg, wu, wd, l, 1, alpha)

        if not last:
            yfc = _fnet_dense_call(fc)
            ync = _ctx_atten_call(qc, kc, vc)
            flat = lambda a: a.reshape(1, a.shape[0] * a.shape[1], a.shape[-1])
            hc = _merge_call(flat(ysfc), flat(ysbc), usc, flat(yfc), flat(ync), gc, hc, tab[bsz:bsz + 1],
                             s5w, mats, l, alpha)
            hc = _ffn_call(hc, tab3[bsz:bsz + 1], wg, wu, wd, l, 1, alpha)
    return h
```

```python
import functools
import math

import jax
import jax.numpy as jnp
import numpy as np
from jax import lax
from jax.experimental import pallas as pl
from jax.experimental.pallas import tpu as pltpu

BF = jnp.bfloat16
F32 = jnp.float32

GRID_W = 64
SSM_GROUPS = 16
SSM_GROUP_CH = 16
SSM_STATE = 64
SSM_WIDTH = SSM_GROUPS * SSM_GROUP_CH
SSM_FLAT = SSM_GROUPS * SSM_STATE
FNET_GROUPS = 4
FNET_GROUP_CH = 64
FNET_WIDTH = FNET_GROUPS * FNET_GROUP_CH
NA_HEADS = 8
NA_HEAD_DIM = 64
NA_WIDTH = NA_HEADS * NA_HEAD_DIM
WIN_ROWS = 8
WIN_COLS = 16
ROPE_THETA = 10000.0
N_MOD = 9
LN_EPS = 1e-6
LOG2E = 1.4426950408889634
NEG_BIG = -1e30

VMEM_LIMIT_BYTES = 56 * 1024 * 1024
ROW_TILE = 512
FFN_ROWS = 1024
MERGE_ROWS = 1024
MIXIN_ROWS = 1024
FFN_CHUNK = 256
LANES = 128
SCAN_ROWS = 8
S5_BLOCK = 4
S5_ROWS = 512
S5_SUB = 256
NA_QROWS = 8
NA_KROWS = 16
NA_HQ = 4
FNET_N2 = 128


def _cparams(n_axes, flags=None):
    return pltpu.CompilerParams(dimension_semantics=("arbitrary",) * n_axes,
                                vmem_limit_bytes=VMEM_LIMIT_BYTES, flags=flags)


def _ln(x):
    mu = jnp.mean(x, axis=-1, keepdims=True)
    xc = x - mu
    var = jnp.mean(xc * xc, axis=-1, keepdims=True)
    return xc * lax.rsqrt(var + LN_EPS)


def _dot(a, b):
    return jnp.dot(a, b, preferred_element_type=F32)


def _dot_t(a, b):
    return lax.dot_general(a, b, (((1,), (1,)), ((), ())), preferred_element_type=F32)


def _layer_spec(a, *lead, mode=None):
    nl = len(lead)
    zeros = (0,) * (a.ndim - nl)
    return pl.BlockSpec((None,) * nl + tuple(a.shape[nl:]), lambda *_: tuple(lead) + zeros, pipeline_mode=mode)


def _mod_kernel(c_ref, w_ref, b_ref, o_ref):
    c = c_ref[...]
    s = (c * jax.nn.sigmoid(c)).astype(BF)
    o_ref[...] = _dot(s, w_ref[...].astype(BF)) + b_ref[...]


def _mod_call(c8, w_mod, b_mod):
    depth, d, nd = w_mod.shape
    nb = nd // d
    return pl.pallas_call(
        _mod_kernel,
        grid=(depth, nb),
        in_specs=[pl.BlockSpec((8, d), lambda l, j: (0, 0)),
                  pl.BlockSpec((None, d, d), lambda l, j: (l, 0, j)),
                  pl.BlockSpec((None, 1, d), lambda l, j: (l, 0, j))],
        out_specs=pl.BlockSpec((None, 8, d), lambda l, j: (l, 0, j)),
        out_shape=jax.ShapeDtypeStruct((depth, 8, nd), F32),
        compiler_params=_cparams(2),
        name="mod",
    )(c8, w_mod, b_mod.reshape(depth, 1, nd))


def _ffn_kernel(x_ref, t_ref, wg_ref, wu_ref, wd_ref, o_ref, act_scr, *, alpha):
    t = t_ref[...]
    tiles = [slice(r0, r0 + ROW_TILE) for r0 in range(0, x_ref.shape[0], ROW_TILE)]
    us = [(_ln(x_ref[rows, :]) * (1.0 + t[1:2]) + t[0:1]).astype(BF) for rows in tiles]
    for c0 in range(0, wg_ref.shape[1], FFN_CHUNK):
        for k, u in enumerate(us):
            a = _dot(u, wg_ref[:, c0:c0 + FFN_CHUNK])
            b = _dot(u, wu_ref[:, c0:c0 + FFN_CHUNK])
            act_scr[k, :, c0:c0 + FFN_CHUNK] = (a * jax.nn.sigmoid(a) * b).astype(BF)
    ys = [_dot(act_scr[k], wd_ref[...]) for k in range(len(tiles))]
    for y, rows in zip(ys, tiles):
        v = alpha * x_ref[rows, :] + (0.5 * t[2:3]) * y
        o_ref[rows, :] = _ln(v) * t[3:4] + t[4:5]


def _ffn_call(x, tab, wg, wu, wd, l, k, alpha):
    nb, n, d = x.shape
    tm = min(FFN_ROWS, n)
    once = pl.Buffered(1)
    return pl.pallas_call(
        functools.partial(_ffn_kernel, alpha=alpha),
        grid=(nb, n // tm),
        in_specs=[pl.BlockSpec((None, tm, d), lambda b, i: (b, i, 0)),
                  pl.BlockSpec((None, 8, d), lambda b, i: (b, 0, 0)),
                  _layer_spec(wg, l, k, mode=once), _layer_spec(wu, l, k, mode=once),
                  _layer_spec(wd, l, k, mode=once)],
        out_specs=pl.BlockSpec((None, tm, d), lambda b, i: (b, i, 0)),
        out_shape=jax.ShapeDtypeStruct(x.shape, F32),
        scratch_shapes=[pltpu.VMEM((pl.cdiv(tm, ROW_TILE), min(ROW_TILE, tm), wg.shape[-1]), BF)],
        compiler_params=_cparams(2),
        name="ffn",
    )(x, tab, wg, wu, wd)


def _rope(x, cos, sin_signed):
    n = x.shape[-1]
    lane = lax.broadcasted_iota(jnp.int32, x.shape, 1)
    first = (lane % 32) < 16
    partner = jnp.where(first, pltpu.roll(x, n - 16, axis=1), pltpu.roll(x, 16, axis=1))
    return x * cos + partner * sin_signed


def _mixin_kernel(x_ref, t_ref, w_ref, cos_ref, sin_ref,
                  us_ref, u4_ref, k_ref, v_ref, f_ref, q_ref, g_ref, us_f32, *, q_scale):
    t = t_ref[...]
    c_k = SSM_WIDTH
    c_v = c_k + NA_WIDTH
    c_f = c_v + NA_WIDTH
    c_q = c_f + FNET_WIDTH
    c_g = c_q + NA_WIDTH
    tiles = [slice(r0, r0 + ROW_TILE) for r0 in range(0, x_ref.shape[0], ROW_TILE)]
    us = [(_ln(x_ref[rows, :]) * (1.0 + t[1:2]) + t[0:1]).astype(BF) for rows in tiles]
    rows4 = ROW_TILE // S5_BLOCK
    for u, rows in zip(us, tiles):
        zs = _dot(u, w_ref[:, 0:c_k])
        us_ref[rows, :] = zs.astype(BF)
        for hf in range(c_k // LANES):
            us_f32[hf, rows, :] = zs[:, hf * LANES:(hf + 1) * LANES]
        q0 = rows.start // S5_BLOCK
        for j in range(S5_BLOCK):
            for hf in range(c_k // LANES):
                c0 = j * c_k + hf * LANES
                u4_ref[q0:q0 + rows4, c0:c0 + LANES] = (
                    us_f32[hf, pl.ds(rows.start + j, rows4, stride=S5_BLOCK), :].astype(BF))
    for u, rows in zip(us, tiles):
        k_ref[rows, :] = _rope(_dot(u, w_ref[:, c_k:c_v]), cos_ref[rows, :], sin_ref[rows, :]).astype(BF)
    for u, rows in zip(us, tiles):
        v_ref[rows, :] = _dot(u, w_ref[:, c_v:c_f]).astype(BF)
    for u, rows in zip(us, tiles):
        f_ref[rows, :] = _dot(u, w_ref[:, c_f:c_q]).astype(BF)
    for u, rows in zip(us, tiles):
        q = _rope(_dot(u, w_ref[:, c_q:c_g]), cos_ref[rows, :], sin_ref[rows, :])
        q_ref[rows, :] = (q * q_scale).astype(BF)
    gw = NA_WIDTH
    for c0 in range(0, g_ref.shape[-1], gw):
        for u, rows in zip(us, tiles):
            g = _dot(u, w_ref[:, c_g + c0:c_g + c0 + gw])
            g_ref[rows, c0:c0 + gw] = jax.nn.sigmoid(g).astype(BF)


def _mixin_call(x, tab, w_in, l, cos_t, sin_t, q_scale):
    nb, n, d = x.shape
    win = w_in.shape[-1]
    ng = win - (SSM_WIDTH + 3 * NA_WIDTH + FNET_WIDTH)
    tm = min(MIXIN_ROWS, n)
    nt_pos = cos_t.shape[0] // tm

    def tok(width):
        return pl.BlockSpec((None, tm, width), lambda b, i: (b, i, 0))

    def shape(width):
        return jax.ShapeDtypeStruct((nb, n, width), BF)

    u4_spec = pl.BlockSpec((None, tm // S5_BLOCK, S5_BLOCK * SSM_WIDTH), lambda b, i: (b, i, 0))
    u4_shape = jax.ShapeDtypeStruct((nb, n // S5_BLOCK, S5_BLOCK * SSM_WIDTH), BF)
    return pl.pallas_call(
        functools.partial(_mixin_kernel, q_scale=q_scale),
        grid=(nb, n // tm),
        in_specs=[tok(d),
                  pl.BlockSpec((None, 8, d), lambda b, i: (b, 0, 0)),
                  _layer_spec(w_in, l, mode=pl.Buffered(1)),
                  pl.BlockSpec((tm, NA_WIDTH), lambda b, i: (i % nt_pos, 0)),
                  pl.BlockSpec((tm, NA_WIDTH), lambda b, i: (i % nt_pos, 0))],
        out_specs=[tok(SSM_WIDTH), u4_spec, tok(NA_WIDTH), tok(NA_WIDTH), tok(FNET_WIDTH), tok(NA_WIDTH),
                   tok(ng)],
        out_shape=[shape(SSM_WIDTH), u4_shape, shape(NA_WIDTH), shape(NA_WIDTH), shape(FNET_WIDTH),
                   shape(NA_WIDTH), shape(ng)],
        scratch_shapes=[pltpu.VMEM((SSM_WIDTH // LANES, tm, LANES), F32)],
        compiler_params=_cparams(2),
        name="mixin",
    )(x, tab, w_in, cos_t, sin_t)


def _s5_kernel(u_ref, h0_ref, wst_ref, wconv_ref, wout_ref, sc_ref, y_ref, hfin_ref, hbuf, hprev, carry,
               *, reverse):
    i = pl.program_id(1)
    ns = SSM_FLAT
    rows = hbuf.shape[0]
    sub = min(S5_SUB, rows)
    order = (lambda r: list(r)[::-1]) if reverse else list
    subs = order(range(0, rows, sub))

    @pl.when(i == 0)
    def _():
        carry[...] = h0_ref[...]

    for s0 in subs:
        hbuf[s0:s0 + sub, :] = _dot(u_ref[s0:s0 + sub, :], wst_ref[...])
    row_id = lax.broadcasted_iota(jnp.int32, (SCAN_ROWS, ns), 0)
    edge = row_id == ((SCAN_ROWS - 1) if reverse else 0)

    cr = carry[:, 0:ns]
    ci = carry[:, ns:2 * ns]
    for s0 in subs:
        for r0 in order(range(s0, s0 + sub, SCAN_ROWS)):
            xr = hbuf[r0:r0 + SCAN_ROWS, 0:ns]
            xi = hbuf[r0:r0 + SCAN_ROWS, ns:2 * ns]
            for k, s in enumerate((1, 2, 4)):
                ar = sc_ref[k, :, 0:ns]
                ai = sc_ref[k, :, ns:2 * ns]
                shift = (SCAN_ROWS - s) if reverse else s
                sr = pltpu.roll(xr, shift, axis=0)
                si = pltpu.roll(xi, shift, axis=0)
                xr, xi = xr + ar * sr - ai * si, xi + ar * si + ai * sr
            pr = sc_ref[3, :, 0:ns]
            pi = sc_ref[3, :, ns:2 * ns]
            hr = xr + pr * cr - pi * ci
            hi = xi + pr * ci + pi * cr
            one = (SCAN_ROWS - 1) if reverse else 1
            hprev[r0:r0 + SCAN_ROWS, 0:ns] = jnp.where(edge, cr, pltpu.roll(hr, one, axis=0))
            hprev[r0:r0 + SCAN_ROWS, ns:2 * ns] = jnp.where(edge, ci, pltpu.roll(hi, one, axis=0))
            last = 0 if reverse else SCAN_ROWS - 1
            cr = jnp.broadcast_to(hr[last:last + 1, :], hr.shape)
            ci = jnp.broadcast_to(hi[last:last + 1, :], hi.shape)
    carry[:, 0:ns] = cr
    carry[:, ns:2 * ns] = ci
    hfin_ref[...] = carry[...]

    for s0 in subs:
        y = (_dot(u_ref[s0:s0 + sub, :], wconv_ref[...])
             + _dot(hprev[s0:s0 + sub, :].astype(BF), wout_ref[...]))
        y_ref[s0:s0 + sub, :] = y.astype(y_ref.dtype)


def _s5_call(u4, h0, tables, l, z, rows):
    wst, wconv, wout, sc = tables
    b, n, w = u4.shape
    nc = n // rows
    ns2 = 2 * SSM_FLAT
    reverse = z == 1
    if reverse:
        cmap = lambda bb, i: (bb, nc - 1 - i, 0)
    else:
        cmap = lambda bb, i: (bb, i, 0)
    return pl.pallas_call(
        functools.partial(_s5_kernel, reverse=reverse),
        grid=(b, nc),
        in_specs=[pl.BlockSpec((None, rows, w), cmap),
                  pl.BlockSpec((None, 8, ns2), lambda bb, i: (bb, 0, 0)),
                  _layer_spec(wst, l, z), _layer_spec(wconv, l, z), _layer_spec(wout, l, z),
                  _layer_spec(sc, l, z)],
        out_specs=[pl.BlockSpec((None, rows, w), cmap),
                   pl.BlockSpec((None, 8, ns2), lambda bb, i: (bb, 0, 0))],
        out_shape=[jax.ShapeDtypeStruct((b, n, w), BF),
                   jax.ShapeDtypeStruct((b, 8, ns2), F32)],
        scratch_shapes=[pltpu.VMEM((rows, ns2), F32), pltpu.VMEM((rows, ns2), F32), pltpu.VMEM((8, ns2), F32)],
        compiler_params=_cparams(2),
        name="s5_bwd" if reverse else "s5_fwd",
    )(u4, h0, wst, wconv, wout, sc)


def _s5_tables(log_dt, a_re, a_im, b_re, b_im, c_re, c_im):
    depth = log_dt.shape[0]
    g, p, c = b_re.shape[-3:]
    tb = S5_BLOCK
    hp = lax.Precision.HIGHEST
    f32 = lambda a: a.astype(F32)
    a_re, a_im, b_re, b_im, c_re, c_im = map(f32, (a_re, a_im, b_re, b_im, c_re, c_im))
    dt = jnp.exp(f32(log_dt))[..., None]
    adt_re = a_re * dt
    adt_im = a_im * dt

    def apow(k):
        kk = jnp.asarray(k, F32)[None, :, :, None, None]
        mag = jnp.exp(kk * adt_re[:, :, None])
        ang = kk * adt_im[:, :, None]
        return mag * jnp.cos(ang), mag * jnp.sin(ang)

    mag = jnp.exp(adt_re)
    ab_re = mag * jnp.cos(adt_im)
    ab_im = mag * jnp.sin(adt_im)
    den = a_re * a_re + a_im * a_im
    nr = ab_re - 1.0
    fr = (nr * a_re + ab_im * a_im) / den
    fi = (ab_im * a_re - nr * a_im) / den
    bb_re = fr[..., None] * b_re - fi[..., None] * b_im
    bb_im = fr[..., None] * b_im + fi[..., None] * b_re

    step = np.arange(tb)
    k_state = np.stack([tb - 1 - step, step])
    k_out = np.stack([step + 1, tb - step])
    k_tap = np.stack([step, step])
    k_scan = np.stack([tb * (np.arange(SCAN_ROWS) + 1)] * 2)

    def block_diag(compact, row_group, col_width):
        n = compact.shape[3]
        dst = np.arange(n * g * col_width)
        src = (dst // (g * col_width)) * col_width + dst % col_width
        copy = np.zeros((n * col_width, n * g * col_width), np.float32)
        copy[src, dst] = 1.0
        cols = jnp.einsum('dzrk,kn->dzrn', compact.reshape(compact.shape[:3] + (n * col_width,)),
                          jnp.asarray(copy), precision=hp)
        keep = jnp.asarray(row_group[:, None] == ((dst // col_width) % g)[None, :])
        return jnp.where(keep, cols, 0.0).astype(BF)

    chan_rows = (np.arange(tb * g * c) // c) % g
    state_rows = (np.arange(2 * g * p) // p) % g

    def cmul_b(pr, pi):
        return (pr[..., None] * bb_re[:, :, None] - pi[..., None] * bb_im[:, :, None],
                pr[..., None] * bb_im[:, :, None] + pi[..., None] * bb_re[:, :, None])

    wr, wi = cmul_b(*apow(k_state))
    w2 = jnp.stack([wr, wi], axis=3)
    w2 = jnp.transpose(w2, (0, 1, 2, 4, 6, 3, 5))
    wst = block_diag(w2.reshape(depth, 2, tb * g * c, 2, p), chan_rows, p)

    pr, pi = apow(k_out)
    cm_re = c_re[:, :, None] * pr[:, :, :, :, None, :] - c_im[:, :, None] * pi[:, :, :, :, None, :]
    cm_im = c_re[:, :, None] * pi[:, :, :, :, None, :] + c_im[:, :, None] * pr[:, :, :, :, None, :]
    cm = jnp.stack([cm_re, -cm_im], axis=3)
    cm = jnp.transpose(cm, (0, 1, 3, 4, 6, 2, 5))
    wout = block_diag(cm.reshape(depth, 2, 2 * g * p, tb, c), state_rows, c)

    abr, abi = cmul_b(*apow(k_tap))
    taps = (jnp.einsum('dzgcp,dztgpe->dztgec', c_re, abr, precision=hp)
            - jnp.einsum('dzgcp,dztgpe->dztgec', c_im, abi, precision=hp))
    sel = np.zeros((2, tb, tb, tb), np.float32)
    for i in range(tb):
        for j in range(tb):
            if j >= i:
                sel[0, i, j, j - i] = 1.0
            if i >= j:
                sel[1, i, j, i - j] = 1.0
    kij = jnp.einsum('dztgec,zijt->dzigejc', taps, jnp.asarray(sel), precision=hp)
    wconv = block_diag(kij.reshape(depth, 2, tb * g * c, tb, c), chan_rows, c)

    qr, qi = apow(k_scan)
    qr = qr.reshape(depth, 2, SCAN_ROWS, g * p)
    qi = qi.reshape(depth, 2, SCAN_ROWS, g * p)
    rows = np.arange(SCAN_ROWS)
    tabs = []
    for s in (1, 2, 4):
        keep = jnp.asarray(np.stack([rows >= s, rows + s <= SCAN_ROWS - 1]), F32)[None, :, :, None]
        tabs.append(jnp.concatenate([keep * qr[:, :, s - 1:s], keep * qi[:, :, s - 1:s]], axis=-1))
    carry_pow = lambda t: jnp.stack([t[:, 0], t[:, 1, ::-1]], axis=1)
    tabs.append(jnp.concatenate([carry_pow(qr), carry_pow(qi)], axis=-1))
    sc = jnp.stack(tabs, axis=2)
    return wst, wconv, wout, sc


def _fnet1_kernel(x_ref, f2_ref, tc_ref, ts_ref, o_ref):
    n2 = x_ref.shape[0]
    y = _dot(f2_ref[...], x_ref[...])
    br = y[0:n2]
    bi = y[n2:2 * n2]
    tc = tc_ref[...]
    ts = ts_ref[...]
    o_ref[0] = (br * tc + bi * ts).astype(BF)
    o_ref[1] = (bi * tc - br * ts).astype(BF)


def _fnet2_kernel(b_ref, f1_ref, cc_ref, sc_ref, o_ref, *, scale):
    kb = b_ref.shape[1]
    n1 = b_ref.shape[2]
    c = b_ref.shape[3]
    zr, zi = [], []
    for j in range(kb):
        rhs = jnp.concatenate([b_ref[0, j], b_ref[1, j]], axis=0)
        z = _dot(f1_ref[...], rhs)
        zr.append(z[0:n1])
        zi.append(z[n1:2 * n1])
    zr = jnp.concatenate(zr, axis=0).astype(BF)
    zi = jnp.concatenate(zi, axis=0).astype(BF)
    y = (_dot(zr, cc_ref[...]) + _dot(zi, sc_ref[...])) * scale
    for j in range(kb):
        o_ref[:, j * c:(j + 1) * c] = y[j * n1:(j + 1) * n1].astype(o_ref.dtype)


def _dft_tables(n):
    idx = np.arange(n)
    ang = 2.0 * np.pi * ((idx[:, None] * idx[None, :]) % n) / n
    return np.cos(ang), np.sin(ang)


def _chan_tables():
    cc, sc = _dft_tables(FNET_GROUP_CH)
    eye = np.eye(FNET_GROUPS)
    return jnp.asarray(np.kron(eye, cc), BF), jnp.asarray(np.kron(eye, sc), BF)


def _fnet_call(f):
    b, n, c = f.shape
    n2 = FNET_N2
    n1 = n // n2
    c2, s2 = _dft_tables(n2)
    f2 = jnp.asarray(np.concatenate([c2, -s2], axis=0), BF)
    k2 = np.arange(n2)[:, None]
    nn1 = np.arange(n1)[None, :]
    ang = 2.0 * np.pi * ((k2 * nn1) % n) / n
    tc = jnp.asarray(np.repeat(np.cos(ang), c, axis=1), F32)
    ts = jnp.asarray(np.repeat(np.sin(ang), c, axis=1), F32)
    c1, s1 = _dft_tables(n1)
    f1 = jnp.asarray(np.block([[c1, s1], [-s1, c1]]), BF)
    cc, sc = _chan_tables()

    wc = min(n1 * c, 4096)
    x2 = f.reshape(b, n2, n1 * c)
    st1 = pl.pallas_call(
        _fnet1_kernel,
        grid=(b, (n1 * c) // wc),
        in_specs=[pl.BlockSpec((None, n2, wc), lambda bb, j: (bb, 0, j)),
                  pl.BlockSpec((2 * n2, n2), lambda bb, j: (0, 0)),
                  pl.BlockSpec((n2, wc), lambda bb, j: (0, j)),
                  pl.BlockSpec((n2, wc), lambda bb, j: (0, j))],
        out_specs=pl.BlockSpec((None, 2, n2, wc), lambda bb, j: (bb, 0, 0, j)),
        out_shape=jax.ShapeDtypeStruct((b, 2, n2, n1 * c), BF),
        compiler_params=_cparams(2),
        name="fnet1",
    )(x2, f2, tc, ts)
    st1 = st1.reshape(b, 2, n2, n1, c)
    kb = 8
    y = pl.pallas_call(
        functools.partial(_fnet2_kernel, scale=1.0 / math.sqrt(n * FNET_GROUP_CH)),
        grid=(b, n2 // kb),
        in_specs=[pl.BlockSpec((None, 2, kb, n1, c), lambda bb, j: (bb, 0, j, 0, 0)),
                  pl.BlockSpec((2 * n1, 2 * n1), lambda bb, j: (0, 0)),
                  pl.BlockSpec((c, c), lambda bb, j: (0, 0)),
                  pl.BlockSpec((c, c), lambda bb, j: (0, 0))],
        out_specs=pl.BlockSpec((None, n1, kb * c), lambda bb, j: (bb, 0, j)),
        out_shape=jax.ShapeDtypeStruct((b, n1, n2 * c), BF),
        compiler_params=_cparams(2),
        name="fnet2",
    )(st1, f1, cc, sc)
    return y.reshape(b, n, c)


def _fnet_dense_kernel(x_ref, cl_ref, sl_ref, cc_ref, sc_ref, o_ref, *, scale):
    x = x_ref[...]
    xr = _dot(x, cc_ref[...]).astype(BF)
    xi = _dot(x, sc_ref[...]).astype(BF)
    o_ref[...] = ((_dot(cl_ref[...], xr) - _dot(sl_ref[...], xi)) * scale).astype(o_ref.dtype)


def _fnet_dense_call(f):
    b, n, c = f.shape
    cl, sl = _dft_tables(n)
    cc, sc = _chan_tables()
    full = lambda bb: (0, 0)
    return pl.pallas_call(
        functools.partial(_fnet_dense_kernel, scale=1.0 / math.sqrt(n * FNET_GROUP_CH)),
        grid=(b,),
        in_specs=[pl.BlockSpec((None, n, c), lambda bb: (bb, 0, 0)),
                  pl.BlockSpec((n, n), full), pl.BlockSpec((n, n), full),
                  pl.BlockSpec((c, c), full), pl.BlockSpec((c, c), full)],
        out_specs=pl.BlockSpec((None, n, c), lambda bb: (bb, 0, 0)),
        out_shape=jax.ShapeDtypeStruct((b, n, c), BF),
        compiler_params=_cparams(1),
        name="fnet_ctx",
    )(f, jnp.asarray(cl, BF), jnp.asarray(sl, BF), cc, sc)


def _head_masks(width):
    lane = lax.broadcasted_iota(jnp.int32, (1, width), 1)
    return [(lane // NA_HEAD_DIM) == h for h in range(width // NA_HEAD_DIM)]


def _attend(q, k_parts, v_parts, ranges, bias_of, o_ref):
    zero = jnp.zeros((), BF)
    hw = NA_HQ * NA_HEAD_DIM
    masks = _head_masks(hw)
    m_rows = q.shape[0]
    rb = min(GRID_W, m_rows)
    live = [j for j, (a, b) in enumerate(ranges) if b > a]
    k_parts = [k_parts[j] for j in live]
    v_parts = [v_parts[j] for j in live]
    ranges = [ranges[j] for j in live]
    quad = lambda x, h: x[:, (h // NA_HQ) * hw:(h // NA_HQ + 1) * hw]

    def scores(h):
        qh = jnp.where(masks[h % NA_HQ], quad(q, h), zero)
        s_parts = []
        for j, (kp, (a, b)) in enumerate(zip(k_parts, ranges)):
            s = _dot_t(qh[a:b], quad(kp, h))
            bias = bias_of(h, live[j], a, b)
            s_parts.append(s if bias is None else s + bias)
        return s_parts

    def lane_tiles(x):
        return [x[:, c:c + LANES] for c in range(0, x.shape[1], LANES)]

    def softmax(s_parts):
        p_rows = [[] for _ in s_parts]
        dens = []
        for r0 in range(0, m_rows, rb):
            pieces = [(j, s[r0 - a:r0 - a + rb]) for j, (s, (a, b)) in enumerate(zip(s_parts, ranges))
                      if a <= r0 < b]
            m = functools.reduce(jnp.maximum, [t for _, pc in pieces for t in lane_tiles(pc)])
            m = m.max(axis=-1, keepdims=True)
            acc = None
            for j, pc in pieces:
                p = jnp.exp2(pc - m)
                acc = functools.reduce(jnp.add, lane_tiles(p), acc) if acc is not None else \
                    functools.reduce(jnp.add, lane_tiles(p))
                p_rows[j].append(p.astype(BF))
            dens.append(acc.sum(axis=-1, keepdims=True))
        return [jnp.concatenate(r, axis=0) for r in p_rows], jnp.concatenate(dens, axis=0)

    def values(h, p_parts, den):
        out = None
        for p, vp, (a, b) in zip(p_parts, v_parts, ranges):
            o = _dot(p, quad(vp, h))
            pad = [jnp.zeros((n, o.shape[1]), F32) for n in (a, m_rows - b)]
            o = jnp.concatenate([x for x in (pad[0], o, pad[1]) if x.shape[0]], axis=0)
            out = o if out is None else out + o
        return out / den

    n = q.shape[-1] // NA_HEAD_DIM
    s_of, p_of, outs = {}, {}, [None] * (n // NA_HQ)
    for step in range(n + 2):
        if step < n:
            s_of[step] = scores(step)
        if 0 <= step - 1 < n:
            p_of[step - 1] = softmax(s_of.pop(step - 1))
        if 0 <= step - 2 < n:
            h = step - 2
            o = values(h, *p_of.pop(h))
            prev = outs[h // NA_HQ]
            outs[h // NA_HQ] = o if prev is None else jnp.where(masks[h % NA_HQ], o, prev)
    o_ref[...] = jnp.concatenate(outs, axis=-1).astype(o_ref.dtype)


def _natten_kernel(q_ref, k0, k1, k2, k3, v0, v1, v2, v3, kc_ref, vc_ref, bias_ref, *rest, ranges):
    o_ref = rest[-1]
    blk = k0.shape[0]
    k_parts = [r[...] for r in (k0, k1, k2, k3)] + [kc_ref[...]]
    v_parts = [r[...] for r in (v0, v1, v2, v3)] + [vc_ref[...]]

    def bias_of(h, j, a, b):
        if j >= 4:
            return None
        return bias_ref[h, a:b, j * blk:(j + 1) * blk].astype(F32)

    _attend(q_ref[...], k_parts, v_parts, ranges, bias_of, o_ref)


def _na_windows(rows):
    nj = rows // NA_QROWS
    out = []
    for j in (0, min(1, nj - 1), nj - 1):
        rq = NA_QROWS * j + np.arange(NA_QROWS)
        ws = int(np.clip(NA_QROWS * j - (NA_KROWS - NA_QROWS) // 2, 0, rows - NA_KROWS))
        rk = ws + np.arange(NA_KROWS)
        r0 = np.clip(rq - WIN_ROWS // 2, 0, rows - WIN_ROWS)
        valid_r = (rk[None, :] >= r0[:, None]) & (rk[None, :] < r0[:, None] + WIN_ROWS)
        dr = rk[None, :] - rq[:, None] + (WIN_ROWS - 1)
        out.append((valid_r, dr))
    return out


def _natten_call(q, k, v, kc, vc, bias, l, variant, prev=None):
    b, n, w = q.shape
    lc = kc.shape[1]
    hw = NA_HQ * NA_HEAD_DIM
    qt = NA_QROWS * GRID_W
    blk = (NA_KROWS // 4) * GRID_W
    nj = n // qt
    nblk = n // blk
    ratio = qt // blk
    j0, count = ((0, 1), (1, nj - 2), (nj - 1, 1))[variant]

    valid_r = _na_windows(n // GRID_W)[variant][0]
    ranges = []
    for t in range(4):
        seen = np.nonzero(valid_r[:, t * (NA_KROWS // 4):(t + 1) * (NA_KROWS // 4)].any(axis=1))[0]
        ranges.append((int(seen.min()) * GRID_W, (int(seen.max()) + 1) * GRID_W) if seen.size else (0, 0))
    ranges.append((0, qt))

    def kv_spec(m):
        def imap(j, bb):
            start = jnp.clip(ratio * (j + j0) - 1, 0, nblk - 4)
            return (bb, start + m, 0)
        return pl.BlockSpec((None, blk, w), imap)

    in_specs = ([pl.BlockSpec((None, qt, w), lambda j, bb: (bb, j + j0, 0))]
                + [kv_spec(m) for m in range(4)] + [kv_spec(m) for m in range(4)]
                + [pl.BlockSpec((None, lc, w), lambda j, bb: (bb, 0, 0)),
                   pl.BlockSpec((None, lc, w), lambda j, bb: (bb, 0, 0)),
                   pl.BlockSpec((None, None, w // NA_HEAD_DIM, qt, 4 * blk), lambda j, bb: (l, variant, 0, 0, 0))])
    args = [q, k, k, k, k, v, v, v, v, kc, vc, bias]
    aliases = {}
    if prev is not None:
        in_specs.append(pl.BlockSpec(memory_space=pl.ANY))
        aliases = {len(args): 0}
        args.append(prev)
    return pl.pallas_call(
        functools.partial(_natten_kernel, ranges=tuple(ranges)),
        grid=(count, b),
        in_specs=in_specs,
        out_specs=pl.BlockSpec((None, qt, w), lambda j, bb: (bb, j + j0, 0)),
        out_shape=jax.ShapeDtypeStruct((b, n, w), BF),
        input_output_aliases=aliases,
        compiler_params=_cparams(2),
        name="natten",
    )(*args)


def _ctx_atten_kernel(q_ref, kc_ref, vc_ref, o_ref):
    _attend(q_ref[...], [kc_ref[...]], [vc_ref[...]], [(0, q_ref.shape[0])], lambda h, j, a, b: None, o_ref)


def _ctx_atten_call(qc, kc, vc):
    b, lc, w = qc.shape
    spec = pl.BlockSpec((None, lc, w), lambda bb: (bb, 0, 0))
    return pl.pallas_call(
        _ctx_atten_kernel,
        grid=(b,),
        in_specs=[spec, spec, spec],
        out_specs=spec,
        out_shape=jax.ShapeDtypeStruct((b, lc, w), BF),
        compiler_params=_cparams(1),
        name="ctx_atten",
    )(qc, kc, vc)


def _na_bias(rpb, rows):
    w = GRID_W
    nj = rows // NA_QROWS
    n_dr = 2 * WIN_ROWS - 1
    n_dc = 2 * WIN_COLS - 1
    cq = np.arange(w)
    cs = np.clip(cq - WIN_COLS // 2, 0, w - WIN_COLS)
    ck = np.arange(w)
    valid_c = (ck[None, :] >= cs[:, None]) & (ck[None, :] < cs[:, None] + WIN_COLS)
    dc = ck[None, :] - cq[:, None] + (WIN_COLS - 1)
    pick_c = (dc[:, :, None] == np.arange(n_dc)) & valid_c[:, :, None]
    tiles = jnp.einsum('lhab,cdb->lhacd', rpb.astype(F32) * LOG2E, jnp.asarray(pick_c, F32),
                       precision=lax.Precision.HIGHEST)
    tiles = tiles + jnp.asarray(np.where(valid_c, 0.0, NEG_BIG), F32)
    tiles = jnp.concatenate([tiles, jnp.full_like(tiles[:, :, :1], NEG_BIG)], axis=2).astype(BF)
    out = []
    for valid_r, dr in _na_windows(rows):
        dr = np.where(valid_r, dr, n_dr)
        out.append(jnp.concatenate(
            [jnp.concatenate([tiles[:, :, int(dr[a, b])] for b in range(NA_KROWS)], axis=-1)
             for a in range(NA_QROWS)], axis=-2))
    return jnp.stack(out, axis=1)


def _rope_tables(n):
    nf = NA_HEAD_DIM // 4
    t = jnp.arange(n, dtype=jnp.int32)
    pos = jnp.stack([t // GRID_W, t % GRID_W], axis=-1).astype(F32)
    inv_freq = ROPE_THETA ** (-jnp.arange(nf, dtype=F32) / nf)
    ang = pos[:, :, None] * inv_freq
    cos = jnp.cos(ang)
    sin = jnp.sin(ang)
    cos_h = jnp.concatenate([cos, cos], axis=-1).reshape(n, NA_HEAD_DIM)
    sin_h = jnp.concatenate([-sin, sin], axis=-1).reshape(n, NA_HEAD_DIM)
    return jnp.tile(cos_h, (1, NA_HEADS)), jnp.tile(sin_h, (1, NA_HEADS))


def _merge_kernel(ysf_ref, ysb_ref, us_ref, yf_ref, yn_ref, g_ref, h_ref, t_ref,
                  sd_ref, wglu_ref, bglu_ref, wbs_ref, wbf_ref, wbn_ref, wo_ref, o_ref, ys_f32, *, alpha):
    d = h_ref.shape[-1]
    t = t_ref[...]
    halves = SSM_WIDTH // LANES
    tiles = [slice(r0, r0 + ROW_TILE) for r0 in range(0, h_ref.shape[0], ROW_TILE)]
    rows4 = ROW_TILE // S5_BLOCK
    ys = []
    for rows in tiles:
        q0 = rows.start // S5_BLOCK
        y4 = ysf_ref[q0:q0 + rows4, :].astype(F32) + ysb_ref[q0:q0 + rows4, :].astype(F32)
        for j in range(S5_BLOCK):
            for hf in range(halves):
                c0 = j * SSM_WIDTH + hf * LANES
                ys_f32[hf, pl.ds(rows.start + j, rows4, stride=S5_BLOCK), :] = y4[:, c0:c0 + LANES]
        y = jnp.concatenate([ys_f32[hf, rows, :] for hf in range(halves)], axis=-1)
        ys.append(jax.nn.gelu(y + sd_ref[...] * us_ref[rows, :].astype(F32)))
    glu = [_dot(y.astype(BF), wglu_ref[...]) for y in ys]
    ys = [(y * jax.nn.sigmoid(z + bglu_ref[...])).astype(BF) for y, z in zip(ys, glu)]
    br = [(_dot(y, wbs_ref[...]), _dot(yf_ref[rows, :], wbf_ref[...]), _dot(yn_ref[rows, :], wbn_ref[...]))
          for y, rows in zip(ys, tiles)]
    mix = [(g_ref[rows, 0:d].astype(F32) * bs + g_ref[rows, d:2 * d].astype(F32) * bf
            + g_ref[rows, 2 * d:3 * d].astype(F32) * bn).astype(BF) for (bs, bf, bn), rows in zip(br, tiles)]
    proj = [_dot(m, wo_ref[...]) for m in mix]
    for y, rows in zip(proj, tiles):
        v = alpha * h_ref[rows, :] + t[2:3] * y
        o_ref[rows, :] = _ln(v) * t[3:4] + t[4:5]


def _merge_call(ysf4, ysb4, us, yf, yn, g, h, tab, s5w, mats, l, alpha):
    nb, n, d = h.shape
    tm = min(MERGE_ROWS, n)

    def tok(a):
        return pl.BlockSpec((None, tm, a.shape[-1]), lambda b, i: (b, i, 0))

    def tok4(a):
        return pl.BlockSpec((None, tm // S5_BLOCK, a.shape[-1]), lambda b, i: (b, i, 0))

    toks = (us, yf, yn, g, h)
    return pl.pallas_call(
        functools.partial(_merge_kernel, alpha=alpha),
        grid=(nb, n // tm),
        in_specs=[tok4(ysf4), tok4(ysb4)] + [tok(a) for a in toks]
                 + [pl.BlockSpec((None, 8, d), lambda b, i: (b, 0, 0))]
                 + [_layer_spec(a, l) for a in tuple(s5w) + tuple(mats)],
        out_specs=tok(h),
        out_shape=jax.ShapeDtypeStruct(h.shape, F32),
        scratch_shapes=[pltpu.VMEM((SSM_WIDTH // LANES, tm, LANES), F32)],
        compiler_params=_cparams(2),
        name="merge",
    )(ysf4, ysb4, *toks, tab, *s5w, *mats)


def _table(mod, d, chunks, extra):
    rows = [mod[:, i * d:(i + 1) * d] for i in chunks]
    rows += [jnp.broadcast_to(e[None, :], (8, d)) for e in extra]
    rows += [jnp.zeros((8, d), F32)] * (8 - len(rows))
    return jnp.stack(rows, axis=1)


def kernel(x, c, ctx, c_ctx, w_mod, b_mod, ln_g, ln_b, ffn_w_gate, ffn_w_up, ffn_w_down, w_in, ssm_log_dt, ssm_a_re, ssm_a_im, ssm_b_re, ssm_b_im, ssm_c_re, ssm_c_im, ssm_d, ssm_w_glu, ssm_b_glu, na_rpb, w_br_ssm, w_br_fnet, w_br_na, w_out):
    bsz, n, d = x.shape
    lc = ctx.shape[1]
    depth = w_mod.shape[0]
    rows = n // GRID_W
    assert bsz + 1 <= 8 and rows % NA_QROWS == 0 and rows >= NA_KROWS and n % FNET_N2 == 0
    alpha = (2 * depth) ** 0.25
    q_scale = NA_HEAD_DIM ** -0.5 * LOG2E

    c8 = jnp.concatenate([c, c_ctx[None, :], jnp.zeros((8 - bsz - 1, d), F32)], axis=0)
    mod_all = _mod_call(c8, w_mod, b_mod)

    wg = ffn_w_gate.astype(BF)
    wu = ffn_w_up.astype(BF)
    wd = ffn_w_down.astype(BF)
    win = w_in.astype(BF)
    mats = (w_br_ssm.astype(BF), w_br_fnet.astype(BF), w_br_na.astype(BF), w_out.astype(BF))
    s5w = (ssm_d[:, None, :], ssm_w_glu.astype(BF), ssm_b_glu[:, None, :])
    s5_tabs = _s5_tables(ssm_log_dt, ssm_a_re, ssm_a_im, ssm_b_re, ssm_b_im, ssm_c_re, ssm_c_im)
    na_bias = _na_bias(na_rpb, rows)
    cos_t, sin_t = _rope_tables(n)
    cos_c = jnp.ones((bsz * lc, NA_WIDTH), F32)
    sin_c = jnp.zeros((bsz * lc, NA_WIDTH), F32)
    zero_state = jnp.zeros((bsz, 8, 2 * SSM_FLAT), F32)

    h = x
    hc = ctx.reshape(1, bsz * lc, d)
    for l in range(depth):
        last = l == depth - 1
        mod = mod_all[l]

        tab = _table(mod, d, (0, 1, 2), (ln_g[l, 0], ln_b[l, 0]))
        h = _ffn_call(h, tab[:bsz], wg, wu, wd, l, 0, alpha)
        hc = _ffn_call(hc, tab[bsz:bsz + 1], wg, wu, wd, l, 0, alpha)

        tab = _table(mod, d, (3, 4, 5), (ln_g[l, 1], ln_b[l, 1]))
        us, u4, k, v, f, q, g = _mixin_call(h, tab[:bsz], win, l, cos_t, sin_t, q_scale)
        usc, u4c, kc, vc, fc, qc, gc = _mixin_call(hc, tab[bsz:bsz + 1], win, l, cos_c, sin_c, q_scale)
        u4c, kc, vc, fc, qc = [a.reshape(bsz, a.shape[1] // bsz, a.shape[-1]) for a in (u4c, kc, vc, fc, qc)]

        ysbc, hcb = _s5_call(u4c, zero_state, s5_tabs, l, 1, u4c.shape[1])
        ysfc, hcf = _s5_call(u4c, zero_state, s5_tabs, l, 0, u4c.shape[1])
        ysb, _ = _s5_call(u4, hcb, s5_tabs, l, 1, min(S5_ROWS, u4.shape[1]))
        ysf, _ = _s5_call(u4, hcf, s5_tabs, l, 0, min(S5_ROWS, u4.shape[1]))

        yf = _fnet_call(f)
        yn = None
        for variant in ((1, 0, 2) if rows > 2 * NA_QROWS else (0, 2)):
            yn = _natten_call(q, k, v, kc, vc, na_bias, l, variant, yn)

        h = _merge_call(ysf, ysb, us, yf, yn, g, h, tab[:bsz], s5w, mats, l, alpha)

        tab3 = _table(mod, d, (6, 7, 8), (ln_g[l, 2], ln_b[l, 2]))
        h = _ffn_call(h, tab3[:bsz], wg, wu, wd, l, 1, alpha)

        if not last:
            yfc = _fnet_dense_call(fc)
            ync = _ctx_atten_call(qc, kc, vc)
            flat = lambda a: a.reshape(1, a.shape[0] * a.shape[1], a.shape[-1])
            hc = _merge_call(flat(ysfc), flat(ysbc), usc, flat(yfc), flat(ync), gc, hc, tab[bsz:bsz + 1],
                             s5w, mats, l, alpha)
            hc = _ffn_call(hc, tab3[bsz:bsz + 1], wg, wu, wd, l, 1, alpha)
    return h
```

```python
import functools
import math

import jax
import jax.numpy as jnp
import numpy as np
from jax import lax
from jax.experimental import pallas as pl
from jax.experimental.pallas import tpu as pltpu

BF = jnp.bfloat16
F32 = jnp.float32

GRID_W = 64
SSM_GROUPS = 16
SSM_GROUP_CH = 16
SSM_STATE = 64
SSM_WIDTH = SSM_GROUPS * SSM_GROUP_CH
SSM_FLAT = SSM_GROUPS * SSM_STATE
FNET_GROUPS = 4
FNET_GROUP_CH = 64
FNET_WIDTH = FNET_GROUPS * FNET_GROUP_CH
NA_HEADS = 8
NA_HEAD_DIM = 64
NA_WIDTH = NA_HEADS * NA_HEAD_DIM
WIN_ROWS = 8
WIN_COLS = 16
ROPE_THETA = 10000.0
N_MOD = 9
LN_EPS = 1e-6
LOG2E = 1.4426950408889634
NEG_BIG = -1e30

VMEM_LIMIT_BYTES = 56 * 1024 * 1024
ROW_TILE = 512
FFN_ROWS = 1024
MERGE_ROWS = 1024
MIXIN_ROWS = 1024
FFN_CHUNK = 256
LANES = 128
SCAN_ROWS = 8
S5_BLOCK = 4
S5_ROWS = 512
S5_SUB = 256
NA_QROWS = 8
NA_KROWS = 16
NA_HQ = 4
FNET_N2 = 128


def _cparams(n_axes, flags=None):
    return pltpu.CompilerParams(dimension_semantics=("arbitrary",) * n_axes,
                                vmem_limit_bytes=VMEM_LIMIT_BYTES, flags=flags)


def _ln(x):
    mu = jnp.mean(x, axis=-1, keepdims=True)
    xc = x - mu
    var = jnp.mean(xc * xc, axis=-1, keepdims=True)
    return xc * lax.rsqrt(var + LN_EPS)


def _dot(a, b):
    return jnp.dot(a, b, preferred_element_type=F32)


def _dot_t(a, b):
    return lax.dot_general(a, b, (((1,), (1,)), ((), ())), preferred_element_type=F32)


def _layer_spec(a, *lead, mode=None):
    nl = len(lead)
    zeros = (0,) * (a.ndim - nl)
    return pl.BlockSpec((None,) * nl + tuple(a.shape[nl:]), lambda *_: tuple(lead) + zeros, pipeline_mode=mode)


def _mod_kernel(c_ref, w_ref, b_ref, o_ref):
    c = c_ref[...]
    s = (c * jax.nn.sigmoid(c)).astype(BF)
    o_ref[...] = _dot(s, w_ref[...].astype(BF)) + b_ref[...]


def _mod_call(c8, w_mod, b_mod):
    depth, d, nd = w_mod.shape
    nb = nd // d
    return pl.pallas_call(
        _mod_kernel,
        grid=(depth, nb),
        in_specs=[pl.BlockSpec((8, d), lambda l, j: (0, 0)),
                  pl.BlockSpec((None, d, d), lambda l, j: (l, 0, j)),
                  pl.BlockSpec((None, 1, d), lambda l, j: (l, 0, j))],
        out_specs=pl.BlockSpec((None, 8, d), lambda l, j: (l, 0, j)),
        out_shape=jax.ShapeDtypeStruct((depth, 8, nd), F32),
        compiler_params=_cparams(2),
        name="mod",
    )(c8, w_mod, b_mod.reshape(depth, 1, nd))


def _ffn_kernel(x_ref, t_ref, wg_ref, wu_ref, wd_ref, o_ref, act_scr, *, alpha):
    t = t_ref[...]
    tiles = [slice(r0, r0 + ROW_TILE) for r0 in range(0, x_ref.shape[0], ROW_TILE)]
    us = [(_ln(x_ref[rows, :]) * (1.0 + t[1:2]) + t[0:1]).astype(BF) for rows in tiles]
    for c0 in range(0, wg_ref.shape[1], FFN_CHUNK):
        for k, u in enumerate(us):
            a = _dot(u, wg_ref[:, c0:c0 + FFN_CHUNK])
            b = _dot(u, wu_ref[:, c0:c0 + FFN_CHUNK])
            act_scr[k, :, c0:c0 + FFN_CHUNK] = (a * jax.nn.sigmoid(a) * b).astype(BF)
    ys = [_dot(act_scr[k], wd_ref[...]) for k in range(len(tiles))]
    for y, rows in zip(ys, tiles):
        v = alpha * x_ref[rows, :] + (0.5 * t[2:3]) * y
        o_ref[rows, :] = _ln(v) * t[3:4] + t[4:5]


def _ffn_call(x, tab, wg, wu, wd, l, k, alpha):
    nb, n, d = x.shape
    tm = min(FFN_ROWS, n)
    once = pl.Buffered(1)
    return pl.pallas_call(
        functools.partial(_ffn_kernel, alpha=alpha),
        grid=(nb, n // tm),
        in_specs=[pl.BlockSpec((None, tm, d), lambda b, i: (b, i, 0)),
                  pl.BlockSpec((None, 8, d), lambda b, i: (b, 0, 0)),
                  _layer_spec(wg, l, k, mode=once), _layer_spec(wu, l, k, mode=once),
                  _layer_spec(wd, l, k, mode=once)],
        out_specs=pl.BlockSpec((None, tm, d), lambda b, i: (b, i, 0)),
        out_shape=jax.ShapeDtypeStruct(x.shape, F32),
        scratch_shapes=[pltpu.VMEM((pl.cdiv(tm, ROW_TILE), min(ROW_TILE, tm), wg.shape[-1]), BF)],
        compiler_params=_cparams(2),
        name="ffn",
    )(x, tab, wg, wu, wd)


def _rope(x, cos, sin_signed):
    n = x.shape[-1]
    lane = lax.broadcasted_iota(jnp.int32, x.shape, 1)
    first = (lane % 32) < 16
    partner = jnp.where(first, pltpu.roll(x, n - 16, axis=1), pltpu.roll(x, 16, axis=1))
    return x * cos + partner * sin_signed


def _mixin_kernel(x_ref, t_ref, w_ref, cos_ref, sin_ref,
                  us_ref, u4_ref, k_ref, v_ref, f_ref, q_ref, g_ref, us_f32, *, q_scale):
    t = t_ref[...]
    c_k = SSM_WIDTH
    c_v = c_k + NA_WIDTH
    c_f = c_v + NA_WIDTH
    c_q = c_f + FNET_WIDTH
    c_g = c_q + NA_WIDTH
    tiles = [slice(r0, r0 + ROW_TILE) for r0 in range(0, x_ref.shape[0], ROW_TILE)]
    us = [(_ln(x_ref[rows, :]) * (1.0 + t[1:2]) + t[0:1]).astype(BF) for rows in tiles]
    rows4 = ROW_TILE // S5_BLOCK
    for u, rows in zip(us, tiles):
        zs = _dot(u, w_ref[:, 0:c_k])
        us_ref[rows, :] = zs.astype(BF)
        for hf in range(c_k // LANES):
            us_f32[hf, rows, :] = zs[:, hf * LANES:(hf + 1) * LANES]
        q0 = rows.start // S5_BLOCK
        for j in range(S5_BLOCK):
            for hf in range(c_k // LANES):
                c0 = j * c_k + hf * LANES
                u4_ref[q0:q0 + rows4, c0:c0 + LANES] = (
                    us_f32[hf, pl.ds(rows.start + j, rows4, stride=S5_BLOCK), :].astype(BF))
    for u, rows in zip(us, tiles):
        k_ref[rows, :] = _rope(_dot(u, w_ref[:, c_k:c_v]), cos_ref[rows, :], sin_ref[rows, :]).astype(BF)
    for u, rows in zip(us, tiles):
        v_ref[rows, :] = _dot(u, w_ref[:, c_v:c_f]).astype(BF)
    for u, rows in zip(us, tiles):
        f_ref[rows, :] = _dot(u, w_ref[:, c_f:c_q]).astype(BF)
    for u, rows in zip(us, tiles):
        q = _rope(_dot(u, w_ref[:, c_q:c_g]), cos_ref[rows, :], sin_ref[rows, :])
        q_ref[rows, :] = (q * q_scale).astype(BF)
    gw = NA_WIDTH
    for c0 in range(0, g_ref.shape[-1], gw):
        for u, rows in zip(us, tiles):
            g = _dot(u, w_ref[:, c_g + c0:c_g + c0 + gw])
            g_ref[rows, c0:c0 + gw] = jax.nn.sigmoid(g).astype(BF)


def _mixin_call(x, tab, w_in, l, cos_t, sin_t, q_scale):
    nb, n, d = x.shape
    win = w_in.shape[-1]
    ng = win - (SSM_WIDTH + 3 * NA_WIDTH + FNET_WIDTH)
    tm = min(MIXIN_ROWS, n)
    nt_pos = cos_t.shape[0] // tm

    def tok(width):
        return pl.BlockSpec((None, tm, width), lambda b, i: (b, i, 0))

    def shape(width):
        return jax.ShapeDtypeStruct((nb, n, width), BF)

    u4_spec = pl.BlockSpec((None, tm // S5_BLOCK, S5_BLOCK * SSM_WIDTH), lambda b, i: (b, i, 0))
    u4_shape = jax.ShapeDtypeStruct((nb, n // S5_BLOCK, S5_BLOCK * SSM_WIDTH), BF)
    return pl.pallas_call(
        functools.partial(_mixin_kernel, q_scale=q_scale),
        grid=(nb, n // tm),
        in_specs=[tok(d),
                  pl.BlockSpec((None, 8, d), lambda b, i: (b, 0, 0)),
                  _layer_spec(w_in, l, mode=pl.Buffered(1)),
                  pl.BlockSpec((tm, NA_WIDTH), lambda b, i: (i % nt_pos, 0)),
                  pl.BlockSpec((tm, NA_WIDTH), lambda b, i: (i % nt_pos, 0))],
        out_specs=[tok(SSM_WIDTH), u4_spec, tok(NA_WIDTH), tok(NA_WIDTH), tok(FNET_WIDTH), tok(NA_WIDTH),
                   tok(ng)],
        out_shape=[shape(SSM_WIDTH), u4_shape, shape(NA_WIDTH), shape(NA_WIDTH), shape(FNET_WIDTH),
                   shape(NA_WIDTH), shape(ng)],
        scratch_shapes=[pltpu.VMEM((SSM_WIDTH // LANES, tm, LANES), F32)],
        compiler_params=_cparams(2),
        name="mixin",
    )(x, tab, w_in, cos_t, sin_t)


def _s5_kernel(u_ref, h0_ref, wst_ref, wconv_ref, wout_ref, sc_ref, y_ref, hfin_ref, hbuf, hprev, carry,
               *, reverse):
    i = pl.program_id(1)
    ns = SSM_FLAT
    rows = hbuf.shape[0]
    sub = min(S5_SUB, rows)
    order = (lambda r: list(r)[::-1]) if reverse else list
    subs = order(range(0, rows, sub))

    @pl.when(i == 0)
    def _():
        carry[...] = h0_ref[...]

    for s0 in subs:
        hbuf[s0:s0 + sub, :] = _dot(u_ref[s0:s0 + sub, :], wst_ref[...])
    row_id = lax.broadcasted_iota(jnp.int32, (SCAN_ROWS, ns), 0)
    edge = row_id == ((SCAN_ROWS - 1) if reverse else 0)

    cr = carry[:, 0:ns]
    ci = carry[:, ns:2 * ns]
    for s0 in subs:
        for r0 in order(range(s0, s0 + sub, SCAN_ROWS)):
            xr = hbuf[r0:r0 + SCAN_ROWS, 0:ns]
            xi = hbuf[r0:r0 + SCAN_ROWS, ns:2 * ns]
            for k, s in enumerate((1, 2, 4)):
                ar = sc_ref[k, :, 0:ns]
                ai = sc_ref[k, :, ns:2 * ns]
                shift = (SCAN_ROWS - s) if reverse else s
                sr = pltpu.roll(xr, shift, axis=0)
                si = pltpu.roll(xi, shift, axis=0)
                xr, xi = xr + ar * sr - ai * si, xi + ar * si + ai * sr
            pr = sc_ref[3, :, 0:ns]
            pi = sc_ref[3, :, ns:2 * ns]
            hr = xr + pr * cr - pi * ci
            hi = xi + pr * ci + pi * cr
            one = (SCAN_ROWS - 1) if reverse else 1
            hprev[r0:r0 + SCAN_ROWS, 0:ns] = jnp.where(edge, cr, pltpu.roll(hr, one, axis=0))
            hprev[r0:r0 + SCAN_ROWS, ns:2 * ns] = jnp.where(edge, ci, pltpu.roll(hi, one, axis=0))
            last = 0 if reverse else SCAN_ROWS - 1
            cr = jnp.broadcast_to(hr[last:last + 1, :], hr.shape)
            ci = jnp.broadcast_to(hi[last:last + 1, :], hi.shape)
    carry[:, 0:ns] = cr
    carry[:, ns:2 * ns] = ci
    hfin_ref[...] = carry[...]

    for s0 in subs:
        y = (_dot(u_ref[s0:s0 + sub, :], wconv_ref[...])
             + _dot(hprev[s0:s0 + sub, :].astype(BF), wout_ref[...]))
        y_ref[s0:s0 + sub, :] = y.astype(y_ref.dtype)


def _s5_call(u4, h0, tables, l, z, rows):
    wst, wconv, wout, sc = tables
    b, n, w = u4.shape
    nc = n // rows
    ns2 = 2 * SSM_FLAT
    reverse = z == 1
    if reverse:
        cmap = lambda bb, i: (bb, nc - 1 - i, 0)
    else:
        cmap = lambda bb, i: (bb, i, 0)
    return pl.pallas_call(
        functools.partial(_s5_kernel, reverse=reverse),
        grid=(b, nc),
        in_specs=[pl.BlockSpec((None, rows, w), cmap),
                  pl.BlockSpec((None, 8, ns2), lambda bb, i: (bb, 0, 0)),
                  _layer_spec(wst, l, z), _layer_spec(wconv, l, z), _layer_spec(wout, l, z),
                  _layer_spec(sc, l, z)],
        out_specs=[pl.BlockSpec((None, rows, w), cmap),
                   pl.BlockSpec((None, 8, ns2), lambda bb, i: (bb, 0, 0))],
        out_shape=[jax.ShapeDtypeStruct((b, n, w), BF),
                   jax.ShapeDtypeStruct((b, 8, ns2), F32)],
        scratch_shapes=[pltpu.VMEM((rows, ns2), F32), pltpu.VMEM((rows, ns2), F32), pltpu.VMEM((8, ns2), F32)],
        compiler_params=_cparams(2),
        name="s5_bwd" if reverse else "s5_fwd",
    )(u4, h0, wst, wconv, wout, sc)


def _s5_tables(log_dt, a_re, a_im, b_re, b_im, c_re, c_im):
    depth = log_dt.shape[0]
    g, p, c = b_re.shape[-3:]
    tb = S5_BLOCK
    hp = lax.Precision.HIGHEST
    f32 = lambda a: a.astype(F32)
    a_re, a_im, b_re, b_im, c_re, c_im = map(f32, (a_re, a_im, b_re, b_im, c_re, c_im))
    dt = jnp.exp(f32(log_dt))[..., None]
    adt_re = a_re * dt
    adt_im = a_im * dt

    def apow(k):
        kk = jnp.asarray(k, F32)[None, :, :, None, None]
        mag = jnp.exp(kk * adt_re[:, :, None])
        ang = kk * adt_im[:, :, None]
        return mag * jnp.cos(ang), mag * jnp.sin(ang)

    mag = jnp.exp(adt_re)
    ab_re = mag * jnp.cos(adt_im)
    ab_im = mag * jnp.sin(adt_im)
    den = a_re * a_re + a_im * a_im
    nr = ab_re - 1.0
    fr = (nr * a_re + ab_im * a_im) / den
    fi = (ab_im * a_re - nr * a_im) / den
    bb_re = fr[..., None] * b_re - fi[..., None] * b_im
    bb_im = fr[..., None] * b_im + fi[..., None] * b_re

    step = np.arange(tb)
    k_state = np.stack([tb - 1 - step, step])
    k_out = np.stack([step + 1, tb - step])
    k_tap = np.stack([step, step])
    k_scan = np.stack([tb * (np.arange(SCAN_ROWS) + 1)] * 2)

    def block_diag(compact, row_group, col_width):
        n = compact.shape[3]
        dst = np.arange(n * g * col_width)
        src = (dst // (g * col_width)) * col_width + dst % col_width
        copy = np.zeros((n * col_width, n * g * col_width), np.float32)
        copy[src, dst] = 1.0
        cols = jnp.einsum('dzrk,kn->dzrn', compact.reshape(compact.shape[:3] + (n * col_width,)),
                          jnp.asarray(copy), precision=hp)
        keep = jnp.asarray(row_group[:, None] == ((dst // col_width) % g)[None, :])
        return jnp.where(keep, cols, 0.0).astype(BF)

    chan_rows = (np.arange(tb * g * c) // c) % g
    state_rows = (np.arange(2 * g * p) // p) % g

    def cmul_b(pr, pi):
        return (pr[..., None] * bb_re[:, :, None] - pi[..., None] * bb_im[:, :, None],
                pr[..., None] * bb_im[:, :, None] + pi[..., None] * bb_re[:, :, None])

    wr, wi = cmul_b(*apow(k_state))
    w2 = jnp.stack([wr, wi], axis=3)
    w2 = jnp.transpose(w2, (0, 1, 2, 4, 6, 3, 5))
    wst = block_diag(w2.reshape(depth, 2, tb * g * c, 2, p), chan_rows, p)

    pr, pi = apow(k_out)
    cm_re = c_re[:, :, None] * pr[:, :, :, :, None, :] - c_im[:, :, None] * pi[:, :, :, :, None, :]
    cm_im = c_re[:, :, None] * pi[:, :, :, :, None, :] + c_im[:, :, None] * pr[:, :, :, :, None, :]
    cm = jnp.stack([cm_re, -cm_im], axis=3)
    cm = jnp.transpose(cm, (0, 1, 3, 4, 6, 2, 5))
    wout = block_diag(cm.reshape(depth, 2, 2 * g * p, tb, c), state_rows, c)

    abr, abi = cmul_b(*apow(k_tap))
    taps = (jnp.einsum('dzgcp,dztgpe->dztgec', c_re, abr, precision=hp)
            - jnp.einsum('dzgcp,dztgpe->dztgec', c_im, abi, precision=hp))
    sel = np.zeros((2, tb, tb, tb), np.float32)
    for i in range(tb):
        for j in range(tb):
            if j >= i:
                sel[0, i, j, j - i] = 1.0
            if i >= j:
                sel[1, i, j, i - j] = 1.0
    kij = jnp.einsum('dztgec,zijt->dzigejc', taps, jnp.asarray(sel), precision=hp)
    wconv = block_diag(kij.reshape(depth, 2, tb * g * c, tb, c), chan_rows, c)

    qr, qi = apow(k_scan)
    qr = qr.reshape(depth, 2, SCAN_ROWS, g * p)
    qi = qi.reshape(depth, 2, SCAN_ROWS, g * p)
    rows = np.arange(SCAN_ROWS)
    tabs = []
    for s in (1, 2, 4):
        keep = jnp.asarray(np.stack([rows >= s, rows + s <= SCAN_ROWS - 1]), F32)[None, :, :, None]
        tabs.append(jnp.concatenate([keep * qr[:, :, s - 1:s], keep * qi[:, :, s - 1:s]], axis=-1))
    carry_pow = lambda t: jnp.stack([t[:, 0], t[:, 1, ::-1]], axis=1)
    tabs.append(jnp.concatenate([carry_pow(qr), carry_pow(qi)], axis=-1))
    sc = jnp.stack(tabs, axis=2)
    return wst, wconv, wout, sc


def _fnet1_kernel(x_ref, f2_ref, tc_ref, ts_ref, o_ref):
    n2 = x_ref.shape[0]
    y = _dot(f2_ref[...], x_ref[...])
    br = y[0:n2]
    bi = y[n2:2 * n2]
    tc = tc_ref[...]
    ts = ts_ref[...]
    o_ref[0] = (br * tc + bi * ts).astype(BF)
    o_ref[1] = (bi * tc - br * ts).astype(BF)


def _fnet2_kernel(b_ref, f1_ref, cc_ref, sc_ref, o_ref, *, scale):
    kb = b_ref.shape[1]
    n1 = b_ref.shape[2]
    c = b_ref.shape[3]
    zr, zi = [], []
    for j in range(kb):
        rhs = jnp.concatenate([b_ref[0, j], b_ref[1, j]], axis=0)
        z = _dot(f1_ref[...], rhs)
        zr.append(z[0:n1])
        zi.append(z[n1:2 * n1])
    zr = jnp.concatenate(zr, axis=0).astype(BF)
    zi = jnp.concatenate(zi, axis=0).astype(BF)
    y = (_dot(zr, cc_ref[...]) + _dot(zi, sc_ref[...])) * scale
    for j in range(kb):
        o_ref[:, j * c:(j + 1) * c] = y[j * n1:(j + 1) * n1].astype(o_ref.dtype)


def _dft_tables(n):
    idx = np.arange(n)
    ang = 2.0 * np.pi * ((idx[:, None] * idx[None, :]) % n) / n
    return np.cos(ang), np.sin(ang)


def _chan_tables():
    cc, sc = _dft_tables(FNET_GROUP_CH)
    eye = np.eye(FNET_GROUPS)
    return jnp.asarray(np.kron(eye, cc), BF), jnp.asarray(np.kron(eye, sc), BF)


def _fnet_call(f):
    b, n, c = f.shape
    n2 = FNET_N2
    n1 = n // n2
    c2, s2 = _dft_tables(n2)
    f2 = jnp.asarray(np.concatenate([c2, -s2], axis=0), BF)
    k2 = np.arange(n2)[:, None]
    nn1 = np.arange(n1)[None, :]
    ang = 2.0 * np.pi * ((k2 * nn1) % n) / n
    tc = jnp.asarray(np.repeat(np.cos(ang), c, axis=1), F32)
    ts = jnp.asarray(np.repeat(np.sin(ang), c, axis=1), F32)
    c1, s1 = _dft_tables(n1)
    f1 = jnp.asarray(np.block([[c1, s1], [-s1, c1]]), BF)
    cc, sc = _chan_tables()

    wc = min(n1 * c, 4096)
    x2 = f.reshape(b, n2, n1 * c)
    st1 = pl.pallas_call(
        _fnet1_kernel,
        grid=(b, (n1 * c) // wc),
        in_specs=[pl.BlockSpec((None, n2, wc), lambda bb, j: (bb, 0, j)),
                  pl.BlockSpec((2 * n2, n2), lambda bb, j: (0, 0)),
                  pl.BlockSpec((n2, wc), lambda bb, j: (0, j)),
                  pl.BlockSpec((n2, wc), lambda bb, j: (0, j))],
        out_specs=pl.BlockSpec((None, 2, n2, wc), lambda bb, j: (bb, 0, 0, j)),
        out_shape=jax.ShapeDtypeStruct((b, 2, n2, n1 * c), BF),
        compiler_params=_cparams(2),
        name="fnet1",
    )(x2, f2, tc, ts)
    st1 = st1.reshape(b, 2, n2, n1, c)
    kb = 8
    y = pl.pallas_call(
        functools.partial(_fnet2_kernel, scale=1.0 / math.sqrt(n * FNET_GROUP_CH)),
        grid=(b, n2 // kb),
        in_specs=[pl.BlockSpec((None, 2, kb, n1, c), lambda bb, j: (bb, 0, j, 0, 0)),
                  pl.BlockSpec((2 * n1, 2 * n1), lambda bb, j: (0, 0)),
                  pl.BlockSpec((c, c), lambda bb, j: (0, 0)),
                  pl.BlockSpec((c, c), lambda bb, j: (0, 0))],
        out_specs=pl.BlockSpec((None, n1, kb * c), lambda bb, j: (bb, 0, j)),
        out_shape=jax.ShapeDtypeStruct((b, n1, n2 * c), BF),
        compiler_params=_cparams(2),
        name="fnet2",
    )(st1, f1, cc, sc)
    return y.reshape(b, n, c)


def _fnet_dense_kernel(x_ref, cl_ref, sl_ref, cc_ref, sc_ref, o_ref, *, scale):
    x = x_ref[...]
    xr = _dot(x, cc_ref[...]).astype(BF)
    xi = _dot(x, sc_ref[...]).astype(BF)
    o_ref[...] = ((_dot(cl_ref[...], xr) - _dot(sl_ref[...], xi)) * scale).astype(o_ref.dtype)


def _fnet_dense_call(f):
    b, n, c = f.shape
    cl, sl = _dft_tables(n)
    cc, sc = _chan_tables()
    full = lambda bb: (0, 0)
    return pl.pallas_call(
        functools.partial(_fnet_dense_kernel, scale=1.0 / math.sqrt(n * FNET_GROUP_CH)),
        grid=(b,),
        in_specs=[pl.BlockSpec((None, n, c), lambda bb: (bb, 0, 0)),
                  pl.BlockSpec((n, n), full), pl.BlockSpec((n, n), full),
                  pl.BlockSpec((c, c), full), pl.BlockSpec((c, c), full)],
        out_specs=pl.BlockSpec((None, n, c), lambda bb: (bb, 0, 0)),
        out_shape=jax.ShapeDtypeStruct((b, n, c), BF),
        compiler_params=_cparams(1),
        name="fnet_ctx",
    )(f, jnp.asarray(cl, BF), jnp.asarray(sl, BF), cc, sc)


def _head_masks(width):
    lane = lax.broadcasted_iota(jnp.int32, (1, width), 1)
    return [(lane // NA_HEAD_DIM) == h for h in range(width // NA_HEAD_DIM)]


def _attend(q, k_parts, v_parts, ranges, bias_of, o_ref):
    zero = jnp.zeros((), BF)
    hw = NA_HQ * NA_HEAD_DIM
    masks = _head_masks(hw)
    m_rows = q.shape[0]
    rb = min(GRID_W, m_rows)
    live = [j for j, (a, b) in enumerate(ranges) if b > a]
    k_parts = [k_parts[j] for j in live]
    v_parts = [v_parts[j] for j in live]
    ranges = [ranges[j] for j in live]
    quad = lambda x, h: x[:, (h // NA_HQ) * hw:(h // NA_HQ + 1) * hw]

    def scores(h):
        qh = jnp.where(masks[h % NA_HQ], quad(q, h), zero)
        s_parts = []
        for j, (kp, (a, b)) in enumerate(zip(k_parts, ranges)):
            s = _dot_t(qh[a:b], quad(kp, h))
            bias = bias_of(h, live[j], a, b)
            s_parts.append(s if bias is None else s + bias)
        return s_parts

    def lane_tiles(x):
        return [x[:, c:c + LANES] for c in range(0, x.shape[1], LANES)]

    def softmax(s_parts):
        p_rows = [[] for _ in s_parts]
        dens = []
        for r0 in range(0, m_rows, rb):
            pieces = [(j, s[r0 - a:r0 - a + rb]) for j, (s, (a, b)) in enumerate(zip(s_parts, ranges))
                      if a <= r0 < b]
            m = functools.reduce(jnp.maximum, [t for _, pc in pieces for t in lane_tiles(pc)])
            m = m.max(axis=-1, keepdims=True)
            acc = None
            for j, pc in pieces:
                p = jnp.exp2(pc - m)
                acc = functools.reduce(jnp.add, lane_tiles(p), acc) if acc is not None else \
                    functools.reduce(jnp.add, lane_tiles(p))
                p_rows[j].append(p.astype(BF))
            dens.append(acc.sum(axis=-1, keepdims=True))
        return [jnp.concatenate(r, axis=0) for r in p_rows], jnp.concatenate(dens, axis=0)

    def values(h, p_parts, den):
        out = None
        for p, vp, (a, b) in zip(p_parts, v_parts, ranges):
            o = _dot(p, quad(vp, h))
            pad = [jnp.zeros((n, o.shape[1]), F32) for n in (a, m_rows - b)]
            o = jnp.concatenate([x for x in (pad[0], o, pad[1]) if x.shape[0]], axis=0)
            out = o if out is None else out + o
        return out / den

    n = q.shape[-1] // NA_HEAD_DIM
    s_of, p_of, outs = {}, {}, [None] * (n // NA_HQ)
    for step in range(n + 2):
        if step < n:
            s_of[step] = scores(step)
        if 0 <= step - 1 < n:
            p_of[step - 1] = softmax(s_of.pop(step - 1))
        if 0 <= step - 2 < n:
            h = step - 2
            o = values(h, *p_of.pop(h))
            prev = outs[h // NA_HQ]
            outs[h // NA_HQ] = o if prev is None else jnp.where(masks[h % NA_HQ], o, prev)
    o_ref[...] = jnp.concatenate(outs, axis=-1).astype(o_ref.dtype)


def _natten_kernel(q_ref, k0, k1, k2, k3, v0, v1, v2, v3, kc_ref, vc_ref, bias_ref, *rest, ranges, pair_index):
    o_ref = rest[-1]
    k_parts = [r[...] for r in (k0, k1, k2, k3)] + [kc_ref[...]]
    v_parts = [r[...] for r in (v0, v1, v2, v3)] + [vc_ref[...]]
    pairs_per_part = k0.shape[0] // (2 * GRID_W)

    def bias_of(h, j, a, b):
        if j >= 4:
            return None
        rows = [jnp.concatenate([bias_ref[h, pair_index[r][j * pairs_per_part + p]]
                                 for p in range(pairs_per_part)], axis=1)
                for r in range(a // GRID_W, b // GRID_W)]
        return jnp.concatenate(rows, axis=0).astype(F32)

    _attend(q_ref[...], k_parts, v_parts, ranges, bias_of, o_ref)


def _na_windows(rows):
    nj = rows // NA_QROWS
    out = []
    for j in (0, min(1, nj - 1), nj - 1):
        rq = NA_QROWS * j + np.arange(NA_QROWS)
        ws = int(np.clip(NA_QROWS * j - (NA_KROWS - NA_QROWS) // 2, 0, rows - NA_KROWS))
        rk = ws + np.arange(NA_KROWS)
        r0 = np.clip(rq - WIN_ROWS // 2, 0, rows - WIN_ROWS)
        valid_r = (rk[None, :] >= r0[:, None]) & (rk[None, :] < r0[:, None] + WIN_ROWS)
        dr = rk[None, :] - rq[:, None] + (WIN_ROWS - 1)
        out.append((valid_r, dr))
    return out


def _natten_call(q, k, v, kc, vc, bias, l, variant, prev=None):
    table, index = bias
    b, n, w = q.shape
    lc = kc.shape[1]
    hw = NA_HQ * NA_HEAD_DIM
    qt = NA_QROWS * GRID_W
    blk = (NA_KROWS // 4) * GRID_W
    nj = n // qt
    nblk = n // blk
    ratio = qt // blk
    j0, count = ((0, 1), (1, nj - 2), (nj - 1, 1))[variant]

    valid_r = _na_windows(n // GRID_W)[variant][0]
    ranges = []
    for t in range(4):
        seen = np.nonzero(valid_r[:, t * (NA_KROWS // 4):(t + 1) * (NA_KROWS // 4)].any(axis=1))[0]
        ranges.append((int(seen.min()) * GRID_W, (int(seen.max()) + 1) * GRID_W) if seen.size else (0, 0))
    ranges.append((0, qt))

    def kv_spec(m):
        def imap(j, bb):
            start = jnp.clip(ratio * (j + j0) - 1, 0, nblk - 4)
            return (bb, start + m, 0)
        return pl.BlockSpec((None, blk, w), imap)

    in_specs = ([pl.BlockSpec((None, qt, w), lambda j, bb: (bb, j + j0, 0))]
                + [kv_spec(m) for m in range(4)] + [kv_spec(m) for m in range(4)]
                + [pl.BlockSpec((None, lc, w), lambda j, bb: (bb, 0, 0)),
                   pl.BlockSpec((None, lc, w), lambda j, bb: (bb, 0, 0)),
                   _layer_spec(table, l, mode=pl.Buffered(1))])
    args = [q, k, k, k, k, v, v, v, v, kc, vc, table]
    aliases = {}
    if prev is not None:
        in_specs.append(pl.BlockSpec(memory_space=pl.ANY))
        aliases = {len(args): 0}
        args.append(prev)
    return pl.pallas_call(
        functools.partial(_natten_kernel, ranges=tuple(ranges), pair_index=index[variant]),
        grid=(count, b),
        in_specs=in_specs,
        out_specs=pl.BlockSpec((None, qt, w), lambda j, bb: (bb, j + j0, 0)),
        out_shape=jax.ShapeDtypeStruct((b, n, w), BF),
        input_output_aliases=aliases,
        compiler_params=_cparams(2),
        name="natten",
    )(*args)


def _ctx_atten_kernel(q_ref, kc_ref, vc_ref, o_ref):
    _attend(q_ref[...], [kc_ref[...]], [vc_ref[...]], [(0, q_ref.shape[0])], lambda h, j, a, b: None, o_ref)


def _ctx_atten_call(qc, kc, vc):
    b, lc, w = qc.shape
    spec = pl.BlockSpec((None, lc, w), lambda bb: (bb, 0, 0))
    return pl.pallas_call(
        _ctx_atten_kernel,
        grid=(b,),
        in_specs=[spec, spec, spec],
        out_specs=spec,
        out_shape=jax.ShapeDtypeStruct((b, lc, w), BF),
        compiler_params=_cparams(1),
        name="ctx_atten",
    )(qc, kc, vc)


def _na_bias(rpb, rows):
    w = GRID_W
    n_dr = 2 * WIN_ROWS - 1
    n_dc = 2 * WIN_COLS - 1
    cq = np.arange(w)
    cs = np.clip(cq - WIN_COLS // 2, 0, w - WIN_COLS)
    ck = np.arange(w)
    valid_c = (ck[None, :] >= cs[:, None]) & (ck[None, :] < cs[:, None] + WIN_COLS)
    dc = ck[None, :] - cq[:, None] + (WIN_COLS - 1)
    pick_c = (dc[:, :, None] == np.arange(n_dc)) & valid_c[:, :, None]
    tiles = jnp.einsum('lhab,cdb->lhacd', rpb.astype(F32) * LOG2E, jnp.asarray(pick_c, F32),
                       precision=lax.Precision.HIGHEST)
    tiles = tiles + jnp.asarray(np.where(valid_c, 0.0, NEG_BIG), F32)
    tiles = jnp.concatenate([tiles, jnp.full_like(tiles[:, :, :1], NEG_BIG)], axis=2).astype(BF)
    pairs, index = {}, []
    for valid_r, dr in _na_windows(rows):
        dr = np.where(valid_r, dr, n_dr)
        index.append(tuple(tuple(pairs.setdefault((int(dr[a, 2 * p]), int(dr[a, 2 * p + 1])), len(pairs))
                                 for p in range(NA_KROWS // 2)) for a in range(NA_QROWS)))
    table = jnp.stack([jnp.concatenate([tiles[:, :, i0], tiles[:, :, i1]], axis=-1) for i0, i1 in pairs], axis=2)
    return table, index


def _rope_tables(n):
    nf = NA_HEAD_DIM // 4
    t = jnp.arange(n, dtype=jnp.int32)
    pos = jnp.stack([t // GRID_W, t % GRID_W], axis=-1).astype(F32)
    inv_freq = ROPE_THETA ** (-jnp.arange(nf, dtype=F32) / nf)
    ang = pos[:, :, None] * inv_freq
    cos = jnp.cos(ang)
    sin = jnp.sin(ang)
    cos_h = jnp.concatenate([cos, cos], axis=-1).reshape(n, NA_HEAD_DIM)
    sin_h = jnp.concatenate([-sin, sin], axis=-1).reshape(n, NA_HEAD_DIM)
    return jnp.tile(cos_h, (1, NA_HEADS)), jnp.tile(sin_h, (1, NA_HEADS))


def _merge_kernel(ysf_ref, ysb_ref, us_ref, yf_ref, yn_ref, g_ref, h_ref, t_ref,
                  sd_ref, wglu_ref, bglu_ref, wbs_ref, wbf_ref, wbn_ref, wo_ref, o_ref, ys_f32, *, alpha):
    d = h_ref.shape[-1]
    t = t_ref[...]
    halves = SSM_WIDTH // LANES
    tiles = [slice(r0, r0 + ROW_TILE) for r0 in range(0, h_ref.shape[0], ROW_TILE)]
    rows4 = ROW_TILE // S5_BLOCK
    ys = []
    for rows in tiles:
        q0 = rows.start // S5_BLOCK
        y4 = ysf_ref[q0:q0 + rows4, :].astype(F32) + ysb_ref[q0:q0 + rows4, :].astype(F32)
        for j in range(S5_BLOCK):
            for hf in range(halves):
                c0 = j * SSM_WIDTH + hf * LANES
                ys_f32[hf, pl.ds(rows.start + j, rows4, stride=S5_BLOCK), :] = y4[:, c0:c0 + LANES]
        y = jnp.concatenate([ys_f32[hf, rows, :] for hf in range(halves)], axis=-1)
        ys.append(jax.nn.gelu(y + sd_ref[...] * us_ref[rows, :].astype(F32)))
    glu = [_dot(y.astype(BF), wglu_ref[...]) for y in ys]
    ys = [(y * jax.nn.sigmoid(z + bglu_ref[...])).astype(BF) for y, z in zip(ys, glu)]
    br = [(_dot(y, wbs_ref[...]), _dot(yf_ref[rows, :], wbf_ref[...]), _dot(yn_ref[rows, :], wbn_ref[...]))
          for y, rows in zip(ys, tiles)]
    mix = [(g_ref[rows, 0:d].astype(F32) * bs + g_ref[rows, d:2 * d].astype(F32) * bf
            + g_ref[rows, 2 * d:3 * d].astype(F32) * bn).astype(BF) for (bs, bf, bn), rows in zip(br, tiles)]
    proj = [_dot(m, wo_ref[...]) for m in mix]
    for y, rows in zip(proj, tiles):
        v = alpha * h_ref[rows, :] + t[2:3] * y
        o_ref[rows, :] = _ln(v) * t[3:4] + t[4:5]


def _merge_call(ysf4, ysb4, us, yf, yn, g, h, tab, s5w, mats, l, alpha):
    nb, n, d = h.shape
    tm = min(MERGE_ROWS, n)

    def tok(a):
        return pl.BlockSpec((None, tm, a.shape[-1]), lambda b, i: (b, i, 0))

    def tok4(a):
        return pl.BlockSpec((None, tm // S5_BLOCK, a.shape[-1]), lambda b, i: (b, i, 0))

    toks = (us, yf, yn, g, h)
    return pl.pallas_call(
        functools.partial(_merge_kernel, alpha=alpha),
        grid=(nb, n // tm),
        in_specs=[tok4(ysf4), tok4(ysb4)] + [tok(a) for a in toks]
                 + [pl.BlockSpec((None, 8, d), lambda b, i: (b, 0, 0))]
                 + [_layer_spec(a, l) for a in tuple(s5w) + tuple(mats)],
        out_specs=tok(h),
        out_shape=jax.ShapeDtypeStruct(h.shape, F32),
        scratch_shapes=[pltpu.VMEM((SSM_WIDTH // LANES, tm, LANES), F32)],
        compiler_params=_cparams(2),
        name="merge",
    )(ysf4, ysb4, *toks, tab, *s5w, *mats)


def _table(mod, d, chunks, extra):
    rows = [mod[:, i * d:(i + 1) * d] for i in chunks]
    rows += [jnp.broadcast_to(e[None, :], (8, d)) for e in extra]
    rows += [jnp.zeros((8, d), F32)] * (8 - len(rows))
    return jnp.stack(rows, axis=1)


def kernel(x, c, ctx, c_ctx, w_mod, b_mod, ln_g, ln_b, ffn_w_gate, ffn_w_up, ffn_w_down, w_in, ssm_log_dt, ssm_a_re, ssm_a_im, ssm_b_re, ssm_b_im, ssm_c_re, ssm_c_im, ssm_d, ssm_w_glu, ssm_b_glu, na_rpb, w_br_ssm, w_br_fnet, w_br_na, w_out):
    bsz, n, d = x.shape
    lc = ctx.shape[1]
    depth = w_mod.shape[0]
    rows = n // GRID_W
    assert bsz + 1 <= 8 and rows % NA_QROWS == 0 and rows >= NA_KROWS and n % FNET_N2 == 0
    alpha = (2 * depth) ** 0.25
    q_scale = NA_HEAD_DIM ** -0.5 * LOG2E

    c8 = jnp.concatenate([c, c_ctx[None, :], jnp.zeros((8 - bsz - 1, d), F32)], axis=0)
    mod_all = _mod_call(c8, w_mod, b_mod)

    wg = ffn_w_gate.astype(BF)
    wu = ffn_w_up.astype(BF)
    wd = ffn_w_down.astype(BF)
    win = w_in.astype(BF)
    mats = (w_br_ssm.astype(BF), w_br_fnet.astype(BF), w_br_na.astype(BF), w_out.astype(BF))
    s5w = (ssm_d[:, None, :], ssm_w_glu.astype(BF), ssm_b_glu[:, None, :])
    s5_tabs = _s5_tables(ssm_log_dt, ssm_a_re, ssm_a_im, ssm_b_re, ssm_b_im, ssm_c_re, ssm_c_im)
    na_bias = _na_bias(na_rpb, rows)
    cos_t, sin_t = _rope_tables(n)
    cos_c = jnp.ones((bsz * lc, NA_WIDTH), F32)
    sin_c = jnp.zeros((bsz * lc, NA_WIDTH), F32)
    zero_state = jnp.zeros((bsz, 8, 2 * SSM_FLAT), F32)

    h = x
    hc = ctx.reshape(1, bsz * lc, d)
    for l in range(depth):
        last = l == depth - 1
        mod = mod_all[l]

        tab = _table(mod, d, (0, 1, 2), (ln_g[l, 0], ln_b[l, 0]))
        h = _ffn_call(h, tab[:bsz], wg, wu, wd, l, 0, alpha)
        hc = _ffn_call(hc, tab[bsz:bsz + 1], wg, wu, wd, l, 0, alpha)

        tab = _table(mod, d, (3, 4, 5), (ln_g[l, 1], ln_b[l, 1]))
        us, u4, k, v, f, q, g = _mixin_call(h, tab[:bsz], win, l, cos_t, sin_t, q_scale)
        usc, u4c, kc, vc, fc, qc, gc = _mixin_call(hc, tab[bsz:bsz + 1], win, l, cos_c, sin_c, q_scale)
        u4c, kc, vc, fc, qc = [a.reshape(bsz, a.shape[1] // bsz, a.shape[-1]) for a in (u4c, kc, vc, fc, qc)]

        ysbc, hcb = _s5_call(u4c, zero_state, s5_tabs, l, 1, u4c.shape[1])
        ysfc, hcf = _s5_call(u4c, zero_state, s5_tabs, l, 0, u4c.shape[1])
        ysb, _ = _s5_call(u4, hcb, s5_tabs, l, 1, min(S5_ROWS, u4.shape[1]))
        ysf, _ = _s5_call(u4, hcf, s5_tabs, l, 0, min(S5_ROWS, u4.shape[1]))

        yf = _fnet_call(f)
        yn = None
        for variant in ((1, 0, 2) if rows > 2 * NA_QROWS else (0, 2)):
            yn = _natten_call(q, k, v, kc, vc, na_bias, l, variant, yn)

        h = _merge_call(ysf, ysb, us, yf, yn, g, h, tab[:bsz], s5w, mats, l, alpha)

        tab3 = _table(mod, d, (6, 7, 8), (ln_g[l, 2], ln_b[l, 2]))
        h = _ffn_call(h, tab3[:bsz], wg, wu, wd, l, 1, alpha)

        if not last:
            yfc = _fnet_dense_call(fc)
            ync = _ctx_atten_call(qc, kc, vc)
            flat = lambda a: a.reshape(1, a.shape[0] * a.shape[1], a.shape[-1])
            hc = _merge_call(flat(ysfc), flat(ysbc), usc, flat(yfc), flat(ync), gc, hc, tab[bsz:bsz + 1],
                             s5w, mats, l, alpha)
            hc = _ffn_call(hc, tab3[bsz:bsz + 1], wg, wu, wd, l, 1, alpha)
    return h
```

```python
import functools
import math

import jax
import jax.numpy as jnp
import numpy as np
from jax import lax
from jax.experimental import pallas as pl
from jax.experimental.pallas import tpu as pltpu

BF = jnp.bfloat16
F32 = jnp.float32

GRID_W = 64
SSM_GROUPS = 16
SSM_GROUP_CH = 16
SSM_STATE = 64
SSM_WIDTH = SSM_GROUPS * SSM_GROUP_CH
SSM_FLAT = SSM_GROUPS * SSM_STATE
FNET_GROUPS = 4
FNET_GROUP_CH = 64
FNET_WIDTH = FNET_GROUPS * FNET_GROUP_CH
NA_HEADS = 8
NA_HEAD_DIM = 64
NA_WIDTH = NA_HEADS * NA_HEAD_DIM
WIN_ROWS = 8
WIN_COLS = 16
ROPE_THETA = 10000.0
LN_EPS = 1e-6
LOG2E = 1.4426950408889634
NEG_BIG = -1e30

VMEM_LIMIT_BYTES = 56 * 1024 * 1024
ROW_TILE = 512
FFN_ROWS = 1024
MERGE_ROWS = 1024
MIXIN_ROWS = 1024
FFN_CHUNK = 256
LANES = 128
SCAN_ROWS = 8
S5_BLOCK = 4
S5_ROWS = 512
S5_SUB = 256
NA_QROWS = 8
NA_KROWS = 16
NA_HQ = 4
FNET_N2 = 128
FNET_K2_BLOCK = 8


def _cparams(n_axes, flags=None):
    return pltpu.CompilerParams(dimension_semantics=("arbitrary",) * n_axes,
                                vmem_limit_bytes=VMEM_LIMIT_BYTES, flags=flags)


def _ln(x):
    mu = jnp.mean(x, axis=-1, keepdims=True)
    xc = x - mu
    var = jnp.mean(xc * xc, axis=-1, keepdims=True)
    return xc * lax.rsqrt(var + LN_EPS)


def _dot(a, b):
    return jnp.dot(a, b, preferred_element_type=F32)


def _dot_t(a, b):
    return lax.dot_general(a, b, (((1,), (1,)), ((), ())), preferred_element_type=F32)


def _layer_spec(a, *lead, mode=None):
    nl = len(lead)
    zeros = (0,) * (a.ndim - nl)
    return pl.BlockSpec((None,) * nl + tuple(a.shape[nl:]), lambda *_: tuple(lead) + zeros, pipeline_mode=mode)


def _mod_kernel(c_ref, w_ref, b_ref, o_ref):
    c = c_ref[...]
    s = (c * jax.nn.sigmoid(c)).astype(BF)
    o_ref[...] = _dot(s, w_ref[...].astype(BF)) + b_ref[...]


def _mod_call(c8, w_mod, b_mod):
    depth, d, nd = w_mod.shape
    nb = nd // d
    return pl.pallas_call(
        _mod_kernel,
        grid=(depth, nb),
        in_specs=[pl.BlockSpec((8, d), lambda l, j: (0, 0)),
                  pl.BlockSpec((None, d, d), lambda l, j: (l, 0, j)),
                  pl.BlockSpec((None, 1, d), lambda l, j: (l, 0, j))],
        out_specs=pl.BlockSpec((None, 8, d), lambda l, j: (l, 0, j)),
        out_shape=jax.ShapeDtypeStruct((depth, 8, nd), F32),
        compiler_params=_cparams(2),
        name="mod",
    )(c8, w_mod, b_mod.reshape(depth, 1, nd))


def _ffn_kernel(x_ref, t_ref, wg_ref, wu_ref, wd_ref, o_ref, act_scr, *, alpha):
    t = t_ref[...]
    tiles = [slice(r0, r0 + ROW_TILE) for r0 in range(0, x_ref.shape[0], ROW_TILE)]
    us = [(_ln(x_ref[rows, :]) * (1.0 + t[1:2]) + t[0:1]).astype(BF) for rows in tiles]
    for c0 in range(0, wg_ref.shape[1], FFN_CHUNK):
        for k, u in enumerate(us):
            a = _dot(u, wg_ref[:, c0:c0 + FFN_CHUNK])
            b = _dot(u, wu_ref[:, c0:c0 + FFN_CHUNK])
            act_scr[k, :, c0:c0 + FFN_CHUNK] = (a * jax.nn.sigmoid(a) * b).astype(BF)
    ys = [_dot(act_scr[k], wd_ref[...]) for k in range(len(tiles))]
    for y, rows in zip(ys, tiles):
        v = alpha * x_ref[rows, :] + (0.5 * t[2:3]) * y
        o_ref[rows, :] = _ln(v) * t[3:4] + t[4:5]


def _ffn_call(x, tab, wg, wu, wd, l, k, alpha):
    nb, n, d = x.shape
    tm = min(FFN_ROWS, n)
    once = pl.Buffered(1)
    return pl.pallas_call(
        functools.partial(_ffn_kernel, alpha=alpha),
        grid=(nb, n // tm),
        in_specs=[pl.BlockSpec((None, tm, d), lambda b, i: (b, i, 0)),
                  pl.BlockSpec((None, 8, d), lambda b, i: (b, 0, 0)),
                  _layer_spec(wg, l, k, mode=once), _layer_spec(wu, l, k, mode=once),
                  _layer_spec(wd, l, k, mode=once)],
        out_specs=pl.BlockSpec((None, tm, d), lambda b, i: (b, i, 0)),
        out_shape=jax.ShapeDtypeStruct(x.shape, F32),
        scratch_shapes=[pltpu.VMEM((pl.cdiv(tm, ROW_TILE), min(ROW_TILE, tm), wg.shape[-1]), BF)],
        compiler_params=_cparams(2),
        name="ffn",
    )(x, tab, wg, wu, wd)


def _rope(x, cos, sin_signed):
    n = x.shape[-1]
    half = NA_HEAD_DIM // 4
    lane = lax.broadcasted_iota(jnp.int32, x.shape, 1)
    first = (lane % (2 * half)) < half
    partner = jnp.where(first, pltpu.roll(x, n - half, axis=1), pltpu.roll(x, half, axis=1))
    return x * cos + partner * sin_signed


def _mixin_kernel(x_ref, t_ref, w_ref, cos_ref, sin_ref,
                  us_ref, u4_ref, k_ref, v_ref, f_ref, q_ref, g_ref, us_f32, *, q_scale):
    t = t_ref[...]
    c_k = SSM_WIDTH
    c_v = c_k + NA_WIDTH
    c_f = c_v + NA_WIDTH
    c_q = c_f + FNET_WIDTH
    c_g = c_q + NA_WIDTH
    tiles = [slice(r0, r0 + ROW_TILE) for r0 in range(0, x_ref.shape[0], ROW_TILE)]
    us = [(_ln(x_ref[rows, :]) * (1.0 + t[1:2]) + t[0:1]).astype(BF) for rows in tiles]
    rows4 = ROW_TILE // S5_BLOCK
    for u, rows in zip(us, tiles):
        zs = _dot(u, w_ref[:, 0:c_k])
        us_ref[rows, :] = zs.astype(BF)
        for hf in range(c_k // LANES):
            us_f32[hf, rows, :] = zs[:, hf * LANES:(hf + 1) * LANES]
        q0 = rows.start // S5_BLOCK
        for j in range(S5_BLOCK):
            for hf in range(c_k // LANES):
                c0 = j * c_k + hf * LANES
                u4_ref[q0:q0 + rows4, c0:c0 + LANES] = (
                    us_f32[hf, pl.ds(rows.start + j, rows4, stride=S5_BLOCK), :].astype(BF))
    for u, rows in zip(us, tiles):
        k_ref[rows, :] = _rope(_dot(u, w_ref[:, c_k:c_v]), cos_ref[rows, :], sin_ref[rows, :]).astype(BF)
    for u, rows in zip(us, tiles):
        v_ref[rows, :] = _dot(u, w_ref[:, c_v:c_f]).astype(BF)
    for u, rows in zip(us, tiles):
        f_ref[rows, :] = _dot(u, w_ref[:, c_f:c_q]).astype(BF)
    for u, rows in zip(us, tiles):
        q = _rope(_dot(u, w_ref[:, c_q:c_g]), cos_ref[rows, :], sin_ref[rows, :])
        q_ref[rows, :] = (q * q_scale).astype(BF)
    gw = NA_WIDTH
    for c0 in range(0, g_ref.shape[-1], gw):
        for u, rows in zip(us, tiles):
            g = _dot(u, w_ref[:, c_g + c0:c_g + c0 + gw])
            g_ref[rows, c0:c0 + gw] = jax.nn.sigmoid(g).astype(BF)


def _mixin_call(x, tab, w_in, l, cos_t, sin_t, q_scale):
    nb, n, d = x.shape
    win = w_in.shape[-1]
    ng = win - (SSM_WIDTH + 3 * NA_WIDTH + FNET_WIDTH)
    tm = min(MIXIN_ROWS, n)
    nt_pos = cos_t.shape[0] // tm

    def tok(width):
        return pl.BlockSpec((None, tm, width), lambda b, i: (b, i, 0))

    def shape(width):
        return jax.ShapeDtypeStruct((nb, n, width), BF)

    u4_spec = pl.BlockSpec((None, tm // S5_BLOCK, S5_BLOCK * SSM_WIDTH), lambda b, i: (b, i, 0))
    u4_shape = jax.ShapeDtypeStruct((nb, n // S5_BLOCK, S5_BLOCK * SSM_WIDTH), BF)
    return pl.pallas_call(
        functools.partial(_mixin_kernel, q_scale=q_scale),
        grid=(nb, n // tm),
        in_specs=[tok(d),
                  pl.BlockSpec((None, 8, d), lambda b, i: (b, 0, 0)),
                  _layer_spec(w_in, l, mode=pl.Buffered(1)),
                  pl.BlockSpec((tm, NA_WIDTH), lambda b, i: (i % nt_pos, 0)),
                  pl.BlockSpec((tm, NA_WIDTH), lambda b, i: (i % nt_pos, 0))],
        out_specs=[tok(SSM_WIDTH), u4_spec, tok(NA_WIDTH), tok(NA_WIDTH), tok(FNET_WIDTH), tok(NA_WIDTH),
                   tok(ng)],
        out_shape=[shape(SSM_WIDTH), u4_shape, shape(NA_WIDTH), shape(NA_WIDTH), shape(FNET_WIDTH),
                   shape(NA_WIDTH), shape(ng)],
        scratch_shapes=[pltpu.VMEM((SSM_WIDTH // LANES, tm, LANES), F32)],
        compiler_params=_cparams(2),
        name="mixin",
    )(x, tab, w_in, cos_t, sin_t)


def _s5_kernel(*refs):
    ins, outs, scr = refs[0:12], refs[12:16], refs[16:22]
    dirs = [dict(zip(("u", "h0", "wst", "wconv", "wout", "sc"), ins[6 * z:6 * z + 6]),
                 y=outs[2 * z], hfin=outs[2 * z + 1],
                 hbuf=scr[3 * z], hprev=scr[3 * z + 1], carry=scr[3 * z + 2], reverse=bool(z)) for z in range(2)]
    i = pl.program_id(1)
    ns = SSM_FLAT
    rows = dirs[0]["hbuf"].shape[0]
    sub = min(S5_SUB, rows)
    row_id = lax.broadcasted_iota(jnp.int32, (SCAN_ROWS, ns), 0)

    @pl.when(i == 0)
    def _():
        for d in dirs:
            d["carry"][...] = d["h0"][...]

    for d in dirs:
        order = (lambda r: list(r)[::-1]) if d["reverse"] else list
        d["order"] = order
        d["subs"] = order(range(0, rows, sub))
        d["edge"] = row_id == ((SCAN_ROWS - 1) if d["reverse"] else 0)
        d["c"] = (d["carry"][:, 0:ns], d["carry"][:, ns:2 * ns])

    for k in range(rows // sub):
        for d in dirs:
            s0 = d["subs"][k]
            d["hbuf"][s0:s0 + sub, :] = _dot(d["u"][s0:s0 + sub, :], d["wst"][...])

    def scan(d, s0):
        reverse, hbuf, hprev, sc_ref = d["reverse"], d["hbuf"], d["hprev"], d["sc"]
        cr, ci = d["c"]
        for r0 in d["order"](range(s0, s0 + sub, SCAN_ROWS)):
            xr = hbuf[r0:r0 + SCAN_ROWS, 0:ns]
            xi = hbuf[r0:r0 + SCAN_ROWS, ns:2 * ns]
            for k, s in enumerate((1, 2, 4)):
                ar = sc_ref[k, :, 0:ns]
                ai = sc_ref[k, :, ns:2 * ns]
                shift = (SCAN_ROWS - s) if reverse else s
                sr = pltpu.roll(xr, shift, axis=0)
                si = pltpu.roll(xi, shift, axis=0)
                xr, xi = xr + ar * sr - ai * si, xi + ar * si + ai * sr
            pr = sc_ref[3, :, 0:ns]
            pi = sc_ref[3, :, ns:2 * ns]
            hr = xr + pr * cr - pi * ci
            hi = xi + pr * ci + pi * cr
            one = (SCAN_ROWS - 1) if reverse else 1
            hprev[r0:r0 + SCAN_ROWS, 0:ns] = jnp.where(d["edge"], cr, pltpu.roll(hr, one, axis=0))
            hprev[r0:r0 + SCAN_ROWS, ns:2 * ns] = jnp.where(d["edge"], ci, pltpu.roll(hi, one, axis=0))
            last = 0 if reverse else SCAN_ROWS - 1
            cr = jnp.broadcast_to(hr[last:last + 1, :], hr.shape)
            ci = jnp.broadcast_to(hi[last:last + 1, :], hi.shape)
        d["c"] = (cr, ci)

    def readout(d, s0):
        y = (_dot(d["u"][s0:s0 + sub, :], d["wconv"][...])
             + _dot(d["hprev"][s0:s0 + sub, :].astype(BF), d["wout"][...]))
        d["y"][s0:s0 + sub, :] = y.astype(d["y"].dtype)

    for k in range(rows // sub):
        for d in dirs:
            scan(d, d["subs"][k])
        if k > 0:
            for d in dirs:
                readout(d, d["subs"][k - 1])
    for d in dirs:
        readout(d, d["subs"][-1])
        d["carry"][:, 0:ns] = d["c"][0]
        d["carry"][:, ns:2 * ns] = d["c"][1]
        d["hfin"][...] = d["carry"][...]


def _s5_call(u4, h0f, h0b, tables, l, rows):
    wst, wconv, wout, sc = tables
    b, n, w = u4.shape
    nc = n // rows
    ns2 = 2 * SSM_FLAT
    cmaps = (lambda bb, i: (bb, i, 0), lambda bb, i: (bb, nc - 1 - i, 0))
    state = pl.BlockSpec((None, 8, ns2), lambda bb, i: (bb, 0, 0))
    once = pl.Buffered(1)
    in_specs, args, out_specs, out_shape, scratch = [], [], [], [], []
    for z, h0 in enumerate((h0f, h0b)):
        in_specs += [pl.BlockSpec((None, rows, w), cmaps[z]), state] + [
            _layer_spec(t, l, z, mode=once) for t in (wst, wconv, wout, sc)]
        args += [u4, h0, wst, wconv, wout, sc]
        out_specs += [pl.BlockSpec((None, rows, w), cmaps[z]), state]
        out_shape += [jax.ShapeDtypeStruct((b, n, w), BF), jax.ShapeDtypeStruct((b, 8, ns2), F32)]
        scratch += [pltpu.VMEM((rows, ns2), F32), pltpu.VMEM((rows, ns2), F32), pltpu.VMEM((8, ns2), F32)]
    return pl.pallas_call(
        _s5_kernel,
        grid=(b, nc),
        in_specs=in_specs,
        out_specs=out_specs,
        out_shape=out_shape,
        scratch_shapes=scratch,
        compiler_params=_cparams(2),
        name="s5",
    )(*args)


def _s5_tables(log_dt, a_re, a_im, b_re, b_im, c_re, c_im):
    depth = log_dt.shape[0]
    g, p, c = b_re.shape[-3:]
    tb = S5_BLOCK
    hp = lax.Precision.HIGHEST
    f32 = lambda a: a.astype(F32)
    a_re, a_im, b_re, b_im, c_re, c_im = map(f32, (a_re, a_im, b_re, b_im, c_re, c_im))
    dt = jnp.exp(f32(log_dt))[..., None]
    adt_re = a_re * dt
    adt_im = a_im * dt

    def apow(k):
        kk = jnp.asarray(k, F32)[None, :, :, None, None]
        mag = jnp.exp(kk * adt_re[:, :, None])
        ang = kk * adt_im[:, :, None]
        return mag * jnp.cos(ang), mag * jnp.sin(ang)

    mag = jnp.exp(adt_re)
    ab_re = mag * jnp.cos(adt_im)
    ab_im = mag * jnp.sin(adt_im)
    den = a_re * a_re + a_im * a_im
    nr = ab_re - 1.0
    fr = (nr * a_re + ab_im * a_im) / den
    fi = (ab_im * a_re - nr * a_im) / den
    bb_re = fr[..., None] * b_re - fi[..., None] * b_im
    bb_im = fr[..., None] * b_im + fi[..., None] * b_re

    step = np.arange(tb)
    k_state = np.stack([tb - 1 - step, step])
    k_out = np.stack([step + 1, tb - step])
    k_tap = np.stack([step, step])
    k_scan = np.stack([tb * (np.arange(SCAN_ROWS) + 1)] * 2)

    def block_diag(compact, row_group, col_width):
        n = compact.shape[3]
        dst = np.arange(n * g * col_width)
        src = (dst // (g * col_width)) * col_width + dst % col_width
        copy = np.zeros((n * col_width, n * g * col_width), np.float32)
        copy[src, dst] = 1.0
        cols = jnp.einsum('dzrk,kn->dzrn', compact.reshape(compact.shape[:3] + (n * col_width,)),
                          jnp.asarray(copy), precision=hp, preferred_element_type=BF)
        keep = jnp.asarray(row_group[:, None] == ((dst // col_width) % g)[None, :])
        return jnp.where(keep, cols, jnp.zeros((), BF))

    chan_rows = (np.arange(tb * g * c) // c) % g
    state_rows = (np.arange(2 * g * p) // p) % g

    def cmul_b(pr, pi):
        return (pr[..., None] * bb_re[:, :, None] - pi[..., None] * bb_im[:, :, None],
                pr[..., None] * bb_im[:, :, None] + pi[..., None] * bb_re[:, :, None])

    wr, wi = cmul_b(*apow(k_state))
    w2 = jnp.stack([wr, wi], axis=3)
    w2 = jnp.transpose(w2, (0, 1, 2, 4, 6, 3, 5))
    wst = block_diag(w2.reshape(depth, 2, tb * g * c, 2, p), chan_rows, p)

    pr, pi = apow(k_out)
    cm_re = c_re[:, :, None] * pr[:, :, :, :, None, :] - c_im[:, :, None] * pi[:, :, :, :, None, :]
    cm_im = c_re[:, :, None] * pi[:, :, :, :, None, :] + c_im[:, :, None] * pr[:, :, :, :, None, :]
    cm = jnp.stack([cm_re, -cm_im], axis=3)
    cm = jnp.transpose(cm, (0, 1, 3, 4, 6, 2, 5))
    wout = block_diag(cm.reshape(depth, 2, 2 * g * p, tb, c), state_rows, c)

    abr, abi = cmul_b(*apow(k_tap))
    taps = (jnp.einsum('dzgcp,dztgpe->dztgec', c_re, abr, precision=hp)
            - jnp.einsum('dzgcp,dztgpe->dztgec', c_im, abi, precision=hp))
    sel = np.zeros((2, tb, tb, tb), np.float32)
    for i in range(tb):
        for j in range(tb):
            if j >= i:
                sel[0, i, j, j - i] = 1.0
            if i >= j:
                sel[1, i, j, i - j] = 1.0
    kij = jnp.einsum('dztgec,zijt->dzigejc', taps, jnp.asarray(sel), precision=hp)
    wconv = block_diag(kij.reshape(depth, 2, tb * g * c, tb, c), chan_rows, c)

    qr, qi = apow(k_scan)
    qr = qr.reshape(depth, 2, SCAN_ROWS, g * p)
    qi = qi.reshape(depth, 2, SCAN_ROWS, g * p)
    rows = np.arange(SCAN_ROWS)
    tabs = []
    for s in (1, 2, 4):
        keep = jnp.asarray(np.stack([rows >= s, rows + s <= SCAN_ROWS - 1]), F32)[None, :, :, None]
        tabs.append(jnp.concatenate([keep * qr[:, :, s - 1:s], keep * qi[:, :, s - 1:s]], axis=-1))
    carry_pow = lambda t: jnp.stack([t[:, 0], t[:, 1, ::-1]], axis=1)
    tabs.append(jnp.concatenate([carry_pow(qr), carry_pow(qi)], axis=-1))
    sc = jnp.stack(tabs, axis=2)
    return wst, wconv, wout, sc


def _fnet1_kernel(x_ref, f2_ref, tc_ref, ts_ref, o_ref):
    n2 = x_ref.shape[0]
    y = _dot(f2_ref[...], x_ref[...])
    br = y[0:n2]
    bi = y[n2:2 * n2]
    tc = tc_ref[...]
    ts = ts_ref[...]
    o_ref[0] = (br * tc + bi * ts).astype(BF)
    o_ref[1] = (bi * tc - br * ts).astype(BF)


def _fnet2_kernel(b_ref, f1_ref, cc_ref, sc_ref, o_ref, *, scale):
    kb = b_ref.shape[1]
    n1 = b_ref.shape[2]
    c = b_ref.shape[3]
    zr, zi = [], []
    for j in range(kb):
        rhs = jnp.concatenate([b_ref[0, j], b_ref[1, j]], axis=0)
        z = _dot(f1_ref[...], rhs)
        zr.append(z[0:n1])
        zi.append(z[n1:2 * n1])
    zr = jnp.concatenate(zr, axis=0).astype(BF)
    zi = jnp.concatenate(zi, axis=0).astype(BF)
    y = (_dot(zr, cc_ref[...]) + _dot(zi, sc_ref[...])) * scale
    for j in range(kb):
        o_ref[:, j * c:(j + 1) * c] = y[j * n1:(j + 1) * n1].astype(o_ref.dtype)


def _dft_tables(n):
    idx = np.arange(n)
    ang = 2.0 * np.pi * ((idx[:, None] * idx[None, :]) % n) / n
    return np.cos(ang), np.sin(ang)


def _chan_tables():
    cc, sc = _dft_tables(FNET_GROUP_CH)
    eye = np.eye(FNET_GROUPS)
    return jnp.asarray(np.kron(eye, cc), BF), jnp.asarray(np.kron(eye, sc), BF)


def _fnet_call(f):
    b, n, c = f.shape
    n2 = FNET_N2
    n1 = n // n2
    c2, s2 = _dft_tables(n2)
    f2 = jnp.asarray(np.concatenate([c2, -s2], axis=0), BF)
    k2 = np.arange(n2)[:, None]
    nn1 = np.arange(n1)[None, :]
    ang = 2.0 * np.pi * ((k2 * nn1) % n) / n
    tc = jnp.asarray(np.repeat(np.cos(ang), c, axis=1), F32)
    ts = jnp.asarray(np.repeat(np.sin(ang), c, axis=1), F32)
    c1, s1 = _dft_tables(n1)
    f1 = jnp.asarray(np.block([[c1, s1], [-s1, c1]]), BF)
    cc, sc = _chan_tables()

    wc = min(n1 * c, 4096)
    x2 = f.reshape(b, n2, n1 * c)
    st1 = pl.pallas_call(
        _fnet1_kernel,
        grid=(b, (n1 * c) // wc),
        in_specs=[pl.BlockSpec((None, n2, wc), lambda bb, j: (bb, 0, j)),
                  pl.BlockSpec((2 * n2, n2), lambda bb, j: (0, 0)),
                  pl.BlockSpec((n2, wc), lambda bb, j: (0, j)),
                  pl.BlockSpec((n2, wc), lambda bb, j: (0, j))],
        out_specs=pl.BlockSpec((None, 2, n2, wc), lambda bb, j: (bb, 0, 0, j)),
        out_shape=jax.ShapeDtypeStruct((b, 2, n2, n1 * c), BF),
        compiler_params=_cparams(2),
        name="fnet1",
    )(x2, f2, tc, ts)
    st1 = st1.reshape(b, 2, n2, n1, c)
    kb = FNET_K2_BLOCK
    y = pl.pallas_call(
        functools.partial(_fnet2_kernel, scale=1.0 / math.sqrt(n * FNET_GROUP_CH)),
        grid=(b, n2 // kb),
        in_specs=[pl.BlockSpec((None, 2, kb, n1, c), lambda bb, j: (bb, 0, j, 0, 0)),
                  pl.BlockSpec((2 * n1, 2 * n1), lambda bb, j: (0, 0)),
                  pl.BlockSpec((c, c), lambda bb, j: (0, 0)),
                  pl.BlockSpec((c, c), lambda bb, j: (0, 0))],
        out_specs=pl.BlockSpec((None, n1, kb * c), lambda bb, j: (bb, 0, j)),
        out_shape=jax.ShapeDtypeStruct((b, n1, n2 * c), BF),
        compiler_params=_cparams(2),
        name="fnet2",
    )(st1, f1, cc, sc)
    return y.reshape(b, n, c)


def _fnet_dense_kernel(x_ref, cl_ref, sl_ref, cc_ref, sc_ref, o_ref, *, scale):
    x = x_ref[...]
    xr = _dot(x, cc_ref[...]).astype(BF)
    xi = _dot(x, sc_ref[...]).astype(BF)
    o_ref[...] = ((_dot(cl_ref[...], xr) - _dot(sl_ref[...], xi)) * scale).astype(o_ref.dtype)


def _fnet_dense_call(f):
    b, n, c = f.shape
    cl, sl = _dft_tables(n)
    cc, sc = _chan_tables()
    full = lambda bb: (0, 0)
    return pl.pallas_call(
        functools.partial(_fnet_dense_kernel, scale=1.0 / math.sqrt(n * FNET_GROUP_CH)),
        grid=(b,),
        in_specs=[pl.BlockSpec((None, n, c), lambda bb: (bb, 0, 0)),
                  pl.BlockSpec((n, n), full), pl.BlockSpec((n, n), full),
                  pl.BlockSpec((c, c), full), pl.BlockSpec((c, c), full)],
        out_specs=pl.BlockSpec((None, n, c), lambda bb: (bb, 0, 0)),
        out_shape=jax.ShapeDtypeStruct((b, n, c), BF),
        compiler_params=_cparams(1),
        name="fnet_ctx",
    )(f, jnp.asarray(cl, BF), jnp.asarray(sl, BF), cc, sc)


def _head_masks(width):
    lane = lax.broadcasted_iota(jnp.int32, (1, width), 1)
    return [(lane // NA_HEAD_DIM) == h for h in range(width // NA_HEAD_DIM)]


def _attend(q, k_parts, v_parts, ranges, bias_of, o_ref):
    zero = jnp.zeros((), BF)
    hw = NA_HQ * NA_HEAD_DIM
    masks = _head_masks(hw)
    m_rows = q.shape[0]
    rb = min(GRID_W, m_rows)
    live = [j for j, (a, b) in enumerate(ranges) if b > a]
    k_parts = [k_parts[j] for j in live]
    v_parts = [v_parts[j] for j in live]
    ranges = [ranges[j] for j in live]
    quad = lambda x, h: x[:, (h // NA_HQ) * hw:(h // NA_HQ + 1) * hw]

    def scores(h):
        qh = jnp.where(masks[h % NA_HQ], quad(q, h), zero)
        s_parts = []
        for j, (kp, (a, b)) in enumerate(zip(k_parts, ranges)):
            s = _dot_t(qh[a:b], quad(kp, h))
            bias = bias_of(h, live[j], a, b)
            s_parts.append(s if bias is None else s + bias)
        return s_parts

    def lane_tiles(x):
        return [x[:, c:c + LANES] for c in range(0, x.shape[1], LANES)]

    def softmax(s_parts):
        p_rows = [[] for _ in s_parts]
        dens = []
        for r0 in range(0, m_rows, rb):
            pieces = [(j, s[r0 - a:r0 - a + rb]) for j, (s, (a, b)) in enumerate(zip(s_parts, ranges))
                      if a <= r0 < b]
            m = functools.reduce(jnp.maximum, [t for _, pc in pieces for t in lane_tiles(pc)])
            m = m.max(axis=-1, keepdims=True)
            acc = None
            for j, pc in pieces:
                p = jnp.exp2(pc - m)
                acc = functools.reduce(jnp.add, lane_tiles(p), acc) if acc is not None else \
                    functools.reduce(jnp.add, lane_tiles(p))
                p_rows[j].append(p.astype(BF))
            dens.append(acc.sum(axis=-1, keepdims=True))
        return [jnp.concatenate(r, axis=0) for r in p_rows], jnp.concatenate(dens, axis=0)

    def values(h, p_parts, den):
        out = None
        for p, vp, (a, b) in zip(p_parts, v_parts, ranges):
            o = _dot(p, quad(vp, h))
            pad = [jnp.zeros((n, o.shape[1]), F32) for n in (a, m_rows - b)]
            o = jnp.concatenate([x for x in (pad[0], o, pad[1]) if x.shape[0]], axis=0)
            out = o if out is None else out + o
        return out / den

    n = q.shape[-1] // NA_HEAD_DIM
    s_of, p_of, outs = {}, {}, [None] * (n // NA_HQ)
    for step in range(n + 2):
        if step < n:
            s_of[step] = scores(step)
        if 0 <= step - 1 < n:
            p_of[step - 1] = softmax(s_of.pop(step - 1))
        if 0 <= step - 2 < n:
            h = step - 2
            o = values(h, *p_of.pop(h))
            prev = outs[h // NA_HQ]
            outs[h // NA_HQ] = o if prev is None else jnp.where(masks[h % NA_HQ], o, prev)
    o_ref[...] = jnp.concatenate(outs, axis=-1).astype(o_ref.dtype)


def _natten_kernel(q_ref, k0, k1, k2, k3, v0, v1, v2, v3, kc_ref, vc_ref, bias_ref, *rest, ranges, pair_index):
    o_ref = rest[-1]
    k_parts = [r[...] for r in (k0, k1, k2, k3)] + [kc_ref[...]]
    v_parts = [r[...] for r in (v0, v1, v2, v3)] + [vc_ref[...]]
    pairs_per_part = k0.shape[0] // (2 * GRID_W)

    def bias_of(h, j, a, b):
        if j >= 4:
            return None
        rows = [jnp.concatenate([bias_ref[h, pair_index[r][j * pairs_per_part + p]]
                                 for p in range(pairs_per_part)], axis=1)
                for r in range(a // GRID_W, b // GRID_W)]
        return jnp.concatenate(rows, axis=0).astype(F32)

    _attend(q_ref[...], k_parts, v_parts, ranges, bias_of, o_ref)


def _na_windows(rows):
    nj = rows // NA_QROWS
    out = []
    for j in (0, min(1, nj - 1), nj - 1):
        rq = NA_QROWS * j + np.arange(NA_QROWS)
        ws = int(np.clip(NA_QROWS * j - (NA_KROWS - NA_QROWS) // 2, 0, rows - NA_KROWS))
        rk = ws + np.arange(NA_KROWS)
        r0 = np.clip(rq - WIN_ROWS // 2, 0, rows - WIN_ROWS)
        valid_r = (rk[None, :] >= r0[:, None]) & (rk[None, :] < r0[:, None] + WIN_ROWS)
        dr = rk[None, :] - rq[:, None] + (WIN_ROWS - 1)
        out.append((valid_r, dr))
    return out


def _natten_call(q, k, v, kc, vc, bias, l, variant, prev=None):
    table, index = bias
    b, n, w = q.shape
    lc = kc.shape[1]
    qt = NA_QROWS * GRID_W
    blk = (NA_KROWS // 4) * GRID_W
    nj = n // qt
    nblk = n // blk
    ratio = qt // blk
    j0, count = ((0, 1), (1, nj - 2), (nj - 1, 1))[variant]

    valid_r = _na_windows(n // GRID_W)[variant][0]
    ranges = []
    for t in range(4):
        seen = np.nonzero(valid_r[:, t * (NA_KROWS // 4):(t + 1) * (NA_KROWS // 4)].any(axis=1))[0]
        ranges.append((int(seen.min()) * GRID_W, (int(seen.max()) + 1) * GRID_W) if seen.size else (0, 0))
    ranges.append((0, qt))

    def kv_spec(m):
        def imap(j, bb):
            start = jnp.clip(ratio * (j + j0) - 1, 0, nblk - 4)
            return (bb, start + m, 0)
        return pl.BlockSpec((None, blk, w), imap)

    in_specs = ([pl.BlockSpec((None, qt, w), lambda j, bb: (bb, j + j0, 0))]
                + [kv_spec(m) for m in range(4)] + [kv_spec(m) for m in range(4)]
                + [pl.BlockSpec((None, lc, w), lambda j, bb: (bb, 0, 0)),
                   pl.BlockSpec((None, lc, w), lambda j, bb: (bb, 0, 0)),
                   _layer_spec(table, l, mode=pl.Buffered(1))])
    args = [q, k, k, k, k, v, v, v, v, kc, vc, table]
    aliases = {}
    if prev is not None:
        in_specs.append(pl.BlockSpec(memory_space=pl.ANY))
        aliases = {len(args): 0}
        args.append(prev)
    return pl.pallas_call(
        functools.partial(_natten_kernel, ranges=tuple(ranges), pair_index=index[variant]),
        grid=(count, b),
        in_specs=in_specs,
        out_specs=pl.BlockSpec((None, qt, w), lambda j, bb: (bb, j + j0, 0)),
        out_shape=jax.ShapeDtypeStruct((b, n, w), BF),
        input_output_aliases=aliases,
        compiler_params=_cparams(2),
        name="natten",
    )(*args)


def _ctx_atten_kernel(q_ref, kc_ref, vc_ref, o_ref):
    _attend(q_ref[...], [kc_ref[...]], [vc_ref[...]], [(0, q_ref.shape[0])], lambda h, j, a, b: None, o_ref)


def _ctx_atten_call(qc, kc, vc):
    b, lc, w = qc.shape
    spec = pl.BlockSpec((None, lc, w), lambda bb: (bb, 0, 0))
    return pl.pallas_call(
        _ctx_atten_kernel,
        grid=(b,),
        in_specs=[spec, spec, spec],
        out_specs=spec,
        out_shape=jax.ShapeDtypeStruct((b, lc, w), BF),
        compiler_params=_cparams(1),
        name="ctx_atten",
    )(qc, kc, vc)


def _na_bias(rpb, rows):
    w = GRID_W
    n_dr = 2 * WIN_ROWS - 1
    n_dc = 2 * WIN_COLS - 1
    cq = np.arange(w)
    cs = np.clip(cq - WIN_COLS // 2, 0, w - WIN_COLS)
    ck = np.arange(w)
    valid_c = (ck[None, :] >= cs[:, None]) & (ck[None, :] < cs[:, None] + WIN_COLS)
    dc = ck[None, :] - cq[:, None] + (WIN_COLS - 1)
    pick_c = (dc[:, :, None] == np.arange(n_dc)) & valid_c[:, :, None]
    tiles = jnp.einsum('lhab,cdb->lhacd', rpb.astype(F32) * LOG2E, jnp.asarray(pick_c, F32),
                       precision=lax.Precision.HIGHEST)
    tiles = tiles + jnp.asarray(np.where(valid_c, 0.0, NEG_BIG), F32)
    tiles = jnp.concatenate([tiles, jnp.full_like(tiles[:, :, :1], NEG_BIG)], axis=2).astype(BF)
    pairs, index = {}, []
    for valid_r, dr in _na_windows(rows):
        dr = np.where(valid_r, dr, n_dr)
        index.append(tuple(tuple(pairs.setdefault((int(dr[a, 2 * p]), int(dr[a, 2 * p + 1])), len(pairs))
                                 for p in range(NA_KROWS // 2)) for a in range(NA_QROWS)))
    table = jnp.stack([jnp.concatenate([tiles[:, :, i0], tiles[:, :, i1]], axis=-1) for i0, i1 in pairs], axis=2)
    return table, index


def _rope_tables(n):
    nf = NA_HEAD_DIM // 4
    t = jnp.arange(n, dtype=jnp.int32)
    pos = jnp.stack([t // GRID_W, t % GRID_W], axis=-1).astype(F32)
    inv_freq = ROPE_THETA ** (-jnp.arange(nf, dtype=F32) / nf)
    ang = pos[:, :, None] * inv_freq
    cos = jnp.cos(ang)
    sin = jnp.sin(ang)
    cos_h = jnp.concatenate([cos, cos], axis=-1).reshape(n, NA_HEAD_DIM)
    sin_h = jnp.concatenate([-sin, sin], axis=-1).reshape(n, NA_HEAD_DIM)
    return jnp.tile(cos_h, (1, NA_HEADS)), jnp.tile(sin_h, (1, NA_HEADS))


def _merge_kernel(ysf_ref, ysb_ref, us_ref, yf_ref, yn_ref, g_ref, h_ref, t_ref,
                  sd_ref, wglu_ref, bglu_ref, wbs_ref, wbf_ref, wbn_ref, wo_ref, o_ref, ys_f32, *, alpha):
    d = h_ref.shape[-1]
    t = t_ref[...]
    halves = SSM_WIDTH // LANES
    tiles = [slice(r0, r0 + ROW_TILE) for r0 in range(0, h_ref.shape[0], ROW_TILE)]
    rows4 = ROW_TILE // S5_BLOCK
    ys = []
    for rows in tiles:
        q0 = rows.start // S5_BLOCK
        y4 = ysf_ref[q0:q0 + rows4, :].astype(F32) + ysb_ref[q0:q0 + rows4, :].astype(F32)
        for j in range(S5_BLOCK):
            for hf in range(halves):
                c0 = j * SSM_WIDTH + hf * LANES
                ys_f32[hf, pl.ds(rows.start + j, rows4, stride=S5_BLOCK), :] = y4[:, c0:c0 + LANES]
        y = jnp.concatenate([ys_f32[hf, rows, :] for hf in range(halves)], axis=-1)
        ys.append(jax.nn.gelu(y + sd_ref[...] * us_ref[rows, :].astype(F32)))
    glu = [_dot(y.astype(BF), wglu_ref[...]) for y in ys]
    ys = [(y * jax.nn.sigmoid(z + bglu_ref[...])).astype(BF) for y, z in zip(ys, glu)]
    br = [(_dot(y, wbs_ref[...]), _dot(yf_ref[rows, :], wbf_ref[...]), _dot(yn_ref[rows, :], wbn_ref[...]))
          for y, rows in zip(ys, tiles)]
    mix = [(g_ref[rows, 0:d].astype(F32) * bs + g_ref[rows, d:2 * d].astype(F32) * bf
            + g_ref[rows, 2 * d:3 * d].astype(F32) * bn).astype(BF) for (bs, bf, bn), rows in zip(br, tiles)]
    proj = [_dot(m, wo_ref[...]) for m in mix]
    for y, rows in zip(proj, tiles):
        v = alpha * h_ref[rows, :] + t[2:3] * y
        o_ref[rows, :] = _ln(v) * t[3:4] + t[4:5]


def _merge_call(ysf4, ysb4, us, yf, yn, g, h, tab, s5w, mats, l, alpha):
    nb, n, d = h.shape
    tm = min(MERGE_ROWS, n)

    def tok(a):
        return pl.BlockSpec((None, tm, a.shape[-1]), lambda b, i: (b, i, 0))

    def tok4(a):
        return pl.BlockSpec((None, tm // S5_BLOCK, a.shape[-1]), lambda b, i: (b, i, 0))

    toks = (us, yf, yn, g, h)
    return pl.pallas_call(
        functools.partial(_merge_kernel, alpha=alpha),
        grid=(nb, n // tm),
        in_specs=[tok4(ysf4), tok4(ysb4)] + [tok(a) for a in toks]
                 + [pl.BlockSpec((None, 8, d), lambda b, i: (b, 0, 0))]
                 + [_layer_spec(a, l) for a in tuple(s5w) + tuple(mats)],
        out_specs=tok(h),
        out_shape=jax.ShapeDtypeStruct(h.shape, F32),
        scratch_shapes=[pltpu.VMEM((SSM_WIDTH // LANES, tm, LANES), F32)],
        compiler_params=_cparams(2),
        name="merge",
    )(ysf4, ysb4, *toks, tab, *s5w, *mats)


def _table(mod, d, chunks, extra):
    rows = [mod[:, i * d:(i + 1) * d] for i in chunks]
    rows += [jnp.broadcast_to(e[None, :], (8, d)) for e in extra]
    rows += [jnp.zeros((8, d), F32)] * (8 - len(rows))
    return jnp.stack(rows, axis=1)


def kernel(x, c, ctx, c_ctx, w_mod, b_mod, ln_g, ln_b, ffn_w_gate, ffn_w_up, ffn_w_down, w_in, ssm_log_dt, ssm_a_re, ssm_a_im, ssm_b_re, ssm_b_im, ssm_c_re, ssm_c_im, ssm_d, ssm_w_glu, ssm_b_glu, na_rpb, w_br_ssm, w_br_fnet, w_br_na, w_out):
    bsz, n, d = x.shape
    lc = ctx.shape[1]
    depth = w_mod.shape[0]
    rows = n // GRID_W
    assert bsz + 1 <= 8 and rows % NA_QROWS == 0 and rows >= NA_KROWS and n % FNET_N2 == 0
    alpha = (2 * depth) ** 0.25
    q_scale = NA_HEAD_DIM ** -0.5 * LOG2E

    c8 = jnp.concatenate([c, c_ctx[None, :], jnp.zeros((8 - bsz - 1, d), F32)], axis=0)
    mod_all = _mod_call(c8, w_mod, b_mod)

    wg = ffn_w_gate.astype(BF)
    wu = ffn_w_up.astype(BF)
    wd = ffn_w_down.astype(BF)
    win = w_in.astype(BF)
    mats = (w_br_ssm.astype(BF), w_br_fnet.astype(BF), w_br_na.astype(BF), w_out.astype(BF))
    s5w = (ssm_d[:, None, :], ssm_w_glu.astype(BF), ssm_b_glu[:, None, :])
    s5_tabs = _s5_tables(ssm_log_dt, ssm_a_re, ssm_a_im, ssm_b_re, ssm_b_im, ssm_c_re, ssm_c_im)
    na_bias = _na_bias(na_rpb, rows)
    cos_t, sin_t = _rope_tables(n)
    cos_c = jnp.ones((bsz * lc, NA_WIDTH), F32)
    sin_c = jnp.zeros((bsz * lc, NA_WIDTH), F32)
    zero_state = jnp.zeros((bsz, 8, 2 * SSM_FLAT), F32)

    h = x
    hc = ctx.reshape(1, bsz * lc, d)
    for l in range(depth):
        last = l == depth - 1
        mod = mod_all[l]

        tab = _table(mod, d, (0, 1, 2), (ln_g[l, 0], ln_b[l, 0]))
        h = _ffn_call(h, tab[:bsz], wg, wu, wd, l, 0, alpha)
        hc = _ffn_call(hc, tab[bsz:bsz + 1], wg, wu, wd, l, 0, alpha)

        tab = _table(mod, d, (3, 4, 5), (ln_g[l, 1], ln_b[l, 1]))
        us, u4, k, v, f, q, g = _mixin_call(h, tab[:bsz], win, l, cos_t, sin_t, q_scale)
        usc, u4c, kc, vc, fc, qc, gc = _mixin_call(hc, tab[bsz:bsz + 1], win, l, cos_c, sin_c, q_scale)
        u4c, kc, vc, fc, qc = [a.reshape(bsz, a.shape[1] // bsz, a.shape[-1]) for a in (u4c, kc, vc, fc, qc)]

        ysfc, hcf, ysbc, hcb = _s5_call(u4c, zero_state, zero_state, s5_tabs, l, u4c.shape[1])
        ysf, _, ysb, _ = _s5_call(u4, hcf, hcb, s5_tabs, l, min(S5_ROWS, u4.shape[1]))

        yf = _fnet_call(f)
        yn = None
        for variant in ((1, 0, 2) if rows > 2 * NA_QROWS else (0, 2)):
            yn = _natten_call(q, k, v, kc, vc, na_bias, l, variant, yn)

        h = _merge_call(ysf, ysb, us, yf, yn, g, h, tab[:bsz], s5w, mats, l, alpha)

        tab3 = _table(mod, d, (6, 7, 8), (ln_g[l, 2], ln_b[l, 2]))
        h = _ffn_call(h, tab3[:bsz], wg, wu, wd, l, 1, alpha)

        if not last:
            yfc = _fnet_dense_call(fc)
            ync = _ctx_atten_call(qc, kc, vc)
            flat = lambda a: a.reshape(1, a.shape[0] * a.shape[1], a.shape[-1])
            hc = _merge_call(flat(ysfc), flat(ysbc), usc, flat(yfc), flat(ync), gc, hc, tab[bsz:bsz + 1],
                             s5w, mats, l, alpha)
            hc = _ffn_call(hc, tab3[bsz:bsz + 1], wg, wu, wd, l, 1, alpha)
    return h
```

```python
import functools
import math

import jax
import jax.numpy as jnp
import numpy as np
from jax import lax
from jax.experimental import pallas as pl
from jax.experimental.pallas import tpu as pltpu

BF = jnp.bfloat16
F32 = jnp.float32

GRID_W = 64
SSM_GROUPS = 16
SSM_GROUP_CH = 16
SSM_STATE = 64
SSM_WIDTH = SSM_GROUPS * SSM_GROUP_CH
SSM_FLAT = SSM_GROUPS * SSM_STATE
FNET_GROUPS = 4
FNET_GROUP_CH = 64
FNET_WIDTH = FNET_GROUPS * FNET_GROUP_CH
NA_HEADS = 8
NA_HEAD_DIM = 64
NA_WIDTH = NA_HEADS * NA_HEAD_DIM
WIN_ROWS = 8
WIN_COLS = 16
ROPE_THETA = 10000.0
LN_EPS = 1e-6
LOG2E = 1.4426950408889634
NEG_BIG = -1e30

VMEM_LIMIT_BYTES = 56 * 1024 * 1024
ROW_TILE = 512
FFN_ROWS = 1024
MERGE_ROWS = 1024
MIXIN_ROWS = 1024
FFN_CHUNK = 256
LANES = 128
SCAN_ROWS = 8
S5_BLOCK = 4
S5_ROWS = 512
S5_SUB = 256
NA_QROWS = 8
NA_KROWS = 16
NA_HQ = 4
FNET_N2 = 128
FNET_K2_BLOCK = 8


def _cparams(n_axes, flags=None):
    return pltpu.CompilerParams(dimension_semantics=("arbitrary",) * n_axes,
                                vmem_limit_bytes=VMEM_LIMIT_BYTES, flags=flags)


def _ln(x):
    mu = jnp.mean(x, axis=-1, keepdims=True)
    xc = x - mu
    var = jnp.mean(xc * xc, axis=-1, keepdims=True)
    return xc * lax.rsqrt(var + LN_EPS)


def _dot(a, b):
    return jnp.dot(a, b, preferred_element_type=F32)


def _dot_t(a, b):
    return lax.dot_general(a, b, (((1,), (1,)), ((), ())), preferred_element_type=F32)


def _layer_spec(a, *lead, mode=None):
    nl = len(lead)
    zeros = (0,) * (a.ndim - nl)
    return pl.BlockSpec((None,) * nl + tuple(a.shape[nl:]), lambda *_: tuple(lead) + zeros, pipeline_mode=mode)


def _mod_kernel(c_ref, w_ref, b_ref, o_ref):
    c = c_ref[...]
    s = (c * jax.nn.sigmoid(c)).astype(BF)
    o_ref[...] = _dot(s, w_ref[...].astype(BF)) + b_ref[...]


def _mod_call(c8, w_mod, b_mod):
    depth, d, nd = w_mod.shape
    nb = nd // d
    return pl.pallas_call(
        _mod_kernel,
        grid=(depth, nb),
        in_specs=[pl.BlockSpec((8, d), lambda l, j: (0, 0)),
                  pl.BlockSpec((None, d, d), lambda l, j: (l, 0, j)),
                  pl.BlockSpec((None, 1, d), lambda l, j: (l, 0, j))],
        out_specs=pl.BlockSpec((None, 8, d), lambda l, j: (l, 0, j)),
        out_shape=jax.ShapeDtypeStruct((depth, 8, nd), F32),
        compiler_params=_cparams(2),
        name="mod",
    )(c8, w_mod, b_mod.reshape(depth, 1, nd))


def _ffn_kernel(x_ref, t_ref, wg_ref, wu_ref, wd_ref, o_ref, act_scr, *, alpha):
    t = t_ref[...]
    tiles = [slice(r0, r0 + ROW_TILE) for r0 in range(0, x_ref.shape[0], ROW_TILE)]
    us = [(_ln(x_ref[rows, :]) * (1.0 + t[1:2]) + t[0:1]).astype(BF) for rows in tiles]
    for c0 in range(0, wg_ref.shape[1], FFN_CHUNK):
        for k, u in enumerate(us):
            a = _dot(u, wg_ref[:, c0:c0 + FFN_CHUNK])
            b = _dot(u, wu_ref[:, c0:c0 + FFN_CHUNK])
            act_scr[k, :, c0:c0 + FFN_CHUNK] = (a * jax.nn.sigmoid(a) * b).astype(BF)
    ys = [_dot(act_scr[k], wd_ref[...]) for k in range(len(tiles))]
    for y, rows in zip(ys, tiles):
        v = alpha * x_ref[rows, :] + (0.5 * t[2:3]) * y
        o_ref[rows, :] = _ln(v) * t[3:4] + t[4:5]


def _ffn_call(x, tab, wg, wu, wd, l, k, alpha):
    nb, n, d = x.shape
    tm = min(FFN_ROWS, n)
    once = pl.Buffered(1)
    return pl.pallas_call(
        functools.partial(_ffn_kernel, alpha=alpha),
        grid=(nb, n // tm),
        in_specs=[pl.BlockSpec((None, tm, d), lambda b, i: (b, i, 0)),
                  pl.BlockSpec((None, 8, d), lambda b, i: (b, 0, 0)),
                  _layer_spec(wg, l, k, mode=once), _layer_spec(wu, l, k, mode=once),
                  _layer_spec(wd, l, k, mode=once)],
        out_specs=pl.BlockSpec((None, tm, d), lambda b, i: (b, i, 0)),
        out_shape=jax.ShapeDtypeStruct(x.shape, F32),
        scratch_shapes=[pltpu.VMEM((pl.cdiv(tm, ROW_TILE), min(ROW_TILE, tm), wg.shape[-1]), BF)],
        compiler_params=_cparams(2),
        name="ffn",
    )(x, tab, wg, wu, wd)


def _rope(x, cos, sin_signed):
    n = x.shape[-1]
    half = NA_HEAD_DIM // 4
    lane = lax.broadcasted_iota(jnp.int32, x.shape, 1)
    first = (lane % (2 * half)) < half
    partner = jnp.where(first, pltpu.roll(x, n - half, axis=1), pltpu.roll(x, half, axis=1))
    reps = n // cos.shape[-1]
    cos = jnp.concatenate([cos] * reps, axis=-1)
    sin_signed = jnp.concatenate([sin_signed] * reps, axis=-1)
    return x * cos + partner * sin_signed


def _mixin_kernel(x_ref, t_ref, w_ref, cos_ref, sin_ref,
                  us_ref, u4_ref, k_ref, v_ref, f_ref, q_ref, g_ref, us_f32, *, q_scale):
    t = t_ref[...]
    c_k = SSM_WIDTH
    c_v = c_k + NA_WIDTH
    c_f = c_v + NA_WIDTH
    c_q = c_f + FNET_WIDTH
    c_g = c_q + NA_WIDTH
    tiles = [slice(r0, r0 + ROW_TILE) for r0 in range(0, x_ref.shape[0], ROW_TILE)]
    us = [(_ln(x_ref[rows, :]) * (1.0 + t[1:2]) + t[0:1]).astype(BF) for rows in tiles]
    rows4 = ROW_TILE // S5_BLOCK
    for u, rows in zip(us, tiles):
        zs = _dot(u, w_ref[:, 0:c_k])
        us_ref[rows, :] = zs.astype(BF)
        for hf in range(c_k // LANES):
            us_f32[hf, rows, :] = zs[:, hf * LANES:(hf + 1) * LANES]
        q0 = rows.start // S5_BLOCK
        for j in range(S5_BLOCK):
            for hf in range(c_k // LANES):
                c0 = j * c_k + hf * LANES
                u4_ref[q0:q0 + rows4, c0:c0 + LANES] = (
                    us_f32[hf, pl.ds(rows.start + j, rows4, stride=S5_BLOCK), :].astype(BF))
    for u, rows in zip(us, tiles):
        k_ref[rows, :] = _rope(_dot(u, w_ref[:, c_k:c_v]), cos_ref[rows, :], sin_ref[rows, :]).astype(BF)
    for u, rows in zip(us, tiles):
        v_ref[rows, :] = _dot(u, w_ref[:, c_v:c_f]).astype(BF)
    for u, rows in zip(us, tiles):
        f_ref[rows, :] = _dot(u, w_ref[:, c_f:c_q]).astype(BF)
    for u, rows in zip(us, tiles):
        q = _rope(_dot(u, w_ref[:, c_q:c_g]), cos_ref[rows, :], sin_ref[rows, :])
        q_ref[rows, :] = (q * q_scale).astype(BF)
    gw = NA_WIDTH
    for c0 in range(0, g_ref.shape[-1], gw):
        for u, rows in zip(us, tiles):
            g = _dot(u, w_ref[:, c_g + c0:c_g + c0 + gw])
            g_ref[rows, c0:c0 + gw] = jax.nn.sigmoid(g).astype(BF)


def _mixin_call(x, tab, w_in, l, cos_t, sin_t, q_scale):
    nb, n, d = x.shape
    win = w_in.shape[-1]
    ng = win - (SSM_WIDTH + 3 * NA_WIDTH + FNET_WIDTH)
    tm = min(MIXIN_ROWS, n)
    nt_pos = cos_t.shape[0] // tm

    def tok(width):
        return pl.BlockSpec((None, tm, width), lambda b, i: (b, i, 0))

    def shape(width):
        return jax.ShapeDtypeStruct((nb, n, width), BF)

    u4_spec = pl.BlockSpec((None, tm // S5_BLOCK, S5_BLOCK * SSM_WIDTH), lambda b, i: (b, i, 0))
    u4_shape = jax.ShapeDtypeStruct((nb, n // S5_BLOCK, S5_BLOCK * SSM_WIDTH), BF)
    return pl.pallas_call(
        functools.partial(_mixin_kernel, q_scale=q_scale),
        grid=(nb, n // tm),
        in_specs=[tok(d),
                  pl.BlockSpec((None, 8, d), lambda b, i: (b, 0, 0)),
                  _layer_spec(w_in, l, mode=pl.Buffered(1)),
                  pl.BlockSpec((tm, NA_HEAD_DIM), lambda b, i: (i % nt_pos, 0)),
                  pl.BlockSpec((tm, NA_HEAD_DIM), lambda b, i: (i % nt_pos, 0))],
        out_specs=[tok(SSM_WIDTH), u4_spec, tok(NA_WIDTH), tok(NA_WIDTH), tok(FNET_WIDTH), tok(NA_WIDTH),
                   tok(ng)],
        out_shape=[shape(SSM_WIDTH), u4_shape, shape(NA_WIDTH), shape(NA_WIDTH), shape(FNET_WIDTH),
                   shape(NA_WIDTH), shape(ng)],
        scratch_shapes=[pltpu.VMEM((SSM_WIDTH // LANES, tm, LANES), F32)],
        compiler_params=_cparams(2),
        name="mixin",
    )(x, tab, w_in, cos_t, sin_t)


def _s5_kernel(*refs):
    ins, outs, scr = refs[0:12], refs[12:16], refs[16:22]
    dirs = [dict(zip(("u", "h0", "wst", "wconv", "wout", "sc"), ins[6 * z:6 * z + 6]),
                 y=outs[2 * z], hfin=outs[2 * z + 1],
                 hbuf=scr[3 * z], hprev=scr[3 * z + 1], carry=scr[3 * z + 2], reverse=bool(z)) for z in range(2)]
    i = pl.program_id(1)
    ns = SSM_FLAT
    rows = dirs[0]["hbuf"].shape[0]
    sub = min(S5_SUB, rows)
    row_id = lax.broadcasted_iota(jnp.int32, (SCAN_ROWS, ns), 0)

    @pl.when(i == 0)
    def _():
        for d in dirs:
            d["carry"][...] = d["h0"][...]

    for d in dirs:
        order = (lambda r: list(r)[::-1]) if d["reverse"] else list
        d["order"] = order
        d["subs"] = order(range(0, rows, sub))
        d["edge"] = row_id == ((SCAN_ROWS - 1) if d["reverse"] else 0)
        d["c"] = (d["carry"][:, 0:ns], d["carry"][:, ns:2 * ns])

    for k in range(rows // sub):
        for d in dirs:
            s0 = d["subs"][k]
            d["hbuf"][s0:s0 + sub, :] = _dot(d["u"][s0:s0 + sub, :], d["wst"][...])

    def scan(d, s0):
        reverse, hbuf, hprev, sc_ref = d["reverse"], d["hbuf"], d["hprev"], d["sc"]
        cr, ci = d["c"]
        for r0 in d["order"](range(s0, s0 + sub, SCAN_ROWS)):
            xr = hbuf[r0:r0 + SCAN_ROWS, 0:ns]
            xi = hbuf[r0:r0 + SCAN_ROWS, ns:2 * ns]
            for k, s in enumerate((1, 2, 4)):
                ar = sc_ref[k, :, 0:ns]
                ai = sc_ref[k, :, ns:2 * ns]
                shift = (SCAN_ROWS - s) if reverse else s
                sr = pltpu.roll(xr, shift, axis=0)
                si = pltpu.roll(xi, shift, axis=0)
                xr, xi = xr + ar * sr - ai * si, xi + ar * si + ai * sr
            pr = sc_ref[3, :, 0:ns]
            pi = sc_ref[3, :, ns:2 * ns]
            hr = xr + pr * cr - pi * ci
            hi = xi + pr * ci + pi * cr
            one = (SCAN_ROWS - 1) if reverse else 1
            hprev[r0:r0 + SCAN_ROWS, 0:ns] = jnp.where(d["edge"], cr, pltpu.roll(hr, one, axis=0))
            hprev[r0:r0 + SCAN_ROWS, ns:2 * ns] = jnp.where(d["edge"], ci, pltpu.roll(hi, one, axis=0))
            last = 0 if reverse else SCAN_ROWS - 1
            cr = jnp.broadcast_to(hr[last:last + 1, :], hr.shape)
            ci = jnp.broadcast_to(hi[last:last + 1, :], hi.shape)
        d["c"] = (cr, ci)

    def readout(d, s0):
        y = (_dot(d["u"][s0:s0 + sub, :], d["wconv"][...])
             + _dot(d["hprev"][s0:s0 + sub, :].astype(BF), d["wout"][...]))
        d["y"][s0:s0 + sub, :] = y.astype(d["y"].dtype)

    for k in range(rows // sub):
        for d in dirs:
            scan(d, d["subs"][k])
        if k > 0:
            for d in dirs:
                readout(d, d["subs"][k - 1])
    for d in dirs:
        readout(d, d["subs"][-1])
        d["carry"][:, 0:ns] = d["c"][0]
        d["carry"][:, ns:2 * ns] = d["c"][1]
        d["hfin"][...] = d["carry"][...]


def _s5_call(u4, h0f, h0b, tables, l, rows):
    wst, wconv, wout, sc = tables
    b, n, w = u4.shape
    nc = n // rows
    ns2 = 2 * SSM_FLAT
    cmaps = (lambda bb, i: (bb, i, 0), lambda bb, i: (bb, nc - 1 - i, 0))
    state = pl.BlockSpec((None, 8, ns2), lambda bb, i: (bb, 0, 0))
    once = pl.Buffered(1)
    in_specs, args, out_specs, out_shape, scratch = [], [], [], [], []
    for z, h0 in enumerate((h0f, h0b)):
        in_specs += [pl.BlockSpec((None, rows, w), cmaps[z]), state] + [
            _layer_spec(t, l, z, mode=once) for t in (wst, wconv, wout, sc)]
        args += [u4, h0, wst, wconv, wout, sc]
        out_specs += [pl.BlockSpec((None, rows, w), cmaps[z]), state]
        out_shape += [jax.ShapeDtypeStruct((b, n, w), BF), jax.ShapeDtypeStruct((b, 8, ns2), F32)]
        scratch += [pltpu.VMEM((rows, ns2), F32), pltpu.VMEM((rows, ns2), F32), pltpu.VMEM((8, ns2), F32)]
    return pl.pallas_call(
        _s5_kernel,
        grid=(b, nc),
        in_specs=in_specs,
        out_specs=out_specs,
        out_shape=out_shape,
        scratch_shapes=scratch,
        compiler_params=_cparams(2),
        name="s5",
    )(*args)


def _s5_tables(log_dt, a_re, a_im, b_re, b_im, c_re, c_im):
    depth = log_dt.shape[0]
    g, p, c = b_re.shape[-3:]
    tb = S5_BLOCK
    hp = lax.Precision.HIGHEST
    f32 = lambda a: a.astype(F32)
    a_re, a_im, b_re, b_im, c_re, c_im = map(f32, (a_re, a_im, b_re, b_im, c_re, c_im))
    dt = jnp.exp(f32(log_dt))[..., None]
    adt_re = a_re * dt
    adt_im = a_im * dt

    def apow(k):
        kk = jnp.asarray(k, F32)[None, :, :, None, None]
        mag = jnp.exp(kk * adt_re[:, :, None])
        ang = kk * adt_im[:, :, None]
        return mag * jnp.cos(ang), mag * jnp.sin(ang)

    mag = jnp.exp(adt_re)
    ab_re = mag * jnp.cos(adt_im)
    ab_im = mag * jnp.sin(adt_im)
    den = a_re * a_re + a_im * a_im
    nr = ab_re - 1.0
    fr = (nr * a_re + ab_im * a_im) / den
    fi = (ab_im * a_re - nr * a_im) / den
    bb_re = fr[..., None] * b_re - fi[..., None] * b_im
    bb_im = fr[..., None] * b_im + fi[..., None] * b_re

    step = np.arange(tb)
    k_state = np.stack([tb - 1 - step, step])
    k_out = np.stack([step + 1, tb - step])
    k_tap = np.stack([step, step])
    k_scan = np.stack([tb * (np.arange(SCAN_ROWS) + 1)] * 2)

    def block_diag(compact, row_group, col_width):
        n = compact.shape[3]
        dst = np.arange(n * g * col_width)
        src = (dst // (g * col_width)) * col_width + dst % col_width
        copy = np.zeros((n * col_width, n * g * col_width), np.float32)
        copy[src, dst] = 1.0
        cols = jnp.einsum('dzrk,kn->dzrn', compact.reshape(compact.shape[:3] + (n * col_width,)),
                          jnp.asarray(copy), precision=hp, preferred_element_type=BF)
        keep = jnp.asarray(row_group[:, None] == ((dst // col_width) % g)[None, :])
        return jnp.where(keep, cols, jnp.zeros((), BF))

    chan_rows = (np.arange(tb * g * c) // c) % g
    state_rows = (np.arange(2 * g * p) // p) % g

    def cmul_b(pr, pi):
        return (pr[..., None] * bb_re[:, :, None] - pi[..., None] * bb_im[:, :, None],
                pr[..., None] * bb_im[:, :, None] + pi[..., None] * bb_re[:, :, None])

    wr, wi = cmul_b(*apow(k_state))
    w2 = jnp.stack([wr, wi], axis=3)
    w2 = jnp.transpose(w2, (0, 1, 2, 4, 6, 3, 5))
    wst = block_diag(w2.reshape(depth, 2, tb * g * c, 2, p), chan_rows, p)

    pr, pi = apow(k_out)
    cm_re = c_re[:, :, None] * pr[:, :, :, :, None, :] - c_im[:, :, None] * pi[:, :, :, :, None, :]
    cm_im = c_re[:, :, None] * pi[:, :, :, :, None, :] + c_im[:, :, None] * pr[:, :, :, :, None, :]
    cm = jnp.stack([cm_re, -cm_im], axis=3)
    cm = jnp.transpose(cm, (0, 1, 3, 4, 6, 2, 5))
    wout = block_diag(cm.reshape(depth, 2, 2 * g * p, tb, c), state_rows, c)

    abr, abi = cmul_b(*apow(k_tap))
    taps = (jnp.einsum('dzgcp,dztgpe->dztgec', c_re, abr, precision=hp)
            - jnp.einsum('dzgcp,dztgpe->dztgec', c_im, abi, precision=hp))
    sel = np.zeros((2, tb, tb, tb), np.float32)
    for i in range(tb):
        for j in range(tb):
            if j >= i:
                sel[0, i, j, j - i] = 1.0
            if i >= j:
                sel[1, i, j, i - j] = 1.0
    kij = jnp.einsum('dztgec,zijt->dzigejc', taps, jnp.asarray(sel), precision=hp)
    wconv = block_diag(kij.reshape(depth, 2, tb * g * c, tb, c), chan_rows, c)

    qr, qi = apow(k_scan)
    qr = qr.reshape(depth, 2, SCAN_ROWS, g * p)
    qi = qi.reshape(depth, 2, SCAN_ROWS, g * p)
    rows = np.arange(SCAN_ROWS)
    tabs = []
    for s in (1, 2, 4):
        keep = jnp.asarray(np.stack([rows >= s, rows + s <= SCAN_ROWS - 1]), F32)[None, :, :, None]
        tabs.append(jnp.concatenate([keep * qr[:, :, s - 1:s], keep * qi[:, :, s - 1:s]], axis=-1))
    carry_pow = lambda t: jnp.stack([t[:, 0], t[:, 1, ::-1]], axis=1)
    tabs.append(jnp.concatenate([carry_pow(qr), carry_pow(qi)], axis=-1))
    sc = jnp.stack(tabs, axis=2)
    return wst, wconv, wout, sc


def _fnet1_kernel(x_ref, f2_ref, tc_ref, ts_ref, o_ref):
    n2 = x_ref.shape[0]
    y = _dot(f2_ref[...], x_ref[...])
    br = y[0:n2]
    bi = y[n2:2 * n2]
    tc = tc_ref[...]
    ts = ts_ref[...]
    o_ref[0] = (br * tc + bi * ts).astype(BF)
    o_ref[1] = (bi * tc - br * ts).astype(BF)


def _fnet2_kernel(b_ref, f1_ref, cc_ref, sc_ref, o_ref, *, scale):
    kb = b_ref.shape[1]
    n1 = b_ref.shape[2]
    c = b_ref.shape[3]
    zr, zi = [], []
    for j in range(kb):
        rhs = jnp.concatenate([b_ref[0, j], b_ref[1, j]], axis=0)
        z = _dot(f1_ref[...], rhs)
        zr.append(z[0:n1])
        zi.append(z[n1:2 * n1])
    zr = jnp.concatenate(zr, axis=0).astype(BF)
    zi = jnp.concatenate(zi, axis=0).astype(BF)
    y = (_dot(zr, cc_ref[...]) + _dot(zi, sc_ref[...])) * scale
    for j in range(kb):
        o_ref[:, j * c:(j + 1) * c] = y[j * n1:(j + 1) * n1].astype(o_ref.dtype)


def _dft_tables(n):
    idx = np.arange(n)
    ang = 2.0 * np.pi * ((idx[:, None] * idx[None, :]) % n) / n
    return np.cos(ang), np.sin(ang)


def _chan_tables():
    cc, sc = _dft_tables(FNET_GROUP_CH)
    eye = np.eye(FNET_GROUPS)
    return jnp.asarray(np.kron(eye, cc), BF), jnp.asarray(np.kron(eye, sc), BF)


def _fnet_call(f):
    b, n, c = f.shape
    n2 = FNET_N2
    n1 = n // n2
    c2, s2 = _dft_tables(n2)
    f2 = jnp.asarray(np.concatenate([c2, -s2], axis=0), BF)
    k2 = np.arange(n2)[:, None]
    nn1 = np.arange(n1)[None, :]
    ang = 2.0 * np.pi * ((k2 * nn1) % n) / n
    tc = jnp.asarray(np.repeat(np.cos(ang), c, axis=1), F32)
    ts = jnp.asarray(np.repeat(np.sin(ang), c, axis=1), F32)
    c1, s1 = _dft_tables(n1)
    f1 = jnp.asarray(np.block([[c1, s1], [-s1, c1]]), BF)
    cc, sc = _chan_tables()

    wc = min(n1 * c, 4096)
    x2 = f.reshape(b, n2, n1 * c)
    st1 = pl.pallas_call(
        _fnet1_kernel,
        grid=(b, (n1 * c) // wc),
        in_specs=[pl.BlockSpec((None, n2, wc), lambda bb, j: (bb, 0, j)),
                  pl.BlockSpec((2 * n2, n2), lambda bb, j: (0, 0)),
                  pl.BlockSpec((n2, wc), lambda bb, j: (0, j)),
                  pl.BlockSpec((n2, wc), lambda bb, j: (0, j))],
        out_specs=pl.BlockSpec((None, 2, n2, wc), lambda bb, j: (bb, 0, 0, j)),
        out_shape=jax.ShapeDtypeStruct((b, 2, n2, n1 * c), BF),
        compiler_params=_cparams(2),
        name="fnet1",
    )(x2, f2, tc, ts)
    st1 = st1.reshape(b, 2, n2, n1, c)
    kb = FNET_K2_BLOCK
    y = pl.pallas_call(
        functools.partial(_fnet2_kernel, scale=1.0 / math.sqrt(n * FNET_GROUP_CH)),
        grid=(b, n2 // kb),
        in_specs=[pl.BlockSpec((None, 2, kb, n1, c), lambda bb, j: (bb, 0, j, 0, 0)),
                  pl.BlockSpec((2 * n1, 2 * n1), lambda bb, j: (0, 0)),
                  pl.BlockSpec((c, c), lambda bb, j: (0, 0)),
                  pl.BlockSpec((c, c), lambda bb, j: (0, 0))],
        out_specs=pl.BlockSpec((None, n1, kb * c), lambda bb, j: (bb, 0, j)),
        out_shape=jax.ShapeDtypeStruct((b, n1, n2 * c), BF),
        compiler_params=_cparams(2),
        name="fnet2",
    )(st1, f1, cc, sc)
    return y.reshape(b, n, c)


def _fnet_dense_kernel(x_ref, cl_ref, sl_ref, cc_ref, sc_ref, o_ref, *, scale):
    x = x_ref[...]
    xr = _dot(x, cc_ref[...]).astype(BF)
    xi = _dot(x, sc_ref[...]).astype(BF)
    o_ref[...] = ((_dot(cl_ref[...], xr) - _dot(sl_ref[...], xi)) * scale).astype(o_ref.dtype)


def _fnet_dense_call(f):
    b, n, c = f.shape
    cl, sl = _dft_tables(n)
    cc, sc = _chan_tables()
    full = lambda bb: (0, 0)
    return pl.pallas_call(
        functools.partial(_fnet_dense_kernel, scale=1.0 / math.sqrt(n * FNET_GROUP_CH)),
        grid=(b,),
        in_specs=[pl.BlockSpec((None, n, c), lambda bb: (bb, 0, 0)),
                  pl.BlockSpec((n, n), full), pl.BlockSpec((n, n), full),
                  pl.BlockSpec((c, c), full), pl.BlockSpec((c, c), full)],
        out_specs=pl.BlockSpec((None, n, c), lambda bb: (bb, 0, 0)),
        out_shape=jax.ShapeDtypeStruct((b, n, c), BF),
        compiler_params=_cparams(1),
        name="fnet_ctx",
    )(f, jnp.asarray(cl, BF), jnp.asarray(sl, BF), cc, sc)


def _head_masks(width):
    lane = lax.broadcasted_iota(jnp.int32, (1, width), 1)
    return [(lane // NA_HEAD_DIM) == h for h in range(width // NA_HEAD_DIM)]


def _attend(q, k_parts, v_parts, ranges, bias_of, o_ref):
    zero = jnp.zeros((), BF)
    hw = NA_HQ * NA_HEAD_DIM
    masks = _head_masks(hw)
    m_rows = q.shape[0]
    rb = min(GRID_W, m_rows)
    live = [j for j, (a, b) in enumerate(ranges) if b > a]
    k_parts = [k_parts[j] for j in live]
    v_parts = [v_parts[j] for j in live]
    ranges = [ranges[j] for j in live]
    quad = lambda x, h: x[:, (h // NA_HQ) * hw:(h // NA_HQ + 1) * hw]

    def scores(h):
        qh = jnp.where(masks[h % NA_HQ], quad(q, h), zero)
        s_parts = []
        for j, (kp, (a, b)) in enumerate(zip(k_parts, ranges)):
            s = _dot_t(qh[a:b], quad(kp, h))
            bias = bias_of(h, live[j], a, b)
            s_parts.append(s if bias is None else s + bias)
        return s_parts

    def lane_tiles(x):
        return [x[:, c:c + LANES] for c in range(0, x.shape[1], LANES)]

    def softmax(s_parts):
        p_rows = [[] for _ in s_parts]
        dens = []
        for r0 in range(0, m_rows, rb):
            pieces = [(j, s[r0 - a:r0 - a + rb]) for j, (s, (a, b)) in enumerate(zip(s_parts, ranges))
                      if a <= r0 < b]
            m = functools.reduce(jnp.maximum, [t for _, pc in pieces for t in lane_tiles(pc)])
            m = m.max(axis=-1, keepdims=True)
            acc = None
            for j, pc in pieces:
                p = jnp.exp2(pc - m)
                acc = functools.reduce(jnp.add, lane_tiles(p), acc) if acc is not None else \
                    functools.reduce(jnp.add, lane_tiles(p))
                p_rows[j].append(p.astype(BF))
            dens.append(acc.sum(axis=-1, keepdims=True))
        return [jnp.concatenate(r, axis=0) for r in p_rows], jnp.concatenate(dens, axis=0)

    def values(h, p_parts, den):
        out = None
        for p, vp, (a, b) in zip(p_parts, v_parts, ranges):
            o = _dot(p, quad(vp, h))
            pad = [jnp.zeros((n, o.shape[1]), F32) for n in (a, m_rows - b)]
            o = jnp.concatenate([x for x in (pad[0], o, pad[1]) if x.shape[0]], axis=0)
            out = o if out is None else out + o
        return out / den

    n = q.shape[-1] // NA_HEAD_DIM
    s_of, p_of, outs = {}, {}, [None] * (n // NA_HQ)
    for step in range(n + 2):
        if step < n:
            s_of[step] = scores(step)
        if 0 <= step - 1 < n:
            p_of[step - 1] = softmax(s_of.pop(step - 1))
        if 0 <= step - 2 < n:
            h = step - 2
            o = values(h, *p_of.pop(h))
            prev = outs[h // NA_HQ]
            outs[h // NA_HQ] = o if prev is None else jnp.where(masks[h % NA_HQ], o, prev)
    o_ref[...] = jnp.concatenate(outs, axis=-1).astype(o_ref.dtype)


def _natten_kernel(q_ref, k0, k1, k2, k3, v0, v1, v2, v3, kc_ref, vc_ref, bias_ref, *rest, ranges, pair_index):
    o_ref = rest[-1]
    k_parts = [r[...] for r in (k0, k1, k2, k3)] + [kc_ref[...]]
    v_parts = [r[...] for r in (v0, v1, v2, v3)] + [vc_ref[...]]
    pairs_per_part = k0.shape[0] // (2 * GRID_W)

    def bias_of(h, j, a, b):
        if j >= 4:
            return None
        rows = [jnp.concatenate([bias_ref[h, pair_index[r][j * pairs_per_part + p]]
                                 for p in range(pairs_per_part)], axis=1)
                for r in range(a // GRID_W, b // GRID_W)]
        return jnp.concatenate(rows, axis=0).astype(F32)

    _attend(q_ref[...], k_parts, v_parts, ranges, bias_of, o_ref)


def _na_windows(rows):
    nj = rows // NA_QROWS
    out = []
    for j in (0, min(1, nj - 1), nj - 1):
        rq = NA_QROWS * j + np.arange(NA_QROWS)
        ws = int(np.clip(NA_QROWS * j - (NA_KROWS - NA_QROWS) // 2, 0, rows - NA_KROWS))
        rk = ws + np.arange(NA_KROWS)
        r0 = np.clip(rq - WIN_ROWS // 2, 0, rows - WIN_ROWS)
        valid_r = (rk[None, :] >= r0[:, None]) & (rk[None, :] < r0[:, None] + WIN_ROWS)
        dr = rk[None, :] - rq[:, None] + (WIN_ROWS - 1)
        out.append((valid_r, dr))
    return out


def _natten_call(q, k, v, kc, vc, bias, l, variant, prev=None):
    table, index = bias
    b, n, w = q.shape
    lc = kc.shape[1]
    qt = NA_QROWS * GRID_W
    blk = (NA_KROWS // 4) * GRID_W
    nj = n // qt
    nblk = n // blk
    ratio = qt // blk
    j0, count = ((0, 1), (1, nj - 2), (nj - 1, 1))[variant]

    valid_r = _na_windows(n // GRID_W)[variant][0]
    ranges = []
    for t in range(4):
        seen = np.nonzero(valid_r[:, t * (NA_KROWS // 4):(t + 1) * (NA_KROWS // 4)].any(axis=1))[0]
        ranges.append((int(seen.min()) * GRID_W, (int(seen.max()) + 1) * GRID_W) if seen.size else (0, 0))
    ranges.append((0, qt))

    def kv_spec(m):
        def imap(j, bb):
            start = jnp.clip(ratio * (j + j0) - 1, 0, nblk - 4)
            return (bb, start + m, 0)
        return pl.BlockSpec((None, blk, w), imap)

    in_specs = ([pl.BlockSpec((None, qt, w), lambda j, bb: (bb, j + j0, 0))]
                + [kv_spec(m) for m in range(4)] + [kv_spec(m) for m in range(4)]
                + [pl.BlockSpec((None, lc, w), lambda j, bb: (bb, 0, 0)),
                   pl.BlockSpec((None, lc, w), lambda j, bb: (bb, 0, 0)),
                   _layer_spec(table, l, mode=pl.Buffered(1))])
    args = [q, k, k, k, k, v, v, v, v, kc, vc, table]
    aliases = {}
    if prev is not None:
        in_specs.append(pl.BlockSpec(memory_space=pl.ANY))
        aliases = {len(args): 0}
        args.append(prev)
    return pl.pallas_call(
        functools.partial(_natten_kernel, ranges=tuple(ranges), pair_index=index[variant]),
        grid=(count, b),
        in_specs=in_specs,
        out_specs=pl.BlockSpec((None, qt, w), lambda j, bb: (bb, j + j0, 0)),
        out_shape=jax.ShapeDtypeStruct((b, n, w), BF),
        input_output_aliases=aliases,
        compiler_params=_cparams(2),
        name="natten",
    )(*args)


def _ctx_atten_kernel(q_ref, kc_ref, vc_ref, o_ref):
    _attend(q_ref[...], [kc_ref[...]], [vc_ref[...]], [(0, q_ref.shape[0])], lambda h, j, a, b: None, o_ref)


def _ctx_atten_call(qc, kc, vc):
    b, lc, w = qc.shape
    spec = pl.BlockSpec((None, lc, w), lambda bb: (bb, 0, 0))
    return pl.pallas_call(
        _ctx_atten_kernel,
        grid=(b,),
        in_specs=[spec, spec, spec],
        out_specs=spec,
        out_shape=jax.ShapeDtypeStruct((b, lc, w), BF),
        compiler_params=_cparams(1),
        name="ctx_atten",
    )(qc, kc, vc)


def _na_bias(rpb, rows):
    w = GRID_W
    n_dr = 2 * WIN_ROWS - 1
    n_dc = 2 * WIN_COLS - 1
    cq = np.arange(w)
    cs = np.clip(cq - WIN_COLS // 2, 0, w - WIN_COLS)
    ck = np.arange(w)
    valid_c = (ck[None, :] >= cs[:, None]) & (ck[None, :] < cs[:, None] + WIN_COLS)
    dc = ck[None, :] - cq[:, None] + (WIN_COLS - 1)
    pick_c = (dc[:, :, None] == np.arange(n_dc)) & valid_c[:, :, None]
    tiles = jnp.einsum('lhab,cdb->lhacd', rpb.astype(F32) * LOG2E, jnp.asarray(pick_c, F32),
                       precision=lax.Precision.HIGHEST)
    tiles = tiles + jnp.asarray(np.where(valid_c, 0.0, NEG_BIG), F32)
    tiles = jnp.concatenate([tiles, jnp.full_like(tiles[:, :, :1], NEG_BIG)], axis=2).astype(BF)
    pairs, index = {}, []
    for valid_r, dr in _na_windows(rows):
        dr = np.where(valid_r, dr, n_dr)
        index.append(tuple(tuple(pairs.setdefault((int(dr[a, 2 * p]), int(dr[a, 2 * p + 1])), len(pairs))
                                 for p in range(NA_KROWS // 2)) for a in range(NA_QROWS)))
    table = jnp.stack([jnp.concatenate([tiles[:, :, i0], tiles[:, :, i1]], axis=-1) for i0, i1 in pairs], axis=2)
    return table, index


def _rope_tables(n):
    nf = NA_HEAD_DIM // 4
    t = jnp.arange(n, dtype=jnp.int32)
    pos = jnp.stack([t // GRID_W, t % GRID_W], axis=-1).astype(F32)
    inv_freq = ROPE_THETA ** (-jnp.arange(nf, dtype=F32) / nf)
    ang = pos[:, :, None] * inv_freq
    cos = jnp.cos(ang)
    sin = jnp.sin(ang)
    cos_h = jnp.concatenate([cos, cos], axis=-1).reshape(n, NA_HEAD_DIM)
    sin_h = jnp.concatenate([-sin, sin], axis=-1).reshape(n, NA_HEAD_DIM)
    return cos_h, sin_h


def _merge_kernel(ysf_ref, ysb_ref, us_ref, yf_ref, yn_ref, g_ref, h_ref, t_ref,
                  sd_ref, wglu_ref, bglu_ref, wbs_ref, wbf_ref, wbn_ref, wo_ref, o_ref, ys_f32, *, alpha):
    d = h_ref.shape[-1]
    t = t_ref[...]
    halves = SSM_WIDTH // LANES
    tiles = [slice(r0, r0 + ROW_TILE) for r0 in range(0, h_ref.shape[0], ROW_TILE)]
    rows4 = ROW_TILE // S5_BLOCK
    ys = []
    for rows in tiles:
        q0 = rows.start // S5_BLOCK
        y4 = ysf_ref[q0:q0 + rows4, :].astype(F32) + ysb_ref[q0:q0 + rows4, :].astype(F32)
        for j in range(S5_BLOCK):
            for hf in range(halves):
                c0 = j * SSM_WIDTH + hf * LANES
                ys_f32[hf, pl.ds(rows.start + j, rows4, stride=S5_BLOCK), :] = y4[:, c0:c0 + LANES]
        y = jnp.concatenate([ys_f32[hf, rows, :] for hf in range(halves)], axis=-1)
        ys.append(jax.nn.gelu(y + sd_ref[...] * us_ref[rows, :].astype(F32)))
    glu = [_dot(y.astype(BF), wglu_ref[...]) for y in ys]
    ys = [(y * jax.nn.sigmoid(z + bglu_ref[...])).astype(BF) for y, z in zip(ys, glu)]
    br = [(_dot(y, wbs_ref[...]), _dot(yf_ref[rows, :], wbf_ref[...]), _dot(yn_ref[rows, :], wbn_ref[...]))
          for y, rows in zip(ys, tiles)]
    mix = [(g_ref[rows, 0:d].astype(F32) * bs + g_ref[rows, d:2 * d].astype(F32) * bf
            + g_ref[rows, 2 * d:3 * d].astype(F32) * bn).astype(BF) for (bs, bf, bn), rows in zip(br, tiles)]
    proj = [_dot(m, wo_ref[...]) for m in mix]
    for y, rows in zip(proj, tiles):
        v = alpha * h_ref[rows, :] + t[2:3] * y
        o_ref[rows, :] = _ln(v) * t[3:4] + t[4:5]


def _merge_call(ysf4, ysb4, us, yf, yn, g, h, tab, s5w, mats, l, alpha):
    nb, n, d = h.shape
    tm = min(MERGE_ROWS, n)

    def tok(a):
        return pl.BlockSpec((None, tm, a.shape[-1]), lambda b, i: (b, i, 0))

    def tok4(a):
        return pl.BlockSpec((None, tm // S5_BLOCK, a.shape[-1]), lambda b, i: (b, i, 0))

    toks = (us, yf, yn, g, h)
    return pl.pallas_call(
        functools.partial(_merge_kernel, alpha=alpha),
        grid=(nb, n // tm),
        in_specs=[tok4(ysf4), tok4(ysb4)] + [tok(a) for a in toks]
                 + [pl.BlockSpec((None, 8, d), lambda b, i: (b, 0, 0))]
                 + [_layer_spec(a, l) for a in tuple(s5w) + tuple(mats)],
        out_specs=tok(h),
        out_shape=jax.ShapeDtypeStruct(h.shape, F32),
        scratch_shapes=[pltpu.VMEM((SSM_WIDTH // LANES, tm, LANES), F32)],
        compiler_params=_cparams(2),
        name="merge",
    )(ysf4, ysb4, *toks, tab, *s5w, *mats)


def _table(mod, d, chunks, extra):
    rows = [mod[:, i * d:(i + 1) * d] for i in chunks]
    rows += [jnp.broadcast_to(e[None, :], (8, d)) for e in extra]
    rows += [jnp.zeros((8, d), F32)] * (8 - len(rows))
    return jnp.stack(rows, axis=1)


def kernel(x, c, ctx, c_ctx, w_mod, b_mod, ln_g, ln_b, ffn_w_gate, ffn_w_up, ffn_w_down, w_in, ssm_log_dt, ssm_a_re, ssm_a_im, ssm_b_re, ssm_b_im, ssm_c_re, ssm_c_im, ssm_d, ssm_w_glu, ssm_b_glu, na_rpb, w_br_ssm, w_br_fnet, w_br_na, w_out):
    bsz, n, d = x.shape
    lc = ctx.shape[1]
    depth = w_mod.shape[0]
    rows = n // GRID_W
    assert bsz + 1 <= 8 and rows % NA_QROWS == 0 and rows >= NA_KROWS and n % FNET_N2 == 0
    alpha = (2 * depth) ** 0.25
    q_scale = NA_HEAD_DIM ** -0.5 * LOG2E

    c8 = jnp.concatenate([c, c_ctx[None, :], jnp.zeros((8 - bsz - 1, d), F32)], axis=0)
    mod_all = _mod_call(c8, w_mod, b_mod)

    wg = ffn_w_gate.astype(BF)
    wu = ffn_w_up.astype(BF)
    wd = ffn_w_down.astype(BF)
    win = w_in.astype(BF)
    mats = (w_br_ssm.astype(BF), w_br_fnet.astype(BF), w_br_na.astype(BF), w_out.astype(BF))
    s5w = (ssm_d[:, None, :], ssm_w_glu.astype(BF), ssm_b_glu[:, None, :])
    s5_tabs = _s5_tables(ssm_log_dt, ssm_a_re, ssm_a_im, ssm_b_re, ssm_b_im, ssm_c_re, ssm_c_im)
    na_bias = _na_bias(na_rpb, rows)
    cos_t, sin_t = _rope_tables(n)
    cos_c = jnp.ones((bsz * lc, NA_HEAD_DIM), F32)
    sin_c = jnp.zeros((bsz * lc, NA_HEAD_DIM), F32)
    zero_state = jnp.zeros((bsz, 8, 2 * SSM_FLAT), F32)

    h = x
    hc = ctx.reshape(1, bsz * lc, d)
    for l in range(depth):
        last = l == depth - 1
        mod = mod_all[l]

        tab = _table(mod, d, (0, 1, 2), (ln_g[l, 0], ln_b[l, 0]))
        h = _ffn_call(h, tab[:bsz], wg, wu, wd, l, 0, alpha)
        hc = _ffn_call(hc, tab[bsz:bsz + 1], wg, wu, wd, l, 0, alpha)

        tab = _table(mod, d, (3, 4, 5), (ln_g[l, 1], ln_b[l, 1]))
        us, u4, k, v, f, q, g = _mixin_call(h, tab[:bsz], win, l, cos_t, sin_t, q_scale)
        usc, u4c, kc, vc, fc, qc, gc = _mixin_call(hc, tab[bsz:bsz + 1], win, l, cos_c, sin_c, q_scale)
        u4c, kc, vc, fc, qc = [a.reshape(bsz, a.shape[1] // bsz, a.shape[-1]) for a in (u4c, kc, vc, fc, qc)]

        ysfc, hcf, ysbc, hcb = _s5_call(u4c, zero_state, zero_state, s5_tabs, l, u4c.shape[1])
        ysf, _, ysb, _ = _s5_call(u4, hcf, hcb, s5_tabs, l, min(S5_ROWS, u4.shape[1]))

        yf = _fnet_call(f)
        yn = None
        for variant in ((1, 0, 2) if rows > 2 * NA_QROWS else (0, 2)):
            yn = _natten_call(q, k, v, kc, vc, na_bias, l, variant, yn)

        h = _merge_call(ysf, ysb, us, yf, yn, g, h, tab[:bsz], s5w, mats, l, alpha)

        tab3 = _table(mod, d, (6, 7, 8), (ln_g[l, 2], ln_b[l, 2]))
        h = _ffn_call(h, tab3[:bsz], wg, wu, wd, l, 1, alpha)

        if not last:
            yfc = _fnet_dense_call(fc)
            ync = _ctx_atten_call(qc, kc, vc)
            flat = lambda a: a.reshape(1, a.shape[0] * a.shape[1], a.shape[-1])
            hc = _merge_call(flat(ysfc), flat(ysbc), usc, flat(yfc), flat(ync), gc, hc, tab[bsz:bsz + 1],
                             s5w, mats, l, alpha)
            hc = _ffn_call(hc, tab3[bsz:bsz + 1], wg, wu, wd, l, 1, alpha)
    return h
```

```python
import functools
import math

import jax
import jax.numpy as jnp
import numpy as np
from jax import lax
from jax.experimental import pallas as pl
from jax.experimental.pallas import tpu as pltpu

BF = jnp.bfloat16
F32 = jnp.float32

GRID_W = 64
SSM_GROUPS = 16
SSM_GROUP_CH = 16
SSM_STATE = 64
SSM_WIDTH = SSM_GROUPS * SSM_GROUP_CH
SSM_FLAT = SSM_GROUPS * SSM_STATE
FNET_GROUPS = 4
FNET_GROUP_CH = 64
FNET_WIDTH = FNET_GROUPS * FNET_GROUP_CH
NA_HEADS = 8
NA_HEAD_DIM = 64
NA_WIDTH = NA_HEADS * NA_HEAD_DIM
WIN_ROWS = 8
WIN_COLS = 16
ROPE_THETA = 10000.0
LN_EPS = 1e-6
LOG2E = 1.4426950408889634
NEG_BIG = -1e30

VMEM_LIMIT_BYTES = 56 * 1024 * 1024
SUB_TILE = 256
FFN_ROWS = 1024
MERGE_ROWS = 1024
MIXIN_ROWS = 1024
FFN_CHUNK = 256
LANES = 128
SCAN_ROWS = 8
S5_BLOCK = 4
S5_ROWS = 512
S5_SUB = 256
NA_QROWS = 8
NA_KROWS = 16
NA_HQ = 4
FNET_N2 = 128
FNET_K2_BLOCK = 8


def _cparams(n_axes, flags=None):
    return pltpu.CompilerParams(dimension_semantics=("arbitrary",) * n_axes,
                                vmem_limit_bytes=VMEM_LIMIT_BYTES, flags=flags)


def _ln(x):
    mu = jnp.mean(x, axis=-1, keepdims=True)
    xc = x - mu
    var = jnp.mean(xc * xc, axis=-1, keepdims=True)
    return xc * lax.rsqrt(var + LN_EPS)


def _dot(a, b):
    return jnp.dot(a, b, preferred_element_type=F32)


def _dot_t(a, b):
    return lax.dot_general(a, b, (((1,), (1,)), ((), ())), preferred_element_type=F32)


def _layer_spec(a, *lead, mode=None):
    nl = len(lead)
    zeros = (0,) * (a.ndim - nl)
    return pl.BlockSpec((None,) * nl + tuple(a.shape[nl:]), lambda *_: tuple(lead) + zeros, pipeline_mode=mode)


def _mod_kernel(c_ref, w_ref, b_ref, o_ref):
    c = c_ref[...]
    s = (c * jax.nn.sigmoid(c)).astype(BF)
    o_ref[...] = _dot(s, w_ref[...].astype(BF)) + b_ref[...]


def _mod_call(c8, w_mod, b_mod):
    depth, d, nd = w_mod.shape
    nb = nd // d
    return pl.pallas_call(
        _mod_kernel,
        grid=(depth, nb),
        in_specs=[pl.BlockSpec((8, d), lambda l, j: (0, 0)),
                  pl.BlockSpec((None, d, d), lambda l, j: (l, 0, j)),
                  pl.BlockSpec((None, 1, d), lambda l, j: (l, 0, j))],
        out_specs=pl.BlockSpec((None, 8, d), lambda l, j: (l, 0, j)),
        out_shape=jax.ShapeDtypeStruct((depth, 8, nd), F32),
        compiler_params=_cparams(2),
        name="mod",
    )(c8, w_mod, b_mod.reshape(depth, 1, nd))


def _ffn_kernel(x_ref, t_ref, wg_ref, wu_ref, wd_ref, o_ref, act_scr, *, alpha):
    t = t_ref[...]
    tiles = [slice(r0, r0 + SUB_TILE) for r0 in range(0, x_ref.shape[0], SUB_TILE)]
    us = [(_ln(x_ref[rows, :]) * (1.0 + t[1:2]) + t[0:1]).astype(BF) for rows in tiles]
    for c0 in range(0, wg_ref.shape[1], FFN_CHUNK):
        for k, u in enumerate(us):
            a = _dot(u, wg_ref[:, c0:c0 + FFN_CHUNK])
            b = _dot(u, wu_ref[:, c0:c0 + FFN_CHUNK])
            act_scr[k, :, c0:c0 + FFN_CHUNK] = (a * jax.nn.sigmoid(a) * b).astype(BF)
    ys = [_dot(act_scr[k], wd_ref[...]) for k in range(len(tiles))]
    for y, rows in zip(ys, tiles):
        v = alpha * x_ref[rows, :] + (0.5 * t[2:3]) * y
        o_ref[rows, :] = _ln(v) * t[3:4] + t[4:5]


def _ffn_call(x, tab, wg, wu, wd, l, k, alpha):
    nb, n, d = x.shape
    tm = min(FFN_ROWS, n)
    once = pl.Buffered(1)
    return pl.pallas_call(
        functools.partial(_ffn_kernel, alpha=alpha),
        grid=(nb, n // tm),
        in_specs=[pl.BlockSpec((None, tm, d), lambda b, i: (b, i, 0)),
                  pl.BlockSpec((None, 8, d), lambda b, i: (b, 0, 0)),
                  _layer_spec(wg, l, k, mode=once), _layer_spec(wu, l, k, mode=once),
                  _layer_spec(wd, l, k, mode=once)],
        out_specs=pl.BlockSpec((None, tm, d), lambda b, i: (b, i, 0)),
        out_shape=jax.ShapeDtypeStruct(x.shape, F32),
        scratch_shapes=[pltpu.VMEM((pl.cdiv(tm, SUB_TILE), min(SUB_TILE, tm), wg.shape[-1]), BF)],
        compiler_params=_cparams(2),
        name="ffn",
    )(x, tab, wg, wu, wd)


def _rope(x, cos, sin_signed):
    n = x.shape[-1]
    half = NA_HEAD_DIM // 4
    lane = lax.broadcasted_iota(jnp.int32, x.shape, 1)
    first = (lane % (2 * half)) < half
    partner = jnp.where(first, pltpu.roll(x, n - half, axis=1), pltpu.roll(x, half, axis=1))
    reps = n // cos.shape[-1]
    cos = jnp.concatenate([cos] * reps, axis=-1)
    sin_signed = jnp.concatenate([sin_signed] * reps, axis=-1)
    return x * cos + partner * sin_signed


def _mixin_kernel(x_ref, t_ref, w_ref, cos_ref, sin_ref,
                  us_ref, u4_ref, k_ref, v_ref, f_ref, q_ref, g_ref, us_f32, *, q_scale):
    t = t_ref[...]
    c_k = SSM_WIDTH
    c_v = c_k + NA_WIDTH
    c_f = c_v + NA_WIDTH
    c_q = c_f + FNET_WIDTH
    c_g = c_q + NA_WIDTH
    tiles = [slice(r0, r0 + SUB_TILE) for r0 in range(0, x_ref.shape[0], SUB_TILE)]
    us = [(_ln(x_ref[rows, :]) * (1.0 + t[1:2]) + t[0:1]).astype(BF) for rows in tiles]
    rows4 = SUB_TILE // S5_BLOCK
    for u, rows in zip(us, tiles):
        zs = _dot(u, w_ref[:, 0:c_k])
        us_ref[rows, :] = zs.astype(BF)
        for hf in range(c_k // LANES):
            us_f32[hf, rows, :] = zs[:, hf * LANES:(hf + 1) * LANES]
        q0 = rows.start // S5_BLOCK
        for j in range(S5_BLOCK):
            for hf in range(c_k // LANES):
                c0 = j * c_k + hf * LANES
                u4_ref[q0:q0 + rows4, c0:c0 + LANES] = (
                    us_f32[hf, pl.ds(rows.start + j, rows4, stride=S5_BLOCK), :].astype(BF))
    for u, rows in zip(us, tiles):
        k_ref[rows, :] = _rope(_dot(u, w_ref[:, c_k:c_v]), cos_ref[rows, :], sin_ref[rows, :]).astype(BF)
    for u, rows in zip(us, tiles):
        v_ref[rows, :] = _dot(u, w_ref[:, c_v:c_f]).astype(BF)
    for u, rows in zip(us, tiles):
        f_ref[rows, :] = _dot(u, w_ref[:, c_f:c_q]).astype(BF)
    for u, rows in zip(us, tiles):
        q = _rope(_dot(u, w_ref[:, c_q:c_g]), cos_ref[rows, :], sin_ref[rows, :])
        q_ref[rows, :] = (q * q_scale).astype(BF)
    gw = NA_WIDTH
    for c0 in range(0, g_ref.shape[-1], gw):
        for u, rows in zip(us, tiles):
            g = _dot(u, w_ref[:, c_g + c0:c_g + c0 + gw])
            g_ref[rows, c0:c0 + gw] = jax.nn.sigmoid(g).astype(BF)


def _mixin_call(x, tab, w_in, l, cos_t, sin_t, q_scale):
    nb, n, d = x.shape
    win = w_in.shape[-1]
    ng = win - (SSM_WIDTH + 3 * NA_WIDTH + FNET_WIDTH)
    tm = min(MIXIN_ROWS, n)
    nt_pos = cos_t.shape[0] // tm

    def tok(width):
        return pl.BlockSpec((None, tm, width), lambda b, i: (b, i, 0))

    def shape(width):
        return jax.ShapeDtypeStruct((nb, n, width), BF)

    u4_spec = pl.BlockSpec((None, tm // S5_BLOCK, S5_BLOCK * SSM_WIDTH), lambda b, i: (b, i, 0))
    u4_shape = jax.ShapeDtypeStruct((nb, n // S5_BLOCK, S5_BLOCK * SSM_WIDTH), BF)
    return pl.pallas_call(
        functools.partial(_mixin_kernel, q_scale=q_scale),
        grid=(nb, n // tm),
        in_specs=[tok(d),
                  pl.BlockSpec((None, 8, d), lambda b, i: (b, 0, 0)),
                  _layer_spec(w_in, l, mode=pl.Buffered(1)),
                  pl.BlockSpec((tm, NA_HEAD_DIM), lambda b, i: (i % nt_pos, 0)),
                  pl.BlockSpec((tm, NA_HEAD_DIM), lambda b, i: (i % nt_pos, 0))],
        out_specs=[tok(SSM_WIDTH), u4_spec, tok(NA_WIDTH), tok(NA_WIDTH), tok(FNET_WIDTH), tok(NA_WIDTH),
                   tok(ng)],
        out_shape=[shape(SSM_WIDTH), u4_shape, shape(NA_WIDTH), shape(NA_WIDTH), shape(FNET_WIDTH),
                   shape(NA_WIDTH), shape(ng)],
        scratch_shapes=[pltpu.VMEM((SSM_WIDTH // LANES, tm, LANES), F32)],
        compiler_params=_cparams(2),
        name="mixin",
    )(x, tab, w_in, cos_t, sin_t)


def _s5_kernel(*refs):
    ins, outs, scr = refs[0:12], refs[12:16], refs[16:22]
    dirs = [dict(zip(("u", "h0", "wst", "wconv", "wout", "sc"), ins[6 * z:6 * z + 6]),
                 y=outs[2 * z], hfin=outs[2 * z + 1],
                 hbuf=scr[3 * z], hprev=scr[3 * z + 1], carry=scr[3 * z + 2], reverse=bool(z)) for z in range(2)]
    i = pl.program_id(1)
    ns = SSM_FLAT
    rows = dirs[0]["hbuf"].shape[0]
    sub = min(S5_SUB, rows)
    row_id = lax.broadcasted_iota(jnp.int32, (SCAN_ROWS, ns), 0)

    @pl.when(i == 0)
    def _():
        for d in dirs:
            d["carry"][...] = d["h0"][...]

    for d in dirs:
        order = (lambda r: list(r)[::-1]) if d["reverse"] else list
        d["order"] = order
        d["subs"] = order(range(0, rows, sub))
        d["edge"] = row_id == ((SCAN_ROWS - 1) if d["reverse"] else 0)
        d["c"] = (d["carry"][:, 0:ns], d["carry"][:, ns:2 * ns])

    for k in range(rows // sub):
        for d in dirs:
            s0 = d["subs"][k]
            d["hbuf"][s0:s0 + sub, :] = _dot(d["u"][s0:s0 + sub, :], d["wst"][...])

    def scan(d, s0):
        reverse, hbuf, hprev, sc_ref = d["reverse"], d["hbuf"], d["hprev"], d["sc"]
        cr, ci = d["c"]
        for r0 in d["order"](range(s0, s0 + sub, SCAN_ROWS)):
            xr = hbuf[r0:r0 + SCAN_ROWS, 0:ns]
            xi = hbuf[r0:r0 + SCAN_ROWS, ns:2 * ns]
            for k, s in enumerate((1, 2, 4)):
                ar = sc_ref[k, :, 0:ns]
                ai = sc_ref[k, :, ns:2 * ns]
                shift = (SCAN_ROWS - s) if reverse else s
                sr = pltpu.roll(xr, shift, axis=0)
                si = pltpu.roll(xi, shift, axis=0)
                xr, xi = xr + ar * sr - ai * si, xi + ar * si + ai * sr
            pr = sc_ref[3, :, 0:ns]
            pi = sc_ref[3, :, ns:2 * ns]
            hr = xr + pr * cr - pi * ci
            hi = xi + pr * ci + pi * cr
            one = (SCAN_ROWS - 1) if reverse else 1
            hprev[r0:r0 + SCAN_ROWS, 0:ns] = jnp.where(d["edge"], cr, pltpu.roll(hr, one, axis=0))
            hprev[r0:r0 + SCAN_ROWS, ns:2 * ns] = jnp.where(d["edge"], ci, pltpu.roll(hi, one, axis=0))
            last = 0 if reverse else SCAN_ROWS - 1
            cr = jnp.broadcast_to(hr[last:last + 1, :], hr.shape)
            ci = jnp.broadcast_to(hi[last:last + 1, :], hi.shape)
        d["c"] = (cr, ci)

    def readout(d, s0):
        y = (_dot(d["u"][s0:s0 + sub, :], d["wconv"][...])
             + _dot(d["hprev"][s0:s0 + sub, :].astype(BF), d["wout"][...]))
        d["y"][s0:s0 + sub, :] = y.astype(d["y"].dtype)

    for k in range(rows // sub):
        for d in dirs:
            scan(d, d["subs"][k])
        if k > 0:
            for d in dirs:
                readout(d, d["subs"][k - 1])
    for d in dirs:
        readout(d, d["subs"][-1])
        d["carry"][:, 0:ns] = d["c"][0]
        d["carry"][:, ns:2 * ns] = d["c"][1]
        d["hfin"][...] = d["carry"][...]


def _s5_call(u4, h0f, h0b, tables, l, rows):
    wst, wconv, wout, sc = tables
    b, n, w = u4.shape
    nc = n // rows
    ns2 = 2 * SSM_FLAT
    cmaps = (lambda bb, i: (bb, i, 0), lambda bb, i: (bb, nc - 1 - i, 0))
    state = pl.BlockSpec((None, 8, ns2), lambda bb, i: (bb, 0, 0))
    once = pl.Buffered(1)
    in_specs, args, out_specs, out_shape, scratch = [], [], [], [], []
    for z, h0 in enumerate((h0f, h0b)):
        in_specs += [pl.BlockSpec((None, rows, w), cmaps[z]), state] + [
            _layer_spec(t, l, z, mode=once) for t in (wst, wconv, wout, sc)]
        args += [u4, h0, wst, wconv, wout, sc]
        out_specs += [pl.BlockSpec((None, rows, w), cmaps[z]), state]
        out_shape += [jax.ShapeDtypeStruct((b, n, w), BF), jax.ShapeDtypeStruct((b, 8, ns2), F32)]
        scratch += [pltpu.VMEM((rows, ns2), F32), pltpu.VMEM((rows, ns2), F32), pltpu.VMEM((8, ns2), F32)]
    return pl.pallas_call(
        _s5_kernel,
        grid=(b, nc),
        in_specs=in_specs,
        out_specs=out_specs,
        out_shape=out_shape,
        scratch_shapes=scratch,
        compiler_params=_cparams(2),
        name="s5",
    )(*args)


def _s5_tables(log_dt, a_re, a_im, b_re, b_im, c_re, c_im):
    depth = log_dt.shape[0]
    g, p, c = b_re.shape[-3:]
    tb = S5_BLOCK
    hp = lax.Precision.HIGHEST
    f32 = lambda a: a.astype(F32)
    a_re, a_im, b_re, b_im, c_re, c_im = map(f32, (a_re, a_im, b_re, b_im, c_re, c_im))
    dt = jnp.exp(f32(log_dt))[..., None]
    adt_re = a_re * dt
    adt_im = a_im * dt

    def apow(k):
        kk = jnp.asarray(k, F32)[None, :, :, None, None]
        mag = jnp.exp(kk * adt_re[:, :, None])
        ang = kk * adt_im[:, :, None]
        return mag * jnp.cos(ang), mag * jnp.sin(ang)

    mag = jnp.exp(adt_re)
    ab_re = mag * jnp.cos(adt_im)
    ab_im = mag * jnp.sin(adt_im)
    den = a_re * a_re + a_im * a_im
    nr = ab_re - 1.0
    fr = (nr * a_re + ab_im * a_im) / den
    fi = (ab_im * a_re - nr * a_im) / den
    bb_re = fr[..., None] * b_re - fi[..., None] * b_im
    bb_im = fr[..., None] * b_im + fi[..., None] * b_re

    step = np.arange(tb)
    k_state = np.stack([tb - 1 - step, step])
    k_out = np.stack([step + 1, tb - step])
    k_tap = np.stack([step, step])
    k_scan = np.stack([tb * (np.arange(SCAN_ROWS) + 1)] * 2)

    def block_diag(compact, row_group, col_width):
        n = compact.shape[3]
        dst = np.arange(n * g * col_width)
        src = (dst // (g * col_width)) * col_width + dst % col_width
        copy = np.zeros((n * col_width, n * g * col_width), np.float32)
        copy[src, dst] = 1.0
        cols = jnp.einsum('dzrk,kn->dzrn', compact.reshape(compact.shape[:3] + (n * col_width,)),
                          jnp.asarray(copy), precision=hp, preferred_element_type=BF)
        keep = jnp.asarray(row_group[:, None] == ((dst // col_width) % g)[None, :])
        return jnp.where(keep, cols, jnp.zeros((), BF))

    chan_rows = (np.arange(tb * g * c) // c) % g
    state_rows = (np.arange(2 * g * p) // p) % g

    def cmul_b(pr, pi):
        return (pr[..., None] * bb_re[:, :, None] - pi[..., None] * bb_im[:, :, None],
                pr[..., None] * bb_im[:, :, None] + pi[..., None] * bb_re[:, :, None])

    wr, wi = cmul_b(*apow(k_state))
    w2 = jnp.stack([wr, wi], axis=3)
    w2 = jnp.transpose(w2, (0, 1, 2, 4, 6, 3, 5))
    wst = block_diag(w2.reshape(depth, 2, tb * g * c, 2, p), chan_rows, p)

    pr, pi = apow(k_out)
    cm_re = c_re[:, :, None] * pr[:, :, :, :, None, :] - c_im[:, :, None] * pi[:, :, :, :, None, :]
    cm_im = c_re[:, :, None] * pi[:, :, :, :, None, :] + c_im[:, :, None] * pr[:, :, :, :, None, :]
    cm = jnp.stack([cm_re, -cm_im], axis=3)
    cm = jnp.transpose(cm, (0, 1, 3, 4, 6, 2, 5))
    wout = block_diag(cm.reshape(depth, 2, 2 * g * p, tb, c), state_rows, c)

    abr, abi = cmul_b(*apow(k_tap))
    taps = (jnp.einsum('dzgcp,dztgpe->dztgec', c_re, abr, precision=hp)
            - jnp.einsum('dzgcp,dztgpe->dztgec', c_im, abi, precision=hp))
    sel = np.zeros((2, tb, tb, tb), np.float32)
    for i in range(tb):
        for j in range(tb):
            if j >= i:
                sel[0, i, j, j - i] = 1.0
            if i >= j:
                sel[1, i, j, i - j] = 1.0
    kij = jnp.einsum('dztgec,zijt->dzigejc', taps, jnp.asarray(sel), precision=hp)
    wconv = block_diag(kij.reshape(depth, 2, tb * g * c, tb, c), chan_rows, c)

    qr, qi = apow(k_scan)
    qr = qr.reshape(depth, 2, SCAN_ROWS, g * p)
    qi = qi.reshape(depth, 2, SCAN_ROWS, g * p)
    rows = np.arange(SCAN_ROWS)
    tabs = []
    for s in (1, 2, 4):
        keep = jnp.asarray(np.stack([rows >= s, rows + s <= SCAN_ROWS - 1]), F32)[None, :, :, None]
        tabs.append(jnp.concatenate([keep * qr[:, :, s - 1:s], keep * qi[:, :, s - 1:s]], axis=-1))
    carry_pow = lambda t: jnp.stack([t[:, 0], t[:, 1, ::-1]], axis=1)
    tabs.append(jnp.concatenate([carry_pow(qr), carry_pow(qi)], axis=-1))
    sc = jnp.stack(tabs, axis=2)
    return wst, wconv, wout, sc


def _fnet1_kernel(x_ref, f2_ref, tc_ref, ts_ref, o_ref):
    n2 = x_ref.shape[0]
    y = _dot(f2_ref[...], x_ref[...])
    br = y[0:n2]
    bi = y[n2:2 * n2]
    tc = tc_ref[...]
    ts = ts_ref[...]
    o_ref[0] = (br * tc + bi * ts).astype(BF)
    o_ref[1] = (bi * tc - br * ts).astype(BF)


def _fnet2_kernel(b_ref, f1_ref, cc_ref, sc_ref, o_ref, *, scale):
    kb = b_ref.shape[1]
    n1 = b_ref.shape[2]
    c = b_ref.shape[3]
    zr, zi = [], []
    for j in range(kb):
        rhs = jnp.concatenate([b_ref[0, j], b_ref[1, j]], axis=0)
        z = _dot(f1_ref[...], rhs)
        zr.append(z[0:n1])
        zi.append(z[n1:2 * n1])
    zr = jnp.concatenate(zr, axis=0).astype(BF)
    zi = jnp.concatenate(zi, axis=0).astype(BF)
    y = (_dot(zr, cc_ref[...]) + _dot(zi, sc_ref[...])) * scale
    for j in range(kb):
        o_ref[:, j * c:(j + 1) * c] = y[j * n1:(j + 1) * n1].astype(o_ref.dtype)


def _dft_tables(n):
    idx = np.arange(n)
    ang = 2.0 * np.pi * ((idx[:, None] * idx[None, :]) % n) / n
    return np.cos(ang), np.sin(ang)


def _chan_tables():
    cc, sc = _dft_tables(FNET_GROUP_CH)
    eye = np.eye(FNET_GROUPS)
    return jnp.asarray(np.kron(eye, cc), BF), jnp.asarray(np.kron(eye, sc), BF)


def _fnet_call(f):
    b, n, c = f.shape
    n2 = FNET_N2
    n1 = n // n2
    c2, s2 = _dft_tables(n2)
    f2 = jnp.asarray(np.concatenate([c2, -s2], axis=0), BF)
    k2 = np.arange(n2)[:, None]
    nn1 = np.arange(n1)[None, :]
    ang = 2.0 * np.pi * ((k2 * nn1) % n) / n
    tc = jnp.asarray(np.repeat(np.cos(ang), c, axis=1), F32)
    ts = jnp.asarray(np.repeat(np.sin(ang), c, axis=1), F32)
    c1, s1 = _dft_tables(n1)
    f1 = jnp.asarray(np.block([[c1, s1], [-s1, c1]]), BF)
    cc, sc = _chan_tables()

    wc = min(n1 * c, 4096)
    x2 = f.reshape(b, n2, n1 * c)
    st1 = pl.pallas_call(
        _fnet1_kernel,
        grid=(b, (n1 * c) // wc),
        in_specs=[pl.BlockSpec((None, n2, wc), lambda bb, j: (bb, 0, j)),
                  pl.BlockSpec((2 * n2, n2), lambda bb, j: (0, 0)),
                  pl.BlockSpec((n2, wc), lambda bb, j: (0, j)),
                  pl.BlockSpec((n2, wc), lambda bb, j: (0, j))],
        out_specs=pl.BlockSpec((None, 2, n2, wc), lambda bb, j: (bb, 0, 0, j)),
        out_shape=jax.ShapeDtypeStruct((b, 2, n2, n1 * c), BF),
        compiler_params=_cparams(2),
        name="fnet1",
    )(x2, f2, tc, ts)
    st1 = st1.reshape(b, 2, n2, n1, c)
    kb = FNET_K2_BLOCK
    y = pl.pallas_call(
        functools.partial(_fnet2_kernel, scale=1.0 / math.sqrt(n * FNET_GROUP_CH)),
        grid=(b, n2 // kb),
        in_specs=[pl.BlockSpec((None, 2, kb, n1, c), lambda bb, j: (bb, 0, j, 0, 0)),
                  pl.BlockSpec((2 * n1, 2 * n1), lambda bb, j: (0, 0)),
                  pl.BlockSpec((c, c), lambda bb, j: (0, 0)),
                  pl.BlockSpec((c, c), lambda bb, j: (0, 0))],
        out_specs=pl.BlockSpec((None, n1, kb * c), lambda bb, j: (bb, 0, j)),
        out_shape=jax.ShapeDtypeStruct((b, n1, n2 * c), BF),
        compiler_params=_cparams(2),
        name="fnet2",
    )(st1, f1, cc, sc)
    return y.reshape(b, n, c)


def _fnet_dense_kernel(x_ref, cl_ref, sl_ref, cc_ref, sc_ref, o_ref, *, scale):
    x = x_ref[...]
    xr = _dot(x, cc_ref[...]).astype(BF)
    xi = _dot(x, sc_ref[...]).astype(BF)
    o_ref[...] = ((_dot(cl_ref[...], xr) - _dot(sl_ref[...], xi)) * scale).astype(o_ref.dtype)


def _fnet_dense_call(f):
    b, n, c = f.shape
    cl, sl = _dft_tables(n)
    cc, sc = _chan_tables()
    full = lambda bb: (0, 0)
    return pl.pallas_call(
        functools.partial(_fnet_dense_kernel, scale=1.0 / math.sqrt(n * FNET_GROUP_CH)),
        grid=(b,),
        in_specs=[pl.BlockSpec((None, n, c), lambda bb: (bb, 0, 0)),
                  pl.BlockSpec((n, n), full), pl.BlockSpec((n, n), full),
                  pl.BlockSpec((c, c), full), pl.BlockSpec((c, c), full)],
        out_specs=pl.BlockSpec((None, n, c), lambda bb: (bb, 0, 0)),
        out_shape=jax.ShapeDtypeStruct((b, n, c), BF),
        compiler_params=_cparams(1),
        name="fnet_ctx",
    )(f, jnp.asarray(cl, BF), jnp.asarray(sl, BF), cc, sc)


def _head_masks(width):
    lane = lax.broadcasted_iota(jnp.int32, (1, width), 1)
    return [(lane // NA_HEAD_DIM) == h for h in range(width // NA_HEAD_DIM)]


def _attend(q, k_parts, v_parts, ranges, bias_of, o_ref):
    zero = jnp.zeros((), BF)
    hw = NA_HQ * NA_HEAD_DIM
    masks = _head_masks(hw)
    m_rows = q.shape[0]
    rb = min(GRID_W, m_rows)
    live = [j for j, (a, b) in enumerate(ranges) if b > a]
    k_parts = [k_parts[j] for j in live]
    v_parts = [v_parts[j] for j in live]
    ranges = [ranges[j] for j in live]
    quad = lambda x, h: x[:, (h // NA_HQ) * hw:(h // NA_HQ + 1) * hw]

    def scores(h):
        qh = jnp.where(masks[h % NA_HQ], quad(q, h), zero)
        s_parts = []
        for j, (kp, (a, b)) in enumerate(zip(k_parts, ranges)):
            s = _dot_t(qh[a:b], quad(kp, h))
            bias = bias_of(h, live[j], a, b)
            s_parts.append(s if bias is None else s + bias)
        return s_parts

    def lane_tiles(x):
        return [x[:, c:c + LANES] for c in range(0, x.shape[1], LANES)]

    def softmax(s_parts):
        p_rows = [[] for _ in s_parts]
        dens = []
        for r0 in range(0, m_rows, rb):
            pieces = [(j, s[r0 - a:r0 - a + rb]) for j, (s, (a, b)) in enumerate(zip(s_parts, ranges))
                      if a <= r0 < b]
            m = functools.reduce(jnp.maximum, [t for _, pc in pieces for t in lane_tiles(pc)])
            m = m.max(axis=-1, keepdims=True)
            acc = None
            for j, pc in pieces:
                p = jnp.exp2(pc - m)
                acc = functools.reduce(jnp.add, lane_tiles(p), acc) if acc is not None else \
                    functools.reduce(jnp.add, lane_tiles(p))
                p_rows[j].append(p.astype(BF))
            dens.append(acc.sum(axis=-1, keepdims=True))
        return [jnp.concatenate(r, axis=0) for r in p_rows], jnp.concatenate(dens, axis=0)

    def values(h, p_parts, den):
        out = None
        for p, vp, (a, b) in zip(p_parts, v_parts, ranges):
            o = _dot(p, quad(vp, h))
            pad = [jnp.zeros((n, o.shape[1]), F32) for n in (a, m_rows - b)]
            o = jnp.concatenate([x for x in (pad[0], o, pad[1]) if x.shape[0]], axis=0)
            out = o if out is None else out + o
        return out / den

    n = q.shape[-1] // NA_HEAD_DIM
    s_of, p_of, outs = {}, {}, [None] * (n // NA_HQ)
    for step in range(n + 2):
        if step < n:
            s_of[step] = scores(step)
        if 0 <= step - 1 < n:
            p_of[step - 1] = softmax(s_of.pop(step - 1))
        if 0 <= step - 2 < n:
            h = step - 2
            o = values(h, *p_of.pop(h))
            prev = outs[h // NA_HQ]
            outs[h // NA_HQ] = o if prev is None else jnp.where(masks[h % NA_HQ], o, prev)
    o_ref[...] = jnp.concatenate(outs, axis=-1).astype(o_ref.dtype)


def _natten_kernel(q_ref, k0, k1, k2, k3, v0, v1, v2, v3, kc_ref, vc_ref, bias_ref, *rest, ranges, pair_index):
    o_ref = rest[-1]
    k_parts = [r[...] for r in (k0, k1, k2, k3)] + [kc_ref[...]]
    v_parts = [r[...] for r in (v0, v1, v2, v3)] + [vc_ref[...]]
    pairs_per_part = k0.shape[0] // (2 * GRID_W)

    def bias_of(h, j, a, b):
        if j >= 4:
            return None
        rows = [jnp.concatenate([bias_ref[h, pair_index[r][j * pairs_per_part + p]]
                                 for p in range(pairs_per_part)], axis=1)
                for r in range(a // GRID_W, b // GRID_W)]
        return jnp.concatenate(rows, axis=0).astype(F32)

    _attend(q_ref[...], k_parts, v_parts, ranges, bias_of, o_ref)


def _na_windows(rows):
    nj = rows // NA_QROWS
    out = []
    for j in (0, min(1, nj - 1), nj - 1):
        rq = NA_QROWS * j + np.arange(NA_QROWS)
        ws = int(np.clip(NA_QROWS * j - (NA_KROWS - NA_QROWS) // 2, 0, rows - NA_KROWS))
        rk = ws + np.arange(NA_KROWS)
        r0 = np.clip(rq - WIN_ROWS // 2, 0, rows - WIN_ROWS)
        valid_r = (rk[None, :] >= r0[:, None]) & (rk[None, :] < r0[:, None] + WIN_ROWS)
        dr = rk[None, :] - rq[:, None] + (WIN_ROWS - 1)
        out.append((valid_r, dr))
    return out


def _natten_call(q, k, v, kc, vc, bias, l, variant, prev=None):
    table, index = bias
    b, n, w = q.shape
    lc = kc.shape[1]
    qt = NA_QROWS * GRID_W
    blk = (NA_KROWS // 4) * GRID_W
    nj = n // qt
    nblk = n // blk
    ratio = qt // blk
    j0, count = ((0, 1), (1, nj - 2), (nj - 1, 1))[variant]

    valid_r = _na_windows(n // GRID_W)[variant][0]
    ranges = []
    for t in range(4):
        seen = np.nonzero(valid_r[:, t * (NA_KROWS // 4):(t + 1) * (NA_KROWS // 4)].any(axis=1))[0]
        ranges.append((int(seen.min()) * GRID_W, (int(seen.max()) + 1) * GRID_W) if seen.size else (0, 0))
    ranges.append((0, qt))

    def kv_spec(m):
        def imap(j, bb):
            start = jnp.clip(ratio * (j + j0) - 1, 0, nblk - 4)
            return (bb, start + m, 0)
        return pl.BlockSpec((None, blk, w), imap)

    in_specs = ([pl.BlockSpec((None, qt, w), lambda j, bb: (bb, j + j0, 0))]
                + [kv_spec(m) for m in range(4)] + [kv_spec(m) for m in range(4)]
                + [pl.BlockSpec((None, lc, w), lambda j, bb: (bb, 0, 0)),
                   pl.BlockSpec((None, lc, w), lambda j, bb: (bb, 0, 0)),
                   _layer_spec(table, l, mode=pl.Buffered(1))])
    args = [q, k, k, k, k, v, v, v, v, kc, vc, table]
    aliases = {}
    if prev is not None:
        in_specs.append(pl.BlockSpec(memory_space=pl.ANY))
        aliases = {len(args): 0}
        args.append(prev)
    return pl.pallas_call(
        functools.partial(_natten_kernel, ranges=tuple(ranges), pair_index=index[variant]),
        grid=(count, b),
        in_specs=in_specs,
        out_specs=pl.BlockSpec((None, qt, w), lambda j, bb: (bb, j + j0, 0)),
        out_shape=jax.ShapeDtypeStruct((b, n, w), BF),
        input_output_aliases=aliases,
        compiler_params=_cparams(2),
        name="natten",
    )(*args)


def _ctx_atten_kernel(q_ref, kc_ref, vc_ref, o_ref):
    _attend(q_ref[...], [kc_ref[...]], [vc_ref[...]], [(0, q_ref.shape[0])], lambda h, j, a, b: None, o_ref)


def _ctx_atten_call(qc, kc, vc):
    b, lc, w = qc.shape
    spec = pl.BlockSpec((None, lc, w), lambda bb: (bb, 0, 0))
    return pl.pallas_call(
        _ctx_atten_kernel,
        grid=(b,),
        in_specs=[spec, spec, spec],
        out_specs=spec,
        out_shape=jax.ShapeDtypeStruct((b, lc, w), BF),
        compiler_params=_cparams(1),
        name="ctx_atten",
    )(qc, kc, vc)


def _na_bias(rpb, rows):
    w = GRID_W
    n_dr = 2 * WIN_ROWS - 1
    n_dc = 2 * WIN_COLS - 1
    cq = np.arange(w)
    cs = np.clip(cq - WIN_COLS // 2, 0, w - WIN_COLS)
    ck = np.arange(w)
    valid_c = (ck[None, :] >= cs[:, None]) & (ck[None, :] < cs[:, None] + WIN_COLS)
    dc = ck[None, :] - cq[:, None] + (WIN_COLS - 1)
    pick_c = (dc[:, :, None] == np.arange(n_dc)) & valid_c[:, :, None]
    tiles = jnp.einsum('lhab,cdb->lhacd', rpb.astype(F32) * LOG2E, jnp.asarray(pick_c, F32),
                       precision=lax.Precision.HIGHEST)
    tiles = tiles + jnp.asarray(np.where(valid_c, 0.0, NEG_BIG), F32)
    tiles = jnp.concatenate([tiles, jnp.full_like(tiles[:, :, :1], NEG_BIG)], axis=2).astype(BF)
    pairs, index = {}, []
    for valid_r, dr in _na_windows(rows):
        dr = np.where(valid_r, dr, n_dr)
        index.append(tuple(tuple(pairs.setdefault((int(dr[a, 2 * p]), int(dr[a, 2 * p + 1])), len(pairs))
                                 for p in range(NA_KROWS // 2)) for a in range(NA_QROWS)))
    table = jnp.stack([jnp.concatenate([tiles[:, :, i0], tiles[:, :, i1]], axis=-1) for i0, i1 in pairs], axis=2)
    return table, index


def _rope_tables(n):
    nf = NA_HEAD_DIM // 4
    t = jnp.arange(n, dtype=jnp.int32)
    pos = jnp.stack([t // GRID_W, t % GRID_W], axis=-1).astype(F32)
    inv_freq = ROPE_THETA ** (-jnp.arange(nf, dtype=F32) / nf)
    ang = pos[:, :, None] * inv_freq
    cos = jnp.cos(ang)
    sin = jnp.sin(ang)
    cos_h = jnp.concatenate([cos, cos], axis=-1).reshape(n, NA_HEAD_DIM)
    sin_h = jnp.concatenate([-sin, sin], axis=-1).reshape(n, NA_HEAD_DIM)
    return cos_h, sin_h


def _merge_kernel(ysf_ref, ysb_ref, us_ref, yf_ref, yn_ref, g_ref, h_ref, t_ref,
                  sd_ref, wglu_ref, bglu_ref, wbs_ref, wbf_ref, wbn_ref, wo_ref, o_ref, ys_f32, *, alpha):
    d = h_ref.shape[-1]
    t = t_ref[...]
    halves = SSM_WIDTH // LANES
    tiles = [slice(r0, r0 + SUB_TILE) for r0 in range(0, h_ref.shape[0], SUB_TILE)]
    rows4 = SUB_TILE // S5_BLOCK
    ys = []
    for rows in tiles:
        q0 = rows.start // S5_BLOCK
        y4 = ysf_ref[q0:q0 + rows4, :].astype(F32) + ysb_ref[q0:q0 + rows4, :].astype(F32)
        for j in range(S5_BLOCK):
            for hf in range(halves):
                c0 = j * SSM_WIDTH + hf * LANES
                ys_f32[hf, pl.ds(rows.start + j, rows4, stride=S5_BLOCK), :] = y4[:, c0:c0 + LANES]
        y = jnp.concatenate([ys_f32[hf, rows, :] for hf in range(halves)], axis=-1)
        ys.append(jax.nn.gelu(y + sd_ref[...] * us_ref[rows, :].astype(F32)))
    glu = [_dot(y.astype(BF), wglu_ref[...]) for y in ys]
    ys = [(y * jax.nn.sigmoid(z + bglu_ref[...])).astype(BF) for y, z in zip(ys, glu)]
    br = [(_dot(y, wbs_ref[...]), _dot(yf_ref[rows, :], wbf_ref[...]), _dot(yn_ref[rows, :], wbn_ref[...]))
          for y, rows in zip(ys, tiles)]
    mix = [(g_ref[rows, 0:d].astype(F32) * bs + g_ref[rows, d:2 * d].astype(F32) * bf
            + g_ref[rows, 2 * d:3 * d].astype(F32) * bn).astype(BF) for (bs, bf, bn), rows in zip(br, tiles)]
    proj = [_dot(m, wo_ref[...]) for m in mix]
    for y, rows in zip(proj, tiles):
        v = alpha * h_ref[rows, :] + t[2:3] * y
        o_ref[rows, :] = _ln(v) * t[3:4] + t[4:5]


def _merge_call(ysf4, ysb4, us, yf, yn, g, h, tab, s5w, mats, l, alpha):
    nb, n, d = h.shape
    tm = min(MERGE_ROWS, n)

    def tok(a):
        return pl.BlockSpec((None, tm, a.shape[-1]), lambda b, i: (b, i, 0))

    def tok4(a):
        return pl.BlockSpec((None, tm // S5_BLOCK, a.shape[-1]), lambda b, i: (b, i, 0))

    toks = (us, yf, yn, g, h)
    return pl.pallas_call(
        functools.partial(_merge_kernel, alpha=alpha),
        grid=(nb, n // tm),
        in_specs=[tok4(ysf4), tok4(ysb4)] + [tok(a) for a in toks]
                 + [pl.BlockSpec((None, 8, d), lambda b, i: (b, 0, 0))]
                 + [_layer_spec(a, l) for a in tuple(s5w) + tuple(mats)],
        out_specs=tok(h),
        out_shape=jax.ShapeDtypeStruct(h.shape, F32),
        scratch_shapes=[pltpu.VMEM((SSM_WIDTH // LANES, tm, LANES), F32)],
        compiler_params=_cparams(2),
        name="merge",
    )(ysf4, ysb4, *toks, tab, *s5w, *mats)


def _table(mod, d, chunks, extra):
    rows = [mod[:, i * d:(i + 1) * d] for i in chunks]
    rows += [jnp.broadcast_to(e[None, :], (8, d)) for e in extra]
    rows += [jnp.zeros((8, d), F32)] * (8 - len(rows))
    return jnp.stack(rows, axis=1)


def kernel(x, c, ctx, c_ctx, w_mod, b_mod, ln_g, ln_b, ffn_w_gate, ffn_w_up, ffn_w_down, w_in, ssm_log_dt, ssm_a_re, ssm_a_im, ssm_b_re, ssm_b_im, ssm_c_re, ssm_c_im, ssm_d, ssm_w_glu, ssm_b_glu, na_rpb, w_br_ssm, w_br_fnet, w_br_na, w_out):
    bsz, n, d = x.shape
    lc = ctx.shape[1]
    depth = w_mod.shape[0]
    rows = n // GRID_W
    assert bsz + 1 <= 8 and rows % NA_QROWS == 0 and rows >= NA_KROWS and n % FNET_N2 == 0
    alpha = (2 * depth) ** 0.25
    q_scale = NA_HEAD_DIM ** -0.5 * LOG2E

    c8 = jnp.concatenate([c, c_ctx[None, :], jnp.zeros((8 - bsz - 1, d), F32)], axis=0)
    mod_all = _mod_call(c8, w_mod, b_mod)

    wg = ffn_w_gate.astype(BF)
    wu = ffn_w_up.astype(BF)
    wd = ffn_w_down.astype(BF)
    win = w_in.astype(BF)
    mats = (w_br_ssm.astype(BF), w_br_fnet.astype(BF), w_br_na.astype(BF), w_out.astype(BF))
    s5w = (ssm_d[:, None, :], ssm_w_glu.astype(BF), ssm_b_glu[:, None, :])
    s5_tabs = _s5_tables(ssm_log_dt, ssm_a_re, ssm_a_im, ssm_b_re, ssm_b_im, ssm_c_re, ssm_c_im)
    na_bias = _na_bias(na_rpb, rows)
    cos_t, sin_t = _rope_tables(n)
    cos_c = jnp.ones((bsz * lc, NA_HEAD_DIM), F32)
    sin_c = jnp.zeros((bsz * lc, NA_HEAD_DIM), F32)
    zero_state = jnp.zeros((bsz, 8, 2 * SSM_FLAT), F32)

    h = x
    hc = ctx.reshape(1, bsz * lc, d)
    for l in range(depth):
        last = l == depth - 1
        mod = mod_all[l]

        tab = _table(mod, d, (0, 1, 2), (ln_g[l, 0], ln_b[l, 0]))
        h = _ffn_call(h, tab[:bsz], wg, wu, wd, l, 0, alpha)
        hc = _ffn_call(hc, tab[bsz:bsz + 1], wg, wu, wd, l, 0, alpha)

        tab = _table(mod, d, (3, 4, 5), (ln_g[l, 1], ln_b[l, 1]))
        us, u4, k, v, f, q, g = _mixin_call(h, tab[:bsz], win, l, cos_t, sin_t, q_scale)
        usc, u4c, kc, vc, fc, qc, gc = _mixin_call(hc, tab[bsz:bsz + 1], win, l, cos_c, sin_c, q_scale)
        u4c, kc, vc, fc, qc = [a.reshape(bsz, a.shape[1] // bsz, a.shape[-1]) for a in (u4c, kc, vc, fc, qc)]

        ysfc, hcf, ysbc, hcb = _s5_call(u4c, zero_state, zero_state, s5_tabs, l, u4c.shape[1])
        ysf, _, ysb, _ = _s5_call(u4, hcf, hcb, s5_tabs, l, min(S5_ROWS, u4.shape[1]))

        yf = _fnet_call(f)
        yn = None
        for variant in ((1, 0, 2) if rows > 2 * NA_QROWS else (0, 2)):
            yn = _natten_call(q, k, v, kc, vc, na_bias, l, variant, yn)

        h = _merge_call(ysf, ysb, us, yf, yn, g, h, tab[:bsz], s5w, mats, l, alpha)

        tab3 = _table(mod, d, (6, 7, 8), (ln_g[l, 2], ln_b[l, 2]))
        h = _ffn_call(h, tab3[:bsz], wg, wu, wd, l, 1, alpha)

        if not last:
            yfc = _fnet_dense_call(fc)
            ync = _ctx_atten_call(qc, kc, vc)
            flat = lambda a: a.reshape(1, a.shape[0] * a.shape[1], a.shape[-1])
            hc = _merge_call(flat(ysfc), flat(ysbc), usc, flat(yfc), flat(ync), gc, hc, tab[bsz:bsz + 1],
                             s5w, mats, l, alpha)
            hc = _ffn_call(hc, tab3[bsz:bsz + 1], wg, wu, wd, l, 1, alpha)
    return h
```

```python
import functools
import math

import jax
import jax.numpy as jnp
import numpy as np
from jax import lax
from jax.experimental import pallas as pl
from jax.experimental.pallas import tpu as pltpu

BF = jnp.bfloat16
F32 = jnp.float32

GRID_W = 64
SSM_GROUPS = 16
SSM_GROUP_CH = 16
SSM_STATE = 64
SSM_WIDTH = SSM_GROUPS * SSM_GROUP_CH
SSM_FLAT = SSM_GROUPS * SSM_STATE
FNET_GROUPS = 4
FNET_GROUP_CH = 64
FNET_WIDTH = FNET_GROUPS * FNET_GROUP_CH
NA_HEADS = 8
NA_HEAD_DIM = 64
NA_WIDTH = NA_HEADS * NA_HEAD_DIM
WIN_ROWS = 8
WIN_COLS = 16
ROPE_THETA = 10000.0
LN_EPS = 1e-6
LOG2E = 1.4426950408889634
NEG_BIG = -1e30

VMEM_LIMIT_BYTES = 56 * 1024 * 1024
SUB_TILE = 256
FFN_ROWS = 1024
MERGE_ROWS = 1024
MIXIN_ROWS = 1024
FFN_CHUNK = 256
LANES = 128
SCAN_ROWS = 8
S5_BLOCK = 4
S5_ROWS = 512
S5_SUB = 256
S5_LANE_SPLIT = 2
NA_QROWS = 8
NA_KROWS = 16
NA_HQ = 4
FNET_N2 = 128
FNET_K2_BLOCK = 8


def _cparams(n_axes, flags=None):
    return pltpu.CompilerParams(dimension_semantics=("arbitrary",) * n_axes,
                                vmem_limit_bytes=VMEM_LIMIT_BYTES, flags=flags)


def _ln(x):
    mu = jnp.mean(x, axis=-1, keepdims=True)
    xc = x - mu
    var = jnp.mean(xc * xc, axis=-1, keepdims=True)
    return xc * lax.rsqrt(var + LN_EPS)


def _dot(a, b):
    return jnp.dot(a, b, preferred_element_type=F32)


def _dot_t(a, b):
    return lax.dot_general(a, b, (((1,), (1,)), ((), ())), preferred_element_type=F32)


def _layer_spec(a, *lead, mode=None):
    nl = len(lead)
    zeros = (0,) * (a.ndim - nl)
    return pl.BlockSpec((None,) * nl + tuple(a.shape[nl:]), lambda *_: tuple(lead) + zeros, pipeline_mode=mode)


def _mod_kernel(c_ref, w_ref, b_ref, o_ref):
    c = c_ref[...]
    s = (c * jax.nn.sigmoid(c)).astype(BF)
    o_ref[...] = _dot(s, w_ref[...].astype(BF)) + b_ref[...]


def _mod_call(c8, w_mod, b_mod):
    depth, d, nd = w_mod.shape
    nb = nd // d
    return pl.pallas_call(
        _mod_kernel,
        grid=(depth, nb),
        in_specs=[pl.BlockSpec((8, d), lambda l, j: (0, 0)),
                  pl.BlockSpec((None, d, d), lambda l, j: (l, 0, j)),
                  pl.BlockSpec((None, 1, d), lambda l, j: (l, 0, j))],
        out_specs=pl.BlockSpec((None, 8, d), lambda l, j: (l, 0, j)),
        out_shape=jax.ShapeDtypeStruct((depth, 8, nd), F32),
        compiler_params=_cparams(2),
        name="mod",
    )(c8, w_mod, b_mod.reshape(depth, 1, nd))


def _ffn_kernel(x_ref, t_ref, wg_ref, wu_ref, wd_ref, o_ref, act_scr, *, alpha):
    t = t_ref[...]
    tiles = [slice(r0, r0 + SUB_TILE) for r0 in range(0, x_ref.shape[0], SUB_TILE)]
    us = [(_ln(x_ref[rows, :]) * (1.0 + t[1:2]) + t[0:1]).astype(BF) for rows in tiles]
    for c0 in range(0, wg_ref.shape[1], FFN_CHUNK):
        for k, u in enumerate(us):
            a = _dot(u, wg_ref[:, c0:c0 + FFN_CHUNK])
            b = _dot(u, wu_ref[:, c0:c0 + FFN_CHUNK])
            act_scr[k, :, c0:c0 + FFN_CHUNK] = (a * jax.nn.sigmoid(a) * b).astype(BF)
    ys = [_dot(act_scr[k], wd_ref[...]) for k in range(len(tiles))]
    for y, rows in zip(ys, tiles):
        v = alpha * x_ref[rows, :] + (0.5 * t[2:3]) * y
        o_ref[rows, :] = _ln(v) * t[3:4] + t[4:5]


def _ffn_call(x, tab, wg, wu, wd, l, k, alpha):
    nb, n, d = x.shape
    tm = min(FFN_ROWS, n)
    once = pl.Buffered(1)
    return pl.pallas_call(
        functools.partial(_ffn_kernel, alpha=alpha),
        grid=(nb, n // tm),
        in_specs=[pl.BlockSpec((None, tm, d), lambda b, i: (b, i, 0)),
                  pl.BlockSpec((None, 8, d), lambda b, i: (b, 0, 0)),
                  _layer_spec(wg, l, k, mode=once), _layer_spec(wu, l, k, mode=once),
                  _layer_spec(wd, l, k, mode=once)],
        out_specs=pl.BlockSpec((None, tm, d), lambda b, i: (b, i, 0)),
        out_shape=jax.ShapeDtypeStruct(x.shape, F32),
        scratch_shapes=[pltpu.VMEM((pl.cdiv(tm, SUB_TILE), min(SUB_TILE, tm), wg.shape[-1]), BF)],
        compiler_params=_cparams(2),
        name="ffn",
    )(x, tab, wg, wu, wd)


def _rope(x, cos, sin_signed):
    n = x.shape[-1]
    half = NA_HEAD_DIM // 4
    lane = lax.broadcasted_iota(jnp.int32, x.shape, 1)
    first = (lane % (2 * half)) < half
    partner = jnp.where(first, pltpu.roll(x, n - half, axis=1), pltpu.roll(x, half, axis=1))
    reps = n // cos.shape[-1]
    cos = jnp.concatenate([cos] * reps, axis=-1)
    sin_signed = jnp.concatenate([sin_signed] * reps, axis=-1)
    return x * cos + partner * sin_signed


def _mixin_kernel(x_ref, t_ref, w_ref, cos_ref, sin_ref,
                  us_ref, u4_ref, k_ref, v_ref, f_ref, q_ref, g_ref, us_f32, *, q_scale):
    t = t_ref[...]
    c_k = SSM_WIDTH
    c_v = c_k + NA_WIDTH
    c_f = c_v + NA_WIDTH
    c_q = c_f + FNET_WIDTH
    c_g = c_q + NA_WIDTH
    tiles = [slice(r0, r0 + SUB_TILE) for r0 in range(0, x_ref.shape[0], SUB_TILE)]
    us = [(_ln(x_ref[rows, :]) * (1.0 + t[1:2]) + t[0:1]).astype(BF) for rows in tiles]
    rows4 = SUB_TILE // S5_BLOCK
    for u, rows in zip(us, tiles):
        zs = _dot(u, w_ref[:, 0:c_k])
        us_ref[rows, :] = zs.astype(BF)
        for hf in range(c_k // LANES):
            us_f32[hf, rows, :] = zs[:, hf * LANES:(hf + 1) * LANES]
        q0 = rows.start // S5_BLOCK
        for j in range(S5_BLOCK):
            for hf in range(c_k // LANES):
                c0 = j * c_k + hf * LANES
                u4_ref[q0:q0 + rows4, c0:c0 + LANES] = (
                    us_f32[hf, pl.ds(rows.start + j, rows4, stride=S5_BLOCK), :].astype(BF))
    for u, rows in zip(us, tiles):
        k_ref[rows, :] = _rope(_dot(u, w_ref[:, c_k:c_v]), cos_ref[rows, :], sin_ref[rows, :]).astype(BF)
    for u, rows in zip(us, tiles):
        v_ref[rows, :] = _dot(u, w_ref[:, c_v:c_f]).astype(BF)
    for u, rows in zip(us, tiles):
        f_ref[rows, :] = _dot(u, w_ref[:, c_f:c_q]).astype(BF)
    for u, rows in zip(us, tiles):
        q = _rope(_dot(u, w_ref[:, c_q:c_g]), cos_ref[rows, :], sin_ref[rows, :])
        q_ref[rows, :] = (q * q_scale).astype(BF)
    gw = NA_WIDTH
    for c0 in range(0, g_ref.shape[-1], gw):
        for u, rows in zip(us, tiles):
            g = _dot(u, w_ref[:, c_g + c0:c_g + c0 + gw])
            g_ref[rows, c0:c0 + gw] = jax.nn.sigmoid(g).astype(BF)


def _mixin_call(x, tab, w_in, l, cos_t, sin_t, q_scale):
    nb, n, d = x.shape
    win = w_in.shape[-1]
    ng = win - (SSM_WIDTH + 3 * NA_WIDTH + FNET_WIDTH)
    tm = min(MIXIN_ROWS, n)
    nt_pos = cos_t.shape[0] // tm

    def tok(width):
        return pl.BlockSpec((None, tm, width), lambda b, i: (b, i, 0))

    def shape(width):
        return jax.ShapeDtypeStruct((nb, n, width), BF)

    u4_spec = pl.BlockSpec((None, tm // S5_BLOCK, S5_BLOCK * SSM_WIDTH), lambda b, i: (b, i, 0))
    u4_shape = jax.ShapeDtypeStruct((nb, n // S5_BLOCK, S5_BLOCK * SSM_WIDTH), BF)
    return pl.pallas_call(
        functools.partial(_mixin_kernel, q_scale=q_scale),
        grid=(nb, n // tm),
        in_specs=[tok(d),
                  pl.BlockSpec((None, 8, d), lambda b, i: (b, 0, 0)),
                  _layer_spec(w_in, l, mode=pl.Buffered(1)),
                  pl.BlockSpec((tm, NA_HEAD_DIM), lambda b, i: (i % nt_pos, 0)),
                  pl.BlockSpec((tm, NA_HEAD_DIM), lambda b, i: (i % nt_pos, 0))],
        out_specs=[tok(SSM_WIDTH), u4_spec, tok(NA_WIDTH), tok(NA_WIDTH), tok(FNET_WIDTH), tok(NA_WIDTH),
                   tok(ng)],
        out_shape=[shape(SSM_WIDTH), u4_shape, shape(NA_WIDTH), shape(NA_WIDTH), shape(FNET_WIDTH),
                   shape(NA_WIDTH), shape(ng)],
        scratch_shapes=[pltpu.VMEM((SSM_WIDTH // LANES, tm, LANES), F32)],
        compiler_params=_cparams(2),
        name="mixin",
    )(x, tab, w_in, cos_t, sin_t)


def _s5_kernel(*refs):
    ins, outs, scr = refs[0:12], refs[12:16], refs[16:22]
    dirs = [dict(zip(("u", "h0", "wst", "wconv", "wout", "sc"), ins[6 * z:6 * z + 6]),
                 y=outs[2 * z], hfin=outs[2 * z + 1],
                 hbuf=scr[3 * z], hprev=scr[3 * z + 1], carry=scr[3 * z + 2], reverse=bool(z)) for z in range(2)]
    i = pl.program_id(1)
    ns = SSM_FLAT
    rows = dirs[0]["hbuf"].shape[0]
    sub = min(S5_SUB, rows)
    row_id = lax.broadcasted_iota(jnp.int32, (SCAN_ROWS, ns), 0)

    @pl.when(i == 0)
    def _():
        for d in dirs:
            d["carry"][...] = d["h0"][...]

    for d in dirs:
        order = (lambda r: list(r)[::-1]) if d["reverse"] else list
        d["order"] = order
        d["subs"] = order(range(0, rows, sub))
        d["edge"] = row_id == ((SCAN_ROWS - 1) if d["reverse"] else 0)
        d["c"] = (d["carry"][:, 0:ns], d["carry"][:, ns:2 * ns])

    for k in range(rows // sub):
        for d in dirs:
            s0 = d["subs"][k]
            d["hbuf"][s0:s0 + sub, :] = _dot(d["u"][s0:s0 + sub, :], d["wst"][...])

    def scan(d, s0):
        reverse, hbuf, hprev, sc_ref = d["reverse"], d["hbuf"], d["hprev"], d["sc"]
        w = ns // S5_LANE_SPLIT
        slabs = [(slice(c, c + w), slice(ns + c, ns + c + w)) for c in range(0, ns, w)]
        carries = [(d["c"][0][:, re], d["c"][1][:, re]) for re, _ in slabs]
        edge = lax.broadcasted_iota(jnp.int32, (SCAN_ROWS, w), 0) == ((SCAN_ROWS - 1) if reverse else 0)
        for r0 in d["order"](range(s0, s0 + sub, SCAN_ROWS)):
            for n, (re, im) in enumerate(slabs):
                cr, ci = carries[n]
                xr = hbuf[r0:r0 + SCAN_ROWS, re]
                xi = hbuf[r0:r0 + SCAN_ROWS, im]
                for k, s in enumerate((1, 2, 4)):
                    ar = sc_ref[k, :, re]
                    ai = sc_ref[k, :, im]
                    shift = (SCAN_ROWS - s) if reverse else s
                    sr = pltpu.roll(xr, shift, axis=0)
                    si = pltpu.roll(xi, shift, axis=0)
                    xr, xi = xr + ar * sr - ai * si, xi + ar * si + ai * sr
                pr = sc_ref[3, :, re]
                pi = sc_ref[3, :, im]
                hr = xr + pr * cr - pi * ci
                hi = xi + pr * ci + pi * cr
                one = (SCAN_ROWS - 1) if reverse else 1
                hprev[r0:r0 + SCAN_ROWS, re] = jnp.where(edge, cr, pltpu.roll(hr, one, axis=0))
                hprev[r0:r0 + SCAN_ROWS, im] = jnp.where(edge, ci, pltpu.roll(hi, one, axis=0))
                last = 0 if reverse else SCAN_ROWS - 1
                carries[n] = (jnp.broadcast_to(hr[last:last + 1, :], hr.shape),
                              jnp.broadcast_to(hi[last:last + 1, :], hi.shape))
        d["c"] = (jnp.concatenate([c[0] for c in carries], axis=1),
                  jnp.concatenate([c[1] for c in carries], axis=1))

    def readout(d, s0):
        y = (_dot(d["u"][s0:s0 + sub, :], d["wconv"][...])
             + _dot(d["hprev"][s0:s0 + sub, :].astype(BF), d["wout"][...]))
        d["y"][s0:s0 + sub, :] = y.astype(d["y"].dtype)

    for k in range(rows // sub):
        for d in dirs:
            scan(d, d["subs"][k])
        if k > 0:
            for d in dirs:
                readout(d, d["subs"][k - 1])
    for d in dirs:
        readout(d, d["subs"][-1])
        d["carry"][:, 0:ns] = d["c"][0]
        d["carry"][:, ns:2 * ns] = d["c"][1]
        d["hfin"][...] = d["carry"][...]


def _s5_call(u4, h0f, h0b, tables, l, rows):
    wst, wconv, wout, sc = tables
    b, n, w = u4.shape
    nc = n // rows
    ns2 = 2 * SSM_FLAT
    cmaps = (lambda bb, i: (bb, i, 0), lambda bb, i: (bb, nc - 1 - i, 0))
    state = pl.BlockSpec((None, 8, ns2), lambda bb, i: (bb, 0, 0))
    once = pl.Buffered(1)
    in_specs, args, out_specs, out_shape, scratch = [], [], [], [], []
    for z, h0 in enumerate((h0f, h0b)):
        in_specs += [pl.BlockSpec((None, rows, w), cmaps[z]), state] + [
            _layer_spec(t, l, z, mode=once) for t in (wst, wconv, wout, sc)]
        args += [u4, h0, wst, wconv, wout, sc]
        out_specs += [pl.BlockSpec((None, rows, w), cmaps[z]), state]
        out_shape += [jax.ShapeDtypeStruct((b, n, w), BF), jax.ShapeDtypeStruct((b, 8, ns2), F32)]
        scratch += [pltpu.VMEM((rows, ns2), F32), pltpu.VMEM((rows, ns2), F32), pltpu.VMEM((8, ns2), F32)]
    return pl.pallas_call(
        _s5_kernel,
        grid=(b, nc),
        in_specs=in_specs,
        out_specs=out_specs,
        out_shape=out_shape,
        scratch_shapes=scratch,
        compiler_params=_cparams(2),
        name="s5",
    )(*args)


def _s5_tables(log_dt, a_re, a_im, b_re, b_im, c_re, c_im):
    depth = log_dt.shape[0]
    g, p, c = b_re.shape[-3:]
    tb = S5_BLOCK
    hp = lax.Precision.HIGHEST
    f32 = lambda a: a.astype(F32)
    a_re, a_im, b_re, b_im, c_re, c_im = map(f32, (a_re, a_im, b_re, b_im, c_re, c_im))
    dt = jnp.exp(f32(log_dt))[..., None]
    adt_re = a_re * dt
    adt_im = a_im * dt

    def apow(k):
        kk = jnp.asarray(k, F32)[None, :, :, None, None]
        mag = jnp.exp(kk * adt_re[:, :, None])
        ang = kk * adt_im[:, :, None]
        return mag * jnp.cos(ang), mag * jnp.sin(ang)

    mag = jnp.exp(adt_re)
    ab_re = mag * jnp.cos(adt_im)
    ab_im = mag * jnp.sin(adt_im)
    den = a_re * a_re + a_im * a_im
    nr = ab_re - 1.0
    fr = (nr * a_re + ab_im * a_im) / den
    fi = (ab_im * a_re - nr * a_im) / den
    bb_re = fr[..., None] * b_re - fi[..., None] * b_im
    bb_im = fr[..., None] * b_im + fi[..., None] * b_re

    step = np.arange(tb)
    k_state = np.stack([tb - 1 - step, step])
    k_out = np.stack([step + 1, tb - step])
    k_tap = np.stack([step, step])
    k_scan = np.stack([tb * (np.arange(SCAN_ROWS) + 1)] * 2)

    def block_diag(compact, row_group, col_width):
        n = compact.shape[3]
        dst = np.arange(n * g * col_width)
        src = (dst // (g * col_width)) * col_width + dst % col_width
        copy = np.zeros((n * col_width, n * g * col_width), np.float32)
        copy[src, dst] = 1.0
        cols = jnp.einsum('dzrk,kn->dzrn', compact.reshape(compact.shape[:3] + (n * col_width,)),
                          jnp.asarray(copy), precision=hp, preferred_element_type=BF)
        keep = jnp.asarray(row_group[:, None] == ((dst // col_width) % g)[None, :])
        return jnp.where(keep, cols, jnp.zeros((), BF))

    chan_rows = (np.arange(tb * g * c) // c) % g
    state_rows = (np.arange(2 * g * p) // p) % g

    def cmul_b(pr, pi):
        return (pr[..., None] * bb_re[:, :, None] - pi[..., None] * bb_im[:, :, None],
                pr[..., None] * bb_im[:, :, None] + pi[..., None] * bb_re[:, :, None])

    wr, wi = cmul_b(*apow(k_state))
    w2 = jnp.stack([wr, wi], axis=3)
    w2 = jnp.transpose(w2, (0, 1, 2, 4, 6, 3, 5))
    wst = block_diag(w2.reshape(depth, 2, tb * g * c, 2, p), chan_rows, p)

    pr, pi = apow(k_out)
    cm_re = c_re[:, :, None] * pr[:, :, :, :, None, :] - c_im[:, :, None] * pi[:, :, :, :, None, :]
    cm_im = c_re[:, :, None] * pi[:, :, :, :, None, :] + c_im[:, :, None] * pr[:, :, :, :, None, :]
    cm = jnp.stack([cm_re, -cm_im], axis=3)
    cm = jnp.transpose(cm, (0, 1, 3, 4, 6, 2, 5))
    wout = block_diag(cm.reshape(depth, 2, 2 * g * p, tb, c), state_rows, c)

    abr, abi = cmul_b(*apow(k_tap))
    taps = (jnp.einsum('dzgcp,dztgpe->dztgec', c_re, abr, precision=hp)
            - jnp.einsum('dzgcp,dztgpe->dztgec', c_im, abi, precision=hp))
    sel = np.zeros((2, tb, tb, tb), np.float32)
    for i in range(tb):
        for j in range(tb):
            if j >= i:
                sel[0, i, j, j - i] = 1.0
            if i >= j:
                sel[1, i, j, i - j] = 1.0
    kij = jnp.einsum('dztgec,zijt->dzigejc', taps, jnp.asarray(sel), precision=hp)
    wconv = block_diag(kij.reshape(depth, 2, tb * g * c, tb, c), chan_rows, c)

    qr, qi = apow(k_scan)
    qr = qr.reshape(depth, 2, SCAN_ROWS, g * p)
    qi = qi.reshape(depth, 2, SCAN_ROWS, g * p)
    rows = np.arange(SCAN_ROWS)
    tabs = []
    for s in (1, 2, 4):
        keep = jnp.asarray(np.stack([rows >= s, rows + s <= SCAN_ROWS - 1]), F32)[None, :, :, None]
        tabs.append(jnp.concatenate([keep * qr[:, :, s - 1:s], keep * qi[:, :, s - 1:s]], axis=-1))
    carry_pow = lambda t: jnp.stack([t[:, 0], t[:, 1, ::-1]], axis=1)
    tabs.append(jnp.concatenate([carry_pow(qr), carry_pow(qi)], axis=-1))
    sc = jnp.stack(tabs, axis=2)
    return wst, wconv, wout, sc


def _fnet1_kernel(x_ref, f2_ref, tc_ref, ts_ref, o_ref):
    n2 = x_ref.shape[0]
    y = _dot(f2_ref[...], x_ref[...])
    br = y[0:n2]
    bi = y[n2:2 * n2]
    tc = tc_ref[...]
    ts = ts_ref[...]
    o_ref[0] = (br * tc + bi * ts).astype(BF)
    o_ref[1] = (bi * tc - br * ts).astype(BF)


def _fnet2_kernel(b_ref, f1_ref, cc_ref, sc_ref, o_ref, *, scale):
    kb = b_ref.shape[1]
    n1 = b_ref.shape[2]
    c = b_ref.shape[3]
    zr, zi = [], []
    for j in range(kb):
        rhs = jnp.concatenate([b_ref[0, j], b_ref[1, j]], axis=0)
        z = _dot(f1_ref[...], rhs)
        zr.append(z[0:n1])
        zi.append(z[n1:2 * n1])
    zr = jnp.concatenate(zr, axis=0).astype(BF)
    zi = jnp.concatenate(zi, axis=0).astype(BF)
    y = (_dot(zr, cc_ref[...]) + _dot(zi, sc_ref[...])) * scale
    for j in range(kb):
        o_ref[:, j * c:(j + 1) * c] = y[j * n1:(j + 1) * n1].astype(o_ref.dtype)


def _dft_tables(n):
    idx = np.arange(n)
    ang = 2.0 * np.pi * ((idx[:, None] * idx[None, :]) % n) / n
    return np.cos(ang), np.sin(ang)


def _chan_tables():
    cc, sc = _dft_tables(FNET_GROUP_CH)
    eye = np.eye(FNET_GROUPS)
    return jnp.asarray(np.kron(eye, cc), BF), jnp.asarray(np.kron(eye, sc), BF)


def _fnet_call(f):
    b, n, c = f.shape
    n2 = FNET_N2
    n1 = n // n2
    c2, s2 = _dft_tables(n2)
    f2 = jnp.asarray(np.concatenate([c2, -s2], axis=0), BF)
    k2 = np.arange(n2)[:, None]
    nn1 = np.arange(n1)[None, :]
    ang = 2.0 * np.pi * ((k2 * nn1) % n) / n
    tc = jnp.asarray(np.repeat(np.cos(ang), c, axis=1), F32)
    ts = jnp.asarray(np.repeat(np.sin(ang), c, axis=1), F32)
    c1, s1 = _dft_tables(n1)
    f1 = jnp.asarray(np.block([[c1, s1], [-s1, c1]]), BF)
    cc, sc = _chan_tables()

    wc = min(n1 * c, 4096)
    x2 = f.reshape(b, n2, n1 * c)
    st1 = pl.pallas_call(
        _fnet1_kernel,
        grid=(b, (n1 * c) // wc),
        in_specs=[pl.BlockSpec((None, n2, wc), lambda bb, j: (bb, 0, j)),
                  pl.BlockSpec((2 * n2, n2), lambda bb, j: (0, 0)),
                  pl.BlockSpec((n2, wc), lambda bb, j: (0, j)),
                  pl.BlockSpec((n2, wc), lambda bb, j: (0, j))],
        out_specs=pl.BlockSpec((None, 2, n2, wc), lambda bb, j: (bb, 0, 0, j)),
        out_shape=jax.ShapeDtypeStruct((b, 2, n2, n1 * c), BF),
        compiler_params=_cparams(2),
        name="fnet1",
    )(x2, f2, tc, ts)
    st1 = st1.reshape(b, 2, n2, n1, c)
    kb = FNET_K2_BLOCK
    y = pl.pallas_call(
        functools.partial(_fnet2_kernel, scale=1.0 / math.sqrt(n * FNET_GROUP_CH)),
        grid=(b, n2 // kb),
        in_specs=[pl.BlockSpec((None, 2, kb, n1, c), lambda bb, j: (bb, 0, j, 0, 0)),
                  pl.BlockSpec((2 * n1, 2 * n1), lambda bb, j: (0, 0)),
                  pl.BlockSpec((c, c), lambda bb, j: (0, 0)),
                  pl.BlockSpec((c, c), lambda bb, j: (0, 0))],
        out_specs=pl.BlockSpec((None, n1, kb * c), lambda bb, j: (bb, 0, j)),
        out_shape=jax.ShapeDtypeStruct((b, n1, n2 * c), BF),
        compiler_params=_cparams(2),
        name="fnet2",
    )(st1, f1, cc, sc)
    return y.reshape(b, n, c)


def _fnet_dense_kernel(x_ref, cl_ref, sl_ref, cc_ref, sc_ref, o_ref, *, scale):
    x = x_ref[...]
    xr = _dot(x, cc_ref[...]).astype(BF)
    xi = _dot(x, sc_ref[...]).astype(BF)
    o_ref[...] = ((_dot(cl_ref[...], xr) - _dot(sl_ref[...], xi)) * scale).astype(o_ref.dtype)


def _fnet_dense_call(f):
    b, n, c = f.shape
    cl, sl = _dft_tables(n)
    cc, sc = _chan_tables()
    full = lambda bb: (0, 0)
    return pl.pallas_call(
        functools.partial(_fnet_dense_kernel, scale=1.0 / math.sqrt(n * FNET_GROUP_CH)),
        grid=(b,),
        in_specs=[pl.BlockSpec((None, n, c), lambda bb: (bb, 0, 0)),
                  pl.BlockSpec((n, n), full), pl.BlockSpec((n, n), full),
                  pl.BlockSpec((c, c), full), pl.BlockSpec((c, c), full)],
        out_specs=pl.BlockSpec((None, n, c), lambda bb: (bb, 0, 0)),
        out_shape=jax.ShapeDtypeStruct((b, n, c), BF),
        compiler_params=_cparams(1),
        name="fnet_ctx",
    )(f, jnp.asarray(cl, BF), jnp.asarray(sl, BF), cc, sc)


def _head_masks(width):
    lane = lax.broadcasted_iota(jnp.int32, (1, width), 1)
    return [(lane // NA_HEAD_DIM) == h for h in range(width // NA_HEAD_DIM)]


def _attend(q, k_parts, v_parts, ranges, bias_of, o_ref):
    zero = jnp.zeros((), BF)
    hw = NA_HQ * NA_HEAD_DIM
    masks = _head_masks(hw)
    m_rows = q.shape[0]
    rb = min(GRID_W, m_rows)
    live = [j for j, (a, b) in enumerate(ranges) if b > a]
    k_parts = [k_parts[j] for j in live]
    v_parts = [v_parts[j] for j in live]
    ranges = [ranges[j] for j in live]
    quad = lambda x, h: x[:, (h // NA_HQ) * hw:(h // NA_HQ + 1) * hw]

    def scores(h):
        qh = jnp.where(masks[h % NA_HQ], quad(q, h), zero)
        s_parts = []
        for j, (kp, (a, b)) in enumerate(zip(k_parts, ranges)):
            s = _dot_t(qh[a:b], quad(kp, h))
            bias = bias_of(h, live[j], a, b)
            s_parts.append(s if bias is None else s + bias)
        return s_parts

    def lane_tiles(x):
        return [x[:, c:c + LANES] for c in range(0, x.shape[1], LANES)]

    def softmax(s_parts):
        p_rows = [[] for _ in s_parts]
        dens = []
        for r0 in range(0, m_rows, rb):
            pieces = [(j, s[r0 - a:r0 - a + rb]) for j, (s, (a, b)) in enumerate(zip(s_parts, ranges))
                      if a <= r0 < b]
            m = functools.reduce(jnp.maximum, [t for _, pc in pieces for t in lane_tiles(pc)])
            m = m.max(axis=-1, keepdims=True)
            acc = None
            for j, pc in pieces:
                p = jnp.exp2(pc - m)
                acc = functools.reduce(jnp.add, lane_tiles(p), acc) if acc is not None else \
                    functools.reduce(jnp.add, lane_tiles(p))
                p_rows[j].append(p.astype(BF))
            dens.append(acc.sum(axis=-1, keepdims=True))
        return [jnp.concatenate(r, axis=0) for r in p_rows], jnp.concatenate(dens, axis=0)

    def values(h, p_parts, den):
        out = None
        for p, vp, (a, b) in zip(p_parts, v_parts, ranges):
            o = _dot(p, quad(vp, h))
            pad = [jnp.zeros((n, o.shape[1]), F32) for n in (a, m_rows - b)]
            o = jnp.concatenate([x for x in (pad[0], o, pad[1]) if x.shape[0]], axis=0)
            out = o if out is None else out + o
        return out / den

    n = q.shape[-1] // NA_HEAD_DIM
    s_of, p_of, outs = {}, {}, [None] * (n // NA_HQ)
    for step in range(n + 2):
        if step < n:
            s_of[step] = scores(step)
        if 0 <= step - 1 < n:
            p_of[step - 1] = softmax(s_of.pop(step - 1))
        if 0 <= step - 2 < n:
            h = step - 2
            o = values(h, *p_of.pop(h))
            prev = outs[h // NA_HQ]
            outs[h // NA_HQ] = o if prev is None else jnp.where(masks[h % NA_HQ], o, prev)
    o_ref[...] = jnp.concatenate(outs, axis=-1).astype(o_ref.dtype)


def _natten_kernel(q_ref, k0, k1, k2, k3, v0, v1, v2, v3, kc_ref, vc_ref, bias_ref, *rest, ranges, pair_index):
    o_ref = rest[-1]
    k_parts = [r[...] for r in (k0, k1, k2, k3)] + [kc_ref[...]]
    v_parts = [r[...] for r in (v0, v1, v2, v3)] + [vc_ref[...]]
    pairs_per_part = k0.shape[0] // (2 * GRID_W)

    def bias_of(h, j, a, b):
        if j >= 4:
            return None
        rows = [jnp.concatenate([bias_ref[h, pair_index[r][j * pairs_per_part + p]]
                                 for p in range(pairs_per_part)], axis=1)
                for r in range(a // GRID_W, b // GRID_W)]
        return jnp.concatenate(rows, axis=0).astype(F32)

    _attend(q_ref[...], k_parts, v_parts, ranges, bias_of, o_ref)


def _na_windows(rows):
    nj = rows // NA_QROWS
    out = []
    for j in (0, min(1, nj - 1), nj - 1):
        rq = NA_QROWS * j + np.arange(NA_QROWS)
        ws = int(np.clip(NA_QROWS * j - (NA_KROWS - NA_QROWS) // 2, 0, rows - NA_KROWS))
        rk = ws + np.arange(NA_KROWS)
        r0 = np.clip(rq - WIN_ROWS // 2, 0, rows - WIN_ROWS)
        valid_r = (rk[None, :] >= r0[:, None]) & (rk[None, :] < r0[:, None] + WIN_ROWS)
        dr = rk[None, :] - rq[:, None] + (WIN_ROWS - 1)
        out.append((valid_r, dr))
    return out


def _natten_call(q, k, v, kc, vc, bias, l, variant, prev=None):
    table, index = bias
    b, n, w = q.shape
    lc = kc.shape[1]
    qt = NA_QROWS * GRID_W
    blk = (NA_KROWS // 4) * GRID_W
    nj = n // qt
    nblk = n // blk
    ratio = qt // blk
    j0, count = ((0, 1), (1, nj - 2), (nj - 1, 1))[variant]

    valid_r = _na_windows(n // GRID_W)[variant][0]
    ranges = []
    for t in range(4):
        seen = np.nonzero(valid_r[:, t * (NA_KROWS // 4):(t + 1) * (NA_KROWS // 4)].any(axis=1))[0]
        ranges.append((int(seen.min()) * GRID_W, (int(seen.max()) + 1) * GRID_W) if seen.size else (0, 0))
    ranges.append((0, qt))

    def kv_spec(m):
        def imap(j, bb):
            start = jnp.clip(ratio * (j + j0) - 1, 0, nblk - 4)
            return (bb, start + m, 0)
        return pl.BlockSpec((None, blk, w), imap)

    in_specs = ([pl.BlockSpec((None, qt, w), lambda j, bb: (bb, j + j0, 0))]
                + [kv_spec(m) for m in range(4)] + [kv_spec(m) for m in range(4)]
                + [pl.BlockSpec((None, lc, w), lambda j, bb: (bb, 0, 0)),
                   pl.BlockSpec((None, lc, w), lambda j, bb: (bb, 0, 0)),
                   _layer_spec(table, l, mode=pl.Buffered(1))])
    args = [q, k, k, k, k, v, v, v, v, kc, vc, table]
    aliases = {}
    if prev is not None:
        in_specs.append(pl.BlockSpec(memory_space=pl.ANY))
        aliases = {len(args): 0}
        args.append(prev)
    return pl.pallas_call(
        functools.partial(_natten_kernel, ranges=tuple(ranges), pair_index=index[variant]),
        grid=(count, b),
        in_specs=in_specs,
        out_specs=pl.BlockSpec((None, qt, w), lambda j, bb: (bb, j + j0, 0)),
        out_shape=jax.ShapeDtypeStruct((b, n, w), BF),
        input_output_aliases=aliases,
        compiler_params=_cparams(2),
        name="natten",
    )(*args)


def _ctx_atten_kernel(q_ref, kc_ref, vc_ref, o_ref):
    _attend(q_ref[...], [kc_ref[...]], [vc_ref[...]], [(0, q_ref.shape[0])], lambda h, j, a, b: None, o_ref)


def _ctx_atten_call(qc, kc, vc):
    b, lc, w = qc.shape
    spec = pl.BlockSpec((None, lc, w), lambda bb: (bb, 0, 0))
    return pl.pallas_call(
        _ctx_atten_kernel,
        grid=(b,),
        in_specs=[spec, spec, spec],
        out_specs=spec,
        out_shape=jax.ShapeDtypeStruct((b, lc, w), BF),
        compiler_params=_cparams(1),
        name="ctx_atten",
    )(qc, kc, vc)


def _na_bias(rpb, rows):
    w = GRID_W
    n_dr = 2 * WIN_ROWS - 1
    n_dc = 2 * WIN_COLS - 1
    cq = np.arange(w)
    cs = np.clip(cq - WIN_COLS // 2, 0, w - WIN_COLS)
    ck = np.arange(w)
    valid_c = (ck[None, :] >= cs[:, None]) & (ck[None, :] < cs[:, None] + WIN_COLS)
    dc = ck[None, :] - cq[:, None] + (WIN_COLS - 1)
    pick_c = (dc[:, :, None] == np.arange(n_dc)) & valid_c[:, :, None]
    tiles = jnp.einsum('lhab,cdb->lhacd', rpb.astype(F32) * LOG2E, jnp.asarray(pick_c, F32),
                       precision=lax.Precision.HIGHEST)
    tiles = tiles + jnp.asarray(np.where(valid_c, 0.0, NEG_BIG), F32)
    tiles = jnp.concatenate([tiles, jnp.full_like(tiles[:, :, :1], NEG_BIG)], axis=2).astype(BF)
    pairs, index = {}, []
    for valid_r, dr in _na_windows(rows):
        dr = np.where(valid_r, dr, n_dr)
        index.append(tuple(tuple(pairs.setdefault((int(dr[a, 2 * p]), int(dr[a, 2 * p + 1])), len(pairs))
                                 for p in range(NA_KROWS // 2)) for a in range(NA_QROWS)))
    table = jnp.stack([jnp.concatenate([tiles[:, :, i0], tiles[:, :, i1]], axis=-1) for i0, i1 in pairs], axis=2)
    return table, index


def _rope_tables(n):
    nf = NA_HEAD_DIM // 4
    t = jnp.arange(n, dtype=jnp.int32)
    pos = jnp.stack([t // GRID_W, t % GRID_W], axis=-1).astype(F32)
    inv_freq = ROPE_THETA ** (-jnp.arange(nf, dtype=F32) / nf)
    ang = pos[:, :, None] * inv_freq
    cos = jnp.cos(ang)
    sin = jnp.sin(ang)
    cos_h = jnp.concatenate([cos, cos], axis=-1).reshape(n, NA_HEAD_DIM)
    sin_h = jnp.concatenate([-sin, sin], axis=-1).reshape(n, NA_HEAD_DIM)
    return cos_h, sin_h


def _merge_kernel(ysf_ref, ysb_ref, us_ref, yf_ref, yn_ref, g_ref, h_ref, t_ref,
                  sd_ref, wglu_ref, bglu_ref, wbs_ref, wbf_ref, wbn_ref, wo_ref, o_ref, ys_f32, *, alpha):
    d = h_ref.shape[-1]
    t = t_ref[...]
    halves = SSM_WIDTH // LANES
    tiles = [slice(r0, r0 + SUB_TILE) for r0 in range(0, h_ref.shape[0], SUB_TILE)]
    rows4 = SUB_TILE // S5_BLOCK
    ys = []
    for rows in tiles:
        q0 = rows.start // S5_BLOCK
        y4 = ysf_ref[q0:q0 + rows4, :].astype(F32) + ysb_ref[q0:q0 + rows4, :].astype(F32)
        for j in range(S5_BLOCK):
            for hf in range(halves):
                c0 = j * SSM_WIDTH + hf * LANES
                ys_f32[hf, pl.ds(rows.start + j, rows4, stride=S5_BLOCK), :] = y4[:, c0:c0 + LANES]
        y = jnp.concatenate([ys_f32[hf, rows, :] for hf in range(halves)], axis=-1)
        ys.append(jax.nn.gelu(y + sd_ref[...] * us_ref[rows, :].astype(F32)))
    glu = [_dot(y.astype(BF), wglu_ref[...]) for y in ys]
    ys = [(y * jax.nn.sigmoid(z + bglu_ref[...])).astype(BF) for y, z in zip(ys, glu)]
    br = [(_dot(y, wbs_ref[...]), _dot(yf_ref[rows, :], wbf_ref[...]), _dot(yn_ref[rows, :], wbn_ref[...]))
          for y, rows in zip(ys, tiles)]
    mix = [(g_ref[rows, 0:d].astype(F32) * bs + g_ref[rows, d:2 * d].astype(F32) * bf
            + g_ref[rows, 2 * d:3 * d].astype(F32) * bn).astype(BF) for (bs, bf, bn), rows in zip(br, tiles)]
    proj = [_dot(m, wo_ref[...]) for m in mix]
    for y, rows in zip(proj, tiles):
        v = alpha * h_ref[rows, :] + t[2:3] * y
        o_ref[rows, :] = _ln(v) * t[3:4] + t[4:5]


def _merge_call(ysf4, ysb4, us, yf, yn, g, h, tab, s5w, mats, l, alpha):
    nb, n, d = h.shape
    tm = min(MERGE_ROWS, n)

    def tok(a):
        return pl.BlockSpec((None, tm, a.shape[-1]), lambda b, i: (b, i, 0))

    def tok4(a):
        return pl.BlockSpec((None, tm // S5_BLOCK, a.shape[-1]), lambda b, i: (b, i, 0))

    toks = (us, yf, yn, g, h)
    return pl.pallas_call(
        functools.partial(_merge_kernel, alpha=alpha),
        grid=(nb, n // tm),
        in_specs=[tok4(ysf4), tok4(ysb4)] + [tok(a) for a in toks]
                 + [pl.BlockSpec((None, 8, d), lambda b, i: (b, 0, 0))]
                 + [_layer_spec(a, l) for a in tuple(s5w) + tuple(mats)],
        out_specs=tok(h),
        out_shape=jax.ShapeDtypeStruct(h.shape, F32),
        scratch_shapes=[pltpu.VMEM((SSM_WIDTH // LANES, tm, LANES), F32)],
        compiler_params=_cparams(2),
        name="merge",
    )(ysf4, ysb4, *toks, tab, *s5w, *mats)


def _table(mod, d, chunks, extra):
    rows = [mod[:, i * d:(i + 1) * d] for i in chunks]
    rows += [jnp.broadcast_to(e[None, :], (8, d)) for e in extra]
    rows += [jnp.zeros((8, d), F32)] * (8 - len(rows))
    return jnp.stack(rows, axis=1)


def kernel(x, c, ctx, c_ctx, w_mod, b_mod, ln_g, ln_b, ffn_w_gate, ffn_w_up, ffn_w_down, w_in, ssm_log_dt, ssm_a_re, ssm_a_im, ssm_b_re, ssm_b_im, ssm_c_re, ssm_c_im, ssm_d, ssm_w_glu, ssm_b_glu, na_rpb, w_br_ssm, w_br_fnet, w_br_na, w_out):
    bsz, n, d = x.shape
    lc = ctx.shape[1]
    depth = w_mod.shape[0]
    rows = n // GRID_W
    assert bsz + 1 <= 8 and rows % NA_QROWS == 0 and rows >= NA_KROWS and n % FNET_N2 == 0
    alpha = (2 * depth) ** 0.25
    q_scale = NA_HEAD_DIM ** -0.5 * LOG2E

    c8 = jnp.concatenate([c, c_ctx[None, :], jnp.zeros((8 - bsz - 1, d), F32)], axis=0)
    mod_all = _mod_call(c8, w_mod, b_mod)

    wg = ffn_w_gate.astype(BF)
    wu = ffn_w_up.astype(BF)
    wd = ffn_w_down.astype(BF)
    win = w_in.astype(BF)
    mats = (w_br_ssm.astype(BF), w_br_fnet.astype(BF), w_br_na.astype(BF), w_out.astype(BF))
    s5w = (ssm_d[:, None, :], ssm_w_glu.astype(BF), ssm_b_glu[:, None, :])
    s5_tabs = _s5_tables(ssm_log_dt, ssm_a_re, ssm_a_im, ssm_b_re, ssm_b_im, ssm_c_re, ssm_c_im)
    na_bias = _na_bias(na_rpb, rows)
    cos_t, sin_t = _rope_tables(n)
    cos_c = jnp.ones((bsz * lc, NA_HEAD_DIM), F32)
    sin_c = jnp.zeros((bsz * lc, NA_HEAD_DIM), F32)
    zero_state = jnp.zeros((bsz, 8, 2 * SSM_FLAT), F32)

    h = x
    hc = ctx.reshape(1, bsz * lc, d)
    for l in range(depth):
        last = l == depth - 1
        mod = mod_all[l]

        tab = _table(mod, d, (0, 1, 2), (ln_g[l, 0], ln_b[l, 0]))
        h = _ffn_call(h, tab[:bsz], wg, wu, wd, l, 0, alpha)
        hc = _ffn_call(hc, tab[bsz:bsz + 1], wg, wu, wd, l, 0, alpha)

        tab = _table(mod, d, (3, 4, 5), (ln_g[l, 1], ln_b[l, 1]))
        us, u4, k, v, f, q, g = _mixin_call(h, tab[:bsz], win, l, cos_t, sin_t, q_scale)
        usc, u4c, kc, vc, fc, qc, gc = _mixin_call(hc, tab[bsz:bsz + 1], win, l, cos_c, sin_c, q_scale)
        u4c, kc, vc, fc, qc = [a.reshape(bsz, a.shape[1] // bsz, a.shape[-1]) for a in (u4c, kc, vc, fc, qc)]

        ysfc, hcf, ysbc, hcb = _s5_call(u4c, zero_state, zero_state, s5_tabs, l, u4c.shape[1])
        ysf, _, ysb, _ = _s5_call(u4, hcf, hcb, s5_tabs, l, min(S5_ROWS, u4.shape[1]))

        yf = _fnet_call(f)
        yn = None
        for variant in ((1, 0, 2) if rows > 2 * NA_QROWS else (0, 2)):
            yn = _natten_call(q, k, v, kc, vc, na_bias, l, variant, yn)

        h = _merge_call(ysf, ysb, us, yf, yn, g, h, tab[:bsz], s5w, mats, l, alpha)

        tab3 = _table(mod, d, (6, 7, 8), (ln_g[l, 2], ln_b[l, 2]))
        h = _ffn_call(h, tab3[:bsz], wg, wu, wd, l, 1, alpha)

        if not last:
            yfc = _fnet_dense_call(fc)
            ync = _ctx_atten_call(qc, kc, vc)
            flat = lambda a: a.reshape(1, a.shape[0] * a.shape[1], a.shape[-1])
            hc = _merge_call(flat(ysfc), flat(ysbc), usc, flat(yfc), flat(ync), gc, hc, tab[bsz:bsz + 1],
                             s5w, mats, l, alpha)
            hc = _ffn_call(hc, tab3[bsz:bsz + 1], wg, wu, wd, l, 1, alpha)
    return h
```
